```python
import math
import jax, jax.numpy as jnp
from jax import lax
import numpy as np

D_MODEL = 2048
BATCH = 32
SEQ = 256
DEPTH = 2
DEC_BATCH = 2
DEC_SEQ = 4096
PAST_LEN = 256

GRID_W = 64
D_SSM = 512
SSM_GROUP = 16
N_SSM_GROUPS = D_SSM // SSM_GROUP
SSM_STATE = 64
HEAD_DIM = 128
N_HEADS = 8
N_KV_HEADS = 2
D_ATTN = N_HEADS * HEAD_DIM
D_KV = N_KV_HEADS * HEAD_DIM
WINDOW = 128
BLOCK = 128
ROPE_BASE = 10000.0
D_SGU = 512
SGU_CHUNK = 128
SGU_GROUP_CH = 128
N_SGU_GROUPS = D_SGU // SGU_GROUP_CH
N_BRANCH = 3
IN_SIZES = (D_SSM, D_SSM, D_ATTN, D_KV, D_KV, D_ATTN, D_SGU, D_SGU, D_SGU, N_BRANCH * D_MODEL)
IN_SPLITS = tuple(sum(IN_SIZES[:i + 1]) for i in range(len(IN_SIZES) - 1))
D_IN = sum(IN_SIZES)
DEEPNORM_ALPHA = (2 * DEPTH) ** 0.25
DEEPNORM_BETA = (8 * DEPTH) ** -0.25
LN_EPS = 1e-5

kernel_name = 'hybrid_s5_swa_sgu_diffusion_step'


def _layernorm(x):
    xf = x.astype(jnp.float32)
    mu = jnp.mean(xf, axis=-1, keepdims=True)
    var = jnp.mean(jnp.square(xf - mu), axis=-1, keepdims=True)
    return ((xf - mu) * lax.rsqrt(var + LN_EPS)).astype(x.dtype)


def _axial_rope(x, rows):
    row = jnp.repeat(jnp.arange(rows), GRID_W)
    col = jnp.tile(jnp.arange(GRID_W), rows)
    half = HEAD_DIM // 2
    inv = ROPE_BASE ** (-jnp.arange(0, half, 2, dtype=jnp.float32) / half)

    def rot(xh, pos):
        ang = pos.astype(jnp.float32)[:, None] * inv[None, :]
        cos = jnp.cos(ang)[None, :, None, :]
        sin = jnp.sin(ang)[None, :, None, :]
        x1, x2 = jnp.split(xh.astype(jnp.float32), 2, axis=-1)
        return jnp.concatenate([x1 * cos - x2 * sin, x1 * sin + x2 * cos], axis=-1)

    xr, xc = jnp.split(x, 2, axis=-1)
    return jnp.concatenate([rot(xr, row), rot(xc, col)], axis=-1).astype(x.dtype)


def _softmax_with_sink(s, sink_kg):
    sk = jnp.broadcast_to(sink_kg.astype(jnp.float32)[None, :, :, None, None], s.shape[:-1] + (1,))
    p = jax.nn.softmax(jnp.concatenate([s, sk], axis=-1), axis=-1)
    return p[..., :-1]


def _context_attention(q, k, v, sink):
    b, lc = q.shape[:2]
    nb = lc // BLOCK
    g = N_HEADS // N_KV_HEADS
    scale = HEAD_DIM ** -0.5
    sink_kg = sink.reshape(N_KV_HEADS, g)
    qb = q.reshape(b, nb, BLOCK, N_KV_HEADS, g, HEAD_DIM).transpose(1, 0, 2, 3, 4, 5)

    def one(qblk):
        s = jnp.einsum('bqkgd,bskd->bkgqs', qblk, k).astype(jnp.float32) * scale
        p = _softmax_with_sink(s, sink_kg).astype(v.dtype)
        return jnp.einsum('bkgqs,bskd->bqkgd', p, v)

    o = lax.map(one, qb)
    return o.transpose(1, 0, 2, 3, 4, 5).reshape(b, lc, D_ATTN)


def _latent_attention(q, k, v, kc, vc, sink):
    b, l = q.shape[:2]
    nb = l // BLOCK
    g = N_HEADS // N_KV_HEADS
    scale = HEAD_DIM ** -0.5
    sink_kg = sink.reshape(N_KV_HEADS, g)
    pad = ((0, 0), (BLOCK, BLOCK), (0, 0), (0, 0))
    kp = jnp.pad(k, pad).reshape(b, nb + 2, BLOCK, N_KV_HEADS, HEAD_DIM)
    vp = jnp.pad(v, pad).reshape(b, nb + 2, BLOCK, N_KV_HEADS, HEAD_DIM)
    kband = jnp.concatenate([kp[:, :-2], kp[:, 1:-1], kp[:, 2:]], axis=2)
    vband = jnp.concatenate([vp[:, :-2], vp[:, 1:-1], vp[:, 2:]], axis=2)
    qpos = jnp.arange(nb)[:, None] * BLOCK + jnp.arange(BLOCK)[None, :]
    kpos = jnp.arange(nb)[:, None] * BLOCK - BLOCK + jnp.arange(3 * BLOCK)[None, :]
    kq = kpos[:, None, :]
    mask = (jnp.abs(kq - qpos[:, :, None]) <= WINDOW) & (kq >= 0) & (kq < l)
    qb = q.reshape(b, nb, BLOCK, N_KV_HEADS, g, HEAD_DIM).transpose(1, 0, 2, 3, 4, 5)
    xs = (qb, kband.transpose(1, 0, 2, 3, 4), vband.transpose(1, 0, 2, 3, 4), mask)
    n_loc = 3 * BLOCK

    def one(args):
        qblk, kblk, vblk, m = args
        s_loc = jnp.einsum('bqkgd,bskd->bkgqs', qblk, kblk).astype(jnp.float32) * scale
        s_loc = jnp.where(m[None, None, None], s_loc, -jnp.inf)
        s_ctx = jnp.einsum('bqkgd,bskd->bkgqs', qblk, kc).astype(jnp.float32) * scale
        p = _softmax_with_sink(jnp.concatenate([s_loc, s_ctx], axis=-1), sink_kg).astype(v.dtype)
        return (jnp.einsum('bkgqs,bskd->bqkgd', p[..., :n_loc], vblk)
                + jnp.einsum('bkgqs,bskd->bqkgd', p[..., n_loc:], vc))

    o = lax.map(one, xs)
    return o.transpose(1, 0, 2, 3, 4, 5).reshape(b, l, D_ATTN)


def _zoh(lam_re, lam_im, log_step, b_re, b_im):
    lr = lam_re.astype(jnp.float32)
    li = lam_im.astype(jnp.float32)
    dt = jnp.exp(log_step.astype(jnp.float32))[:, None]
    mag = jnp.exp(lr * dt)
    ab_re = mag * jnp.cos(li * dt)
    ab_im = mag * jnp.sin(li * dt)
    den = lr * lr + li * li
    nr = ab_re - 1.0
    f_re = (nr * lr + ab_im * li) / den
    f_im = (ab_im * lr - nr * li) / den
    br = b_re.astype(jnp.float32)
    bi = b_im.astype(jnp.float32)
    bb_re = f_re[..., None] * br - f_im[..., None] * bi
    bb_im = f_re[..., None] * bi + f_im[..., None] * br
    return ab_re, ab_im, bb_re, bb_im


def _complex_scan(a_re, a_im, u_re, u_im, s0_re, s0_im):
    u_re = u_re.at[:, 0].add(a_re * s0_re - a_im * s0_im)
    u_im = u_im.at[:, 0].add(a_re * s0_im + a_im * s0_re)
    A_re = jnp.broadcast_to(a_re, u_re.shape)
    A_im = jnp.broadcast_to(a_im, u_im.shape)

    def comb(e1, e2):
        a1r, a1i, b1r, b1i = e1
        a2r, a2i, b2r, b2i = e2
        return (a2r * a1r - a2i * a1i, a2r * a1i + a2i * a1r,
                a2r * b1r - a2i * b1i + b2r, a2r * b1i + a2i * b1r + b2i)

    _, _, s_re, s_im = lax.associative_scan(comb, (A_re, A_im, u_re, u_im), axis=1)
    return s_re, s_im


def _ssm_branch(xa, s0, lam_re, lam_im, log_step, b_re, b_im, c_re, c_im, d_skip, w_glu, b_glu):
    bsz, L = xa.shape[:2]
    u = xa.astype(jnp.float32).reshape(bsz, L, N_SSM_GROUPS, SSM_GROUP)
    y = u * d_skip.astype(jnp.float32).reshape(N_SSM_GROUPS, SSM_GROUP)
    s0 = s0.astype(jnp.float32)
    finals = []
    for d in range(2):
        ab_re, ab_im, bb_re, bb_im = _zoh(lam_re[d], lam_im[d], log_step[d], b_re[d], b_im[d])
        ud = u if d == 0 else jnp.flip(u, axis=1)
        bu_re = jnp.einsum('gpc,blgc->blgp', bb_re, ud)
        bu_im = jnp.einsum('gpc,blgc->blgp', bb_im, ud)
        s_re, s_im = _complex_scan(ab_re, ab_im, bu_re, bu_im, s0[:, d, 0], s0[:, d, 1])
        yd = (jnp.einsum('gcp,blgp->blgc', c_re[d].astype(jnp.float32), s_re)
              - jnp.einsum('gcp,blgp->blgc', c_im[d].astype(jnp.float32), s_im))
        if d == 1:
            yd = jnp.flip(yd, axis=1)
        y = y + yd
        finals.append(jnp.stack([s_re[:, -1], s_im[:, -1]], axis=1))
    y = jax.nn.gelu(y.reshape(bsz, L, D_SSM))
    y = y * jax.nn.sigmoid(y @ w_glu.astype(jnp.float32) + b_glu.astype(jnp.float32))
    return y.astype(xa.dtype), jnp.stack(finals, axis=1)


def _sgu_branch(u, v, ln_g, ln_b, w_s, b_s):
    u = jax.nn.gelu(u)
    v = _layernorm(jax.nn.gelu(v)) * ln_g + ln_b
    b, L = v.shape[:2]
    nc = L // SGU_CHUNK
    vc = v.reshape(b, nc, SGU_CHUNK, N_SGU_GROUPS, SGU_GROUP_CH)
    vm = jnp.einsum('gpq,bnqgc->bnpgc', w_s, vc) + b_s.T[None, None, :, :, None]
    return u * vm.reshape(b, L, D_SGU)


def _trunk_layer(x, cond, lp, ctx_k, ctx_v, ssm_s0):
    is_context = ctx_k is None
    b, L = x.shape[:2]
    mod = jax.nn.silu(cond) @ lp['w_ada'] + lp['b_ada']
    shift, scale, gate = jnp.split(mod[:, None, :], 3, axis=-1)
    h = _layernorm(x) * (1.0 + scale) + shift
    proj = h @ lp['w_in']
    xa, za, q, k, v, zb, u, vs, zc, mg = jnp.split(proj, IN_SPLITS, axis=-1)
    if is_context:
        ssm_s0 = jnp.zeros((b, 2, 2, N_SSM_GROUPS, SSM_STATE), jnp.float32)
    ya, ssm_final = _ssm_branch(xa, ssm_s0, lp['lam_re'], lp['lam_im'], lp['log_step'],
                                lp['b_re'], lp['b_im'], lp['c_re'], lp['c_im'],
                                lp['d_skip'], lp['w_glu'], lp['b_glu'])
    ya = ya * jax.nn.silu(za)
    q = q.reshape(b, L, N_HEADS, HEAD_DIM)
    k = k.reshape(b, L, N_KV_HEADS, HEAD_DIM)
    v = v.reshape(b, L, N_KV_HEADS, HEAD_DIM)
    if is_context:
        yb = _context_attention(q, k, v, lp['sink'])
    else:
        rows = L // GRID_W
        yb = _latent_attention(_axial_rope(q, rows), _axial_rope(k, rows), v, ctx_k, ctx_v, lp['sink'])
    yb = yb * jax.nn.silu(zb)
    yc = _sgu_branch(u, vs, lp['sgu_g'], lp['sgu_b'], lp['w_s'], lp['b_s']) * jax.nn.silu(zc)
    g_a, g_b, g_c = jnp.split(jax.nn.sigmoid(mg), 3, axis=-1)
    merged = g_a * (ya @ lp['w_pa']) + g_b * (yb @ lp['w_pb']) + g_c * (yc @ lp['w_pc'])
    out = merged @ lp['w_out']
    y = _layernorm(DEEPNORM_ALPHA * x + gate * out) * lp['ln_g'] + lp['ln_b']
    return y, k, v, ssm_final


def setup_inputs(seed: int = 0) -> dict:
    key = jax.random.key(seed)
    ks = jax.random.split(key, 40)
    f32 = jnp.float32

    def nrm(k, shape, scale):
        return jax.random.normal(k, shape, f32) * scale

    G, P, C = N_SSM_GROUPS, SSM_STATE, SSM_GROUP
    n = jnp.arange(SSM_STATE, dtype=f32)
    return {
        'x_prompt': nrm(ks[0], (BATCH, SEQ, D_MODEL), 1.0),
        'x_sample': nrm(ks[1], (DEC_BATCH, DEC_SEQ, D_MODEL), 1.0),
        'cache_k': nrm(ks[2], (DEC_BATCH, DEPTH, PAST_LEN, N_KV_HEADS, HEAD_DIM), 1.0),
        'cache_v': nrm(ks[3], (DEC_BATCH, DEPTH, PAST_LEN, N_KV_HEADS, HEAD_DIM), 1.0),
        'state_ssm': nrm(ks[4], (DEC_BATCH, DEPTH, 2, 2, G, P), 0.1),
        'c': nrm(ks[5], (DEC_BATCH, D_MODEL), 1.0),
        'c_ctx': nrm(ks[6], (D_MODEL,), 1.0),
        'w_ada': nrm(ks[7], (DEPTH, D_MODEL, 3 * D_MODEL), 0.5 * D_MODEL ** -0.5),
        'b_ada': nrm(ks[8], (DEPTH, 3 * D_MODEL), 0.01),
        'w_in': nrm(ks[9], (DEPTH, D_MODEL, D_IN), D_MODEL ** -0.5),
        'ssm_lam_re': -0.5 + nrm(ks[10], (DEPTH, 2, G, P), 0.01),
        'ssm_lam_im': math.pi * n + nrm(ks[11], (DEPTH, 2, G, P), 0.01),
        'ssm_log_step': jax.random.uniform(ks[12], (DEPTH, 2, G), f32,
                                           minval=math.log(1e-3), maxval=math.log(1e-1)),
        'ssm_b_re': nrm(ks[13], (DEPTH, 2, G, P, C), (2.0 * C) ** -0.5),
        'ssm_b_im': nrm(ks[14], (DEPTH, 2, G, P, C), (2.0 * C) ** -0.5),
        'ssm_c_re': nrm(ks[15], (DEPTH, 2, G, C, P), (2.0 * P) ** -0.5),
        'ssm_c_im': nrm(ks[16], (DEPTH, 2, G, C, P), (2.0 * P) ** -0.5),
        'ssm_d': nrm(ks[17], (DEPTH, D_SSM), 1.0),
        'w_glu': nrm(ks[18], (DEPTH, D_SSM, D_SSM), D_SSM ** -0.5),
        'b_glu': nrm(ks[19], (DEPTH, D_SSM), 0.01),
        'attn_sink': nrm(ks[20], (DEPTH, N_HEADS), 0.5),
        'sgu_ln_g': 1.0 + nrm(ks[21], (DEPTH, D_SGU), 0.01),
        'sgu_ln_b': nrm(ks[22], (DEPTH, D_SGU), 0.01),
        'w_spatial': nrm(ks[23], (DEPTH, N_SGU_GROUPS, SGU_CHUNK, SGU_CHUNK), 0.5 * SGU_CHUNK ** -0.5),
        'b_spatial': 1.0 + nrm(ks[24], (DEPTH, N_SGU_GROUPS, SGU_CHUNK), 0.01),
        'w_proj_a': nrm(ks[25], (DEPTH, D_SSM, D_MODEL), DEEPNORM_BETA * D_SSM ** -0.5),
        'w_proj_b': nrm(ks[26], (DEPTH, D_ATTN, D_MODEL), DEEPNORM_BETA * D_ATTN ** -0.5),
        'w_proj_c': nrm(ks[27], (DEPTH, D_SGU, D_MODEL), DEEPNORM_BETA * D_SGU ** -0.5),
        'w_out': nrm(ks[28], (DEPTH, D_MODEL, D_MODEL), DEEPNORM_BETA * D_MODEL ** -0.5),
        'ln_g': 1.0 + nrm(ks[29], (DEPTH, D_MODEL), 0.01),
        'ln_b': nrm(ks[30], (DEPTH, D_MODEL), 0.01),
    }


def reference(x_prompt, x_sample, cache_k, cache_v, state_ssm, c, c_ctx,
              w_ada, b_ada, w_in, ssm_lam_re, ssm_lam_im, ssm_log_step,
              ssm_b_re, ssm_b_im, ssm_c_re, ssm_c_im, ssm_d, w_glu, b_glu,
              attn_sink, sgu_ln_g, sgu_ln_b, w_spatial, b_spatial,
              w_proj_a, w_proj_b, w_proj_c, w_out, ln_g, ln_b):
    def layer_params(l):
        return {
            'w_ada': w_ada[l], 'b_ada': b_ada[l], 'w_in': w_in[l],
            'lam_re': ssm_lam_re[l], 'lam_im': ssm_lam_im[l], 'log_step': ssm_log_step[l],
            'b_re': ssm_b_re[l], 'b_im': ssm_b_im[l], 'c_re': ssm_c_re[l], 'c_im': ssm_c_im[l],
            'd_skip': ssm_d[l], 'w_glu': w_glu[l], 'b_glu': b_glu[l],
            'sink': attn_sink[l], 'sgu_g': sgu_ln_g[l], 'sgu_b': sgu_ln_b[l],
            'w_s': w_spatial[l], 'b_s': b_spatial[l],
            'w_pa': w_proj_a[l], 'w_pb': w_proj_b[l], 'w_pc': w_proj_c[l],
            'w_out': w_out[l], 'ln_g': ln_g[l], 'ln_b': ln_b[l],
        }

    h = x_prompt
    ks_list, vs_list, ss_list = [], [], []
    for l in range(DEPTH):
        h, k_l, v_l, s_l = _trunk_layer(h, c_ctx[None, :], layer_params(l), None, None, None)
        ks_list.append(k_l)
        vs_list.append(v_l)
        ss_list.append(s_l)
    y_prompt = h
    new_cache_k = jnp.stack(ks_list, axis=1)
    new_cache_v = jnp.stack(vs_list, axis=1)
    new_state_ssm = jnp.stack(ss_list, axis=1)

    z = x_sample
    for l in range(DEPTH):
        z, _, _, _ = _trunk_layer(z, c, layer_params(l), cache_k[:, l], cache_v[:, l], state_ssm[:, l])
    y_sample = z
    return (y_prompt, y_sample, new_cache_k, new_cache_v, new_state_ssm)
```

```python
import functools
import math

import jax
import jax.numpy as jnp
from jax import lax
from jax.experimental import pallas as pl
from jax.experimental.pallas import tpu as pltpu

F32 = jnp.float32
BF16 = jnp.bfloat16

D_MODEL = 2048
GRID_W = 64
D_SSM = 512
SSM_GROUP = 16
N_SSM_GROUPS = D_SSM // SSM_GROUP
SSM_STATE = 64
HEAD_DIM = 128
N_HEADS = 8
N_KV_HEADS = 2
Q_PER_KV = N_HEADS // N_KV_HEADS
D_ATTN = N_HEADS * HEAD_DIM
D_KV = N_KV_HEADS * HEAD_DIM
WINDOW = 128
ATTN_BLOCK = 128
ROPE_BASE = 10000.0
D_SGU = 512
SGU_CHUNK = 128
SGU_GROUP_CH = 128
N_SGU_GROUPS = D_SGU // SGU_GROUP_CH
N_BRANCH = 3
LN_EPS = 1e-5

D_MG = N_BRANCH * D_MODEL
COL_MG = 0
COL_XA = COL_MG + D_MG
COL_ZA = COL_XA + D_SSM
COL_Q = COL_ZA + D_SSM
COL_ZB = COL_Q + D_ATTN
COL_K = COL_ZB + D_ATTN
COL_V = COL_K + D_KV
COL_U = COL_V + D_KV
COL_VS = COL_U + D_SGU
COL_ZC = COL_VS + D_SGU
D_IN = COL_ZC + D_SGU

SSM_T = 16
SSM_NC = 16
SSM_TC = SSM_T * SSM_GROUP
SSM_SEG = SSM_T * SSM_NC
SSM_SMALL_ROWS = 40

VMEM_LIMIT_BYTES = 56 * 1024 * 1024

INPROJ_TM = 512
INPROJ_TN = 512
ROW_TM = 512
MERGE_TM = 256


def _params(sem):
    return pltpu.CompilerParams(dimension_semantics=sem, vmem_limit_bytes=VMEM_LIMIT_BYTES)


def _layernorm_f32(x):
    mu = jnp.mean(x, axis=-1, keepdims=True)
    xc = x - mu
    var = jnp.mean(xc * xc, axis=-1, keepdims=True)
    return xc * lax.rsqrt(var + LN_EPS)


def _ada_kernel(cond_ref, w_ref, b_ref, o_ref):
    c = cond_ref[...]
    a = (c * jax.nn.sigmoid(c)).astype(BF16)
    o_ref[...] = jnp.dot(a, w_ref[...].astype(BF16), preferred_element_type=F32) + b_ref[...]


def _ada_mod(cond, w_ada, b_ada):
    depth, d, n = w_ada.shape
    r = cond.shape[0]
    tn = 512
    return pl.pallas_call(
        _ada_kernel,
        out_shape=jax.ShapeDtypeStruct((depth, r, n), F32),
        grid=(depth, n // tn),
        in_specs=[
            pl.BlockSpec((r, d), lambda l, j: (0, 0)),
            pl.BlockSpec((None, d, tn), lambda l, j: (l, 0, j)),
            pl.BlockSpec((None, 1, tn), lambda l, j: (l, 0, j)),
        ],
        out_specs=pl.BlockSpec((None, r, tn), lambda l, j: (l, 0, j)),
        compiler_params=_params(("arbitrary", "arbitrary")),
        name="ada_mod",
    )(cond, w_ada, b_ada.reshape(depth, 1, n))


def _inproj_kernel(x_ref, scale_ref, shift_ref, w_ref, proj_ref, kv_ref, h_scr, *, kv_tile):
    j = pl.program_id(1)

    @pl.when(j == 0)
    def _():
        h = _layernorm_f32(x_ref[...]) * (1.0 + scale_ref[0]) + shift_ref[0]
        h_scr[...] = h.astype(BF16)

    acc = jnp.dot(h_scr[...], w_ref[...], preferred_element_type=F32)
    proj_ref[...] = acc.astype(BF16)

    @pl.when(j == kv_tile)
    def _():
        kv_ref[...] = acc


def _inproj(x2, scale, shift, w_bf16, seq_len):
    t, d = x2.shape
    n = w_bf16.shape[1]
    tm, tn = INPROJ_TM, INPROJ_TN
    assert t % tm == 0 and n % tn == 0 and COL_K % tn == 0 and 2 * D_KV == tn
    nb = scale.shape[0]
    if nb == 1:
        mod_map = lambda i, j: (0, 0, 0)
    else:
        assert seq_len % tm == 0
        tiles_per_seq = seq_len // tm
        mod_map = lambda i, j: (i // tiles_per_seq, 0, 0)
    return pl.pallas_call(
        functools.partial(_inproj_kernel, kv_tile=COL_K // tn),
        out_shape=(jax.ShapeDtypeStruct((t, n), BF16), jax.ShapeDtypeStruct((t, 2 * D_KV), F32)),
        grid=(t // tm, n // tn),
        in_specs=[
            pl.BlockSpec((tm, d), lambda i, j: (i, 0)),
            pl.BlockSpec((1, 1, d), mod_map),
            pl.BlockSpec((1, 1, d), mod_map),
            pl.BlockSpec((d, tn), lambda i, j: (0, j)),
        ],
        out_specs=(
            pl.BlockSpec((tm, tn), lambda i, j: (i, j)),
            pl.BlockSpec((tm, 2 * D_KV), lambda i, j: (i, 0)),
        ),
        scratch_shapes=[pltpu.VMEM((tm, d), BF16)],
        compiler_params=_params(("arbitrary", "arbitrary")),
        name="ln_inproj",
    )(x2, scale, shift, w_bf16)


def _ssm_tables(lam_re, lam_im, log_step, b_re, b_im, c_re, c_im):
    hp = lax.Precision.HIGHEST
    g_n, p_n, c_n, t_n = N_SSM_GROUPS, SSM_STATE, SSM_GROUP, SSM_T
    lr = lam_re.astype(F32)
    li = lam_im.astype(F32)
    dt = jnp.exp(log_step.astype(F32))[..., None]
    mag = jnp.exp(lr * dt)
    ar = mag * jnp.cos(li * dt)
    ai = mag * jnp.sin(li * dt)
    den = lr * lr + li * li
    nr = ar - 1.0
    f_re = (nr * lr + ai * li) / den
    f_im = (ai * lr - nr * li) / den
    br = b_re.astype(F32)
    bi = b_im.astype(F32)
    bb_re = f_re[..., None] * br - f_im[..., None] * bi
    bb_im = f_re[..., None] * bi + f_im[..., None] * br

    def cmul(xr, xi, yr, yi):
        return xr * yr - xi * yi, xr * yi + xi * yr

    pr, pi = [jnp.ones_like(ar)], [jnp.zeros_like(ai)]
    for _ in range(t_n):
        nr_, ni_ = cmul(pr[-1], pi[-1], ar, ai)
        pr.append(nr_)
        pi.append(ni_)
    pw_re = jnp.stack(pr, axis=0)
    pw_im = jnp.stack(pi, axis=0)
    cr = c_re.astype(F32)
    ci = c_im.astype(F32)
    cp_re = cr[None] * pw_re[:, :, :, None, :] - ci[None] * pw_im[:, :, :, None, :]
    cp_im = cr[None] * pw_im[:, :, :, None, :] + ci[None] * pw_re[:, :, :, None, :]
    taps = (jnp.einsum('kdgcp,dgpe->dgkce', cp_re[:t_n], bb_re, precision=hp)
            - jnp.einsum('kdgcp,dgpe->dgkce', cp_im[:t_n], bb_im, precision=hp))
    jj = jnp.arange(t_n)[:, None]
    tt = jnp.arange(t_n)[None, :]
    kf = jnp.where((tt >= jj)[None, :, :, None, None], taps[0][:, jnp.clip(tt - jj, 0, t_n - 1)], 0.0)
    kb = jnp.where((jj >= tt)[None, :, :, None, None], taps[1][:, jnp.clip(jj - tt, 0, t_n - 1)], 0.0)
    w_intra = (kf + kb).transpose(0, 1, 4, 2, 3).reshape(g_n, t_n * c_n, t_n * c_n)

    idx_f = jnp.arange(t_n - 1, -1, -1)
    idx_b = jnp.arange(t_n)
    sf_re, sf_im = cmul(pw_re[idx_f, 0][..., None], pw_im[idx_f, 0][..., None], bb_re[0][None], bb_im[0][None])
    sb_re, sb_im = cmul(pw_re[idx_b, 1][..., None], pw_im[idx_b, 1][..., None], bb_re[1][None], bb_im[1][None])
    w_state = jnp.stack([sf_re, sb_re, sf_im, sb_im], axis=0)
    w_state = w_state.transpose(2, 1, 4, 0, 3).reshape(g_n, t_n * c_n, 4 * p_n)

    tf = jnp.arange(1, t_n + 1)
    tb = jnp.arange(t_n, 0, -1)
    w_enter = jnp.stack([cp_re[tf, 0], cp_re[tb, 1], -cp_im[tf, 0], -cp_im[tb, 1]], axis=0)
    w_enter = w_enter.transpose(2, 0, 4, 1, 3).reshape(g_n, 4 * p_n, t_n * c_n)

    a1r = jnp.concatenate([pw_re[t_n, 0], pw_re[t_n, 1]], axis=-1)
    a1i = jnp.concatenate([pw_im[t_n, 0], pw_im[t_n, 1]], axis=-1)
    qr, qi = [jnp.ones_like(a1r)], [jnp.zeros_like(a1i)]
    for _ in range(SSM_NC):
        nr_, ni_ = cmul(qr[-1], qi[-1], a1r, a1i)
        qr.append(nr_)
        qi.append(ni_)
    qr = jnp.stack(qr, axis=1)
    qi = jnp.stack(qi, axis=1)
    rev = jnp.arange(SSM_NC - 1, -1, -1)
    corr_re = jnp.concatenate([qr[:, :SSM_NC, :p_n], qr[:, rev, p_n:]], axis=-1)
    corr_im = jnp.concatenate([qi[:, :SSM_NC, :p_n], qi[:, rev, p_n:]], axis=-1)
    small = jnp.concatenate([
        corr_re, corr_im, a1r[:, None], a1i[:, None], qr[:, SSM_NC][:, None], qi[:, SSM_NC][:, None],
        jnp.zeros((g_n, SSM_SMALL_ROWS - 2 * SSM_NC - 4, 2 * p_n), F32)], axis=1)
    return w_intra.astype(BF16), w_state.astype(BF16), w_enter.astype(BF16), small


def _ssm_kernel(u_ref, wi_ref, ws_ref, we_ref, small_ref, s0_ref, y_ref, fin_ref,
                v_scr, s_scr, e_scr, *, n_seq, n_seg):
    p2 = 2 * SSM_STATE
    nbp = n_seq * n_seg
    u = u_ref[...]
    y_ref[...] = jnp.dot(u, wi_ref[...], preferred_element_type=F32)
    v_scr[...] = jnp.dot(u, ws_ref[...], preferred_element_type=F32)
    lane = lax.broadcasted_iota(jnp.int32, (1, p2), 1)
    is_fwd = lane < SSM_STATE
    corr_re = small_ref[0:SSM_NC, :]
    corr_im = small_ref[SSM_NC:2 * SSM_NC, :]
    a1r = small_ref[2 * SSM_NC:2 * SSM_NC + 1, :]
    a1i = small_ref[2 * SSM_NC + 1:2 * SSM_NC + 2, :]
    asr = small_ref[2 * SSM_NC + 2:2 * SSM_NC + 3, :]
    asi = small_ref[2 * SSM_NC + 3:2 * SSM_NC + 4, :]

    st_re = jnp.zeros((nbp, p2), F32)
    st_im = jnp.zeros((nbp, p2), F32)
    for c in range(SSM_NC):
        cb = SSM_NC - 1 - c
        rf = slice(c * nbp, (c + 1) * nbp)
        rb = slice(cb * nbp, (cb + 1) * nbp)
        s_scr[rf, 0:SSM_STATE] = st_re[:, 0:SSM_STATE]
        s_scr[rb, SSM_STATE:p2] = st_re[:, SSM_STATE:p2]
        s_scr[rf, p2:p2 + SSM_STATE] = st_im[:, 0:SSM_STATE]
        s_scr[rb, p2 + SSM_STATE:2 * p2] = st_im[:, SSM_STATE:p2]
        v_re = jnp.where(is_fwd, v_scr[rf, 0:p2], v_scr[rb, 0:p2])
        v_im = jnp.where(is_fwd, v_scr[rf, p2:2 * p2], v_scr[rb, p2:2 * p2])
        st_re, st_im = (a1r * st_re - a1i * st_im + v_re, a1r * st_im + a1i * st_re + v_im)

    s0 = s0_ref[...]
    e_re = s0[:, 0:p2]
    e_im = s0[:, p2:2 * p2]
    for k in range(n_seg):
        kb = n_seg - 1 - k
        rf = slice(k * n_seq, (k + 1) * n_seq)
        rb = slice(kb * n_seq, (kb + 1) * n_seq)
        e_scr[rf, 0:SSM_STATE] = e_re[:, 0:SSM_STATE]
        e_scr[rb, SSM_STATE:p2] = e_re[:, SSM_STATE:p2]
        e_scr[rf, p2:p2 + SSM_STATE] = e_im[:, 0:SSM_STATE]
        e_scr[rb, p2 + SSM_STATE:2 * p2] = e_im[:, SSM_STATE:p2]
        f_re = jnp.where(is_fwd, st_re[rf, :], st_re[rb, :])
        f_im = jnp.where(is_fwd, st_im[rf, :], st_im[rb, :])
        e_re, e_im = (asr * e_re - asi * e_im + f_re, asr * e_im + asi * e_re + f_im)
    fin_ref[...] = jnp.concatenate([e_re, e_im], axis=-1)

    en_re = e_scr[:, 0:p2]
    en_im = e_scr[:, p2:2 * p2]
    for c in range(SSM_NC):
        rows = slice(c * nbp, (c + 1) * nbp)
        cr = corr_re[c:c + 1, :]
        ci = corr_im[c:c + 1, :]
        s_scr[rows, 0:p2] = s_scr[rows, 0:p2] + (cr * en_re - ci * en_im)
        s_scr[rows, p2:2 * p2] = s_scr[rows, p2:2 * p2] + (cr * en_im + ci * en_re)

    y_ref[...] += jnp.dot(s_scr[...].astype(BF16), we_ref[...], preferred_element_type=F32)


def _ssm_scan(u, w_intra, w_state, w_enter, small, s0, n_seq, n_seg):
    g_n, r, tc = u.shape
    p4 = 4 * SSM_STATE
    nbp = n_seq * n_seg
    assert r == SSM_NC * nbp
    return pl.pallas_call(
        functools.partial(_ssm_kernel, n_seq=n_seq, n_seg=n_seg),
        out_shape=(jax.ShapeDtypeStruct((g_n, r, tc), F32), jax.ShapeDtypeStruct((g_n, n_seq, p4), F32)),
        grid=(g_n,),
        in_specs=[
            pl.BlockSpec((None, r, tc), lambda g: (g, 0, 0)),
            pl.BlockSpec((None, tc, tc), lambda g: (g, 0, 0)),
            pl.BlockSpec((None, tc, p4), lambda g: (g, 0, 0)),
            pl.BlockSpec((None, p4, tc), lambda g: (g, 0, 0)),
            pl.BlockSpec((None, SSM_SMALL_ROWS, 2 * SSM_STATE), lambda g: (g, 0, 0)),
            pl.BlockSpec((None, n_seq, p4), lambda g: (g, 0, 0)),
        ],
        out_specs=(
            pl.BlockSpec((None, r, tc), lambda g: (g, 0, 0)),
            pl.BlockSpec((None, n_seq, p4), lambda g: (g, 0, 0)),
        ),
        scratch_shapes=[pltpu.VMEM((r, p4), F32), pltpu.VMEM((r, p4), F32), pltpu.VMEM((nbp, p4), F32)],
        compiler_params=_params(("arbitrary",)),
        name="ssm_scan",
    )(u, w_intra, w_state, w_enter, small, s0)


def _ssm_post_kernel(xa_ref, za_ref, ys_ref, d_ref, w_ref, b_ref, o_ref):
    y = xa_ref[...].astype(F32) * d_ref[...] + ys_ref[...]
    y = jax.nn.gelu(y)
    gl = jnp.dot(y.astype(BF16), w_ref[...], preferred_element_type=F32) + b_ref[...]
    y = y * jax.nn.sigmoid(gl)
    o_ref[...] = (y * jax.nn.silu(za_ref[...].astype(F32))).astype(BF16)


def _ssm_post(proj, yssm, d_skip, w_glu_bf16, b_glu):
    t = proj.shape[0]
    tm = ROW_TM
    return pl.pallas_call(
        _ssm_post_kernel,
        out_shape=jax.ShapeDtypeStruct((t, D_SSM), BF16),
        grid=(t // tm,),
        in_specs=[
            pl.BlockSpec((tm, D_SSM), lambda i: (i, COL_XA // D_SSM)),
            pl.BlockSpec((tm, D_SSM), lambda i: (i, COL_ZA // D_SSM)),
            pl.BlockSpec((tm, D_SSM), lambda i: (i, 0)),
            pl.BlockSpec((1, D_SSM), lambda i: (0, 0)),
            pl.BlockSpec((D_SSM, D_SSM), lambda i: (0, 0)),
            pl.BlockSpec((1, D_SSM), lambda i: (0, 0)),
        ],
        out_specs=pl.BlockSpec((tm, D_SSM), lambda i: (i, 0)),
        compiler_params=_params(("arbitrary",)),
        name="ssm_post",
    )(proj, proj, yssm, d_skip.reshape(1, D_SSM), w_glu_bf16, b_glu.reshape(1, D_SSM))


def _softmax_pv(s, sink, v_bf16):
    m = jnp.maximum(jnp.max(s, axis=-1, keepdims=True), sink)
    p = jnp.exp(s - m)
    den = jnp.sum(p, axis=-1, keepdims=True) + jnp.exp(sink - m)
    o = jnp.dot(p.astype(BF16), v_bf16, preferred_element_type=F32)
    return o / den


def _ctx_attn_kernel(sink_ref, q_ref, kv_ref, zb_ref, o_ref):
    scale = HEAD_DIM ** -0.5
    nt = (((1,), (1,)), ((), ()))
    for h in range(N_HEADS):
        kvh = h // Q_PER_KV
        q = q_ref[:, h * HEAD_DIM:(h + 1) * HEAD_DIM]
        k = kv_ref[:, kvh * HEAD_DIM:(kvh + 1) * HEAD_DIM].astype(BF16)
        v = kv_ref[:, D_KV + kvh * HEAD_DIM:D_KV + (kvh + 1) * HEAD_DIM].astype(BF16)
        s = lax.dot_general(q, k, nt, preferred_element_type=F32) * scale
        o = _softmax_pv(s, sink_ref[h], v)
        zb = zb_ref[:, h * HEAD_DIM:(h + 1) * HEAD_DIM].astype(F32)
        o_ref[:, h * HEAD_DIM:(h + 1) * HEAD_DIM] = (o * jax.nn.silu(zb)).astype(BF16)


def _ctx_attention(proj, kv, sink, seq_len):
    t = proj.shape[0]
    return pl.pallas_call(
        _ctx_attn_kernel,
        out_shape=jax.ShapeDtypeStruct((t, D_ATTN), BF16),
        grid=(t // seq_len,),
        in_specs=[
            pl.BlockSpec(memory_space=pltpu.SMEM),
            pl.BlockSpec((seq_len, D_ATTN), lambda b: (b, COL_Q // D_ATTN)),
            pl.BlockSpec((seq_len, 2 * D_KV), lambda b: (b, 0)),
            pl.BlockSpec((seq_len, D_ATTN), lambda b: (b, COL_ZB // D_ATTN)),
        ],
        out_specs=pl.BlockSpec((seq_len, D_ATTN), lambda b: (b, 0)),
        compiler_params=_params(("arbitrary",)),
        name="ctx_attention",
    )(sink, proj, kv, proj)


def _rope_partner(x):
    lane = lax.broadcasted_iota(jnp.int32, x.shape, 1)
    quarter = HEAD_DIM // 4
    first = (lane % (2 * quarter)) < quarter
    return jnp.where(first, pltpu.roll(x, HEAD_DIM - quarter, 1), pltpu.roll(x, quarter, 1))


def _lat_attn_kernel(sink_ref, q_ref, kvp_ref, kvc_ref, kvn_ref, ck_ref, cv_ref, zb_ref,
                     cosq_ref, sinq_ref, cosp_ref, sinp_ref, cosn_ref, sinn_ref, o_ref, *, n_blocks):
    i = pl.program_id(1)
    scale = HEAD_DIM ** -0.5
    nt = (((1,), (1,)), ((), ()))
    blk = ATTN_BLOCK
    cos_q = cosq_ref[...]
    sin_q = sinq_ref[...]
    r = lax.broadcasted_iota(jnp.int32, (Q_PER_KV * blk, 3 * blk), 0) % blk
    s_idx = lax.broadcasted_iota(jnp.int32, (Q_PER_KV * blk, 3 * blk), 1)
    rel = s_idx - blk - r
    ok = (jnp.abs(rel) <= WINDOW)
    ok = ok & ((s_idx >= blk) | (i > 0)) & ((s_idx < 2 * blk) | (i < n_blocks - 1))
    for kvh in range(N_KV_HEADS):
        ks = slice(kvh * HEAD_DIM, (kvh + 1) * HEAD_DIM)
        vs = slice(D_KV + kvh * HEAD_DIM, D_KV + (kvh + 1) * HEAD_DIM)

        def rope_k(ref, cos_ref, sin_ref):
            x = ref[:, ks]
            return (x * cos_ref[...] + _rope_partner(x) * sin_ref[...]).astype(BF16)

        k_loc = jnp.concatenate([rope_k(kvp_ref, cosp_ref, sinp_ref), rope_k(kvc_ref, cosq_ref, sinq_ref),
                                 rope_k(kvn_ref, cosn_ref, sinn_ref)], axis=0)
        v_all = jnp.concatenate([kvp_ref[:, vs].astype(BF16), kvc_ref[:, vs].astype(BF16),
                                 kvn_ref[:, vs].astype(BF16), cv_ref[:, ks].astype(BF16)], axis=0)
        k_ctx = ck_ref[:, ks].astype(BF16)
        qs = []
        for g in range(Q_PER_KV):
            h = kvh * Q_PER_KV + g
            x = q_ref[:, h * HEAD_DIM:(h + 1) * HEAD_DIM].astype(F32)
            qs.append((x * cos_q + _rope_partner(x) * sin_q).astype(BF16))
        q4 = jnp.concatenate(qs, axis=0)
        s_loc = lax.dot_general(q4, k_loc, nt, preferred_element_type=F32) * scale
        s_loc = jnp.where(ok, s_loc, -jnp.inf)
        s_ctx = lax.dot_general(q4, k_ctx, nt, preferred_element_type=F32) * scale
        s = jnp.concatenate([s_loc, s_ctx], axis=-1)
        sink_rows = jnp.concatenate(
            [jnp.full((blk, 1), sink_ref[kvh * Q_PER_KV + g], F32) for g in range(Q_PER_KV)], axis=0)
        o = _softmax_pv(s, sink_rows, v_all)
        for g in range(Q_PER_KV):
            h = kvh * Q_PER_KV + g
            zb = zb_ref[:, h * HEAD_DIM:(h + 1) * HEAD_DIM].astype(F32)
            o_ref[:, h * HEAD_DIM:(h + 1) * HEAD_DIM] = (o[g * blk:(g + 1) * blk] * jax.nn.silu(zb)).astype(BF16)


def _lat_attention(proj, kv, ctx_k, ctx_v, sink, cos_t, sin_t, n_seq, seq_len):
    t = proj.shape[0]
    blk = ATTN_BLOCK
    nb = seq_len // blk
    lc = ctx_k.shape[1]

    def row(b, i):
        return b * nb + i

    def prev(b, i):
        return b * nb + jnp.maximum(i - 1, 0)

    def nxt(b, i):
        return b * nb + jnp.minimum(i + 1, nb - 1)

    return pl.pallas_call(
        functools.partial(_lat_attn_kernel, n_blocks=nb),
        out_shape=jax.ShapeDtypeStruct((t, D_ATTN), BF16),
        grid=(n_seq, nb),
        in_specs=[
            pl.BlockSpec(memory_space=pltpu.SMEM),
            pl.BlockSpec((blk, D_ATTN), lambda b, i: (row(b, i), COL_Q // D_ATTN)),
            pl.BlockSpec((blk, 2 * D_KV), lambda b, i: (prev(b, i), 0)),
            pl.BlockSpec((blk, 2 * D_KV), lambda b, i: (row(b, i), 0)),
            pl.BlockSpec((blk, 2 * D_KV), lambda b, i: (nxt(b, i), 0)),
            pl.BlockSpec((None, lc, D_KV), lambda b, i: (b, 0, 0)),
            pl.BlockSpec((None, lc, D_KV), lambda b, i: (b, 0, 0)),
            pl.BlockSpec((blk, D_ATTN), lambda b, i: (row(b, i), COL_ZB // D_ATTN)),
            pl.BlockSpec((blk, HEAD_DIM), lambda b, i: (i, 0)),
            pl.BlockSpec((blk, HEAD_DIM), lambda b, i: (i, 0)),
            pl.BlockSpec((blk, HEAD_DIM), lambda b, i: (jnp.maximum(i - 1, 0), 0)),
            pl.BlockSpec((blk, HEAD_DIM), lambda b, i: (jnp.maximum(i - 1, 0), 0)),
            pl.BlockSpec((blk, HEAD_DIM), lambda b, i: (jnp.minimum(i + 1, nb - 1), 0)),
            pl.BlockSpec((blk, HEAD_DIM), lambda b, i: (jnp.minimum(i + 1, nb - 1), 0)),
        ],
        out_specs=pl.BlockSpec((blk, D_ATTN), lambda b, i: (row(b, i), 0)),
        compiler_params=_params(("arbitrary", "arbitrary")),
        name="lat_attention",
    )(sink, proj, kv, kv, kv, ctx_k, ctx_v, proj, cos_t, sin_t, cos_t, sin_t, cos_t, sin_t)


def _rope_tables(seq_len):
    rows = seq_len // GRID_W
    row = jnp.repeat(jnp.arange(rows), GRID_W).astype(F32)
    col = jnp.tile(jnp.arange(GRID_W), rows).astype(F32)
    half = HEAD_DIM // 2
    inv = ROPE_BASE ** (-jnp.arange(0, half, 2, dtype=F32) / half)
    ang_r = row[:, None] * inv[None, :]
    ang_c = col[:, None] * inv[None, :]
    cos_t = jnp.concatenate([jnp.cos(ang_r)] * 2 + [jnp.cos(ang_c)] * 2, axis=-1)
    sin_t = jnp.concatenate([-jnp.sin(ang_r), jnp.sin(ang_r), -jnp.sin(ang_c), jnp.sin(ang_c)], axis=-1)
    return cos_t, sin_t


def _sgu_kernel(u_ref, v_ref, z_ref, g_ref, b_ref, ws_ref, bs_ref, o_ref, *, n_chunks):
    for n in range(n_chunks):
        rows = slice(n * SGU_CHUNK, (n + 1) * SGU_CHUNK)
        v = _layernorm_f32(jax.nn.gelu(v_ref[rows, :].astype(F32))) * g_ref[...] + b_ref[...]
        v = v.astype(BF16)
        for g in range(N_SGU_GROUPS):
            cols = slice(g * SGU_GROUP_CH, (g + 1) * SGU_GROUP_CH)
            vm = jnp.dot(ws_ref[g], v[:, cols], preferred_element_type=F32) + bs_ref[:, g:g + 1]
            u = jax.nn.gelu(u_ref[rows, cols].astype(F32))
            z = z_ref[rows, cols].astype(F32)
            o_ref[rows, cols] = (u * vm * jax.nn.silu(z)).astype(BF16)


def _sgu(proj, ln_g, ln_b, w_s_bf16, b_s_t):
    t = proj.shape[0]
    tm = ROW_TM
    return pl.pallas_call(
        functools.partial(_sgu_kernel, n_chunks=tm // SGU_CHUNK),
        out_shape=jax.ShapeDtypeStruct((t, D_SGU), BF16),
        grid=(t // tm,),
        in_specs=[
            pl.BlockSpec((tm, D_SGU), lambda i: (i, COL_U // D_SGU)),
            pl.BlockSpec((tm, D_SGU), lambda i: (i, COL_VS // D_SGU)),
            pl.BlockSpec((tm, D_SGU), lambda i: (i, COL_ZC // D_SGU)),
            pl.BlockSpec((1, D_SGU), lambda i: (0, 0)),
            pl.BlockSpec((1, D_SGU), lambda i: (0, 0)),
            pl.BlockSpec((N_SGU_GROUPS, SGU_CHUNK, SGU_CHUNK), lambda i: (0, 0, 0)),
            pl.BlockSpec((SGU_CHUNK, N_SGU_GROUPS), lambda i: (0, 0)),
        ],
        out_specs=pl.BlockSpec((tm, D_SGU), lambda i: (i, 0)),
        compiler_params=_params(("arbitrary",)),
        name="sgu",
    )(proj, proj, proj, ln_g.reshape(1, D_SGU), ln_b.reshape(1, D_SGU), w_s_bf16, b_s_t)


def _merge_kernel(mg_ref, ya_ref, yb_ref, yc_ref, x_ref, gate_ref, wpa_ref, wpb_ref, wpc_ref, wout_ref,
                  lng_ref, lnb_ref, y_ref, *, alpha):
    d = D_MODEL

    def branch(k, y_r, w_r):
        gate = jax.nn.sigmoid(mg_ref[:, k * d:(k + 1) * d].astype(F32))
        return gate * jnp.dot(y_r[...], w_r[...], preferred_element_type=F32)

    merged = branch(0, ya_ref, wpa_ref) + branch(1, yb_ref, wpb_ref) + branch(2, yc_ref, wpc_ref)
    out = jnp.dot(merged.astype(BF16), wout_ref[...], preferred_element_type=F32)
    z = alpha * x_ref[...] + gate_ref[0] * out
    y_ref[...] = _layernorm_f32(z) * lng_ref[...] + lnb_ref[...]


def _merge(proj, ya, yb, yc, x2, gate, wpa, wpb, wpc, wout, ln_g, ln_b, seq_len, alpha):
    t, d = x2.shape
    tm = MERGE_TM
    if gate.shape[0] == 1:
        mod_map = lambda i: (0, 0, 0)
    else:
        assert seq_len % tm == 0
        tiles_per_seq = seq_len // tm
        mod_map = lambda i: (i // tiles_per_seq, 0, 0)

    def const(shape):
        return pl.BlockSpec(shape, lambda i: (0,) * len(shape), pipeline_mode=pl.Buffered(1))

    return pl.pallas_call(
        functools.partial(_merge_kernel, alpha=alpha),
        out_shape=jax.ShapeDtypeStruct((t, d), F32),
        grid=(t // tm,),
        in_specs=[
            pl.BlockSpec((tm, D_MG), lambda i: (i, 0)),
            pl.BlockSpec((tm, D_SSM), lambda i: (i, 0)),
            pl.BlockSpec((tm, D_ATTN), lambda i: (i, 0)),
            pl.BlockSpec((tm, D_SGU), lambda i: (i, 0)),
            pl.BlockSpec((tm, d), lambda i: (i, 0)),
            pl.BlockSpec((1, 1, d), mod_map),
            const((D_SSM, d)), const((D_ATTN, d)), const((D_SGU, d)), const((d, d)),
            const((1, d)), const((1, d)),
        ],
        out_specs=pl.BlockSpec((tm, d), lambda i: (i, 0)),
        compiler_params=_params(("arbitrary",)),
        name="merge_out",
    )(proj, ya, yb, yc, x2, gate, wpa, wpb, wpc, wout, ln_g.reshape(1, d), ln_b.reshape(1, d))


def _to_chunk_rows(xa, n_seq, n_seg):
    x = xa.reshape(n_seq, n_seg, SSM_NC, SSM_T, N_SSM_GROUPS, SSM_GROUP)
    x = x.transpose(4, 2, 1, 0, 3, 5)
    return x.reshape(N_SSM_GROUPS, SSM_NC * n_seg * n_seq, SSM_TC)


def _from_chunk_rows(y, n_seq, n_seg):
    y = y.reshape(N_SSM_GROUPS, SSM_NC, n_seg, n_seq, SSM_T, SSM_GROUP)
    y = y.transpose(3, 2, 1, 4, 0, 5)
    return y.reshape(n_seq * n_seg * SSM_SEG, D_SSM)


def _layer(x2, n_seq, seq_len, mod, lp, ctx_k, ctx_v, s0, rope, alpha):
    d = D_MODEL
    nbm = mod.shape[0]
    shift = mod[:, 0:d].reshape(nbm, 1, d)
    scale = mod[:, d:2 * d].reshape(nbm, 1, d)
    gate = mod[:, 2 * d:3 * d].reshape(nbm, 1, d)
    proj, kv = _inproj(x2, scale, shift, lp['w_in'], seq_len)

    n_seg = seq_len // SSM_SEG
    u = _to_chunk_rows(proj[:, COL_XA:COL_XA + D_SSM], n_seq, n_seg)
    y_chunks, fin = _ssm_scan(u, lp['w_intra'], lp['w_state'], lp['w_enter'], lp['ssm_small'], s0, n_seq, n_seg)
    yssm = _from_chunk_rows(y_chunks, n_seq, n_seg)
    ya = _ssm_post(proj, yssm, lp['d_skip'], lp['w_glu'], lp['b_glu'])

    if ctx_k is None:
        yb = _ctx_attention(proj, kv, lp['sink'], seq_len)
    else:
        yb = _lat_attention(proj, kv, ctx_k, ctx_v, lp['sink'], rope[0], rope[1], n_seq, seq_len)

    yc = _sgu(proj, lp['sgu_g'], lp['sgu_b'], lp['w_s'], lp['b_s_t'])

    y = _merge(proj, ya, yb, yc, x2, gate, lp['w_pa'], lp['w_pb'], lp['w_pc'], lp['w_out'],
               lp['ln_g'], lp['ln_b'], seq_len, alpha)
    return y, kv, fin


def _states_to_lanes(s):
    b = s.shape[0]
    return s.astype(F32).transpose(3, 0, 2, 1, 4).reshape(N_SSM_GROUPS, b, 4 * SSM_STATE)


def _lanes_to_states(f):
    b = f.shape[1]
    return f.reshape(N_SSM_GROUPS, b, 2, 2, SSM_STATE).transpose(1, 3, 2, 0, 4)


def kernel(x_prompt, x_sample, cache_k, cache_v, state_ssm, c, c_ctx, w_ada, b_ada, w_in, ssm_lam_re, ssm_lam_im, ssm_log_step, ssm_b_re, ssm_b_im, ssm_c_re, ssm_c_im, ssm_d, w_glu, b_glu, attn_sink, sgu_ln_g, sgu_ln_b, w_spatial, b_spatial, w_proj_a, w_proj_b, w_proj_c, w_out, ln_g, ln_b):
    depth = w_in.shape[0]
    batch, seq, d = x_prompt.shape
    dec_batch, dec_seq, _ = x_sample.shape
    past_len = cache_k.shape[2]
    alpha = (2 * depth) ** 0.25

    n_cond = 1 + dec_batch
    cond_rows = -(-n_cond // 8) * 8
    cond = jnp.concatenate([c_ctx[None, :], c, jnp.zeros((cond_rows - n_cond, d), F32)], axis=0)
    mod = _ada_mod(cond, w_ada, b_ada)

    def split_w_in(w):
        sizes = (D_SSM, D_SSM, D_ATTN, D_KV, D_KV, D_ATTN, D_SGU, D_SGU, D_SGU, D_MG)
        offs = [0]
        for s in sizes:
            offs.append(offs[-1] + s)
        xa, za, q, k, v, zb, u, vs, zc, mg = [w[:, offs[i]:offs[i + 1]] for i in range(len(sizes))]
        return jnp.concatenate([mg, xa, za, q, zb, k, v, u, vs, zc], axis=1).astype(BF16)

    layers = []
    for l in range(depth):
        w_intra, w_state, w_enter, small = _ssm_tables(
            ssm_lam_re[l], ssm_lam_im[l], ssm_log_step[l], ssm_b_re[l], ssm_b_im[l], ssm_c_re[l], ssm_c_im[l])
        layers.append({
            'w_in': split_w_in(w_in[l]),
            'w_intra': w_intra, 'w_state': w_state, 'w_enter': w_enter, 'ssm_small': small,
            'd_skip': ssm_d[l], 'w_glu': w_glu[l].astype(BF16), 'b_glu': b_glu[l],
            'sink': attn_sink[l].astype(F32),
            'sgu_g': sgu_ln_g[l], 'sgu_b': sgu_ln_b[l],
            'w_s': w_spatial[l].astype(BF16), 'b_s_t': b_spatial[l].T.astype(F32),
            'w_pa': w_proj_a[l].astype(BF16), 'w_pb': w_proj_b[l].astype(BF16),
            'w_pc': w_proj_c[l].astype(BF16), 'w_out': w_out[l].astype(BF16),
            'ln_g': ln_g[l], 'ln_b': ln_b[l],
        })

    h = x_prompt.reshape(batch * seq, d)
    zero_state = jnp.zeros((N_SSM_GROUPS, batch, 4 * SSM_STATE), F32)
    ks, vs, ss = [], [], []
    for l in range(depth):
        h, kv, fin = _layer(h, batch, seq, mod[l, 0:1], layers[l], None, None, zero_state, None, alpha)
        ks.append(kv[:, :D_KV].reshape(batch, seq, N_KV_HEADS, HEAD_DIM))
        vs.append(kv[:, D_KV:].reshape(batch, seq, N_KV_HEADS, HEAD_DIM))
        ss.append(_lanes_to_states(fin))
    y_prompt = h.reshape(batch, seq, d)
    new_cache_k = jnp.stack(ks, axis=1)
    new_cache_v = jnp.stack(vs, axis=1)
    new_state_ssm = jnp.stack(ss, axis=1)

    rope = _rope_tables(dec_seq)
    z = x_sample.reshape(dec_batch * dec_seq, d)
    for l in range(depth):
        ctx_k = cache_k[:, l].reshape(dec_batch, past_len, D_KV).astype(F32)
        ctx_v = cache_v[:, l].reshape(dec_batch, past_len, D_KV).astype(F32)
        z, _, _ = _layer(z, dec_batch, dec_seq, mod[l, 1:1 + dec_batch], layers[l], ctx_k, ctx_v,
                         _states_to_lanes(state_ssm[:, l]), rope, alpha)
    y_sample = z.reshape(dec_batch, dec_seq, d)
    return (y_prompt, y_sample, new_cache_k, new_cache_v, new_state_ssm)
```

```python
import functools
import math

import jax
import jax.numpy as jnp
from jax import lax
from jax.experimental import pallas as pl
from jax.experimental.pallas import tpu as pltpu

F32 = jnp.float32
BF16 = jnp.bfloat16

D_MODEL = 2048
GRID_W = 64
D_SSM = 512
SSM_GROUP = 16
N_SSM_GROUPS = D_SSM // SSM_GROUP
SSM_STATE = 64
HEAD_DIM = 128
N_HEADS = 8
N_KV_HEADS = 2
Q_PER_KV = N_HEADS // N_KV_HEADS
D_ATTN = N_HEADS * HEAD_DIM
D_KV = N_KV_HEADS * HEAD_DIM
WINDOW = 128
ATTN_BLOCK = 128
ROPE_BASE = 10000.0
D_SGU = 512
SGU_CHUNK = 128
SGU_GROUP_CH = 128
N_SGU_GROUPS = D_SGU // SGU_GROUP_CH
N_BRANCH = 3
LN_EPS = 1e-5

D_MG = N_BRANCH * D_MODEL
COL_MG = 0
COL_XA = COL_MG + D_MG
COL_ZA = COL_XA + D_SSM
COL_Q = COL_ZA + D_SSM
COL_ZB = COL_Q + D_ATTN
COL_K = COL_ZB + D_ATTN
COL_V = COL_K + D_KV
COL_U = COL_V + D_KV
COL_VS = COL_U + D_SGU
COL_ZC = COL_VS + D_SGU
D_IN = COL_ZC + D_SGU

SSM_T = 16
SSM_NC = 16
SSM_TC = SSM_T * SSM_GROUP
SSM_SEG = SSM_T * SSM_NC
SSM_SMALL_ROWS = 40
SSM_LANE_GROUPS = 128 // SSM_GROUP

VMEM_LIMIT_BYTES = 56 * 1024 * 1024

INPROJ_TM = 1024
INPROJ_TN = 1024
ROW_TM = 512
MERGE_TM = 256


def _params(sem):
    return pltpu.CompilerParams(dimension_semantics=sem, vmem_limit_bytes=VMEM_LIMIT_BYTES)


def _layernorm_f32(x):
    mu = jnp.mean(x, axis=-1, keepdims=True)
    xc = x - mu
    var = jnp.mean(xc * xc, axis=-1, keepdims=True)
    return xc * lax.rsqrt(var + LN_EPS)


def _ada_kernel(cond_ref, w_ref, b_ref, o_ref):
    c = cond_ref[...]
    a = (c * jax.nn.sigmoid(c)).astype(BF16)
    o_ref[...] = jnp.dot(a, w_ref[...].astype(BF16), preferred_element_type=F32) + b_ref[...]


def _ada_mod(cond, w_ada, b_ada):
    depth, d, n = w_ada.shape
    r = cond.shape[0]
    tn = 512
    return pl.pallas_call(
        _ada_kernel,
        out_shape=jax.ShapeDtypeStruct((depth, r, n), F32),
        grid=(depth, n // tn),
        in_specs=[
            pl.BlockSpec((r, d), lambda l, j: (0, 0)),
            pl.BlockSpec((None, d, tn), lambda l, j: (l, 0, j)),
            pl.BlockSpec((None, 1, tn), lambda l, j: (l, 0, j)),
        ],
        out_specs=pl.BlockSpec((None, r, tn), lambda l, j: (l, 0, j)),
        compiler_params=_params(("arbitrary", "arbitrary")),
        name="ada_mod",
    )(cond, w_ada, b_ada.reshape(depth, 1, n))


def _inproj_kernel(x_ref, scale_ref, shift_ref, w_ref, proj_ref, kv_ref, xa_ref, h_scr, *, kv_tile, xa_tile):
    j = pl.program_id(1)

    @pl.when(j == 0)
    def _():
        h = _layernorm_f32(x_ref[...]) * (1.0 + scale_ref[0]) + shift_ref[0]
        h_scr[...] = h.astype(BF16)

    acc = jnp.dot(h_scr[...], w_ref[...], preferred_element_type=F32)
    proj_ref[...] = acc.astype(BF16)

    @pl.when(j == kv_tile)
    def _():
        kv_ref[...] = acc[:, 0:2 * D_KV]

    @pl.when(j == xa_tile)
    def _():
        xa_ref[...] = acc[:, 0:D_SSM]


def _inproj(x2, scale, shift, w_bf16, seq_len):
    t, d = x2.shape
    n = w_bf16.shape[1]
    tm, tn = INPROJ_TM, INPROJ_TN
    assert t % tm == 0 and n % tn == 0 and COL_K % tn == 0 and COL_XA % tn == 0
    assert 2 * D_KV <= tn and D_SSM <= tn
    nb = scale.shape[0]
    if nb == 1:
        mod_map = lambda i, j: (0, 0, 0)
    else:
        assert seq_len % tm == 0
        tiles_per_seq = seq_len // tm
        mod_map = lambda i, j: (i // tiles_per_seq, 0, 0)
    return pl.pallas_call(
        functools.partial(_inproj_kernel, kv_tile=COL_K // tn, xa_tile=COL_XA // tn),
        out_shape=(jax.ShapeDtypeStruct((t, n), BF16), jax.ShapeDtypeStruct((t, 2 * D_KV), F32),
                   jax.ShapeDtypeStruct((t, D_SSM), F32)),
        grid=(t // tm, n // tn),
        in_specs=[
            pl.BlockSpec((tm, d), lambda i, j: (i, 0)),
            pl.BlockSpec((1, 1, d), mod_map),
            pl.BlockSpec((1, 1, d), mod_map),
            pl.BlockSpec((d, tn), lambda i, j: (0, j)),
        ],
        out_specs=(
            pl.BlockSpec((tm, tn), lambda i, j: (i, j)),
            pl.BlockSpec((tm, 2 * D_KV), lambda i, j: (i, 0)),
            pl.BlockSpec((tm, D_SSM), lambda i, j: (i, 0)),
        ),
        scratch_shapes=[pltpu.VMEM((tm, d), BF16)],
        compiler_params=_params(("arbitrary", "arbitrary")),
        name="ln_inproj",
    )(x2, scale, shift, w_bf16)


def _ssm_tables(lam_re, lam_im, log_step, b_re, b_im, c_re, c_im):
    hp = lax.Precision.HIGHEST
    g_n, p_n, c_n, t_n = N_SSM_GROUPS, SSM_STATE, SSM_GROUP, SSM_T
    lr = lam_re.astype(F32)
    li = lam_im.astype(F32)
    dt = jnp.exp(log_step.astype(F32))[..., None]
    mag = jnp.exp(lr * dt)
    ar = mag * jnp.cos(li * dt)
    ai = mag * jnp.sin(li * dt)
    den = lr * lr + li * li
    nr = ar - 1.0
    f_re = (nr * lr + ai * li) / den
    f_im = (ai * lr - nr * li) / den
    br = b_re.astype(F32)
    bi = b_im.astype(F32)
    bb_re = f_re[..., None] * br - f_im[..., None] * bi
    bb_im = f_re[..., None] * bi + f_im[..., None] * br

    def cmul(xr, xi, yr, yi):
        return xr * yr - xi * yi, xr * yi + xi * yr

    pr, pi = [jnp.ones_like(ar)], [jnp.zeros_like(ai)]
    for _ in range(t_n):
        nr_, ni_ = cmul(pr[-1], pi[-1], ar, ai)
        pr.append(nr_)
        pi.append(ni_)
    pw_re = jnp.stack(pr, axis=0)
    pw_im = jnp.stack(pi, axis=0)
    cr = c_re.astype(F32)
    ci = c_im.astype(F32)
    cp_re = cr[None] * pw_re[:, :, :, None, :] - ci[None] * pw_im[:, :, :, None, :]
    cp_im = cr[None] * pw_im[:, :, :, None, :] + ci[None] * pw_re[:, :, :, None, :]
    taps = (jnp.einsum('kdgcp,dgpe->dgkce', cp_re[:t_n], bb_re, precision=hp)
            - jnp.einsum('kdgcp,dgpe->dgkce', cp_im[:t_n], bb_im, precision=hp))
    jj = jnp.arange(t_n)[:, None]
    tt = jnp.arange(t_n)[None, :]
    kf = jnp.where((tt >= jj)[None, :, :, None, None], taps[0][:, jnp.clip(tt - jj, 0, t_n - 1)], 0.0)
    kb = jnp.where((jj >= tt)[None, :, :, None, None], taps[1][:, jnp.clip(jj - tt, 0, t_n - 1)], 0.0)
    w_intra = (kf + kb).transpose(0, 1, 4, 2, 3).reshape(g_n, t_n * c_n, t_n * c_n)

    idx_f = jnp.arange(t_n - 1, -1, -1)
    idx_b = jnp.arange(t_n)
    sf_re, sf_im = cmul(pw_re[idx_f, 0][..., None], pw_im[idx_f, 0][..., None], bb_re[0][None], bb_im[0][None])
    sb_re, sb_im = cmul(pw_re[idx_b, 1][..., None], pw_im[idx_b, 1][..., None], bb_re[1][None], bb_im[1][None])
    w_state = jnp.stack([sf_re, sb_re, sf_im, sb_im], axis=0)
    w_state = w_state.transpose(2, 1, 4, 0, 3).reshape(g_n, t_n * c_n, 4 * p_n)

    tf = jnp.arange(1, t_n + 1)
    tb = jnp.arange(t_n, 0, -1)
    w_enter = jnp.stack([cp_re[tf, 0], cp_re[tb, 1], -cp_im[tf, 0], -cp_im[tb, 1]], axis=0)
    w_enter = w_enter.transpose(2, 0, 4, 1, 3).reshape(g_n, 4 * p_n, t_n * c_n)

    a1r = jnp.concatenate([pw_re[t_n, 0], pw_re[t_n, 1]], axis=-1)
    a1i = jnp.concatenate([pw_im[t_n, 0], pw_im[t_n, 1]], axis=-1)
    qr, qi = [jnp.ones_like(a1r)], [jnp.zeros_like(a1i)]
    for _ in range(SSM_NC):
        nr_, ni_ = cmul(qr[-1], qi[-1], a1r, a1i)
        qr.append(nr_)
        qi.append(ni_)
    qr = jnp.stack(qr, axis=1)
    qi = jnp.stack(qi, axis=1)
    rev = jnp.arange(SSM_NC - 1, -1, -1)
    corr_re = jnp.concatenate([qr[:, :SSM_NC, :p_n], qr[:, rev, p_n:]], axis=-1)
    corr_im = jnp.concatenate([qi[:, :SSM_NC, :p_n], qi[:, rev, p_n:]], axis=-1)
    small = jnp.concatenate([
        corr_re, corr_im, a1r[:, None], a1i[:, None], qr[:, SSM_NC][:, None], qi[:, SSM_NC][:, None],
        jnp.zeros((g_n, SSM_SMALL_ROWS - 2 * SSM_NC - 4, 2 * p_n), F32)], axis=1)
    return w_intra.astype(BF16), w_state.astype(BF16), w_enter.astype(BF16), small


def _ssm_fold_matrix():
    n = SSM_T * SSM_LANE_GROUPS * SSM_GROUP
    src = jnp.arange(n)
    j = src // (SSM_LANE_GROUPS * SSM_GROUP)
    g = (src // SSM_GROUP) % SSM_LANE_GROUPS
    c = src % SSM_GROUP
    dst = g * SSM_TC + j * SSM_GROUP + c
    return (dst[:, None] == jnp.arange(n)[None, :]).astype(BF16)


def _ssm_kernel(xa_ref, fold_ref, wi_ref, ws_ref, we_ref, small_ref, s0_ref, y_ref, fin_ref,
                v_scr, s_scr, yy_scr, fre_scr, fim_scr, efre_scr, efim_scr, ebre_scr, ebim_scr,
                *, n_seq, n_seg):
    p2 = 2 * SSM_STATE
    p4 = 4 * SSM_STATE
    nbp = n_seq * n_seg
    lanes = SSM_LANE_GROUPS * SSM_GROUP
    nt = (((1,), (1,)), ((), ()))

    blocks = []
    for c in range(SSM_NC):
        toks = [xa_ref[pl.ds(c * SSM_T + j, nbp, stride=SSM_SEG), :].astype(BF16) for j in range(SSM_T)]
        blocks.append(jnp.concatenate(toks, axis=1))
    xx = jnp.concatenate(blocks, axis=0)
    uu = jnp.dot(xx, fold_ref[...], preferred_element_type=F32).astype(BF16)

    lane = lax.broadcasted_iota(jnp.int32, (1, p2), 1)
    is_fwd = lane < SSM_STATE
    for g in range(SSM_LANE_GROUPS):
        gc = slice(g * SSM_TC, (g + 1) * SSM_TC)
        u = uu[:, gc]
        yy_scr[:, gc] = jnp.dot(u, wi_ref[g], preferred_element_type=F32)
        v_scr[:, gc] = jnp.dot(u, ws_ref[g], preferred_element_type=F32)

    for g in range(SSM_LANE_GROUPS):
        o = g * p4
        a1r = small_ref[g, 2 * SSM_NC:2 * SSM_NC + 1, :]
        a1i = small_ref[g, 2 * SSM_NC + 1:2 * SSM_NC + 2, :]
        asr = small_ref[g, 2 * SSM_NC + 2:2 * SSM_NC + 3, :]
        asi = small_ref[g, 2 * SSM_NC + 3:2 * SSM_NC + 4, :]

        st_re = jnp.zeros((nbp, p2), F32)
        st_im = jnp.zeros((nbp, p2), F32)
        for c in range(SSM_NC):
            cb = SSM_NC - 1 - c
            rf = slice(c * nbp, (c + 1) * nbp)
            rb = slice(cb * nbp, (cb + 1) * nbp)
            s_scr[rf, o:o + SSM_STATE] = st_re[:, 0:SSM_STATE]
            s_scr[rb, o + SSM_STATE:o + p2] = st_re[:, SSM_STATE:p2]
            s_scr[rf, o + p2:o + p2 + SSM_STATE] = st_im[:, 0:SSM_STATE]
            s_scr[rb, o + p2 + SSM_STATE:o + p4] = st_im[:, SSM_STATE:p2]
            v_re = jnp.where(is_fwd, v_scr[rf, o:o + p2], v_scr[rb, o:o + p2])
            v_im = jnp.where(is_fwd, v_scr[rf, o + p2:o + p4], v_scr[rb, o + p2:o + p4])
            st_re, st_im = (a1r * st_re - a1i * st_im + v_re, a1r * st_im + a1i * st_re + v_im)

        fre_scr[...] = st_re
        fim_scr[...] = st_im
        s0 = s0_ref[g]
        e_re = s0[:, 0:p2]
        e_im = s0[:, p2:p4]
        for k in range(n_seg):
            kb = n_seg - 1 - k
            rows_f = pl.ds(k, n_seq, stride=n_seg)
            rows_b = pl.ds(kb, n_seq, stride=n_seg)
            efre_scr[rows_f, :] = e_re
            efim_scr[rows_f, :] = e_im
            ebre_scr[rows_b, :] = e_re
            ebim_scr[rows_b, :] = e_im
            f_re = jnp.where(is_fwd, fre_scr[rows_f, :], fre_scr[rows_b, :])
            f_im = jnp.where(is_fwd, fim_scr[rows_f, :], fim_scr[rows_b, :])
            e_re, e_im = (asr * e_re - asi * e_im + f_re, asr * e_im + asi * e_re + f_im)
        fin_ref[g] = jnp.concatenate([e_re, e_im], axis=-1)

        en_re = jnp.where(is_fwd, efre_scr[...], ebre_scr[...])
        en_im = jnp.where(is_fwd, efim_scr[...], ebim_scr[...])
        for c in range(SSM_NC):
            rows = slice(c * nbp, (c + 1) * nbp)
            cr = small_ref[g, c:c + 1, :]
            ci = small_ref[g, SSM_NC + c:SSM_NC + c + 1, :]
            s_scr[rows, o:o + p2] = s_scr[rows, o:o + p2] + (cr * en_re - ci * en_im)
            s_scr[rows, o + p2:o + p4] = s_scr[rows, o + p2:o + p4] + (cr * en_im + ci * en_re)

    for g in range(SSM_LANE_GROUPS):
        gc = slice(g * SSM_TC, (g + 1) * SSM_TC)
        yy_scr[:, gc] += jnp.dot(s_scr[:, g * p4:(g + 1) * p4].astype(BF16), we_ref[g],
                                 preferred_element_type=F32)

    zz = lax.dot_general(yy_scr[...].astype(BF16), fold_ref[...], nt, preferred_element_type=F32)
    for c in range(SSM_NC):
        for j in range(SSM_T):
            y_ref[pl.ds(c * SSM_T + j, nbp, stride=SSM_SEG), :] = zz[c * nbp:(c + 1) * nbp,
                                                                      j * lanes:(j + 1) * lanes]


def _ssm_scan(xa, fold, w_intra, w_state, w_enter, small, s0, n_seq, n_seg):
    t = xa.shape[0]
    g_n = N_SSM_GROUPS
    lg = SSM_LANE_GROUPS
    lanes = lg * SSM_GROUP
    p4 = 4 * SSM_STATE
    nbp = n_seq * n_seg
    r = SSM_NC * nbp
    assert t == r * SSM_T and fold.shape == (SSM_T * lanes, lg * SSM_TC)
    const = lambda shape: pl.BlockSpec(shape, lambda i: (0,) * len(shape), pipeline_mode=pl.Buffered(1))
    return pl.pallas_call(
        functools.partial(_ssm_kernel, n_seq=n_seq, n_seg=n_seg),
        out_shape=(jax.ShapeDtypeStruct((t, D_SSM), F32), jax.ShapeDtypeStruct((g_n, n_seq, p4), F32)),
        grid=(g_n // lg,),
        in_specs=[
            pl.BlockSpec((t, lanes), lambda i: (0, i)),
            const(fold.shape),
            pl.BlockSpec((lg, SSM_TC, SSM_TC), lambda i: (i, 0, 0)),
            pl.BlockSpec((lg, SSM_TC, p4), lambda i: (i, 0, 0)),
            pl.BlockSpec((lg, p4, SSM_TC), lambda i: (i, 0, 0)),
            pl.BlockSpec((lg, SSM_SMALL_ROWS, 2 * SSM_STATE), lambda i: (i, 0, 0)),
            pl.BlockSpec((lg, n_seq, p4), lambda i: (i, 0, 0)),
        ],
        out_specs=(
            pl.BlockSpec((t, lanes), lambda i: (0, i)),
            pl.BlockSpec((lg, n_seq, p4), lambda i: (i, 0, 0)),
        ),
        scratch_shapes=[pltpu.VMEM((r, lg * p4), F32), pltpu.VMEM((r, lg * p4), F32),
                        pltpu.VMEM((r, lg * SSM_TC), F32)]
        + [pltpu.VMEM((nbp, 2 * SSM_STATE), F32)] * 6,
        compiler_params=_params(("arbitrary",)),
        name="ssm_scan",
    )(xa, fold, w_intra, w_state, w_enter, small, s0)


def _ssm_post_kernel(xa_ref, za_ref, ys_ref, d_ref, w_ref, b_ref, o_ref):
    y = xa_ref[...].astype(F32) * d_ref[...] + ys_ref[...]
    y = jax.nn.gelu(y)
    gl = jnp.dot(y.astype(BF16), w_ref[...], preferred_element_type=F32) + b_ref[...]
    y = y * jax.nn.sigmoid(gl)
    o_ref[...] = (y * jax.nn.silu(za_ref[...].astype(F32))).astype(BF16)


def _ssm_post(proj, yssm, d_skip, w_glu_bf16, b_glu):
    t = proj.shape[0]
    tm = ROW_TM
    return pl.pallas_call(
        _ssm_post_kernel,
        out_shape=jax.ShapeDtypeStruct((t, D_SSM), BF16),
        grid=(t // tm,),
        in_specs=[
            pl.BlockSpec((tm, D_SSM), lambda i: (i, COL_XA // D_SSM)),
            pl.BlockSpec((tm, D_SSM), lambda i: (i, COL_ZA // D_SSM)),
            pl.BlockSpec((tm, D_SSM), lambda i: (i, 0)),
            pl.BlockSpec((1, D_SSM), lambda i: (0, 0)),
            pl.BlockSpec((D_SSM, D_SSM), lambda i: (0, 0)),
            pl.BlockSpec((1, D_SSM), lambda i: (0, 0)),
        ],
        out_specs=pl.BlockSpec((tm, D_SSM), lambda i: (i, 0)),
        compiler_params=_params(("arbitrary",)),
        name="ssm_post",
    )(proj, proj, yssm, d_skip.reshape(1, D_SSM), w_glu_bf16, b_glu.reshape(1, D_SSM))


def _softmax_pv(s, sink, v_bf16):
    m = jnp.maximum(jnp.max(s, axis=-1, keepdims=True), sink)
    p = jnp.exp(s - m)
    den = jnp.sum(p, axis=-1, keepdims=True) + jnp.exp(sink - m)
    o = jnp.dot(p.astype(BF16), v_bf16, preferred_element_type=F32)
    return o / den


def _ctx_attn_kernel(sink_ref, q_ref, kv_ref, zb_ref, o_ref):
    scale = HEAD_DIM ** -0.5
    nt = (((1,), (1,)), ((), ()))
    for h in range(N_HEADS):
        kvh = h // Q_PER_KV
        q = q_ref[:, h * HEAD_DIM:(h + 1) * HEAD_DIM]
        k = kv_ref[:, kvh * HEAD_DIM:(kvh + 1) * HEAD_DIM].astype(BF16)
        v = kv_ref[:, D_KV + kvh * HEAD_DIM:D_KV + (kvh + 1) * HEAD_DIM].astype(BF16)
        s = lax.dot_general(q, k, nt, preferred_element_type=F32) * scale
        o = _softmax_pv(s, sink_ref[h], v)
        zb = zb_ref[:, h * HEAD_DIM:(h + 1) * HEAD_DIM].astype(F32)
        o_ref[:, h * HEAD_DIM:(h + 1) * HEAD_DIM] = (o * jax.nn.silu(zb)).astype(BF16)


def _ctx_attention(proj, kv, sink, seq_len):
    t = proj.shape[0]
    return pl.pallas_call(
        _ctx_attn_kernel,
        out_shape=jax.ShapeDtypeStruct((t, D_ATTN), BF16),
        grid=(t // seq_len,),
        in_specs=[
            pl.BlockSpec(memory_space=pltpu.SMEM),
            pl.BlockSpec((seq_len, D_ATTN), lambda b: (b, COL_Q // D_ATTN)),
            pl.BlockSpec((seq_len, 2 * D_KV), lambda b: (b, 0)),
            pl.BlockSpec((seq_len, D_ATTN), lambda b: (b, COL_ZB // D_ATTN)),
        ],
        out_specs=pl.BlockSpec((seq_len, D_ATTN), lambda b: (b, 0)),
        compiler_params=_params(("arbitrary",)),
        name="ctx_attention",
    )(sink, proj, kv, proj)


def _rope_partner(x):
    lane = lax.broadcasted_iota(jnp.int32, x.shape, 1)
    quarter = HEAD_DIM // 4
    first = (lane % (2 * quarter)) < quarter
    return jnp.where(first, pltpu.roll(x, HEAD_DIM - quarter, 1), pltpu.roll(x, quarter, 1))


def _lat_attn_kernel(sink_ref, q_ref, kvp_ref, kvc_ref, kvn_ref, ck_ref, cv_ref, zb_ref,
                     cosq_ref, sinq_ref, cosp_ref, sinp_ref, cosn_ref, sinn_ref, o_ref, *, n_blocks):
    i = pl.program_id(1)
    scale = HEAD_DIM ** -0.5
    nt = (((1,), (1,)), ((), ()))
    blk = ATTN_BLOCK
    cos_q = cosq_ref[...]
    sin_q = sinq_ref[...]
    r = lax.broadcasted_iota(jnp.int32, (Q_PER_KV * blk, 3 * blk), 0) % blk
    s_idx = lax.broadcasted_iota(jnp.int32, (Q_PER_KV * blk, 3 * blk), 1)
    rel = s_idx - blk - r
    ok = (jnp.abs(rel) <= WINDOW)
    ok = ok & ((s_idx >= blk) | (i > 0)) & ((s_idx < 2 * blk) | (i < n_blocks - 1))
    for kvh in range(N_KV_HEADS):
        ks = slice(kvh * HEAD_DIM, (kvh + 1) * HEAD_DIM)
        vs = slice(D_KV + kvh * HEAD_DIM, D_KV + (kvh + 1) * HEAD_DIM)

        def rope_k(ref, cos_ref, sin_ref):
            x = ref[:, ks]
            return (x * cos_ref[...] + _rope_partner(x) * sin_ref[...]).astype(BF16)

        k_loc = jnp.concatenate([rope_k(kvp_ref, cosp_ref, sinp_ref), rope_k(kvc_ref, cosq_ref, sinq_ref),
                                 rope_k(kvn_ref, cosn_ref, sinn_ref)], axis=0)
        v_all = jnp.concatenate([kvp_ref[:, vs].astype(BF16), kvc_ref[:, vs].astype(BF16),
                                 kvn_ref[:, vs].astype(BF16), cv_ref[:, ks].astype(BF16)], axis=0)
        k_ctx = ck_ref[:, ks].astype(BF16)
        qs = []
        for g in range(Q_PER_KV):
            h = kvh * Q_PER_KV + g
            x = q_ref[:, h * HEAD_DIM:(h + 1) * HEAD_DIM].astype(F32)
            qs.append((x * cos_q + _rope_partner(x) * sin_q).astype(BF16))
        q4 = jnp.concatenate(qs, axis=0)
        s_loc = lax.dot_general(q4, k_loc, nt, preferred_element_type=F32) * scale
        s_loc = jnp.where(ok, s_loc, -jnp.inf)
        s_ctx = lax.dot_general(q4, k_ctx, nt, preferred_element_type=F32) * scale
        s = jnp.concatenate([s_loc, s_ctx], axis=-1)
        sink_rows = jnp.concatenate(
            [jnp.full((blk, 1), sink_ref[kvh * Q_PER_KV + g], F32) for g in range(Q_PER_KV)], axis=0)
        o = _softmax_pv(s, sink_rows, v_all)
        for g in range(Q_PER_KV):
            h = kvh * Q_PER_KV + g
            zb = zb_ref[:, h * HEAD_DIM:(h + 1) * HEAD_DIM].astype(F32)
            o_ref[:, h * HEAD_DIM:(h + 1) * HEAD_DIM] = (o[g * blk:(g + 1) * blk] * jax.nn.silu(zb)).astype(BF16)


def _lat_attention(proj, kv, ctx_k, ctx_v, sink, cos_t, sin_t, n_seq, seq_len):
    t = proj.shape[0]
    blk = ATTN_BLOCK
    nb = seq_len // blk
    lc = ctx_k.shape[1]

    def row(b, i):
        return b * nb + i

    def prev(b, i):
        return b * nb + jnp.maximum(i - 1, 0)

    def nxt(b, i):
        return b * nb + jnp.minimum(i + 1, nb - 1)

    return pl.pallas_call(
        functools.partial(_lat_attn_kernel, n_blocks=nb),
        out_shape=jax.ShapeDtypeStruct((t, D_ATTN), BF16),
        grid=(n_seq, nb),
        in_specs=[
            pl.BlockSpec(memory_space=pltpu.SMEM),
            pl.BlockSpec((blk, D_ATTN), lambda b, i: (row(b, i), COL_Q // D_ATTN)),
            pl.BlockSpec((blk, 2 * D_KV), lambda b, i: (prev(b, i), 0)),
            pl.BlockSpec((blk, 2 * D_KV), lambda b, i: (row(b, i), 0)),
            pl.BlockSpec((blk, 2 * D_KV), lambda b, i: (nxt(b, i), 0)),
            pl.BlockSpec((None, lc, D_KV), lambda b, i: (b, 0, 0)),
            pl.BlockSpec((None, lc, D_KV), lambda b, i: (b, 0, 0)),
            pl.BlockSpec((blk, D_ATTN), lambda b, i: (row(b, i), COL_ZB // D_ATTN)),
            pl.BlockSpec((blk, HEAD_DIM), lambda b, i: (i, 0)),
            pl.BlockSpec((blk, HEAD_DIM), lambda b, i: (i, 0)),
            pl.BlockSpec((blk, HEAD_DIM), lambda b, i: (jnp.maximum(i - 1, 0), 0)),
            pl.BlockSpec((blk, HEAD_DIM), lambda b, i: (jnp.maximum(i - 1, 0), 0)),
            pl.BlockSpec((blk, HEAD_DIM), lambda b, i: (jnp.minimum(i + 1, nb - 1), 0)),
            pl.BlockSpec((blk, HEAD_DIM), lambda b, i: (jnp.minimum(i + 1, nb - 1), 0)),
        ],
        out_specs=pl.BlockSpec((blk, D_ATTN), lambda b, i: (row(b, i), 0)),
        compiler_params=_params(("arbitrary", "arbitrary")),
        name="lat_attention",
    )(sink, proj, kv, kv, kv, ctx_k, ctx_v, proj, cos_t, sin_t, cos_t, sin_t, cos_t, sin_t)


def _rope_tables(seq_len):
    rows = seq_len // GRID_W
    row = jnp.repeat(jnp.arange(rows), GRID_W).astype(F32)
    col = jnp.tile(jnp.arange(GRID_W), rows).astype(F32)
    half = HEAD_DIM // 2
    inv = ROPE_BASE ** (-jnp.arange(0, half, 2, dtype=F32) / half)
    ang_r = row[:, None] * inv[None, :]
    ang_c = col[:, None] * inv[None, :]
    cos_t = jnp.concatenate([jnp.cos(ang_r)] * 2 + [jnp.cos(ang_c)] * 2, axis=-1)
    sin_t = jnp.concatenate([-jnp.sin(ang_r), jnp.sin(ang_r), -jnp.sin(ang_c), jnp.sin(ang_c)], axis=-1)
    return cos_t, sin_t


def _sgu_kernel(u_ref, v_ref, z_ref, g_ref, b_ref, ws_ref, bs_ref, o_ref, *, n_chunks):
    for n in range(n_chunks):
        rows = slice(n * SGU_CHUNK, (n + 1) * SGU_CHUNK)
        v = _layernorm_f32(jax.nn.gelu(v_ref[rows, :].astype(F32))) * g_ref[...] + b_ref[...]
        v = v.astype(BF16)
        for g in range(N_SGU_GROUPS):
            cols = slice(g * SGU_GROUP_CH, (g + 1) * SGU_GROUP_CH)
            vm = jnp.dot(ws_ref[g], v[:, cols], preferred_element_type=F32) + bs_ref[:, g:g + 1]
            u = jax.nn.gelu(u_ref[rows, cols].astype(F32))
            z = z_ref[rows, cols].astype(F32)
            o_ref[rows, cols] = (u * vm * jax.nn.silu(z)).astype(BF16)


def _sgu(proj, ln_g, ln_b, w_s_bf16, b_s_t):
    t = proj.shape[0]
    tm = ROW_TM
    return pl.pallas_call(
        functools.partial(_sgu_kernel, n_chunks=tm // SGU_CHUNK),
        out_shape=jax.ShapeDtypeStruct((t, D_SGU), BF16),
        grid=(t // tm,),
        in_specs=[
            pl.BlockSpec((tm, D_SGU), lambda i: (i, COL_U // D_SGU)),
            pl.BlockSpec((tm, D_SGU), lambda i: (i, COL_VS // D_SGU)),
            pl.BlockSpec((tm, D_SGU), lambda i: (i, COL_ZC // D_SGU)),
            pl.BlockSpec((1, D_SGU), lambda i: (0, 0)),
            pl.BlockSpec((1, D_SGU), lambda i: (0, 0)),
            pl.BlockSpec((N_SGU_GROUPS, SGU_CHUNK, SGU_CHUNK), lambda i: (0, 0, 0)),
            pl.BlockSpec((SGU_CHUNK, N_SGU_GROUPS), lambda i: (0, 0)),
        ],
        out_specs=pl.BlockSpec((tm, D_SGU), lambda i: (i, 0)),
        compiler_params=_params(("arbitrary",)),
        name="sgu",
    )(proj, proj, proj, ln_g.reshape(1, D_SGU), ln_b.reshape(1, D_SGU), w_s_bf16, b_s_t)


def _merge_kernel(mg_ref, ya_ref, yb_ref, yc_ref, x_ref, gate_ref, wpa_ref, wpb_ref, wpc_ref, wout_ref,
                  lng_ref, lnb_ref, y_ref, *, alpha):
    d = D_MODEL

    def branch(k, y_r, w_r):
        gate = jax.nn.sigmoid(mg_ref[:, k * d:(k + 1) * d].astype(F32))
        return gate * jnp.dot(y_r[...], w_r[...], preferred_element_type=F32)

    merged = branch(0, ya_ref, wpa_ref) + branch(1, yb_ref, wpb_ref) + branch(2, yc_ref, wpc_ref)
    out = jnp.dot(merged.astype(BF16), wout_ref[...], preferred_element_type=F32)
    z = alpha * x_ref[...] + gate_ref[0] * out
    y_ref[...] = _layernorm_f32(z) * lng_ref[...] + lnb_ref[...]


def _merge(proj, ya, yb, yc, x2, gate, wpa, wpb, wpc, wout, ln_g, ln_b, seq_len, alpha):
    t, d = x2.shape
    tm = MERGE_TM
    if gate.shape[0] == 1:
        mod_map = lambda i: (0, 0, 0)
    else:
        assert seq_len % tm == 0
        tiles_per_seq = seq_len // tm
        mod_map = lambda i: (i // tiles_per_seq, 0, 0)

    def const(shape):
        return pl.BlockSpec(shape, lambda i: (0,) * len(shape), pipeline_mode=pl.Buffered(1))

    return pl.pallas_call(
        functools.partial(_merge_kernel, alpha=alpha),
        out_shape=jax.ShapeDtypeStruct((t, d), F32),
        grid=(t // tm,),
        in_specs=[
            pl.BlockSpec((tm, D_MG), lambda i: (i, 0)),
            pl.BlockSpec((tm, D_SSM), lambda i: (i, 0)),
            pl.BlockSpec((tm, D_ATTN), lambda i: (i, 0)),
            pl.BlockSpec((tm, D_SGU), lambda i: (i, 0)),
            pl.BlockSpec((tm, d), lambda i: (i, 0)),
            pl.BlockSpec((1, 1, d), mod_map),
            const((D_SSM, d)), const((D_ATTN, d)), const((D_SGU, d)), const((d, d)),
            const((1, d)), const((1, d)),
        ],
        out_specs=pl.BlockSpec((tm, d), lambda i: (i, 0)),
        compiler_params=_params(("arbitrary",)),
        name="merge_out",
    )(proj, ya, yb, yc, x2, gate, wpa, wpb, wpc, wout, ln_g.reshape(1, d), ln_b.reshape(1, d))


def _layer(x2, n_seq, seq_len, mod, lp, ctx_k, ctx_v, s0, rope, alpha):
    d = D_MODEL
    nbm = mod.shape[0]
    shift = mod[:, 0:d].reshape(nbm, 1, d)
    scale = mod[:, d:2 * d].reshape(nbm, 1, d)
    gate = mod[:, 2 * d:3 * d].reshape(nbm, 1, d)
    proj, kv, xa = _inproj(x2, scale, shift, lp['w_in'], seq_len)

    n_seg = seq_len // SSM_SEG
    yssm, fin = _ssm_scan(xa, lp['fold'], lp['w_intra'], lp['w_state'], lp['w_enter'], lp['ssm_small'],
                          s0, n_seq, n_seg)
    ya = _ssm_post(proj, yssm, lp['d_skip'], lp['w_glu'], lp['b_glu'])

    if ctx_k is None:
        yb = _ctx_attention(proj, kv, lp['sink'], seq_len)
    else:
        yb = _lat_attention(proj, kv, ctx_k, ctx_v, lp['sink'], rope[0], rope[1], n_seq, seq_len)

    yc = _sgu(proj, lp['sgu_g'], lp['sgu_b'], lp['w_s'], lp['b_s_t'])

    y = _merge(proj, ya, yb, yc, x2, gate, lp['w_pa'], lp['w_pb'], lp['w_pc'], lp['w_out'],
               lp['ln_g'], lp['ln_b'], seq_len, alpha)
    return y, kv, fin


def _states_to_lanes(s):
    b = s.shape[0]
    return s.astype(F32).transpose(3, 0, 2, 1, 4).reshape(N_SSM_GROUPS, b, 4 * SSM_STATE)


def _lanes_to_states(f):
    b = f.shape[1]
    return f.reshape(N_SSM_GROUPS, b, 2, 2, SSM_STATE).transpose(1, 3, 2, 0, 4)


def kernel(x_prompt, x_sample, cache_k, cache_v, state_ssm, c, c_ctx, w_ada, b_ada, w_in, ssm_lam_re, ssm_lam_im, ssm_log_step, ssm_b_re, ssm_b_im, ssm_c_re, ssm_c_im, ssm_d, w_glu, b_glu, attn_sink, sgu_ln_g, sgu_ln_b, w_spatial, b_spatial, w_proj_a, w_proj_b, w_proj_c, w_out, ln_g, ln_b):
    depth = w_in.shape[0]
    batch, seq, d = x_prompt.shape
    dec_batch, dec_seq, _ = x_sample.shape
    past_len = cache_k.shape[2]
    alpha = (2 * depth) ** 0.25

    n_cond = 1 + dec_batch
    cond_rows = -(-n_cond // 8) * 8
    cond = jnp.concatenate([c_ctx[None, :], c, jnp.zeros((cond_rows - n_cond, d), F32)], axis=0)
    mod = _ada_mod(cond, w_ada, b_ada)

    def split_w_in(w):
        sizes = (D_SSM, D_SSM, D_ATTN, D_KV, D_KV, D_ATTN, D_SGU, D_SGU, D_SGU, D_MG)
        offs = [0]
        for s in sizes:
            offs.append(offs[-1] + s)
        xa, za, q, k, v, zb, u, vs, zc, mg = [w[:, offs[i]:offs[i + 1]] for i in range(len(sizes))]
        return jnp.concatenate([mg, xa, za, q, zb, k, v, u, vs, zc], axis=1).astype(BF16)

    fold = _ssm_fold_matrix()
    layers = []
    for l in range(depth):
        w_intra, w_state, w_enter, small = _ssm_tables(
            ssm_lam_re[l], ssm_lam_im[l], ssm_log_step[l], ssm_b_re[l], ssm_b_im[l], ssm_c_re[l], ssm_c_im[l])
        layers.append({
            'w_in': split_w_in(w_in[l]),
            'fold': fold, 'w_intra': w_intra, 'w_state': w_state, 'w_enter': w_enter, 'ssm_small': small,
            'd_skip': ssm_d[l], 'w_glu': w_glu[l].astype(BF16), 'b_glu': b_glu[l],
            'sink': attn_sink[l].astype(F32),
            'sgu_g': sgu_ln_g[l], 'sgu_b': sgu_ln_b[l],
            'w_s': w_spatial[l].astype(BF16), 'b_s_t': b_spatial[l].T.astype(F32),
            'w_pa': w_proj_a[l].astype(BF16), 'w_pb': w_proj_b[l].astype(BF16),
            'w_pc': w_proj_c[l].astype(BF16), 'w_out': w_out[l].astype(BF16),
            'ln_g': ln_g[l], 'ln_b': ln_b[l],
        })

    h = x_prompt.reshape(batch * seq, d)
    zero_state = jnp.zeros((N_SSM_GROUPS, batch, 4 * SSM_STATE), F32)
    ks, vs, ss = [], [], []
    for l in range(depth):
        h, kv, fin = _layer(h, batch, seq, mod[l, 0:1], layers[l], None, None, zero_state, None, alpha)
        ks.append(kv[:, :D_KV].reshape(batch, seq, N_KV_HEADS, HEAD_DIM))
        vs.append(kv[:, D_KV:].reshape(batch, seq, N_KV_HEADS, HEAD_DIM))
        ss.append(_lanes_to_states(fin))
    y_prompt = h.reshape(batch, seq, d)
    new_cache_k = jnp.stack(ks, axis=1)
    new_cache_v = jnp.stack(vs, axis=1)
    new_state_ssm = jnp.stack(ss, axis=1)

    rope = _rope_tables(dec_seq)
    z = x_sample.reshape(dec_batch * dec_seq, d)
    for l in range(depth):
        ctx_k = cache_k[:, l].reshape(dec_batch, past_len, D_KV).astype(F32)
        ctx_v = cache_v[:, l].reshape(dec_batch, past_len, D_KV).astype(F32)
        z, _, _ = _layer(z, dec_batch, dec_seq, mod[l, 1:1 + dec_batch], layers[l], ctx_k, ctx_v,
                         _states_to_lanes(state_ssm[:, l]), rope, alpha)
    y_sample = z.reshape(dec_batch, dec_seq, d)
    return (y_prompt, y_sample, new_cache_k, new_cache_v, new_state_ssm)
```

```python
import functools
import math

import jax
import jax.numpy as jnp
import numpy as np
from jax import lax
from jax.experimental import pallas as pl
from jax.experimental.pallas import tpu as pltpu

F32 = jnp.float32
BF16 = jnp.bfloat16

D_MODEL = 2048
GRID_W = 64
D_SSM = 512
SSM_GROUP = 16
N_SSM_GROUPS = D_SSM // SSM_GROUP
SSM_STATE = 64
HEAD_DIM = 128
N_HEADS = 8
N_KV_HEADS = 2
Q_PER_KV = N_HEADS // N_KV_HEADS
D_ATTN = N_HEADS * HEAD_DIM
D_KV = N_KV_HEADS * HEAD_DIM
WINDOW = 128
ATTN_BLOCK = 128
ROPE_BASE = 10000.0
D_SGU = 512
SGU_CHUNK = 128
SGU_GROUP_CH = 128
N_SGU_GROUPS = D_SGU // SGU_GROUP_CH
N_BRANCH = 3
LN_EPS = 1e-5

D_MG = N_BRANCH * D_MODEL
COL_MG = 0
COL_XA = COL_MG + D_MG
COL_ZA = COL_XA + D_SSM
COL_Q = COL_ZA + D_SSM
COL_ZB = COL_Q + D_ATTN
COL_K = COL_ZB + D_ATTN
COL_V = COL_K + D_KV
COL_U = COL_V + D_KV
COL_VS = COL_U + D_SGU
COL_ZC = COL_VS + D_SGU
D_IN = COL_ZC + D_SGU

SSM_T = 16
SSM_NC = 16
SSM_TC = SSM_T * SSM_GROUP
SSM_SEG = SSM_T * SSM_NC
SSM_SMALL_ROWS = 40
SSM_LANE_GROUPS = 128 // SSM_GROUP

VMEM_LIMIT_BYTES = 56 * 1024 * 1024

INPROJ_TM = 1024
INPROJ_TN = 1024
ROW_TM = 512
MERGE_TM = 256


def _params(sem):
    return pltpu.CompilerParams(dimension_semantics=sem, vmem_limit_bytes=VMEM_LIMIT_BYTES)


def _layernorm_f32(x):
    mu = jnp.mean(x, axis=-1, keepdims=True)
    xc = x - mu
    var = jnp.mean(xc * xc, axis=-1, keepdims=True)
    return xc * lax.rsqrt(var + LN_EPS)


def _ada_kernel(cond_ref, w_ref, b_ref, o_ref):
    c = cond_ref[...]
    a = (c * jax.nn.sigmoid(c)).astype(BF16)
    o_ref[...] = jnp.dot(a, w_ref[...].astype(BF16), preferred_element_type=F32) + b_ref[...]


def _ada_mod(cond, w_ada, b_ada):
    depth, d, n = w_ada.shape
    r = cond.shape[0]
    tn = 512
    return pl.pallas_call(
        _ada_kernel,
        out_shape=jax.ShapeDtypeStruct((depth, r, n), F32),
        grid=(depth, n // tn),
        in_specs=[
            pl.BlockSpec((r, d), lambda l, j: (0, 0)),
            pl.BlockSpec((None, d, tn), lambda l, j: (l, 0, j)),
            pl.BlockSpec((None, 1, tn), lambda l, j: (l, 0, j)),
        ],
        out_specs=pl.BlockSpec((None, r, tn), lambda l, j: (l, 0, j)),
        compiler_params=_params(("arbitrary", "arbitrary")),
        name="ada_mod",
    )(cond, w_ada, b_ada.reshape(depth, 1, n))


def _inproj_kernel(x_ref, scale_ref, shift_ref, w_ref, proj_ref, kv_ref, xa_ref, h_scr, *, kv_tile, xa_tile):
    j = pl.program_id(1)

    @pl.when(j == 0)
    def _():
        h = _layernorm_f32(x_ref[...]) * (1.0 + scale_ref[0]) + shift_ref[0]
        h_scr[...] = h.astype(BF16)

    acc = jnp.dot(h_scr[...], w_ref[...], preferred_element_type=F32)
    proj_ref[...] = acc.astype(BF16)

    @pl.when(j == kv_tile)
    def _():
        kv_ref[...] = acc[:, 0:2 * D_KV]

    @pl.when(j == xa_tile)
    def _():
        xa_ref[...] = acc[:, 0:D_SSM]


def _inproj(x2, scale, shift, w_bf16, seq_len):
    t, d = x2.shape
    n = w_bf16.shape[1]
    tm, tn = INPROJ_TM, INPROJ_TN
    assert t % tm == 0 and n % tn == 0 and COL_K % tn == 0 and COL_XA % tn == 0
    assert 2 * D_KV <= tn and D_SSM <= tn
    nb = scale.shape[0]
    if nb == 1:
        mod_map = lambda i, j: (0, 0, 0)
    else:
        assert seq_len % tm == 0
        tiles_per_seq = seq_len // tm
        mod_map = lambda i, j: (i // tiles_per_seq, 0, 0)
    return pl.pallas_call(
        functools.partial(_inproj_kernel, kv_tile=COL_K // tn, xa_tile=COL_XA // tn),
        out_shape=(jax.ShapeDtypeStruct((t, n), BF16), jax.ShapeDtypeStruct((t, 2 * D_KV), F32),
                   jax.ShapeDtypeStruct((t, D_SSM), F32)),
        grid=(t // tm, n // tn),
        in_specs=[
            pl.BlockSpec((tm, d), lambda i, j: (i, 0)),
            pl.BlockSpec((1, 1, d), mod_map),
            pl.BlockSpec((1, 1, d), mod_map),
            pl.BlockSpec((d, tn), lambda i, j: (0, j)),
        ],
        out_specs=(
            pl.BlockSpec((tm, tn), lambda i, j: (i, j)),
            pl.BlockSpec((tm, 2 * D_KV), lambda i, j: (i, 0)),
            pl.BlockSpec((tm, D_SSM), lambda i, j: (i, 0)),
        ),
        scratch_shapes=[pltpu.VMEM((tm, d), BF16)],
        compiler_params=_params(("arbitrary", "arbitrary")),
        name="ln_inproj",
    )(x2, scale, shift, w_bf16)


def _ssm_tables(lam_re, lam_im, log_step, b_re, b_im, c_re, c_im):
    hp = lax.Precision.HIGHEST
    g_n, p_n, c_n, t_n = N_SSM_GROUPS, SSM_STATE, SSM_GROUP, SSM_T
    lr = lam_re.astype(F32)
    li = lam_im.astype(F32)
    dt = jnp.exp(log_step.astype(F32))[..., None]
    mag = jnp.exp(lr * dt)
    ar = mag * jnp.cos(li * dt)
    ai = mag * jnp.sin(li * dt)
    den = lr * lr + li * li
    nr = ar - 1.0
    f_re = (nr * lr + ai * li) / den
    f_im = (ai * lr - nr * li) / den
    br = b_re.astype(F32)
    bi = b_im.astype(F32)
    bb_re = f_re[..., None] * br - f_im[..., None] * bi
    bb_im = f_re[..., None] * bi + f_im[..., None] * br

    def cmul(xr, xi, yr, yi):
        return xr * yr - xi * yi, xr * yi + xi * yr

    pr, pi = [jnp.ones_like(ar)], [jnp.zeros_like(ai)]
    for _ in range(t_n):
        nr_, ni_ = cmul(pr[-1], pi[-1], ar, ai)
        pr.append(nr_)
        pi.append(ni_)
    pw_re = jnp.stack(pr, axis=0)
    pw_im = jnp.stack(pi, axis=0)
    cr = c_re.astype(F32)
    ci = c_im.astype(F32)
    cp_re = cr[None] * pw_re[:, :, :, None, :] - ci[None] * pw_im[:, :, :, None, :]
    cp_im = cr[None] * pw_im[:, :, :, None, :] + ci[None] * pw_re[:, :, :, None, :]
    taps = (jnp.einsum('kdgcp,dgpe->dgkce', cp_re[:t_n], bb_re, precision=hp)
            - jnp.einsum('kdgcp,dgpe->dgkce', cp_im[:t_n], bb_im, precision=hp))
    pad = jnp.zeros((g_n, t_n - 1, c_n, c_n), F32)
    band = (jnp.concatenate([pad, taps[0]], axis=1)
            + jnp.concatenate([taps[1][:, ::-1], pad], axis=1))
    toep = jnp.stack([band[:, t_n - 1 - j:2 * t_n - 1 - j] for j in range(t_n)], axis=1)
    w_intra = toep.transpose(0, 1, 4, 2, 3).reshape(g_n, t_n * c_n, t_n * c_n)

    idx_f = jnp.arange(t_n - 1, -1, -1)
    idx_b = jnp.arange(t_n)
    sf_re, sf_im = cmul(pw_re[idx_f, 0][..., None], pw_im[idx_f, 0][..., None], bb_re[0][None], bb_im[0][None])
    sb_re, sb_im = cmul(pw_re[idx_b, 1][..., None], pw_im[idx_b, 1][..., None], bb_re[1][None], bb_im[1][None])
    w_state = jnp.stack([sf_re, sb_re, sf_im, sb_im], axis=0)
    w_state = w_state.transpose(2, 1, 4, 0, 3).reshape(g_n, t_n * c_n, 4 * p_n)

    tf = jnp.arange(1, t_n + 1)
    tb = jnp.arange(t_n, 0, -1)
    w_enter = jnp.stack([cp_re[tf, 0], cp_re[tb, 1], -cp_im[tf, 0], -cp_im[tb, 1]], axis=0)
    w_enter = w_enter.transpose(2, 0, 4, 1, 3).reshape(g_n, 4 * p_n, t_n * c_n)

    a1r = jnp.concatenate([pw_re[t_n, 0], pw_re[t_n, 1]], axis=-1)
    a1i = jnp.concatenate([pw_im[t_n, 0], pw_im[t_n, 1]], axis=-1)
    qr, qi = [jnp.ones_like(a1r)], [jnp.zeros_like(a1i)]
    for _ in range(SSM_NC):
        nr_, ni_ = cmul(qr[-1], qi[-1], a1r, a1i)
        qr.append(nr_)
        qi.append(ni_)
    qr = jnp.stack(qr, axis=1)
    qi = jnp.stack(qi, axis=1)
    rev = jnp.arange(SSM_NC - 1, -1, -1)
    corr_re = jnp.concatenate([qr[:, :SSM_NC, :p_n], qr[:, rev, p_n:]], axis=-1)
    corr_im = jnp.concatenate([qi[:, :SSM_NC, :p_n], qi[:, rev, p_n:]], axis=-1)
    small = jnp.concatenate([
        corr_re, corr_im, a1r[:, None], a1i[:, None], qr[:, SSM_NC][:, None], qi[:, SSM_NC][:, None],
        jnp.zeros((g_n, SSM_SMALL_ROWS - 2 * SSM_NC - 4, 2 * p_n), F32)], axis=1)
    return w_intra.astype(BF16), w_state.astype(BF16), w_enter.astype(BF16), small


def _ssm_fold_matrix():
    n = SSM_T * SSM_LANE_GROUPS * SSM_GROUP
    src = np.arange(n)
    j = src // (SSM_LANE_GROUPS * SSM_GROUP)
    g = (src // SSM_GROUP) % SSM_LANE_GROUPS
    c = src % SSM_GROUP
    dst = g * SSM_TC + j * SSM_GROUP + c
    return jnp.asarray((dst[:, None] == np.arange(n)[None, :]).astype(BF16))


def _ssm_kernel(xa_ref, fold_ref, wi_ref, ws_ref, we_ref, small_ref, s0_ref, y_ref, fin_ref,
                v_scr, s_scr, yy_scr, fre_scr, fim_scr, efre_scr, efim_scr, ebre_scr, ebim_scr,
                *, n_seq, n_seg):
    p2 = 2 * SSM_STATE
    p4 = 4 * SSM_STATE
    nbp = n_seq * n_seg
    lanes = SSM_LANE_GROUPS * SSM_GROUP
    nt = (((1,), (1,)), ((), ()))

    blocks = []
    for c in range(SSM_NC):
        toks = [xa_ref[pl.ds(c * SSM_T + j, nbp, stride=SSM_SEG), :].astype(BF16) for j in range(SSM_T)]
        blocks.append(jnp.concatenate(toks, axis=1))
    xx = jnp.concatenate(blocks, axis=0)
    uu = jnp.dot(xx, fold_ref[...], preferred_element_type=F32).astype(BF16)

    lane = lax.broadcasted_iota(jnp.int32, (1, p2), 1)
    is_fwd = lane < SSM_STATE
    for g in range(SSM_LANE_GROUPS):
        gc = slice(g * SSM_TC, (g + 1) * SSM_TC)
        u = uu[:, gc]
        yy_scr[:, gc] = jnp.dot(u, wi_ref[g], preferred_element_type=F32)
        v_scr[:, gc] = jnp.dot(u, ws_ref[g], preferred_element_type=F32)

    for g in range(SSM_LANE_GROUPS):
        o = g * p4
        a1r = small_ref[g, 2 * SSM_NC:2 * SSM_NC + 1, :]
        a1i = small_ref[g, 2 * SSM_NC + 1:2 * SSM_NC + 2, :]
        asr = small_ref[g, 2 * SSM_NC + 2:2 * SSM_NC + 3, :]
        asi = small_ref[g, 2 * SSM_NC + 3:2 * SSM_NC + 4, :]

        st_re = jnp.zeros((nbp, p2), F32)
        st_im = jnp.zeros((nbp, p2), F32)
        for c in range(SSM_NC):
            cb = SSM_NC - 1 - c
            rf = slice(c * nbp, (c + 1) * nbp)
            rb = slice(cb * nbp, (cb + 1) * nbp)
            s_scr[rf, o:o + SSM_STATE] = st_re[:, 0:SSM_STATE]
            s_scr[rb, o + SSM_STATE:o + p2] = st_re[:, SSM_STATE:p2]
            s_scr[rf, o + p2:o + p2 + SSM_STATE] = st_im[:, 0:SSM_STATE]
            s_scr[rb, o + p2 + SSM_STATE:o + p4] = st_im[:, SSM_STATE:p2]
            v_re = jnp.where(is_fwd, v_scr[rf, o:o + p2], v_scr[rb, o:o + p2])
            v_im = jnp.where(is_fwd, v_scr[rf, o + p2:o + p4], v_scr[rb, o + p2:o + p4])
            st_re, st_im = (a1r * st_re - a1i * st_im + v_re, a1r * st_im + a1i * st_re + v_im)

        fre_scr[...] = st_re
        fim_scr[...] = st_im
        s0 = s0_ref[g]
        e_re = s0[:, 0:p2]
        e_im = s0[:, p2:p4]
        for k in range(n_seg):
            kb = n_seg - 1 - k
            rows_f = pl.ds(k, n_seq, stride=n_seg)
            rows_b = pl.ds(kb, n_seq, stride=n_seg)
            efre_scr[rows_f, :] = e_re
            efim_scr[rows_f, :] = e_im
            ebre_scr[rows_b, :] = e_re
            ebim_scr[rows_b, :] = e_im
            f_re = jnp.where(is_fwd, fre_scr[rows_f, :], fre_scr[rows_b, :])
            f_im = jnp.where(is_fwd, fim_scr[rows_f, :], fim_scr[rows_b, :])
            e_re, e_im = (asr * e_re - asi * e_im + f_re, asr * e_im + asi * e_re + f_im)
        fin_ref[g] = jnp.concatenate([e_re, e_im], axis=-1)

        en_re = jnp.where(is_fwd, efre_scr[...], ebre_scr[...])
        en_im = jnp.where(is_fwd, efim_scr[...], ebim_scr[...])
        for c in range(SSM_NC):
            rows = slice(c * nbp, (c + 1) * nbp)
            cr = small_ref[g, c:c + 1, :]
            ci = small_ref[g, SSM_NC + c:SSM_NC + c + 1, :]
            s_scr[rows, o:o + p2] = s_scr[rows, o:o + p2] + (cr * en_re - ci * en_im)
            s_scr[rows, o + p2:o + p4] = s_scr[rows, o + p2:o + p4] + (cr * en_im + ci * en_re)

    for g in range(SSM_LANE_GROUPS):
        gc = slice(g * SSM_TC, (g + 1) * SSM_TC)
        yy_scr[:, gc] += jnp.dot(s_scr[:, g * p4:(g + 1) * p4].astype(BF16), we_ref[g],
                                 preferred_element_type=F32)

    zz = lax.dot_general(yy_scr[...].astype(BF16), fold_ref[...], nt, preferred_element_type=F32)
    for c in range(SSM_NC):
        for j in range(SSM_T):
            y_ref[pl.ds(c * SSM_T + j, nbp, stride=SSM_SEG), :] = zz[c * nbp:(c + 1) * nbp,
                                                                      j * lanes:(j + 1) * lanes]


def _ssm_scan(xa, fold, w_intra, w_state, w_enter, small, s0, n_seq, n_seg):
    t = xa.shape[0]
    g_n = N_SSM_GROUPS
    lg = SSM_LANE_GROUPS
    lanes = lg * SSM_GROUP
    p4 = 4 * SSM_STATE
    nbp = n_seq * n_seg
    r = SSM_NC * nbp
    assert t == r * SSM_T and fold.shape == (SSM_T * lanes, lg * SSM_TC)
    const = lambda shape: pl.BlockSpec(shape, lambda i: (0,) * len(shape), pipeline_mode=pl.Buffered(1))
    return pl.pallas_call(
        functools.partial(_ssm_kernel, n_seq=n_seq, n_seg=n_seg),
        out_shape=(jax.ShapeDtypeStruct((t, D_SSM), F32), jax.ShapeDtypeStruct((g_n, n_seq, p4), F32)),
        grid=(g_n // lg,),
        in_specs=[
            pl.BlockSpec((t, lanes), lambda i: (0, i)),
            const(fold.shape),
            pl.BlockSpec((lg, SSM_TC, SSM_TC), lambda i: (i, 0, 0)),
            pl.BlockSpec((lg, SSM_TC, p4), lambda i: (i, 0, 0)),
            pl.BlockSpec((lg, p4, SSM_TC), lambda i: (i, 0, 0)),
            pl.BlockSpec((lg, SSM_SMALL_ROWS, 2 * SSM_STATE), lambda i: (i, 0, 0)),
            pl.BlockSpec((lg, n_seq, p4), lambda i: (i, 0, 0)),
        ],
        out_specs=(
            pl.BlockSpec((t, lanes), lambda i: (0, i)),
            pl.BlockSpec((lg, n_seq, p4), lambda i: (i, 0, 0)),
        ),
        scratch_shapes=[pltpu.VMEM((r, lg * p4), F32), pltpu.VMEM((r, lg * p4), F32),
                        pltpu.VMEM((r, lg * SSM_TC), F32)]
        + [pltpu.VMEM((nbp, 2 * SSM_STATE), F32)] * 6,
        compiler_params=_params(("arbitrary",)),
        name="ssm_scan",
    )(xa, fold, w_intra, w_state, w_enter, small, s0)


def _ssm_post_kernel(xa_ref, za_ref, ys_ref, d_ref, w_ref, b_ref, o_ref):
    y = xa_ref[...].astype(F32) * d_ref[...] + ys_ref[...]
    y = jax.nn.gelu(y)
    gl = jnp.dot(y.astype(BF16), w_ref[...], preferred_element_type=F32) + b_ref[...]
    y = y * jax.nn.sigmoid(gl)
    o_ref[...] = (y * jax.nn.silu(za_ref[...].astype(F32))).astype(BF16)


def _ssm_post(proj, yssm, d_skip, w_glu_bf16, b_glu):
    t = proj.shape[0]
    tm = ROW_TM
    return pl.pallas_call(
        _ssm_post_kernel,
        out_shape=jax.ShapeDtypeStruct((t, D_SSM), BF16),
        grid=(t // tm,),
        in_specs=[
            pl.BlockSpec((tm, D_SSM), lambda i: (i, COL_XA // D_SSM)),
            pl.BlockSpec((tm, D_SSM), lambda i: (i, COL_ZA // D_SSM)),
            pl.BlockSpec((tm, D_SSM), lambda i: (i, 0)),
            pl.BlockSpec((1, D_SSM), lambda i: (0, 0)),
            pl.BlockSpec((D_SSM, D_SSM), lambda i: (0, 0)),
            pl.BlockSpec((1, D_SSM), lambda i: (0, 0)),
        ],
        out_specs=pl.BlockSpec((tm, D_SSM), lambda i: (i, 0)),
        compiler_params=_params(("arbitrary",)),
        name="ssm_post",
    )(proj, proj, yssm, d_skip.reshape(1, D_SSM), w_glu_bf16, b_glu.reshape(1, D_SSM))


def _softmax_pv(s, sink, v_bf16):
    hd = v_bf16.shape[1]
    m = jnp.maximum(jnp.max(s, axis=-1, keepdims=True), sink)
    p = jnp.exp(s - m).astype(BF16)
    v_ones = jnp.concatenate([v_bf16, jnp.ones_like(v_bf16)], axis=1)
    o = jnp.dot(p, v_ones, preferred_element_type=F32)
    den = o[:, hd:] + jnp.exp(sink - m)
    return o[:, :hd] / den


def _sink_rows(sink_ref, kvh, rows):
    return jnp.concatenate(
        [jnp.full((rows, 1), sink_ref[kvh * Q_PER_KV + g], F32) for g in range(Q_PER_KV)], axis=0)


def _ctx_attn_kernel(sink_ref, q_ref, kv_ref, zb_ref, o_ref):
    scale = HEAD_DIM ** -0.5
    nt = (((1,), (1,)), ((), ()))
    n = q_ref.shape[0]
    for kvh in range(N_KV_HEADS):
        k = (kv_ref[:, kvh * HEAD_DIM:(kvh + 1) * HEAD_DIM] * scale).astype(BF16)
        v = kv_ref[:, D_KV + kvh * HEAD_DIM:D_KV + (kvh + 1) * HEAD_DIM].astype(BF16)
        q4 = jnp.concatenate([q_ref[:, (kvh * Q_PER_KV + g) * HEAD_DIM:(kvh * Q_PER_KV + g + 1) * HEAD_DIM]
                              for g in range(Q_PER_KV)], axis=0)
        s = lax.dot_general(q4, k, nt, preferred_element_type=F32)
        o = _softmax_pv(s, _sink_rows(sink_ref, kvh, n), v)
        for g in range(Q_PER_KV):
            h = kvh * Q_PER_KV + g
            zb = zb_ref[:, h * HEAD_DIM:(h + 1) * HEAD_DIM].astype(F32)
            o_ref[:, h * HEAD_DIM:(h + 1) * HEAD_DIM] = (o[g * n:(g + 1) * n] * jax.nn.silu(zb)).astype(BF16)


def _ctx_attention(proj, kv, sink, seq_len):
    t = proj.shape[0]
    return pl.pallas_call(
        _ctx_attn_kernel,
        out_shape=jax.ShapeDtypeStruct((t, D_ATTN), BF16),
        grid=(t // seq_len,),
        in_specs=[
            pl.BlockSpec(memory_space=pltpu.SMEM),
            pl.BlockSpec((seq_len, D_ATTN), lambda b: (b, COL_Q // D_ATTN)),
            pl.BlockSpec((seq_len, 2 * D_KV), lambda b: (b, 0)),
            pl.BlockSpec((seq_len, D_ATTN), lambda b: (b, COL_ZB // D_ATTN)),
        ],
        out_specs=pl.BlockSpec((seq_len, D_ATTN), lambda b: (b, 0)),
        compiler_params=_params(("arbitrary",)),
        name="ctx_attention",
    )(sink, proj, kv, proj)


def _rope_partner(x):
    lane = lax.broadcasted_iota(jnp.int32, x.shape, 1)
    quarter = HEAD_DIM // 4
    first = (lane % (2 * quarter)) < quarter
    return jnp.where(first, pltpu.roll(x, HEAD_DIM - quarter, 1), pltpu.roll(x, quarter, 1))


def _rope(x, tab_ref):
    return x * tab_ref[:, 0:HEAD_DIM] + _rope_partner(x) * tab_ref[:, HEAD_DIM:2 * HEAD_DIM]


def _rope_kv_kernel(kv_ref, tab_ref, o_ref):
    for kvh in range(N_KV_HEADS):
        ks = slice(kvh * HEAD_DIM, (kvh + 1) * HEAD_DIM)
        o_ref[:, ks] = _rope(kv_ref[:, ks], tab_ref).astype(BF16)
    o_ref[:, D_KV:2 * D_KV] = kv_ref[:, D_KV:2 * D_KV].astype(BF16)


def _rope_kv(kv, tab_k, seq_len):
    t = kv.shape[0]
    tm = ROW_TM
    assert seq_len % tm == 0
    tiles_per_seq = seq_len // tm
    return pl.pallas_call(
        _rope_kv_kernel,
        out_shape=jax.ShapeDtypeStruct((t, 2 * D_KV), BF16),
        grid=(t // tm,),
        in_specs=[
            pl.BlockSpec((tm, 2 * D_KV), lambda i: (i, 0)),
            pl.BlockSpec((tm, 2 * HEAD_DIM), lambda i: (i % tiles_per_seq, 0)),
        ],
        out_specs=pl.BlockSpec((tm, 2 * D_KV), lambda i: (i, 0)),
        compiler_params=_params(("arbitrary",)),
        name="rope_kv",
    )(kv, tab_k)


def _lat_attn_kernel(sink_ref, q_ref, kvp_ref, kvc_ref, kvn_ref, ck_ref, cv_ref, zb_ref,
                     tabq_ref, swap_ref, bias_ref, o_ref):
    nt = (((1,), (1,)), ((), ()))
    blk = ATTN_BLOCK
    bias = bias_ref[...]
    cos_q = tabq_ref[:, 0:HEAD_DIM]
    sin_q = tabq_ref[:, HEAD_DIM:2 * HEAD_DIM]
    for kvh in range(N_KV_HEADS):
        ks = slice(kvh * HEAD_DIM, (kvh + 1) * HEAD_DIM)
        vs = slice(D_KV + kvh * HEAD_DIM, D_KV + (kvh + 1) * HEAD_DIM)
        k_loc = jnp.concatenate([kvp_ref[:, ks], kvc_ref[:, ks], kvn_ref[:, ks]], axis=0)
        v_all = jnp.concatenate([kvp_ref[:, vs], kvc_ref[:, vs], kvn_ref[:, vs],
                                 cv_ref[:, ks].astype(BF16)], axis=0)
        k_ctx = ck_ref[:, ks].astype(BF16)
        qx = jnp.concatenate(
            [q_ref[:, (kvh * Q_PER_KV + g) * HEAD_DIM:(kvh * Q_PER_KV + g + 1) * HEAD_DIM]
             for g in range(Q_PER_KV)], axis=0)
        partner = jnp.dot(qx, swap_ref[...], preferred_element_type=F32)
        qf = qx.astype(F32).reshape(Q_PER_KV, blk, HEAD_DIM)
        q4 = (qf * cos_q[None] + partner.reshape(Q_PER_KV, blk, HEAD_DIM) * sin_q[None])
        q4 = q4.reshape(Q_PER_KV * blk, HEAD_DIM).astype(BF16)
        s_loc = lax.dot_general(q4, k_loc, nt, preferred_element_type=F32)
        s_loc = (s_loc.reshape(Q_PER_KV, blk, 3 * blk) + bias[None]).reshape(Q_PER_KV * blk, 3 * blk)
        s_ctx = lax.dot_general(q4, k_ctx, nt, preferred_element_type=F32)
        s = jnp.concatenate([s_loc, s_ctx], axis=-1)
        o = _softmax_pv(s, _sink_rows(sink_ref, kvh, blk), v_all)
        for g in range(Q_PER_KV):
            h = kvh * Q_PER_KV + g
            zb = zb_ref[:, h * HEAD_DIM:(h + 1) * HEAD_DIM].astype(F32)
            o_ref[:, h * HEAD_DIM:(h + 1) * HEAD_DIM] = (o[g * blk:(g + 1) * blk] * jax.nn.silu(zb)).astype(BF16)


def _lat_attention(proj, kv, ctx_k, ctx_v, sink, tab_q, tab_k, bias, swap, n_seq, seq_len):
    kv = _rope_kv(kv, tab_k, seq_len)
    t = proj.shape[0]
    blk = ATTN_BLOCK
    nb = seq_len // blk
    lc = ctx_k.shape[1]

    def row(b, i):
        return b * nb + i

    def prev(i):
        return jnp.maximum(i - 1, 0)

    def nxt(i):
        return jnp.minimum(i + 1, nb - 1)

    def bias_variant(b, i):
        return ((i > 0).astype(jnp.int32) * 2 + (i < nb - 1).astype(jnp.int32), 0, 0)

    return pl.pallas_call(
        _lat_attn_kernel,
        out_shape=jax.ShapeDtypeStruct((t, D_ATTN), BF16),
        grid=(n_seq, nb),
        in_specs=[
            pl.BlockSpec(memory_space=pltpu.SMEM),
            pl.BlockSpec((blk, D_ATTN), lambda b, i: (row(b, i), COL_Q // D_ATTN)),
            pl.BlockSpec((blk, 2 * D_KV), lambda b, i: (row(b, prev(i)), 0)),
            pl.BlockSpec((blk, 2 * D_KV), lambda b, i: (row(b, i), 0)),
            pl.BlockSpec((blk, 2 * D_KV), lambda b, i: (row(b, nxt(i)), 0)),
            pl.BlockSpec((None, lc, D_KV), lambda b, i: (b, 0, 0)),
            pl.BlockSpec((None, lc, D_KV), lambda b, i: (b, 0, 0)),
            pl.BlockSpec((blk, D_ATTN), lambda b, i: (row(b, i), COL_ZB // D_ATTN)),
            pl.BlockSpec((blk, 2 * HEAD_DIM), lambda b, i: (i, 0)),
            pl.BlockSpec((HEAD_DIM, HEAD_DIM), lambda b, i: (0, 0)),
            pl.BlockSpec((None, blk, 3 * blk), bias_variant),
        ],
        out_specs=pl.BlockSpec((blk, D_ATTN), lambda b, i: (row(b, i), 0)),
        compiler_params=_params(("arbitrary", "arbitrary")),
        name="lat_attention",
    )(sink, proj, kv, kv, kv, ctx_k, ctx_v, proj, tab_q, swap, bias)


def _rope_tables(seq_len):
    rows = seq_len // GRID_W
    row = np.repeat(np.arange(rows), GRID_W).astype(np.float64)
    col = np.tile(np.arange(GRID_W), rows).astype(np.float64)
    half = HEAD_DIM // 2
    inv = ROPE_BASE ** (-np.arange(0, half, 2, dtype=np.float64) / half)
    ang_r = row[:, None] * inv[None, :]
    ang_c = col[:, None] * inv[None, :]
    cos_t = np.concatenate([np.cos(ang_r)] * 2 + [np.cos(ang_c)] * 2, axis=-1)
    sin_t = np.concatenate([-np.sin(ang_r), np.sin(ang_r), -np.sin(ang_c), np.sin(ang_c)], axis=-1)
    tab_k = np.concatenate([cos_t, sin_t], axis=-1)
    tab_q = tab_k * HEAD_DIM ** -0.5
    return jnp.asarray(tab_q.astype(np.float32)), jnp.asarray(tab_k.astype(np.float32))


def _rope_swap_matrix():
    quarter = HEAD_DIM // 4
    d = np.arange(HEAD_DIM)
    src = np.where(d % (2 * quarter) < quarter, d + quarter, d - quarter)
    return jnp.asarray((np.arange(HEAD_DIM)[:, None] == src[None, :]).astype(BF16))


def _window_bias():
    blk = ATTN_BLOCK
    r = np.arange(blk)[:, None]
    s = np.arange(3 * blk)[None, :]
    band = np.abs(s - blk - r) <= WINDOW
    out = []
    for has_prev in (False, True):
        for has_next in (False, True):
            ok = band & ((s >= blk) | has_prev) & ((s < 2 * blk) | has_next)
            out.append(np.where(ok, 0.0, -np.inf).astype(np.float32))
    return jnp.asarray(np.stack(out, axis=0))


def _sgu_kernel(u_ref, v_ref, z_ref, g_ref, b_ref, ws_ref, bs_ref, o_ref, *, n_chunks):
    for n in range(n_chunks):
        rows = slice(n * SGU_CHUNK, (n + 1) * SGU_CHUNK)
        v = _layernorm_f32(jax.nn.gelu(v_ref[rows, :].astype(F32))) * g_ref[...] + b_ref[...]
        v = v.astype(BF16)
        for g in range(N_SGU_GROUPS):
            cols = slice(g * SGU_GROUP_CH, (g + 1) * SGU_GROUP_CH)
            vm = jnp.dot(ws_ref[g], v[:, cols], preferred_element_type=F32) + bs_ref[:, g:g + 1]
            u = jax.nn.gelu(u_ref[rows, cols].astype(F32))
            z = z_ref[rows, cols].astype(F32)
            o_ref[rows, cols] = (u * vm * jax.nn.silu(z)).astype(BF16)


def _sgu(proj, ln_g, ln_b, w_s_bf16, b_s_t):
    t = proj.shape[0]
    tm = ROW_TM
    return pl.pallas_call(
        functools.partial(_sgu_kernel, n_chunks=tm // SGU_CHUNK),
        out_shape=jax.ShapeDtypeStruct((t, D_SGU), BF16),
        grid=(t // tm,),
        in_specs=[
            pl.BlockSpec((tm, D_SGU), lambda i: (i, COL_U // D_SGU)),
            pl.BlockSpec((tm, D_SGU), lambda i: (i, COL_VS // D_SGU)),
            pl.BlockSpec((tm, D_SGU), lambda i: (i, COL_ZC // D_SGU)),
            pl.BlockSpec((1, D_SGU), lambda i: (0, 0)),
            pl.BlockSpec((1, D_SGU), lambda i: (0, 0)),
            pl.BlockSpec((N_SGU_GROUPS, SGU_CHUNK, SGU_CHUNK), lambda i: (0, 0, 0)),
            pl.BlockSpec((SGU_CHUNK, N_SGU_GROUPS), lambda i: (0, 0)),
        ],
        out_specs=pl.BlockSpec((tm, D_SGU), lambda i: (i, 0)),
        compiler_params=_params(("arbitrary",)),
        name="sgu",
    )(proj, proj, proj, ln_g.reshape(1, D_SGU), ln_b.reshape(1, D_SGU), w_s_bf16, b_s_t)


def _merge_kernel(mg_ref, ya_ref, yb_ref, yc_ref, x_ref, gate_ref, wpa_ref, wpb_ref, wpc_ref, wout_ref,
                  lng_ref, lnb_ref, y_ref, *, alpha):
    d = D_MODEL

    def branch(k, y_r, w_r):
        gate = jax.nn.sigmoid(mg_ref[:, k * d:(k + 1) * d].astype(F32))
        return gate * jnp.dot(y_r[...], w_r[...], preferred_element_type=F32)

    merged = branch(0, ya_ref, wpa_ref) + branch(1, yb_ref, wpb_ref) + branch(2, yc_ref, wpc_ref)
    out = jnp.dot(merged.astype(BF16), wout_ref[...], preferred_element_type=F32)
    z = alpha * x_ref[...] + gate_ref[0] * out
    y_ref[...] = _layernorm_f32(z) * lng_ref[...] + lnb_ref[...]


def _merge(proj, ya, yb, yc, x2, gate, wpa, wpb, wpc, wout, ln_g, ln_b, seq_len, alpha):
    t, d = x2.shape
    tm = MERGE_TM
    if gate.shape[0] == 1:
        mod_map = lambda i: (0, 0, 0)
    else:
        assert seq_len % tm == 0
        tiles_per_seq = seq_len // tm
        mod_map = lambda i: (i // tiles_per_seq, 0, 0)

    def const(shape):
        return pl.BlockSpec(shape, lambda i: (0,) * len(shape), pipeline_mode=pl.Buffered(1))

    return pl.pallas_call(
        functools.partial(_merge_kernel, alpha=alpha),
        out_shape=jax.ShapeDtypeStruct((t, d), F32),
        grid=(t // tm,),
        in_specs=[
            pl.BlockSpec((tm, D_MG), lambda i: (i, 0)),
            pl.BlockSpec((tm, D_SSM), lambda i: (i, 0)),
            pl.BlockSpec((tm, D_ATTN), lambda i: (i, 0)),
            pl.BlockSpec((tm, D_SGU), lambda i: (i, 0)),
            pl.BlockSpec((tm, d), lambda i: (i, 0)),
            pl.BlockSpec((1, 1, d), mod_map),
            const((D_SSM, d)), const((D_ATTN, d)), const((D_SGU, d)), const((d, d)),
            const((1, d)), const((1, d)),
        ],
        out_specs=pl.BlockSpec((tm, d), lambda i: (i, 0)),
        compiler_params=_params(("arbitrary",)),
        name="merge_out",
    )(proj, ya, yb, yc, x2, gate, wpa, wpb, wpc, wout, ln_g.reshape(1, d), ln_b.reshape(1, d))


def _layer(x2, n_seq, seq_len, mod, lp, ctx_k, ctx_v, s0, rope, alpha):
    d = D_MODEL
    nbm = mod.shape[0]
    shift = mod[:, 0:d].reshape(nbm, 1, d)
    scale = mod[:, d:2 * d].reshape(nbm, 1, d)
    gate = mod[:, 2 * d:3 * d].reshape(nbm, 1, d)
    proj, kv, xa = _inproj(x2, scale, shift, lp['w_in'], seq_len)

    n_seg = seq_len // SSM_SEG
    yssm, fin = _ssm_scan(xa, lp['fold'], lp['w_intra'], lp['w_state'], lp['w_enter'], lp['ssm_small'],
                          s0, n_seq, n_seg)
    ya = _ssm_post(proj, yssm, lp['d_skip'], lp['w_glu'], lp['b_glu'])

    if ctx_k is None:
        yb = _ctx_attention(proj, kv, lp['sink'], seq_len)
    else:
        yb = _lat_attention(proj, kv, ctx_k, ctx_v, lp['sink'], *rope, n_seq, seq_len)

    yc = _sgu(proj, lp['sgu_g'], lp['sgu_b'], lp['w_s'], lp['b_s_t'])

    y = _merge(proj, ya, yb, yc, x2, gate, lp['w_pa'], lp['w_pb'], lp['w_pc'], lp['w_out'],
               lp['ln_g'], lp['ln_b'], seq_len, alpha)
    return y, kv, fin


def _states_to_lanes(s):
    b = s.shape[0]
    return s.astype(F32).transpose(3, 0, 2, 1, 4).reshape(N_SSM_GROUPS, b, 4 * SSM_STATE)


def _lanes_to_states(f):
    b = f.shape[1]
    return f.reshape(N_SSM_GROUPS, b, 2, 2, SSM_STATE).transpose(1, 3, 2, 0, 4)


def kernel(x_prompt, x_sample, cache_k, cache_v, state_ssm, c, c_ctx, w_ada, b_ada, w_in, ssm_lam_re, ssm_lam_im, ssm_log_step, ssm_b_re, ssm_b_im, ssm_c_re, ssm_c_im, ssm_d, w_glu, b_glu, attn_sink, sgu_ln_g, sgu_ln_b, w_spatial, b_spatial, w_proj_a, w_proj_b, w_proj_c, w_out, ln_g, ln_b):
    depth = w_in.shape[0]
    batch, seq, d = x_prompt.shape
    dec_batch, dec_seq, _ = x_sample.shape
    past_len = cache_k.shape[2]
    alpha = (2 * depth) ** 0.25

    n_cond = 1 + dec_batch
    cond_rows = -(-n_cond // 8) * 8
    cond = jnp.concatenate([c_ctx[None, :], c, jnp.zeros((cond_rows - n_cond, d), F32)], axis=0)
    mod = _ada_mod(cond, w_ada, b_ada)

    def split_w_in(w):
        sizes = (D_SSM, D_SSM, D_ATTN, D_KV, D_KV, D_ATTN, D_SGU, D_SGU, D_SGU, D_MG)
        offs = [0]
        for s in sizes:
            offs.append(offs[-1] + s)
        xa, za, q, k, v, zb, u, vs, zc, mg = [w[:, offs[i]:offs[i + 1]] for i in range(len(sizes))]
        return jnp.concatenate([mg, xa, za, q, zb, k, v, u, vs, zc], axis=1).astype(BF16)

    fold = _ssm_fold_matrix()
    layers = []
    for l in range(depth):
        w_intra, w_state, w_enter, small = _ssm_tables(
            ssm_lam_re[l], ssm_lam_im[l], ssm_log_step[l], ssm_b_re[l], ssm_b_im[l], ssm_c_re[l], ssm_c_im[l])
        layers.append({
            'w_in': split_w_in(w_in[l]),
            'fold': fold, 'w_intra': w_intra, 'w_state': w_state, 'w_enter': w_enter, 'ssm_small': small,
            'd_skip': ssm_d[l], 'w_glu': w_glu[l].astype(BF16), 'b_glu': b_glu[l],
            'sink': attn_sink[l].astype(F32),
            'sgu_g': sgu_ln_g[l], 'sgu_b': sgu_ln_b[l],
            'w_s': w_spatial[l].astype(BF16), 'b_s_t': b_spatial[l].T.astype(F32),
            'w_pa': w_proj_a[l].astype(BF16), 'w_pb': w_proj_b[l].astype(BF16),
            'w_pc': w_proj_c[l].astype(BF16), 'w_out': w_out[l].astype(BF16),
            'ln_g': ln_g[l], 'ln_b': ln_b[l],
        })

    h = x_prompt.reshape(batch * seq, d)
    zero_state = jnp.zeros((N_SSM_GROUPS, batch, 4 * SSM_STATE), F32)
    ks, vs, ss = [], [], []
    for l in range(depth):
        h, kv, fin = _layer(h, batch, seq, mod[l, 0:1], layers[l], None, None, zero_state, None, alpha)
        ks.append(kv[:, :D_KV].reshape(batch, seq, N_KV_HEADS, HEAD_DIM))
        vs.append(kv[:, D_KV:].reshape(batch, seq, N_KV_HEADS, HEAD_DIM))
        ss.append(_lanes_to_states(fin))
    y_prompt = h.reshape(batch, seq, d)
    new_cache_k = jnp.stack(ks, axis=1)
    new_cache_v = jnp.stack(vs, axis=1)
    new_state_ssm = jnp.stack(ss, axis=1)

    rope = _rope_tables(dec_seq) + (_window_bias(), _rope_swap_matrix())
    z = x_sample.reshape(dec_batch * dec_seq, d)
    for l in range(depth):
        ctx_k = cache_k[:, l].reshape(dec_batch, past_len, D_KV).astype(F32)
        ctx_v = cache_v[:, l].reshape(dec_batch, past_len, D_KV).astype(F32)
        z, _, _ = _layer(z, dec_batch, dec_seq, mod[l, 1:1 + dec_batch], layers[l], ctx_k, ctx_v,
                         _states_to_lanes(state_ssm[:, l]), rope, alpha)
    y_sample = z.reshape(dec_batch, dec_seq, d)
    return (y_prompt, y_sample, new_cache_k, new_cache_v, new_state_ssm)
```

```python
import functools
import math

import jax
import jax.numpy as jnp
import numpy as np
from jax import lax
from jax.experimental import pallas as pl
from jax.experimental.pallas import tpu as pltpu

F32 = jnp.float32
BF16 = jnp.bfloat16

D_MODEL = 2048
GRID_W = 64
D_SSM = 512
SSM_GROUP = 16
N_SSM_GROUPS = D_SSM // SSM_GROUP
SSM_STATE = 64
HEAD_DIM = 128
N_HEADS = 8
N_KV_HEADS = 2
Q_PER_KV = N_HEADS // N_KV_HEADS
D_ATTN = N_HEADS * HEAD_DIM
D_KV = N_KV_HEADS * HEAD_DIM
WINDOW = 128
ATTN_BLOCK = 128
ROPE_BASE = 10000.0
D_SGU = 512
SGU_CHUNK = 128
SGU_GROUP_CH = 128
N_SGU_GROUPS = D_SGU // SGU_GROUP_CH
N_BRANCH = 3
LN_EPS = 1e-5

D_MG = N_BRANCH * D_MODEL
COL_MG = 0
COL_XA = COL_MG + D_MG
COL_ZA = COL_XA + D_SSM
COL_Q = COL_ZA + D_SSM
COL_ZB = COL_Q + D_ATTN
COL_K = COL_ZB + D_ATTN
COL_V = COL_K + D_KV
COL_U = COL_V + D_KV
COL_VS = COL_U + D_SGU
COL_ZC = COL_VS + D_SGU
D_IN = COL_ZC + D_SGU

SSM_T = 16
SSM_NC = 16
SSM_TC = SSM_T * SSM_GROUP
SSM_SEG = SSM_T * SSM_NC
SSM_SMALL_ROWS = 40
SSM_LANE_GROUPS = 128 // SSM_GROUP

VMEM_LIMIT_BYTES = 56 * 1024 * 1024

INPROJ_TM = 1024
INPROJ_TN = 1024
ROW_TM = 512
MERGE_TM = 256


def _params(sem):
    return pltpu.CompilerParams(dimension_semantics=sem, vmem_limit_bytes=VMEM_LIMIT_BYTES)


def _layernorm_f32(x):
    mu = jnp.mean(x, axis=-1, keepdims=True)
    xc = x - mu
    var = jnp.mean(xc * xc, axis=-1, keepdims=True)
    return xc * lax.rsqrt(var + LN_EPS)


def _ada_kernel(cond_ref, w_ref, b_ref, o_ref):
    c = cond_ref[...]
    a = (c * jax.nn.sigmoid(c)).astype(BF16)
    o_ref[...] = jnp.dot(a, w_ref[...].astype(BF16), preferred_element_type=F32) + b_ref[...]


def _ada_mod(cond, w_ada, b_ada):
    depth, d, n = w_ada.shape
    r = cond.shape[0]
    tn = 512
    return pl.pallas_call(
        _ada_kernel,
        out_shape=jax.ShapeDtypeStruct((depth, r, n), F32),
        grid=(depth, n // tn),
        in_specs=[
            pl.BlockSpec((r, d), lambda l, j: (0, 0)),
            pl.BlockSpec((None, d, tn), lambda l, j: (l, 0, j)),
            pl.BlockSpec((None, 1, tn), lambda l, j: (l, 0, j)),
        ],
        out_specs=pl.BlockSpec((None, r, tn), lambda l, j: (l, 0, j)),
        compiler_params=_params(("arbitrary", "arbitrary")),
        name="ada_mod",
    )(cond, w_ada, b_ada.reshape(depth, 1, n))


def _inproj_kernel(x_ref, scale_ref, shift_ref, w_ref, proj_ref, kv_ref, xa_ref, h_scr, *, kv_tile, xa_tile):
    j = pl.program_id(1)

    @pl.when(j == 0)
    def _():
        h = _layernorm_f32(x_ref[...]) * (1.0 + scale_ref[0]) + shift_ref[0]
        h_scr[...] = h.astype(BF16)

    acc = jnp.dot(h_scr[...], w_ref[...], preferred_element_type=F32)
    proj_ref[...] = acc.astype(BF16)

    @pl.when(j == kv_tile)
    def _():
        kv_ref[...] = acc[:, 0:2 * D_KV]

    @pl.when(j == xa_tile)
    def _():
        xa_ref[...] = acc[:, 0:D_SSM]


def _inproj(x2, scale, shift, w_bf16, seq_len):
    t, d = x2.shape
    n = w_bf16.shape[1]
    tm, tn = INPROJ_TM, INPROJ_TN
    assert t % tm == 0 and n % tn == 0 and COL_K % tn == 0 and COL_XA % tn == 0
    assert 2 * D_KV <= tn and D_SSM <= tn
    nb = scale.shape[0]
    if nb == 1:
        mod_map = lambda i, j: (0, 0, 0)
    else:
        assert seq_len % tm == 0
        tiles_per_seq = seq_len // tm
        mod_map = lambda i, j: (i // tiles_per_seq, 0, 0)
    return pl.pallas_call(
        functools.partial(_inproj_kernel, kv_tile=COL_K // tn, xa_tile=COL_XA // tn),
        out_shape=(jax.ShapeDtypeStruct((t, n), BF16), jax.ShapeDtypeStruct((t, 2 * D_KV), F32),
                   jax.ShapeDtypeStruct((t, D_SSM), F32)),
        grid=(t // tm, n // tn),
        in_specs=[
            pl.BlockSpec((tm, d), lambda i, j: (i, 0)),
            pl.BlockSpec((1, 1, d), mod_map),
            pl.BlockSpec((1, 1, d), mod_map),
            pl.BlockSpec((d, tn), lambda i, j: (0, j)),
        ],
        out_specs=(
            pl.BlockSpec((tm, tn), lambda i, j: (i, j)),
            pl.BlockSpec((tm, 2 * D_KV), lambda i, j: (i, 0)),
            pl.BlockSpec((tm, D_SSM), lambda i, j: (i, 0)),
        ),
        scratch_shapes=[pltpu.VMEM((tm, d), BF16)],
        compiler_params=_params(("arbitrary", "arbitrary")),
        name="ln_inproj",
    )(x2, scale, shift, w_bf16)


def _ssm_tables(lam_re, lam_im, log_step, b_re, b_im, c_re, c_im):
    hp = lax.Precision.HIGHEST
    g_n, p_n, c_n, t_n = N_SSM_GROUPS, SSM_STATE, SSM_GROUP, SSM_T
    lr = lam_re.astype(F32)
    li = lam_im.astype(F32)
    dt = jnp.exp(log_step.astype(F32))[..., None]
    mag = jnp.exp(lr * dt)
    ar = mag * jnp.cos(li * dt)
    ai = mag * jnp.sin(li * dt)
    den = lr * lr + li * li
    nr = ar - 1.0
    f_re = (nr * lr + ai * li) / den
    f_im = (ai * lr - nr * li) / den
    br = b_re.astype(F32)
    bi = b_im.astype(F32)
    bb_re = f_re[..., None] * br - f_im[..., None] * bi
    bb_im = f_re[..., None] * bi + f_im[..., None] * br

    def cmul(xr, xi, yr, yi):
        return xr * yr - xi * yi, xr * yi + xi * yr

    pr, pi = [jnp.ones_like(ar)], [jnp.zeros_like(ai)]
    for _ in range(t_n):
        nr_, ni_ = cmul(pr[-1], pi[-1], ar, ai)
        pr.append(nr_)
        pi.append(ni_)
    pw_re = jnp.stack(pr, axis=-1)
    pw_im = jnp.stack(pi, axis=-1)
    ct_re = c_re.astype(F32).transpose(0, 1, 3, 2)
    ct_im = c_im.astype(F32).transpose(0, 1, 3, 2)
    cp_re, cp_im = cmul(ct_re[:, :, :, None, :], ct_im[:, :, :, None, :],
                        pw_re[..., None], pw_im[..., None])
    taps = (jnp.einsum('dgpkc,dgpe->dgekc', cp_re[:, :, :, :t_n], bb_re, precision=hp)
            - jnp.einsum('dgpkc,dgpe->dgekc', cp_im[:, :, :, :t_n], bb_im, precision=hp))
    pad = jnp.zeros((g_n, c_n, t_n - 1, c_n), F32)
    band = (jnp.concatenate([pad, taps[0]], axis=2)
            + jnp.concatenate([taps[1][:, :, ::-1], pad], axis=2))
    band = band.reshape(g_n, c_n, (2 * t_n - 1) * c_n)
    w_intra = jnp.stack([band[:, :, (t_n - 1 - j) * c_n:(2 * t_n - 1 - j) * c_n] for j in range(t_n)],
                        axis=1).reshape(g_n, t_n * c_n, t_n * c_n)

    bt_re = bb_re.transpose(0, 1, 3, 2)
    bt_im = bb_im.transpose(0, 1, 3, 2)
    pj_re = pw_re.transpose(0, 1, 3, 2)[:, :, :t_n]
    pj_im = pw_im.transpose(0, 1, 3, 2)[:, :, :t_n]
    sf_re, sf_im = cmul(pj_re[0][:, ::-1, None, :], pj_im[0][:, ::-1, None, :], bt_re[0][:, None], bt_im[0][:, None])
    sb_re, sb_im = cmul(pj_re[1][:, :, None, :], pj_im[1][:, :, None, :], bt_re[1][:, None], bt_im[1][:, None])
    w_state = jnp.stack([sf_re, sb_re, sf_im, sb_im], axis=3)
    w_state = w_state.reshape(g_n, t_n * c_n, 4 * p_n)

    w_enter = jnp.stack([cp_re[0][:, :, 1:], cp_re[1][:, :, :0:-1], -cp_im[0][:, :, 1:], -cp_im[1][:, :, :0:-1]],
                        axis=1)
    w_enter = w_enter.reshape(g_n, 4 * p_n, t_n * c_n)

    a1r = jnp.concatenate([pw_re[0, :, :, t_n], pw_re[1, :, :, t_n]], axis=-1)
    a1i = jnp.concatenate([pw_im[0, :, :, t_n], pw_im[1, :, :, t_n]], axis=-1)
    qr, qi = [jnp.ones_like(a1r)], [jnp.zeros_like(a1i)]
    for _ in range(SSM_NC):
        nr_, ni_ = cmul(qr[-1], qi[-1], a1r, a1i)
        qr.append(nr_)
        qi.append(ni_)
    qr = jnp.stack(qr, axis=1)
    qi = jnp.stack(qi, axis=1)
    rev = jnp.arange(SSM_NC - 1, -1, -1)
    corr_re = jnp.concatenate([qr[:, :SSM_NC, :p_n], qr[:, rev, p_n:]], axis=-1)
    corr_im = jnp.concatenate([qi[:, :SSM_NC, :p_n], qi[:, rev, p_n:]], axis=-1)
    small = jnp.concatenate([
        corr_re, corr_im, a1r[:, None], a1i[:, None], qr[:, SSM_NC][:, None], qi[:, SSM_NC][:, None],
        jnp.zeros((g_n, SSM_SMALL_ROWS - 2 * SSM_NC - 4, 2 * p_n), F32)], axis=1)
    return w_intra.astype(BF16), w_state.astype(BF16), w_enter.astype(BF16), small


def _ssm_fold_matrix():
    n = SSM_T * SSM_LANE_GROUPS * SSM_GROUP
    src = np.arange(n)
    j = src // (SSM_LANE_GROUPS * SSM_GROUP)
    g = (src // SSM_GROUP) % SSM_LANE_GROUPS
    c = src % SSM_GROUP
    dst = g * SSM_TC + j * SSM_GROUP + c
    return jnp.asarray((dst[:, None] == np.arange(n)[None, :]).astype(BF16))


def _ssm_kernel(xa_ref, fold_ref, wi_ref, ws_ref, we_ref, small_ref, s0_ref, y_ref, fin_ref,
                v_scr, s_scr, yy_scr, fre_scr, fim_scr, efre_scr, efim_scr, ebre_scr, ebim_scr,
                *, n_seq, n_seg):
    p2 = 2 * SSM_STATE
    p4 = 4 * SSM_STATE
    nbp = n_seq * n_seg
    lanes = SSM_LANE_GROUPS * SSM_GROUP
    nt = (((1,), (1,)), ((), ()))

    blocks = []
    for c in range(SSM_NC):
        toks = [xa_ref[pl.ds(c * SSM_T + j, nbp, stride=SSM_SEG), :].astype(BF16) for j in range(SSM_T)]
        blocks.append(jnp.concatenate(toks, axis=1))
    xx = jnp.concatenate(blocks, axis=0)
    uu = jnp.dot(xx, fold_ref[...], preferred_element_type=F32).astype(BF16)

    lane = lax.broadcasted_iota(jnp.int32, (1, p2), 1)
    is_fwd = lane < SSM_STATE
    for g in range(SSM_LANE_GROUPS):
        gc = slice(g * SSM_TC, (g + 1) * SSM_TC)
        u = uu[:, gc]
        yy_scr[:, gc] = jnp.dot(u, wi_ref[g], preferred_element_type=F32)
        v_scr[:, gc] = jnp.dot(u, ws_ref[g], preferred_element_type=F32)

    for g in range(SSM_LANE_GROUPS):
        o = g * p4
        a1r = small_ref[g, 2 * SSM_NC:2 * SSM_NC + 1, :]
        a1i = small_ref[g, 2 * SSM_NC + 1:2 * SSM_NC + 2, :]
        asr = small_ref[g, 2 * SSM_NC + 2:2 * SSM_NC + 3, :]
        asi = small_ref[g, 2 * SSM_NC + 3:2 * SSM_NC + 4, :]

        st_re = jnp.zeros((nbp, p2), F32)
        st_im = jnp.zeros((nbp, p2), F32)
        for c in range(SSM_NC):
            cb = SSM_NC - 1 - c
            rf = slice(c * nbp, (c + 1) * nbp)
            rb = slice(cb * nbp, (cb + 1) * nbp)
            s_scr[rf, o:o + SSM_STATE] = st_re[:, 0:SSM_STATE]
            s_scr[rb, o + SSM_STATE:o + p2] = st_re[:, SSM_STATE:p2]
            s_scr[rf, o + p2:o + p2 + SSM_STATE] = st_im[:, 0:SSM_STATE]
            s_scr[rb, o + p2 + SSM_STATE:o + p4] = st_im[:, SSM_STATE:p2]
            v_re = jnp.where(is_fwd, v_scr[rf, o:o + p2], v_scr[rb, o:o + p2])
            v_im = jnp.where(is_fwd, v_scr[rf, o + p2:o + p4], v_scr[rb, o + p2:o + p4])
            st_re, st_im = (a1r * st_re - a1i * st_im + v_re, a1r * st_im + a1i * st_re + v_im)

        fre_scr[...] = st_re
        fim_scr[...] = st_im
        s0 = s0_ref[g]
        e_re = s0[:, 0:p2]
        e_im = s0[:, p2:p4]
        for k in range(n_seg):
            kb = n_seg - 1 - k
            rows_f = pl.ds(k, n_seq, stride=n_seg)
            rows_b = pl.ds(kb, n_seq, stride=n_seg)
            efre_scr[rows_f, :] = e_re
            efim_scr[rows_f, :] = e_im
            ebre_scr[rows_b, :] = e_re
            ebim_scr[rows_b, :] = e_im
            f_re = jnp.where(is_fwd, fre_scr[rows_f, :], fre_scr[rows_b, :])
            f_im = jnp.where(is_fwd, fim_scr[rows_f, :], fim_scr[rows_b, :])
            e_re, e_im = (asr * e_re - asi * e_im + f_re, asr * e_im + asi * e_re + f_im)
        fin_ref[g] = jnp.concatenate([e_re, e_im], axis=-1)

        en_re = jnp.where(is_fwd, efre_scr[...], ebre_scr[...])
        en_im = jnp.where(is_fwd, efim_scr[...], ebim_scr[...])
        for c in range(SSM_NC):
            rows = slice(c * nbp, (c + 1) * nbp)
            cr = small_ref[g, c:c + 1, :]
            ci = small_ref[g, SSM_NC + c:SSM_NC + c + 1, :]
            s_scr[rows, o:o + p2] = s_scr[rows, o:o + p2] + (cr * en_re - ci * en_im)
            s_scr[rows, o + p2:o + p4] = s_scr[rows, o + p2:o + p4] + (cr * en_im + ci * en_re)

    for g in range(SSM_LANE_GROUPS):
        gc = slice(g * SSM_TC, (g + 1) * SSM_TC)
        yy_scr[:, gc] += jnp.dot(s_scr[:, g * p4:(g + 1) * p4].astype(BF16), we_ref[g],
                                 preferred_element_type=F32)

    zz = lax.dot_general(yy_scr[...].astype(BF16), fold_ref[...], nt, preferred_element_type=F32)
    for c in range(SSM_NC):
        for j in range(SSM_T):
            y_ref[pl.ds(c * SSM_T + j, nbp, stride=SSM_SEG), :] = zz[c * nbp:(c + 1) * nbp,
                                                                      j * lanes:(j + 1) * lanes]


def _ssm_scan(xa, fold, w_intra, w_state, w_enter, small, s0, n_seq, n_seg):
    t = xa.shape[0]
    g_n = N_SSM_GROUPS
    lg = SSM_LANE_GROUPS
    lanes = lg * SSM_GROUP
    p4 = 4 * SSM_STATE
    nbp = n_seq * n_seg
    r = SSM_NC * nbp
    assert t == r * SSM_T and fold.shape == (SSM_T * lanes, lg * SSM_TC)
    const = lambda shape: pl.BlockSpec(shape, lambda i: (0,) * len(shape), pipeline_mode=pl.Buffered(1))
    return pl.pallas_call(
        functools.partial(_ssm_kernel, n_seq=n_seq, n_seg=n_seg),
        out_shape=(jax.ShapeDtypeStruct((t, D_SSM), F32), jax.ShapeDtypeStruct((g_n, n_seq, p4), F32)),
        grid=(g_n // lg,),
        in_specs=[
            pl.BlockSpec((t, lanes), lambda i: (0, i)),
            const(fold.shape),
            pl.BlockSpec((lg, SSM_TC, SSM_TC), lambda i: (i, 0, 0)),
            pl.BlockSpec((lg, SSM_TC, p4), lambda i: (i, 0, 0)),
            pl.BlockSpec((lg, p4, SSM_TC), lambda i: (i, 0, 0)),
            pl.BlockSpec((lg, SSM_SMALL_ROWS, 2 * SSM_STATE), lambda i: (i, 0, 0)),
            pl.BlockSpec((lg, n_seq, p4), lambda i: (i, 0, 0)),
        ],
        out_specs=(
            pl.BlockSpec((t, lanes), lambda i: (0, i)),
            pl.BlockSpec((lg, n_seq, p4), lambda i: (i, 0, 0)),
        ),
        scratch_shapes=[pltpu.VMEM((r, lg * p4), F32), pltpu.VMEM((r, lg * p4), F32),
                        pltpu.VMEM((r, lg * SSM_TC), F32)]
        + [pltpu.VMEM((nbp, 2 * SSM_STATE), F32)] * 6,
        compiler_params=_params(("arbitrary",)),
        name="ssm_scan",
    )(xa, fold, w_intra, w_state, w_enter, small, s0)


def _ssm_post_kernel(xa_ref, za_ref, ys_ref, d_ref, w_ref, b_ref, o_ref):
    y = xa_ref[...].astype(F32) * d_ref[...] + ys_ref[...]
    y = jax.nn.gelu(y)
    gl = jnp.dot(y.astype(BF16), w_ref[...], preferred_element_type=F32) + b_ref[...]
    y = y * jax.nn.sigmoid(gl)
    o_ref[...] = (y * jax.nn.silu(za_ref[...].astype(F32))).astype(BF16)


def _ssm_post(proj, yssm, d_skip, w_glu_bf16, b_glu):
    t = proj.shape[0]
    tm = ROW_TM
    return pl.pallas_call(
        _ssm_post_kernel,
        out_shape=jax.ShapeDtypeStruct((t, D_SSM), BF16),
        grid=(t // tm,),
        in_specs=[
            pl.BlockSpec((tm, D_SSM), lambda i: (i, COL_XA // D_SSM)),
            pl.BlockSpec((tm, D_SSM), lambda i: (i, COL_ZA // D_SSM)),
            pl.BlockSpec((tm, D_SSM), lambda i: (i, 0)),
            pl.BlockSpec((1, D_SSM), lambda i: (0, 0)),
            pl.BlockSpec((D_SSM, D_SSM), lambda i: (0, 0)),
            pl.BlockSpec((1, D_SSM), lambda i: (0, 0)),
        ],
        out_specs=pl.BlockSpec((tm, D_SSM), lambda i: (i, 0)),
        compiler_params=_params(("arbitrary",)),
        name="ssm_post",
    )(proj, proj, yssm, d_skip.reshape(1, D_SSM), w_glu_bf16, b_glu.reshape(1, D_SSM))


def _softmax_pv(scores, sink, values):
    m = sink
    for s in scores:
        m = jnp.maximum(m, jnp.max(s, axis=-1, keepdims=True))
    den = jnp.exp(sink - m)
    o = None
    for s, v in zip(scores, values):
        p = jnp.exp(s - m)
        den = den + jnp.sum(p, axis=-1, keepdims=True)
        pv = jnp.dot(p.astype(BF16), v, preferred_element_type=F32)
        o = pv if o is None else o + pv
    return o / den


def _softmax_pv_wide(s, sink, v_bf16):
    hd = v_bf16.shape[1]
    m = jnp.maximum(jnp.max(s, axis=-1, keepdims=True), sink)
    p = jnp.exp(s - m).astype(BF16)
    v_ones = jnp.concatenate([v_bf16, jnp.ones_like(v_bf16)], axis=1)
    o = jnp.dot(p, v_ones, preferred_element_type=F32)
    den = o[:, hd:] + jnp.exp(sink - m)
    return o[:, :hd] / den


def _ctx_attn_kernel(sink_ref, q_ref, kv_ref, zb_ref, o_ref):
    scale = HEAD_DIM ** -0.5
    nt = (((1,), (1,)), ((), ()))
    for kvh in range(N_KV_HEADS):
        k = (kv_ref[:, kvh * HEAD_DIM:(kvh + 1) * HEAD_DIM] * scale).astype(BF16)
        v = kv_ref[:, D_KV + kvh * HEAD_DIM:D_KV + (kvh + 1) * HEAD_DIM].astype(BF16)
        for g in range(Q_PER_KV):
            h = kvh * Q_PER_KV + g
            hs = slice(h * HEAD_DIM, (h + 1) * HEAD_DIM)
            s = lax.dot_general(q_ref[:, hs], k, nt, preferred_element_type=F32)
            o = _softmax_pv([s], sink_ref[h], [v])
            o_ref[:, hs] = (o * jax.nn.silu(zb_ref[:, hs].astype(F32))).astype(BF16)


def _ctx_attention(proj, kv, sink, seq_len):
    t = proj.shape[0]
    return pl.pallas_call(
        _ctx_attn_kernel,
        out_shape=jax.ShapeDtypeStruct((t, D_ATTN), BF16),
        grid=(t // seq_len,),
        in_specs=[
            pl.BlockSpec(memory_space=pltpu.SMEM),
            pl.BlockSpec((seq_len, D_ATTN), lambda b: (b, COL_Q // D_ATTN)),
            pl.BlockSpec((seq_len, 2 * D_KV), lambda b: (b, 0)),
            pl.BlockSpec((seq_len, D_ATTN), lambda b: (b, COL_ZB // D_ATTN)),
        ],
        out_specs=pl.BlockSpec((seq_len, D_ATTN), lambda b: (b, 0)),
        compiler_params=_params(("arbitrary",)),
        name="ctx_attention",
    )(sink, proj, kv, proj)


def _rope_partner(x):
    lane = lax.broadcasted_iota(jnp.int32, x.shape, 1)
    quarter = HEAD_DIM // 4
    first = (lane % (2 * quarter)) < quarter
    return jnp.where(first, pltpu.roll(x, HEAD_DIM - quarter, 1), pltpu.roll(x, quarter, 1))


def _rope(x, tab_ref):
    return x * tab_ref[:, 0:HEAD_DIM] + _rope_partner(x) * tab_ref[:, HEAD_DIM:2 * HEAD_DIM]


def _rope_kv_kernel(kv_ref, tab_ref, o_ref):
    for kvh in range(N_KV_HEADS):
        ks = slice(kvh * HEAD_DIM, (kvh + 1) * HEAD_DIM)
        o_ref[:, ks] = _rope(kv_ref[:, ks], tab_ref).astype(BF16)
    o_ref[:, D_KV:2 * D_KV] = kv_ref[:, D_KV:2 * D_KV].astype(BF16)


def _rope_kv(kv, tab_k, seq_len):
    t = kv.shape[0]
    tm = ROW_TM
    assert seq_len % tm == 0
    tiles_per_seq = seq_len // tm
    return pl.pallas_call(
        _rope_kv_kernel,
        out_shape=jax.ShapeDtypeStruct((t, 2 * D_KV), BF16),
        grid=(t // tm,),
        in_specs=[
            pl.BlockSpec((tm, 2 * D_KV), lambda i: (i, 0)),
            pl.BlockSpec((tm, 2 * HEAD_DIM), lambda i: (i % tiles_per_seq, 0)),
        ],
        out_specs=pl.BlockSpec((tm, 2 * D_KV), lambda i: (i, 0)),
        compiler_params=_params(("arbitrary",)),
        name="rope_kv",
    )(kv, tab_k)


def _lat_attn_kernel(sink_ref, q_ref, kvp_ref, kvc_ref, kvn_ref, ck_ref, cv_ref, zb_ref,
                     tabq_ref, swap_ref, bias_ref, o_ref):
    nt = (((1,), (1,)), ((), ()))
    blk = ATTN_BLOCK
    cos_q = tabq_ref[:, 0:HEAD_DIM]
    sin_q = tabq_ref[:, HEAD_DIM:2 * HEAD_DIM]
    for kvh in range(N_KV_HEADS):
        ks = slice(kvh * HEAD_DIM, (kvh + 1) * HEAD_DIM)
        vs = slice(D_KV + kvh * HEAD_DIM, D_KV + (kvh + 1) * HEAD_DIM)
        keys = [jnp.concatenate([kvp_ref[:, ks], kvc_ref[:, ks], kvn_ref[:, ks]], axis=0),
                ck_ref[:, ks].astype(BF16)]
        values = [jnp.concatenate([kvp_ref[:, vs], kvc_ref[:, vs], kvn_ref[:, vs]], axis=0),
                  cv_ref[:, ks].astype(BF16)]
        qx = jnp.concatenate(
            [q_ref[:, (kvh * Q_PER_KV + g) * HEAD_DIM:(kvh * Q_PER_KV + g + 1) * HEAD_DIM]
             for g in range(Q_PER_KV)], axis=0)
        partner = jnp.dot(qx, swap_ref[...], preferred_element_type=F32)
        qf = qx.astype(F32).reshape(Q_PER_KV, blk, HEAD_DIM)
        q4 = (qf * cos_q[None] + partner.reshape(Q_PER_KV, blk, HEAD_DIM) * sin_q[None])
        q4 = q4.reshape(Q_PER_KV * blk, HEAD_DIM).astype(BF16)
        s_loc = lax.dot_general(q4, keys[0], nt, preferred_element_type=F32)
        s_loc = (s_loc.reshape(Q_PER_KV, blk, 3 * blk) + bias_ref[...][None]).reshape(Q_PER_KV * blk, 3 * blk)
        s_ctx = lax.dot_general(q4, keys[1], nt, preferred_element_type=F32)
        sink_rows = jnp.concatenate(
            [jnp.full((blk, 1), sink_ref[kvh * Q_PER_KV + g], F32) for g in range(Q_PER_KV)], axis=0)
        o = _softmax_pv_wide(jnp.concatenate([s_loc, s_ctx], axis=-1), sink_rows,
                             jnp.concatenate(values, axis=0))
        for g in range(Q_PER_KV):
            hs = slice((kvh * Q_PER_KV + g) * HEAD_DIM, (kvh * Q_PER_KV + g + 1) * HEAD_DIM)
            o_ref[:, hs] = (o[g * blk:(g + 1) * blk] * jax.nn.silu(zb_ref[:, hs].astype(F32))).astype(BF16)


def _lat_attention(proj, kv, ctx_k, ctx_v, sink, tab_q, tab_k, bias, swap, n_seq, seq_len):
    kv = _rope_kv(kv, tab_k, seq_len)
    t = proj.shape[0]
    blk = ATTN_BLOCK
    assert blk - 1 <= WINDOW <= blk
    nb = seq_len // blk
    lc = ctx_k.shape[1]

    def row(b, i):
        return b * nb + i

    def prev(i):
        return jnp.maximum(i - 1, 0)

    def nxt(i):
        return jnp.minimum(i + 1, nb - 1)

    def bias_variant(b, i):
        return ((i > 0).astype(jnp.int32) * 2 + (i < nb - 1).astype(jnp.int32), 0, 0)

    return pl.pallas_call(
        _lat_attn_kernel,
        out_shape=jax.ShapeDtypeStruct((t, D_ATTN), BF16),
        grid=(n_seq, nb),
        in_specs=[
            pl.BlockSpec(memory_space=pltpu.SMEM),
            pl.BlockSpec((blk, D_ATTN), lambda b, i: (row(b, i), COL_Q // D_ATTN)),
            pl.BlockSpec((blk, 2 * D_KV), lambda b, i: (row(b, prev(i)), 0)),
            pl.BlockSpec((blk, 2 * D_KV), lambda b, i: (row(b, i), 0)),
            pl.BlockSpec((blk, 2 * D_KV), lambda b, i: (row(b, nxt(i)), 0)),
            pl.BlockSpec((None, lc, D_KV), lambda b, i: (b, 0, 0)),
            pl.BlockSpec((None, lc, D_KV), lambda b, i: (b, 0, 0)),
            pl.BlockSpec((blk, D_ATTN), lambda b, i: (row(b, i), COL_ZB // D_ATTN)),
            pl.BlockSpec((blk, 2 * HEAD_DIM), lambda b, i: (i, 0)),
            pl.BlockSpec((HEAD_DIM, HEAD_DIM), lambda b, i: (0, 0)),
            pl.BlockSpec((None, blk, 3 * blk), bias_variant),
        ],
        out_specs=pl.BlockSpec((blk, D_ATTN), lambda b, i: (row(b, i), 0)),
        compiler_params=_params(("arbitrary", "arbitrary")),
        name="lat_attention",
    )(sink, proj, kv, kv, kv, ctx_k, ctx_v, proj, tab_q, swap, bias)


def _rope_tables(seq_len):
    rows = seq_len // GRID_W
    row = np.repeat(np.arange(rows), GRID_W).astype(np.float64)
    col = np.tile(np.arange(GRID_W), rows).astype(np.float64)
    half = HEAD_DIM // 2
    inv = ROPE_BASE ** (-np.arange(0, half, 2, dtype=np.float64) / half)
    ang_r = row[:, None] * inv[None, :]
    ang_c = col[:, None] * inv[None, :]
    cos_t = np.concatenate([np.cos(ang_r)] * 2 + [np.cos(ang_c)] * 2, axis=-1)
    sin_t = np.concatenate([-np.sin(ang_r), np.sin(ang_r), -np.sin(ang_c), np.sin(ang_c)], axis=-1)
    tab_k = np.concatenate([cos_t, sin_t], axis=-1)
    tab_q = tab_k * HEAD_DIM ** -0.5
    return jnp.asarray(tab_q.astype(np.float32)), jnp.asarray(tab_k.astype(np.float32))


def _rope_swap_matrix():
    quarter = HEAD_DIM // 4
    d = np.arange(HEAD_DIM)
    src = np.where(d % (2 * quarter) < quarter, d + quarter, d - quarter)
    return jnp.asarray((np.arange(HEAD_DIM)[:, None] == src[None, :]).astype(BF16))


def _window_bias():
    blk = ATTN_BLOCK
    r = np.arange(blk)[:, None]
    s = np.arange(3 * blk)[None, :]
    band = np.abs(s - blk - r) <= WINDOW
    out = []
    for has_prev in (False, True):
        for has_next in (False, True):
            ok = band & ((s >= blk) | has_prev) & ((s < 2 * blk) | has_next)
            out.append(np.where(ok, 0.0, -np.inf).astype(np.float32))
    return jnp.asarray(np.stack(out, axis=0))


def _sgu_kernel(u_ref, v_ref, z_ref, g_ref, b_ref, ws_ref, bs_ref, o_ref, *, n_chunks):
    for n in range(n_chunks):
        rows = slice(n * SGU_CHUNK, (n + 1) * SGU_CHUNK)
        v = _layernorm_f32(jax.nn.gelu(v_ref[rows, :].astype(F32))) * g_ref[...] + b_ref[...]
        v = v.astype(BF16)
        for g in range(N_SGU_GROUPS):
            cols = slice(g * SGU_GROUP_CH, (g + 1) * SGU_GROUP_CH)
            vm = jnp.dot(ws_ref[g], v[:, cols], preferred_element_type=F32) + bs_ref[:, g:g + 1]
            u = jax.nn.gelu(u_ref[rows, cols].astype(F32))
            z = z_ref[rows, cols].astype(F32)
            o_ref[rows, cols] = (u * vm * jax.nn.silu(z)).astype(BF16)


def _sgu(proj, ln_g, ln_b, w_s_bf16, b_s_t):
    t = proj.shape[0]
    tm = ROW_TM
    return pl.pallas_call(
        functools.partial(_sgu_kernel, n_chunks=tm // SGU_CHUNK),
        out_shape=jax.ShapeDtypeStruct((t, D_SGU), BF16),
        grid=(t // tm,),
        in_specs=[
            pl.BlockSpec((tm, D_SGU), lambda i: (i, COL_U // D_SGU)),
            pl.BlockSpec((tm, D_SGU), lambda i: (i, COL_VS // D_SGU)),
            pl.BlockSpec((tm, D_SGU), lambda i: (i, COL_ZC // D_SGU)),
            pl.BlockSpec((1, D_SGU), lambda i: (0, 0)),
            pl.BlockSpec((1, D_SGU), lambda i: (0, 0)),
            pl.BlockSpec((N_SGU_GROUPS, SGU_CHUNK, SGU_CHUNK), lambda i: (0, 0, 0)),
            pl.BlockSpec((SGU_CHUNK, N_SGU_GROUPS), lambda i: (0, 0)),
        ],
        out_specs=pl.BlockSpec((tm, D_SGU), lambda i: (i, 0)),
        compiler_params=_params(("arbitrary",)),
        name="sgu",
    )(proj, proj, proj, ln_g.reshape(1, D_SGU), ln_b.reshape(1, D_SGU), w_s_bf16, b_s_t)


def _merge_kernel(mg_ref, ya_ref, yb_ref, yc_ref, x_ref, gate_ref, wpa_ref, wpb_ref, wpc_ref, wout_ref,
                  lng_ref, lnb_ref, y_ref, *, alpha):
    d = D_MODEL

    def branch(k, y_r, w_r):
        gate = jax.nn.sigmoid(mg_ref[:, k * d:(k + 1) * d].astype(F32))
        return gate * jnp.dot(y_r[...], w_r[...], preferred_element_type=F32)

    merged = branch(0, ya_ref, wpa_ref) + branch(1, yb_ref, wpb_ref) + branch(2, yc_ref, wpc_ref)
    out = jnp.dot(merged.astype(BF16), wout_ref[...], preferred_element_type=F32)
    z = alpha * x_ref[...] + gate_ref[0] * out
    y_ref[...] = _layernorm_f32(z) * lng_ref[...] + lnb_ref[...]


def _merge(proj, ya, yb, yc, x2, gate, wpa, wpb, wpc, wout, ln_g, ln_b, seq_len, alpha):
    t, d = x2.shape
    tm = MERGE_TM
    if gate.shape[0] == 1:
        mod_map = lambda i: (0, 0, 0)
    else:
        assert seq_len % tm == 0
        tiles_per_seq = seq_len // tm
        mod_map = lambda i: (i // tiles_per_seq, 0, 0)

    def const(shape):
        return pl.BlockSpec(shape, lambda i: (0,) * len(shape), pipeline_mode=pl.Buffered(1))

    return pl.pallas_call(
        functools.partial(_merge_kernel, alpha=alpha),
        out_shape=jax.ShapeDtypeStruct((t, d), F32),
        grid=(t // tm,),
        in_specs=[
            pl.BlockSpec((tm, D_MG), lambda i: (i, 0)),
            pl.BlockSpec((tm, D_SSM), lambda i: (i, 0)),
            pl.BlockSpec((tm, D_ATTN), lambda i: (i, 0)),
            pl.BlockSpec((tm, D_SGU), lambda i: (i, 0)),
            pl.BlockSpec((tm, d), lambda i: (i, 0)),
            pl.BlockSpec((1, 1, d), mod_map),
            const((D_SSM, d)), const((D_ATTN, d)), const((D_SGU, d)), const((d, d)),
            const((1, d)), const((1, d)),
        ],
        out_specs=pl.BlockSpec((tm, d), lambda i: (i, 0)),
        compiler_params=_params(("arbitrary",)),
        name="merge_out",
    )(proj, ya, yb, yc, x2, gate, wpa, wpb, wpc, wout, ln_g.reshape(1, d), ln_b.reshape(1, d))


def _layer(x2, n_seq, seq_len, mod, lp, ctx_k, ctx_v, s0, rope, alpha):
    d = D_MODEL
    nbm = mod.shape[0]
    shift = mod[:, 0:d].reshape(nbm, 1, d)
    scale = mod[:, d:2 * d].reshape(nbm, 1, d)
    gate = mod[:, 2 * d:3 * d].reshape(nbm, 1, d)
    proj, kv, xa = _inproj(x2, scale, shift, lp['w_in'], seq_len)

    n_seg = seq_len // SSM_SEG
    yssm, fin = _ssm_scan(xa, lp['fold'], lp['w_intra'], lp['w_state'], lp['w_enter'], lp['ssm_small'],
                          s0, n_seq, n_seg)
    ya = _ssm_post(proj, yssm, lp['d_skip'], lp['w_glu'], lp['b_glu'])

    if ctx_k is None:
        yb = _ctx_attention(proj, kv, lp['sink'], seq_len)
    else:
        yb = _lat_attention(proj, kv, ctx_k, ctx_v, lp['sink'], *rope, n_seq, seq_len)

    yc = _sgu(proj, lp['sgu_g'], lp['sgu_b'], lp['w_s'], lp['b_s_t'])

    y = _merge(proj, ya, yb, yc, x2, gate, lp['w_pa'], lp['w_pb'], lp['w_pc'], lp['w_out'],
               lp['ln_g'], lp['ln_b'], seq_len, alpha)
    return y, kv, fin


def _states_to_lanes(s):
    b = s.shape[0]
    return s.astype(F32).transpose(3, 0, 2, 1, 4).reshape(N_SSM_GROUPS, b, 4 * SSM_STATE)


def _lanes_to_states(f):
    b = f.shape[1]
    return f.reshape(N_SSM_GROUPS, b, 2, 2, SSM_STATE).transpose(1, 3, 2, 0, 4)


def kernel(x_prompt, x_sample, cache_k, cache_v, state_ssm, c, c_ctx, w_ada, b_ada, w_in, ssm_lam_re, ssm_lam_im, ssm_log_step, ssm_b_re, ssm_b_im, ssm_c_re, ssm_c_im, ssm_d, w_glu, b_glu, attn_sink, sgu_ln_g, sgu_ln_b, w_spatial, b_spatial, w_proj_a, w_proj_b, w_proj_c, w_out, ln_g, ln_b):
    depth = w_in.shape[0]
    batch, seq, d = x_prompt.shape
    dec_batch, dec_seq, _ = x_sample.shape
    past_len = cache_k.shape[2]
    alpha = (2 * depth) ** 0.25

    n_cond = 1 + dec_batch
    cond_rows = -(-n_cond // 8) * 8
    cond = jnp.concatenate([c_ctx[None, :], c, jnp.zeros((cond_rows - n_cond, d), F32)], axis=0)
    mod = _ada_mod(cond, w_ada, b_ada)

    def split_w_in(w):
        sizes = (D_SSM, D_SSM, D_ATTN, D_KV, D_KV, D_ATTN, D_SGU, D_SGU, D_SGU, D_MG)
        offs = [0]
        for s in sizes:
            offs.append(offs[-1] + s)
        xa, za, q, k, v, zb, u, vs, zc, mg = [w[:, offs[i]:offs[i + 1]] for i in range(len(sizes))]
        return jnp.concatenate([mg, xa, za, q, zb, k, v, u, vs, zc], axis=1).astype(BF16)

    fold = _ssm_fold_matrix()
    layers = []
    for l in range(depth):
        w_intra, w_state, w_enter, small = _ssm_tables(
            ssm_lam_re[l], ssm_lam_im[l], ssm_log_step[l], ssm_b_re[l], ssm_b_im[l], ssm_c_re[l], ssm_c_im[l])
        layers.append({
            'w_in': split_w_in(w_in[l]),
            'fold': fold, 'w_intra': w_intra, 'w_state': w_state, 'w_enter': w_enter, 'ssm_small': small,
            'd_skip': ssm_d[l], 'w_glu': w_glu[l].astype(BF16), 'b_glu': b_glu[l],
            'sink': attn_sink[l].astype(F32),
            'sgu_g': sgu_ln_g[l], 'sgu_b': sgu_ln_b[l],
            'w_s': w_spatial[l].astype(BF16), 'b_s_t': b_spatial[l].T.astype(F32),
            'w_pa': w_proj_a[l].astype(BF16), 'w_pb': w_proj_b[l].astype(BF16),
            'w_pc': w_proj_c[l].astype(BF16), 'w_out': w_out[l].astype(BF16),
            'ln_g': ln_g[l], 'ln_b': ln_b[l],
        })

    h = x_prompt.reshape(batch * seq, d)
    zero_state = jnp.zeros((N_SSM_GROUPS, batch, 4 * SSM_STATE), F32)
    ks, vs, ss = [], [], []
    for l in range(depth):
        h, kv, fin = _layer(h, batch, seq, mod[l, 0:1], layers[l], None, None, zero_state, None, alpha)
        ks.append(kv[:, :D_KV].reshape(batch, seq, N_KV_HEADS, HEAD_DIM))
        vs.append(kv[:, D_KV:].reshape(batch, seq, N_KV_HEADS, HEAD_DIM))
        ss.append(_lanes_to_states(fin))
    y_prompt = h.reshape(batch, seq, d)
    new_cache_k = jnp.stack(ks, axis=1)
    new_cache_v = jnp.stack(vs, axis=1)
    new_state_ssm = jnp.stack(ss, axis=1)

    rope = _rope_tables(dec_seq) + (_window_bias(), _rope_swap_matrix())
    z = x_sample.reshape(dec_batch * dec_seq, d)
    for l in range(depth):
        ctx_k = cache_k[:, l].reshape(dec_batch, past_len, D_KV).astype(F32)
        ctx_v = cache_v[:, l].reshape(dec_batch, past_len, D_KV).astype(F32)
        z, _, _ = _layer(z, dec_batch, dec_seq, mod[l, 1:1 + dec_batch], layers[l], ctx_k, ctx_v,
                         _states_to_lanes(state_ssm[:, l]), rope, alpha)
    y_sample = z.reshape(dec_batch, dec_seq, d)
    return (y_prompt, y_sample, new_cache_k, new_cache_v, new_state_ssm)
```

```python
import functools
import math

import jax
import jax.numpy as jnp
import numpy as np
from jax import lax
from jax.experimental import pallas as pl
from jax.experimental.pallas import tpu as pltpu

F32 = jnp.float32
BF16 = jnp.bfloat16

D_MODEL = 2048
GRID_W = 64
D_SSM = 512
SSM_GROUP = 16
N_SSM_GROUPS = D_SSM // SSM_GROUP
SSM_STATE = 64
HEAD_DIM = 128
N_HEADS = 8
N_KV_HEADS = 2
Q_PER_KV = N_HEADS // N_KV_HEADS
D_ATTN = N_HEADS * HEAD_DIM
D_KV = N_KV_HEADS * HEAD_DIM
WINDOW = 128
ATTN_BLOCK = 128
ROPE_BASE = 10000.0
D_SGU = 512
SGU_CHUNK = 128
SGU_GROUP_CH = 128
N_SGU_GROUPS = D_SGU // SGU_GROUP_CH
N_BRANCH = 3
LN_EPS = 1e-5

D_MG = N_BRANCH * D_MODEL
COL_MG = 0
COL_XA = COL_MG + D_MG
COL_ZA = COL_XA + D_SSM
COL_Q = COL_ZA + D_SSM
COL_K = COL_Q + D_ATTN
COL_V = COL_K + D_KV
COL_ZB = COL_V + D_KV
COL_U = COL_ZB + D_ATTN
COL_VS = COL_U + D_SGU
COL_ZC = COL_VS + D_SGU
D_IN = COL_ZC + D_SGU
W_IN_ROTATE = D_IN - D_MG

SSM_T = 16
SSM_NC = 16
SSM_TC = SSM_T * SSM_GROUP
SSM_SEG = SSM_T * SSM_NC
SSM_SMALL_ROWS = 40
SSM_LANE_GROUPS = 128 // SSM_GROUP

VMEM_LIMIT_BYTES = 56 * 1024 * 1024

INPROJ_TM = 1024
INPROJ_TN = 1024
ROW_TM = 512
MERGE_TM = 512


def _params(sem):
    return pltpu.CompilerParams(dimension_semantics=sem, vmem_limit_bytes=VMEM_LIMIT_BYTES)


def _layernorm_f32(x):
    mu = jnp.mean(x, axis=-1, keepdims=True)
    xc = x - mu
    var = jnp.mean(xc * xc, axis=-1, keepdims=True)
    return xc * lax.rsqrt(var + LN_EPS)


def _ada_kernel(cond_ref, w_ref, b_ref, o_ref):
    c = cond_ref[...]
    a = (c * jax.nn.sigmoid(c)).astype(BF16)
    o_ref[...] = jnp.dot(a, w_ref[...].astype(BF16), preferred_element_type=F32) + b_ref[...]


def _ada_mod(cond, w_ada, b_ada):
    depth, d, n = w_ada.shape
    r = cond.shape[0]
    tn = 512
    return pl.pallas_call(
        _ada_kernel,
        out_shape=jax.ShapeDtypeStruct((depth, r, n), F32),
        grid=(depth, n // tn),
        in_specs=[
            pl.BlockSpec((r, d), lambda l, j: (0, 0)),
            pl.BlockSpec((None, d, tn), lambda l, j: (l, 0, j)),
            pl.BlockSpec((None, 1, tn), lambda l, j: (l, 0, j)),
        ],
        out_specs=pl.BlockSpec((None, r, tn), lambda l, j: (l, 0, j)),
        compiler_params=_params(("arbitrary", "arbitrary")),
        name="ada_mod",
    )(cond, w_ada, b_ada.reshape(depth, 1, n))


def _inproj_kernel(x_ref, scale_ref, shift_ref, w_ref, proj_ref, kv_ref, xa_ref, h_scr, *, kv_tile, xa_tile):
    j = pl.program_id(1)

    @pl.when(j == 0)
    def _():
        h = _layernorm_f32(x_ref[...]) * (1.0 + scale_ref[0]) + shift_ref[0]
        h_scr[...] = h.astype(BF16)

    acc = jnp.dot(h_scr[...], w_ref[...], preferred_element_type=F32)
    proj_ref[...] = acc.astype(BF16)

    @pl.when(j == kv_tile)
    def _():
        kv_ref[...] = acc[:, 0:2 * D_KV]

    @pl.when(j == xa_tile)
    def _():
        xa_ref[...] = acc[:, 0:D_SSM]


def _inproj(x2, scale, shift, w_bf16, layer, seq_len):
    t, d = x2.shape
    n = w_bf16.shape[2]
    tm, tn = INPROJ_TM, INPROJ_TN
    assert t % tm == 0 and n % tn == 0 and COL_K % tn == 0 and COL_XA % tn == 0
    assert 2 * D_KV <= tn and D_SSM <= tn and W_IN_ROTATE % tn == 0
    n_tiles = n // tn
    rot_tiles = W_IN_ROTATE // tn
    nb = scale.shape[0]
    if nb == 1:
        mod_map = lambda i, j: (0, 0, 0)
    else:
        assert seq_len % tm == 0
        tiles_per_seq = seq_len // tm
        mod_map = lambda i, j: (i // tiles_per_seq, 0, 0)
    return pl.pallas_call(
        functools.partial(_inproj_kernel, kv_tile=COL_K // tn, xa_tile=COL_XA // tn),
        out_shape=(jax.ShapeDtypeStruct((t, n), BF16), jax.ShapeDtypeStruct((t, 2 * D_KV), F32),
                   jax.ShapeDtypeStruct((t, D_SSM), F32)),
        grid=(t // tm, n // tn),
        in_specs=[
            pl.BlockSpec((tm, d), lambda i, j: (i, 0)),
            pl.BlockSpec((1, 1, d), mod_map),
            pl.BlockSpec((1, 1, d), mod_map),
            pl.BlockSpec((None, d, tn), lambda i, j: (layer, 0, (j + rot_tiles) % n_tiles)),
        ],
        out_specs=(
            pl.BlockSpec((tm, tn), lambda i, j: (i, j)),
            pl.BlockSpec((tm, 2 * D_KV), lambda i, j: (i, 0)),
            pl.BlockSpec((tm, D_SSM), lambda i, j: (i, 0)),
        ),
        scratch_shapes=[pltpu.VMEM((tm, d), BF16)],
        compiler_params=_params(("arbitrary", "arbitrary")),
        name="ln_inproj",
    )(x2, scale, shift, w_bf16)


def _ssm_tables(lam_re, lam_im, log_step, b_re, b_im, c_re, c_im):
    hp = lax.Precision.HIGHEST
    g_n, p_n, c_n, t_n = N_SSM_GROUPS, SSM_STATE, SSM_GROUP, SSM_T
    lr = lam_re.astype(F32)
    li = lam_im.astype(F32)
    dt = jnp.exp(log_step.astype(F32))[..., None]
    mag = jnp.exp(lr * dt)
    ar = mag * jnp.cos(li * dt)
    ai = mag * jnp.sin(li * dt)
    den = lr * lr + li * li
    nr = ar - 1.0
    f_re = (nr * lr + ai * li) / den
    f_im = (ai * lr - nr * li) / den
    br = b_re.astype(F32)
    bi = b_im.astype(F32)
    bb_re = f_re[..., None] * br - f_im[..., None] * bi
    bb_im = f_re[..., None] * bi + f_im[..., None] * br

    def cmul(xr, xi, yr, yi):
        return xr * yr - xi * yi, xr * yi + xi * yr

    pr, pi = [jnp.ones_like(ar)], [jnp.zeros_like(ai)]
    for _ in range(t_n):
        nr_, ni_ = cmul(pr[-1], pi[-1], ar, ai)
        pr.append(nr_)
        pi.append(ni_)
    pw_re = jnp.stack(pr, axis=-1)
    pw_im = jnp.stack(pi, axis=-1)
    ct_re = c_re.astype(F32).transpose(0, 1, 3, 2)
    ct_im = c_im.astype(F32).transpose(0, 1, 3, 2)
    cp_re, cp_im = cmul(ct_re[:, :, :, None, :], ct_im[:, :, :, None, :],
                        pw_re[..., None], pw_im[..., None])
    pk_re = pw_re[..., :t_n].transpose(0, 1, 3, 2)[:, :, :, None, :]
    pk_im = pw_im[..., :t_n].transpose(0, 1, 3, 2)[:, :, :, None, :]
    ck_re, ck_im = cmul(c_re.astype(F32)[:, :, None], c_im.astype(F32)[:, :, None], pk_re, pk_im)
    taps = (jnp.einsum('dgkcp,dgpe->dgkce', ck_re, bb_re, precision=hp)
            - jnp.einsum('dgkcp,dgpe->dgkce', ck_im, bb_im, precision=hp)).transpose(0, 1, 4, 2, 3)
    pad = jnp.zeros((g_n, c_n, t_n - 1, c_n), F32)
    band = (jnp.concatenate([pad, taps[0]], axis=2)
            + jnp.concatenate([taps[1][:, :, ::-1], pad], axis=2))
    band = band.reshape(g_n, c_n, (2 * t_n - 1) * c_n)
    w_intra = jnp.stack([band[:, :, (t_n - 1 - j) * c_n:(2 * t_n - 1 - j) * c_n] for j in range(t_n)],
                        axis=1).reshape(g_n, t_n * c_n, t_n * c_n)

    bt_re = bb_re.transpose(0, 1, 3, 2)
    bt_im = bb_im.transpose(0, 1, 3, 2)
    pj_re = pw_re.transpose(0, 1, 3, 2)[:, :, :t_n]
    pj_im = pw_im.transpose(0, 1, 3, 2)[:, :, :t_n]
    sf_re, sf_im = cmul(pj_re[0][:, ::-1, None, :], pj_im[0][:, ::-1, None, :], bt_re[0][:, None], bt_im[0][:, None])
    sb_re, sb_im = cmul(pj_re[1][:, :, None, :], pj_im[1][:, :, None, :], bt_re[1][:, None], bt_im[1][:, None])
    w_state = jnp.stack([sf_re, sb_re, sf_im, sb_im], axis=3)
    w_state = w_state.reshape(g_n, t_n * c_n, 4 * p_n)

    w_enter = jnp.stack([cp_re[0][:, :, 1:], cp_re[1][:, :, :0:-1], -cp_im[0][:, :, 1:], -cp_im[1][:, :, :0:-1]],
                        axis=1)
    w_enter = w_enter.reshape(g_n, 4 * p_n, t_n * c_n)

    a1r = jnp.concatenate([pw_re[0, :, :, t_n], pw_re[1, :, :, t_n]], axis=-1)
    a1i = jnp.concatenate([pw_im[0, :, :, t_n], pw_im[1, :, :, t_n]], axis=-1)
    qr, qi = [jnp.ones_like(a1r)], [jnp.zeros_like(a1i)]
    for _ in range(SSM_NC):
        nr_, ni_ = cmul(qr[-1], qi[-1], a1r, a1i)
        qr.append(nr_)
        qi.append(ni_)
    qr = jnp.stack(qr, axis=1)
    qi = jnp.stack(qi, axis=1)
    rev = jnp.arange(SSM_NC - 1, -1, -1)
    corr_re = jnp.concatenate([qr[:, :SSM_NC, :p_n], qr[:, rev, p_n:]], axis=-1)
    corr_im = jnp.concatenate([qi[:, :SSM_NC, :p_n], qi[:, rev, p_n:]], axis=-1)
    small = jnp.concatenate([
        corr_re, corr_im, a1r[:, None], a1i[:, None], qr[:, SSM_NC][:, None], qi[:, SSM_NC][:, None],
        jnp.zeros((g_n, SSM_SMALL_ROWS - 2 * SSM_NC - 4, 2 * p_n), F32)], axis=1)
    return w_intra.astype(BF16), w_state.astype(BF16), w_enter.astype(BF16), small


def _ssm_fold_matrix():
    n = SSM_T * SSM_LANE_GROUPS * SSM_GROUP
    src = np.arange(n)
    j = src // (SSM_LANE_GROUPS * SSM_GROUP)
    g = (src // SSM_GROUP) % SSM_LANE_GROUPS
    c = src % SSM_GROUP
    dst = g * SSM_TC + j * SSM_GROUP + c
    return jnp.asarray((dst[:, None] == np.arange(n)[None, :]).astype(BF16))


def _ssm_kernel(xa_ref, fold_ref, wi_ref, ws_ref, we_ref, small_ref, s0_ref, y_ref, fin_ref,
                v_scr, s_scr, yy_scr, fre_scr, fim_scr, efre_scr, efim_scr, ebre_scr, ebim_scr,
                *, n_seq, n_seg):
    p2 = 2 * SSM_STATE
    p4 = 4 * SSM_STATE
    nbp = n_seq * n_seg
    lanes = SSM_LANE_GROUPS * SSM_GROUP
    nt = (((1,), (1,)), ((), ()))

    blocks = []
    for c in range(SSM_NC):
        toks = [xa_ref[pl.ds(c * SSM_T + j, nbp, stride=SSM_SEG), :].astype(BF16) for j in range(SSM_T)]
        blocks.append(jnp.concatenate(toks, axis=1))
    xx = jnp.concatenate(blocks, axis=0)
    uu = jnp.dot(xx, fold_ref[...], preferred_element_type=F32).astype(BF16)

    lane = lax.broadcasted_iota(jnp.int32, (1, p2), 1)
    is_fwd = lane < SSM_STATE
    for g in range(SSM_LANE_GROUPS):
        gc = slice(g * SSM_TC, (g + 1) * SSM_TC)
        u = uu[:, gc]
        yy_scr[:, gc] = jnp.dot(u, wi_ref[g], preferred_element_type=F32)
        v_scr[:, gc] = jnp.dot(u, ws_ref[g], preferred_element_type=F32)

    for g in range(SSM_LANE_GROUPS):
        o = g * p4
        a1r = small_ref[g, 2 * SSM_NC:2 * SSM_NC + 1, :]
        a1i = small_ref[g, 2 * SSM_NC + 1:2 * SSM_NC + 2, :]
        asr = small_ref[g, 2 * SSM_NC + 2:2 * SSM_NC + 3, :]
        asi = small_ref[g, 2 * SSM_NC + 3:2 * SSM_NC + 4, :]

        st_re = jnp.zeros((nbp, p2), F32)
        st_im = jnp.zeros((nbp, p2), F32)
        for c in range(SSM_NC):
            cb = SSM_NC - 1 - c
            rf = slice(c * nbp, (c + 1) * nbp)
            rb = slice(cb * nbp, (cb + 1) * nbp)
            s_scr[rf, o:o + SSM_STATE] = st_re[:, 0:SSM_STATE]
            s_scr[rb, o + SSM_STATE:o + p2] = st_re[:, SSM_STATE:p2]
            s_scr[rf, o + p2:o + p2 + SSM_STATE] = st_im[:, 0:SSM_STATE]
            s_scr[rb, o + p2 + SSM_STATE:o + p4] = st_im[:, SSM_STATE:p2]
            v_re = jnp.where(is_fwd, v_scr[rf, o:o + p2], v_scr[rb, o:o + p2])
            v_im = jnp.where(is_fwd, v_scr[rf, o + p2:o + p4], v_scr[rb, o + p2:o + p4])
            st_re, st_im = (a1r * st_re - a1i * st_im + v_re, a1r * st_im + a1i * st_re + v_im)

        fre_scr[...] = st_re
        fim_scr[...] = st_im
        s0 = s0_ref[g]
        e_re = s0[:, 0:p2]
        e_im = s0[:, p2:p4]
        for k in range(n_seg):
            kb = n_seg - 1 - k
            rows_f = pl.ds(k, n_seq, stride=n_seg)
            rows_b = pl.ds(kb, n_seq, stride=n_seg)
            efre_scr[rows_f, :] = e_re
            efim_scr[rows_f, :] = e_im
            ebre_scr[rows_b, :] = e_re
            ebim_scr[rows_b, :] = e_im
            f_re = jnp.where(is_fwd, fre_scr[rows_f, :], fre_scr[rows_b, :])
            f_im = jnp.where(is_fwd, fim_scr[rows_f, :], fim_scr[rows_b, :])
            e_re, e_im = (asr * e_re - asi * e_im + f_re, asr * e_im + asi * e_re + f_im)
        fin_ref[g] = jnp.concatenate([e_re, e_im], axis=-1)

        en_re = jnp.where(is_fwd, efre_scr[...], ebre_scr[...])
        en_im = jnp.where(is_fwd, efim_scr[...], ebim_scr[...])
        for c in range(SSM_NC):
            rows = slice(c * nbp, (c + 1) * nbp)
            cr = small_ref[g, c:c + 1, :]
            ci = small_ref[g, SSM_NC + c:SSM_NC + c + 1, :]
            s_scr[rows, o:o + p2] = s_scr[rows, o:o + p2] + (cr * en_re - ci * en_im)
            s_scr[rows, o + p2:o + p4] = s_scr[rows, o + p2:o + p4] + (cr * en_im + ci * en_re)

    for g in range(SSM_LANE_GROUPS):
        gc = slice(g * SSM_TC, (g + 1) * SSM_TC)
        yy_scr[:, gc] += jnp.dot(s_scr[:, g * p4:(g + 1) * p4].astype(BF16), we_ref[g],
                                 preferred_element_type=F32)

    zz = lax.dot_general(yy_scr[...].astype(BF16), fold_ref[...], nt, preferred_element_type=F32)
    for c in range(SSM_NC):
        for j in range(SSM_T):
            y_ref[pl.ds(c * SSM_T + j, nbp, stride=SSM_SEG), :] = zz[c * nbp:(c + 1) * nbp,
                                                                      j * lanes:(j + 1) * lanes]


def _ssm_scan(xa, fold, w_intra, w_state, w_enter, small, layer, s0, n_seq, n_seg):
    t = xa.shape[0]
    g_n = N_SSM_GROUPS
    lg = SSM_LANE_GROUPS
    lanes = lg * SSM_GROUP
    p4 = 4 * SSM_STATE
    nbp = n_seq * n_seg
    r = SSM_NC * nbp
    assert t == r * SSM_T and fold.shape == (SSM_T * lanes, lg * SSM_TC)
    const = lambda shape: pl.BlockSpec(shape, lambda i: (0,) * len(shape), pipeline_mode=pl.Buffered(1))
    return pl.pallas_call(
        functools.partial(_ssm_kernel, n_seq=n_seq, n_seg=n_seg),
        out_shape=(jax.ShapeDtypeStruct((t, D_SSM), F32), jax.ShapeDtypeStruct((g_n, n_seq, p4), F32)),
        grid=(g_n // lg,),
        in_specs=[
            pl.BlockSpec((t, lanes), lambda i: (0, i)),
            const(fold.shape),
            pl.BlockSpec((None, lg, SSM_TC, SSM_TC), lambda i: (layer, i, 0, 0)),
            pl.BlockSpec((None, lg, SSM_TC, p4), lambda i: (layer, i, 0, 0)),
            pl.BlockSpec((None, lg, p4, SSM_TC), lambda i: (layer, i, 0, 0)),
            pl.BlockSpec((None, lg, SSM_SMALL_ROWS, 2 * SSM_STATE), lambda i: (layer, i, 0, 0)),
            pl.BlockSpec((lg, n_seq, p4), lambda i: (i, 0, 0)),
        ],
        out_specs=(
            pl.BlockSpec((t, lanes), lambda i: (0, i)),
            pl.BlockSpec((lg, n_seq, p4), lambda i: (i, 0, 0)),
        ),
        scratch_shapes=[pltpu.VMEM((r, lg * p4), F32), pltpu.VMEM((r, lg * p4), F32),
                        pltpu.VMEM((r, lg * SSM_TC), F32)]
        + [pltpu.VMEM((nbp, 2 * SSM_STATE), F32)] * 6,
        compiler_params=_params(("arbitrary",)),
        name="ssm_scan",
    )(xa, fold, w_intra, w_state, w_enter, small, s0)


def _ssm_post_kernel(xa_ref, za_ref, ys_ref, d_ref, w_ref, b_ref, o_ref):
    y = xa_ref[...].astype(F32) * d_ref[...] + ys_ref[...]
    y = jax.nn.gelu(y)
    gl = jnp.dot(y.astype(BF16), w_ref[...], preferred_element_type=F32) + b_ref[...]
    y = y * jax.nn.sigmoid(gl)
    o_ref[...] = (y * jax.nn.silu(za_ref[...].astype(F32))).astype(BF16)


def _ssm_post(proj, yssm, d_skip, w_glu_bf16, b_glu):
    t = proj.shape[0]
    tm = ROW_TM
    return pl.pallas_call(
        _ssm_post_kernel,
        out_shape=jax.ShapeDtypeStruct((t, D_SSM), BF16),
        grid=(t // tm,),
        in_specs=[
            pl.BlockSpec((tm, D_SSM), lambda i: (i, COL_XA // D_SSM)),
            pl.BlockSpec((tm, D_SSM), lambda i: (i, COL_ZA // D_SSM)),
            pl.BlockSpec((tm, D_SSM), lambda i: (i, 0)),
            pl.BlockSpec((1, D_SSM), lambda i: (0, 0)),
            pl.BlockSpec((D_SSM, D_SSM), lambda i: (0, 0)),
            pl.BlockSpec((1, D_SSM), lambda i: (0, 0)),
        ],
        out_specs=pl.BlockSpec((tm, D_SSM), lambda i: (i, 0)),
        compiler_params=_params(("arbitrary",)),
        name="ssm_post",
    )(proj, proj, yssm, d_skip.reshape(1, D_SSM), w_glu_bf16, b_glu.reshape(1, D_SSM))


def _softmax_pv(scores, sink, values):
    m = sink
    for s in scores:
        m = jnp.maximum(m, jnp.max(s, axis=-1, keepdims=True))
    den = jnp.exp(sink - m)
    o = None
    for s, v in zip(scores, values):
        p = jnp.exp(s - m)
        den = den + jnp.sum(p, axis=-1, keepdims=True)
        pv = jnp.dot(p.astype(BF16), v, preferred_element_type=F32)
        o = pv if o is None else o + pv
    return o / den


def _softmax_pv_wide(s, sink, v_bf16):
    hd = v_bf16.shape[1]
    m = jnp.maximum(jnp.max(s, axis=-1, keepdims=True), sink)
    p = jnp.exp(s - m).astype(BF16)
    v_ones = jnp.concatenate([v_bf16, jnp.ones_like(v_bf16)], axis=1)
    o = jnp.dot(p, v_ones, preferred_element_type=F32)
    den = o[:, hd:] + jnp.exp(sink - m)
    return o[:, :hd] / den


def _head_gate(zb_refs, h):
    per = N_HEADS // 2
    z = zb_refs[h // per][:, (h % per) * HEAD_DIM:(h % per + 1) * HEAD_DIM].astype(F32)
    return jax.nn.silu(z)


def _ctx_attn_kernel(sink_ref, q_ref, kv_ref, zb0_ref, zb1_ref, o_ref):
    scale = HEAD_DIM ** -0.5
    nt = (((1,), (1,)), ((), ()))
    for kvh in range(N_KV_HEADS):
        k = (kv_ref[:, kvh * HEAD_DIM:(kvh + 1) * HEAD_DIM] * scale).astype(BF16)
        v = kv_ref[:, D_KV + kvh * HEAD_DIM:D_KV + (kvh + 1) * HEAD_DIM].astype(BF16)
        for g in range(Q_PER_KV):
            h = kvh * Q_PER_KV + g
            hs = slice(h * HEAD_DIM, (h + 1) * HEAD_DIM)
            s = lax.dot_general(q_ref[:, hs], k, nt, preferred_element_type=F32)
            o = _softmax_pv([s], sink_ref[h], [v])
            o_ref[:, hs] = (o * _head_gate((zb0_ref, zb1_ref), h)).astype(BF16)


def _ctx_attention(proj, kv, sink, seq_len):
    t = proj.shape[0]
    return pl.pallas_call(
        _ctx_attn_kernel,
        out_shape=jax.ShapeDtypeStruct((t, D_ATTN), BF16),
        grid=(t // seq_len,),
        in_specs=[
            pl.BlockSpec(memory_space=pltpu.SMEM),
            pl.BlockSpec((seq_len, D_ATTN), lambda b: (b, COL_Q // D_ATTN)),
            pl.BlockSpec((seq_len, 2 * D_KV), lambda b: (b, 0)),
            pl.BlockSpec((seq_len, D_ATTN // 2), lambda b: (b, COL_ZB // (D_ATTN // 2))),
            pl.BlockSpec((seq_len, D_ATTN // 2), lambda b: (b, COL_ZB // (D_ATTN // 2) + 1)),
        ],
        out_specs=pl.BlockSpec((seq_len, D_ATTN), lambda b: (b, 0)),
        compiler_params=_params(("arbitrary",)),
        name="ctx_attention",
    )(sink, proj, kv, proj, proj)


def _rope_partner(x):
    lane = lax.broadcasted_iota(jnp.int32, x.shape, 1)
    quarter = HEAD_DIM // 4
    first = (lane % (2 * quarter)) < quarter
    return jnp.where(first, pltpu.roll(x, HEAD_DIM - quarter, 1), pltpu.roll(x, quarter, 1))


def _rope(x, tab_ref):
    return x * tab_ref[:, 0:HEAD_DIM] + _rope_partner(x) * tab_ref[:, HEAD_DIM:2 * HEAD_DIM]


def _rope_kv_kernel(kv_ref, tab_ref, o_ref):
    for kvh in range(N_KV_HEADS):
        ks = slice(kvh * HEAD_DIM, (kvh + 1) * HEAD_DIM)
        o_ref[:, ks] = _rope(kv_ref[:, ks], tab_ref).astype(BF16)
    o_ref[:, D_KV:2 * D_KV] = kv_ref[:, D_KV:2 * D_KV].astype(BF16)


def _rope_kv(kv, tab_k, seq_len):
    t = kv.shape[0]
    tm = ROW_TM
    assert seq_len % tm == 0
    tiles_per_seq = seq_len // tm
    return pl.pallas_call(
        _rope_kv_kernel,
        out_shape=jax.ShapeDtypeStruct((t, 2 * D_KV), BF16),
        grid=(t // tm,),
        in_specs=[
            pl.BlockSpec((tm, 2 * D_KV), lambda i: (i, 0)),
            pl.BlockSpec((tm, 2 * HEAD_DIM), lambda i: (i % tiles_per_seq, 0)),
        ],
        out_specs=pl.BlockSpec((tm, 2 * D_KV), lambda i: (i, 0)),
        compiler_params=_params(("arbitrary",)),
        name="rope_kv",
    )(kv, tab_k)


def _lat_attn_kernel(sink_ref, q_ref, kvp_ref, kvc_ref, kvn_ref, ck_ref, cv_ref, zb0_ref, zb1_ref,
                     tabq_ref, swap_ref, bias_ref, o_ref):
    nt = (((1,), (1,)), ((), ()))
    blk = ATTN_BLOCK
    cos_q = tabq_ref[:, 0:HEAD_DIM]
    sin_q = tabq_ref[:, HEAD_DIM:2 * HEAD_DIM]
    for kvh in range(N_KV_HEADS):
        ks = slice(kvh * HEAD_DIM, (kvh + 1) * HEAD_DIM)
        vs = slice(D_KV + kvh * HEAD_DIM, D_KV + (kvh + 1) * HEAD_DIM)
        keys = [jnp.concatenate([kvp_ref[:, ks], kvc_ref[:, ks], kvn_ref[:, ks]], axis=0),
                ck_ref[:, ks].astype(BF16)]
        values = [jnp.concatenate([kvp_ref[:, vs], kvc_ref[:, vs], kvn_ref[:, vs]], axis=0),
                  cv_ref[:, ks].astype(BF16)]
        qx = jnp.concatenate(
            [q_ref[:, (kvh * Q_PER_KV + g) * HEAD_DIM:(kvh * Q_PER_KV + g + 1) * HEAD_DIM]
             for g in range(Q_PER_KV)], axis=0)
        partner = jnp.dot(qx, swap_ref[...], preferred_element_type=F32)
        qf = qx.astype(F32).reshape(Q_PER_KV, blk, HEAD_DIM)
        q4 = (qf * cos_q[None] + partner.reshape(Q_PER_KV, blk, HEAD_DIM) * sin_q[None])
        q4 = q4.reshape(Q_PER_KV * blk, HEAD_DIM).astype(BF16)
        s_loc = lax.dot_general(q4, keys[0], nt, preferred_element_type=F32)
        s_loc = (s_loc.reshape(Q_PER_KV, blk, 3 * blk) + bias_ref[...][None]).reshape(Q_PER_KV * blk, 3 * blk)
        s_ctx = lax.dot_general(q4, keys[1], nt, preferred_element_type=F32)
        sink_rows = jnp.concatenate(
            [jnp.full((blk, 1), sink_ref[kvh * Q_PER_KV + g], F32) for g in range(Q_PER_KV)], axis=0)
        o = _softmax_pv_wide(jnp.concatenate([s_loc, s_ctx], axis=-1), sink_rows,
                             jnp.concatenate(values, axis=0))
        for g in range(Q_PER_KV):
            h = kvh * Q_PER_KV + g
            o_ref[:, h * HEAD_DIM:(h + 1) * HEAD_DIM] = (
                o[g * blk:(g + 1) * blk] * _head_gate((zb0_ref, zb1_ref), h)).astype(BF16)


def _lat_attention(proj, kv, ctx_k, ctx_v, sink, tab_q, tab_k, bias, swap, n_seq, seq_len):
    kv = _rope_kv(kv, tab_k, seq_len)
    t = proj.shape[0]
    blk = ATTN_BLOCK
    assert blk - 1 <= WINDOW <= blk
    nb = seq_len // blk
    lc = ctx_k.shape[1]

    def row(b, i):
        return b * nb + i

    def prev(i):
        return jnp.maximum(i - 1, 0)

    def nxt(i):
        return jnp.minimum(i + 1, nb - 1)

    def bias_variant(b, i):
        return ((i > 0).astype(jnp.int32) * 2 + (i < nb - 1).astype(jnp.int32), 0, 0)

    return pl.pallas_call(
        _lat_attn_kernel,
        out_shape=jax.ShapeDtypeStruct((t, D_ATTN), BF16),
        grid=(n_seq, nb),
        in_specs=[
            pl.BlockSpec(memory_space=pltpu.SMEM),
            pl.BlockSpec((blk, D_ATTN), lambda b, i: (row(b, i), COL_Q // D_ATTN)),
            pl.BlockSpec((blk, 2 * D_KV), lambda b, i: (row(b, prev(i)), 0)),
            pl.BlockSpec((blk, 2 * D_KV), lambda b, i: (row(b, i), 0)),
            pl.BlockSpec((blk, 2 * D_KV), lambda b, i: (row(b, nxt(i)), 0)),
            pl.BlockSpec((None, lc, D_KV), lambda b, i: (b, 0, 0)),
            pl.BlockSpec((None, lc, D_KV), lambda b, i: (b, 0, 0)),
            pl.BlockSpec((blk, D_ATTN // 2), lambda b, i: (row(b, i), COL_ZB // (D_ATTN // 2))),
            pl.BlockSpec((blk, D_ATTN // 2), lambda b, i: (row(b, i), COL_ZB // (D_ATTN // 2) + 1)),
            pl.BlockSpec((blk, 2 * HEAD_DIM), lambda b, i: (i, 0)),
            pl.BlockSpec((HEAD_DIM, HEAD_DIM), lambda b, i: (0, 0)),
            pl.BlockSpec((None, blk, 3 * blk), bias_variant),
        ],
        out_specs=pl.BlockSpec((blk, D_ATTN), lambda b, i: (row(b, i), 0)),
        compiler_params=_params(("arbitrary", "arbitrary")),
        name="lat_attention",
    )(sink, proj, kv, kv, kv, ctx_k, ctx_v, proj, proj, tab_q, swap, bias)


def _rope_tables(seq_len):
    rows = seq_len // GRID_W
    row = np.repeat(np.arange(rows), GRID_W).astype(np.float64)
    col = np.tile(np.arange(GRID_W), rows).astype(np.float64)
    half = HEAD_DIM // 2
    inv = ROPE_BASE ** (-np.arange(0, half, 2, dtype=np.float64) / half)
    ang_r = row[:, None] * inv[None, :]
    ang_c = col[:, None] * inv[None, :]
    cos_t = np.concatenate([np.cos(ang_r)] * 2 + [np.cos(ang_c)] * 2, axis=-1)
    sin_t = np.concatenate([-np.sin(ang_r), np.sin(ang_r), -np.sin(ang_c), np.sin(ang_c)], axis=-1)
    tab_k = np.concatenate([cos_t, sin_t], axis=-1)
    tab_q = tab_k * HEAD_DIM ** -0.5
    return jnp.asarray(tab_q.astype(np.float32)), jnp.asarray(tab_k.astype(np.float32))


def _rope_swap_matrix():
    quarter = HEAD_DIM // 4
    d = np.arange(HEAD_DIM)
    src = np.where(d % (2 * quarter) < quarter, d + quarter, d - quarter)
    return jnp.asarray((np.arange(HEAD_DIM)[:, None] == src[None, :]).astype(BF16))


def _window_bias():
    blk = ATTN_BLOCK
    r = np.arange(blk)[:, None]
    s = np.arange(3 * blk)[None, :]
    band = np.abs(s - blk - r) <= WINDOW
    out = []
    for has_prev in (False, True):
        for has_next in (False, True):
            ok = band & ((s >= blk) | has_prev) & ((s < 2 * blk) | has_next)
            out.append(np.where(ok, 0.0, -np.inf).astype(np.float32))
    return jnp.asarray(np.stack(out, axis=0))


def _sgu_kernel(u_ref, v_ref, z_ref, g_ref, b_ref, ws_ref, bs_ref, o_ref, *, n_chunks):
    for n in range(n_chunks):
        rows = slice(n * SGU_CHUNK, (n + 1) * SGU_CHUNK)
        v = _layernorm_f32(jax.nn.gelu(v_ref[rows, :].astype(F32))) * g_ref[...] + b_ref[...]
        v = v.astype(BF16)
        for g in range(N_SGU_GROUPS):
            cols = slice(g * SGU_GROUP_CH, (g + 1) * SGU_GROUP_CH)
            vm = jnp.dot(ws_ref[g], v[:, cols], preferred_element_type=F32) + bs_ref[:, g:g + 1]
            u = jax.nn.gelu(u_ref[rows, cols].astype(F32))
            z = z_ref[rows, cols].astype(F32)
            o_ref[rows, cols] = (u * vm * jax.nn.silu(z)).astype(BF16)


def _sgu(proj, ln_g, ln_b, w_s_bf16, b_s_t):
    t = proj.shape[0]
    tm = ROW_TM
    return pl.pallas_call(
        functools.partial(_sgu_kernel, n_chunks=tm // SGU_CHUNK),
        out_shape=jax.ShapeDtypeStruct((t, D_SGU), BF16),
        grid=(t // tm,),
        in_specs=[
            pl.BlockSpec((tm, D_SGU), lambda i: (i, COL_U // D_SGU)),
            pl.BlockSpec((tm, D_SGU), lambda i: (i, COL_VS // D_SGU)),
            pl.BlockSpec((tm, D_SGU), lambda i: (i, COL_ZC // D_SGU)),
            pl.BlockSpec((1, D_SGU), lambda i: (0, 0)),
            pl.BlockSpec((1, D_SGU), lambda i: (0, 0)),
            pl.BlockSpec((N_SGU_GROUPS, SGU_CHUNK, SGU_CHUNK), lambda i: (0, 0, 0)),
            pl.BlockSpec((SGU_CHUNK, N_SGU_GROUPS), lambda i: (0, 0)),
        ],
        out_specs=pl.BlockSpec((tm, D_SGU), lambda i: (i, 0)),
        compiler_params=_params(("arbitrary",)),
        name="sgu",
    )(proj, proj, proj, ln_g.reshape(1, D_SGU), ln_b.reshape(1, D_SGU), w_s_bf16, b_s_t)


def _merge_kernel(mg_ref, ya_ref, yb_ref, yc_ref, x_ref, gate_ref, wpa_ref, wpb_ref, wpc_ref, wout_ref,
                  lng_ref, lnb_ref, y_ref, *, alpha):
    d = D_MODEL

    def branch(k, y_r, w_r):
        gate = jax.nn.sigmoid(mg_ref[:, k * d:(k + 1) * d].astype(F32))
        return gate * jnp.dot(y_r[...], w_r[...], preferred_element_type=F32)

    merged = branch(0, ya_ref, wpa_ref) + branch(1, yb_ref, wpb_ref) + branch(2, yc_ref, wpc_ref)
    out = jnp.dot(merged.astype(BF16), wout_ref[...], preferred_element_type=F32)
    z = alpha * x_ref[...] + gate_ref[0] * out
    y_ref[...] = _layernorm_f32(z) * lng_ref[...] + lnb_ref[...]


def _merge(proj, ya, yb, yc, x2, gate, wpa, wpb, wpc, wout, ln_g, ln_b, seq_len, alpha):
    t, d = x2.shape
    tm = MERGE_TM
    if gate.shape[0] == 1:
        mod_map = lambda i: (0, 0, 0)
    else:
        assert seq_len % tm == 0
        tiles_per_seq = seq_len // tm
        mod_map = lambda i: (i // tiles_per_seq, 0, 0)

    def const(shape):
        return pl.BlockSpec(shape, lambda i: (0,) * len(shape), pipeline_mode=pl.Buffered(1))

    return pl.pallas_call(
        functools.partial(_merge_kernel, alpha=alpha),
        out_shape=jax.ShapeDtypeStruct((t, d), F32),
        grid=(t // tm,),
        in_specs=[
            pl.BlockSpec((tm, D_MG), lambda i: (i, 0)),
            pl.BlockSpec((tm, D_SSM), lambda i: (i, 0)),
            pl.BlockSpec((tm, D_ATTN), lambda i: (i, 0)),
            pl.BlockSpec((tm, D_SGU), lambda i: (i, 0)),
            pl.BlockSpec((tm, d), lambda i: (i, 0)),
            pl.BlockSpec((1, 1, d), mod_map),
            const((D_SSM, d)), const((D_ATTN, d)), const((D_SGU, d)), const((d, d)),
            const((1, d)), const((1, d)),
        ],
        out_specs=pl.BlockSpec((tm, d), lambda i: (i, 0)),
        compiler_params=_params(("arbitrary",)),
        name="merge_out",
    )(proj, ya, yb, yc, x2, gate, wpa, wpb, wpc, wout, ln_g.reshape(1, d), ln_b.reshape(1, d))


def _layer(x2, n_seq, seq_len, mod, lp, ctx_k, ctx_v, s0, rope, alpha):
    d = D_MODEL
    nbm = mod.shape[0]
    shift = mod[:, 0:d].reshape(nbm, 1, d)
    scale = mod[:, d:2 * d].reshape(nbm, 1, d)
    gate = mod[:, 2 * d:3 * d].reshape(nbm, 1, d)
    proj, kv, xa = _inproj(x2, scale, shift, lp['w_in'], lp['layer'], seq_len)

    n_seg = seq_len // SSM_SEG
    yssm, fin = _ssm_scan(xa, lp['fold'], lp['w_intra'], lp['w_state'], lp['w_enter'], lp['ssm_small'],
                          lp['layer'], s0, n_seq, n_seg)
    ya = _ssm_post(proj, yssm, lp['d_skip'], lp['w_glu'], lp['b_glu'])

    if ctx_k is None:
        yb = _ctx_attention(proj, kv, lp['sink'], seq_len)
    else:
        yb = _lat_attention(proj, kv, ctx_k, ctx_v, lp['sink'], *rope, n_seq, seq_len)

    yc = _sgu(proj, lp['sgu_g'], lp['sgu_b'], lp['w_s'], lp['b_s_t'])

    y = _merge(proj, ya, yb, yc, x2, gate, lp['w_pa'], lp['w_pb'], lp['w_pc'], lp['w_out'],
               lp['ln_g'], lp['ln_b'], seq_len, alpha)
    return y, kv, fin


def _states_to_lanes(s):
    b = s.shape[0]
    return s.astype(F32).transpose(3, 0, 2, 1, 4).reshape(N_SSM_GROUPS, b, 4 * SSM_STATE)


def _lanes_to_states(f):
    b = f.shape[1]
    return f.reshape(N_SSM_GROUPS, b, 2, 2, SSM_STATE).transpose(1, 3, 2, 0, 4)


def kernel(x_prompt, x_sample, cache_k, cache_v, state_ssm, c, c_ctx, w_ada, b_ada, w_in, ssm_lam_re, ssm_lam_im, ssm_log_step, ssm_b_re, ssm_b_im, ssm_c_re, ssm_c_im, ssm_d, w_glu, b_glu, attn_sink, sgu_ln_g, sgu_ln_b, w_spatial, b_spatial, w_proj_a, w_proj_b, w_proj_c, w_out, ln_g, ln_b):
    depth = w_in.shape[0]
    batch, seq, d = x_prompt.shape
    dec_batch, dec_seq, _ = x_sample.shape
    past_len = cache_k.shape[2]
    alpha = (2 * depth) ** 0.25

    n_cond = 1 + dec_batch
    cond_rows = -(-n_cond // 8) * 8
    cond = jnp.concatenate([c_ctx[None, :], c, jnp.zeros((cond_rows - n_cond, d), F32)], axis=0)
    mod = _ada_mod(cond, w_ada, b_ada)

    w_in_bf16 = w_in.astype(BF16)
    fold = _ssm_fold_matrix()
    tables = jax.vmap(_ssm_tables)(ssm_lam_re, ssm_lam_im, ssm_log_step, ssm_b_re, ssm_b_im, ssm_c_re, ssm_c_im)
    layers = []
    for l in range(depth):
        w_intra, w_state, w_enter, small = tables
        layers.append({
            'w_in': w_in_bf16, 'layer': l,
            'fold': fold, 'w_intra': w_intra, 'w_state': w_state, 'w_enter': w_enter, 'ssm_small': small,
            'd_skip': ssm_d[l], 'w_glu': w_glu[l].astype(BF16), 'b_glu': b_glu[l],
            'sink': attn_sink[l].astype(F32),
            'sgu_g': sgu_ln_g[l], 'sgu_b': sgu_ln_b[l],
            'w_s': w_spatial[l].astype(BF16), 'b_s_t': b_spatial[l].T.astype(F32),
            'w_pa': w_proj_a[l].astype(BF16), 'w_pb': w_proj_b[l].astype(BF16),
            'w_pc': w_proj_c[l].astype(BF16), 'w_out': w_out[l].astype(BF16),
            'ln_g': ln_g[l], 'ln_b': ln_b[l],
        })

    h = x_prompt.reshape(batch * seq, d)
    zero_state = jnp.zeros((N_SSM_GROUPS, batch, 4 * SSM_STATE), F32)
    ks, vs, ss = [], [], []
    for l in range(depth):
        h, kv, fin = _layer(h, batch, seq, mod[l, 0:1], layers[l], None, None, zero_state, None, alpha)
        ks.append(kv[:, :D_KV].reshape(batch, seq, N_KV_HEADS, HEAD_DIM))
        vs.append(kv[:, D_KV:].reshape(batch, seq, N_KV_HEADS, HEAD_DIM))
        ss.append(_lanes_to_states(fin))
    y_prompt = h.reshape(batch, seq, d)
    new_cache_k = jnp.stack(ks, axis=1)
    new_cache_v = jnp.stack(vs, axis=1)
    new_state_ssm = jnp.stack(ss, axis=1)

    rope = _rope_tables(dec_seq) + (_window_bias(), _rope_swap_matrix())
    z = x_sample.reshape(dec_batch * dec_seq, d)
    for l in range(depth):
        ctx_k = cache_k[:, l].reshape(dec_batch, past_len, D_KV).astype(F32)
        ctx_v = cache_v[:, l].reshape(dec_batch, past_len, D_KV).astype(F32)
        z, _, _ = _layer(z, dec_batch, dec_seq, mod[l, 1:1 + dec_batch], layers[l], ctx_k, ctx_v,
                         _states_to_lanes(state_ssm[:, l]), rope, alpha)
    y_sample = z.reshape(dec_batch, dec_seq, d)
    return (y_prompt, y_sample, new_cache_k, new_cache_v, new_state_ssm)
```

```python
import functools
import math

import jax
import jax.numpy as jnp
import numpy as np
from jax import lax
from jax.experimental import pallas as pl
from jax.experimental.pallas import tpu as pltpu

F32 = jnp.float32
BF16 = jnp.bfloat16

D_MODEL = 2048
GRID_W = 64
D_SSM = 512
SSM_GROUP = 16
N_SSM_GROUPS = D_SSM // SSM_GROUP
SSM_STATE = 64
HEAD_DIM = 128
N_HEADS = 8
N_KV_HEADS = 2
Q_PER_KV = N_HEADS // N_KV_HEADS
D_ATTN = N_HEADS * HEAD_DIM
D_KV = N_KV_HEADS * HEAD_DIM
WINDOW = 128
ATTN_BLOCK = 128
ROPE_BASE = 10000.0
D_SGU = 512
SGU_CHUNK = 128
SGU_GROUP_CH = 128
N_SGU_GROUPS = D_SGU // SGU_GROUP_CH
N_BRANCH = 3
LN_EPS = 1e-5

D_MG = N_BRANCH * D_MODEL
COL_MG = 0
COL_XA = COL_MG + D_MG
COL_ZA = COL_XA + D_SSM
COL_Q = COL_ZA + D_SSM
COL_K = COL_Q + D_ATTN
COL_V = COL_K + D_KV
COL_ZB = COL_V + D_KV
COL_U = COL_ZB + D_ATTN
COL_VS = COL_U + D_SGU
COL_ZC = COL_VS + D_SGU
D_IN = COL_ZC + D_SGU
W_IN_ROTATE = D_IN - D_MG

SSM_T = 16
SSM_NC = 16
SSM_TC = SSM_T * SSM_GROUP
SSM_SEG = SSM_T * SSM_NC
SSM_SMALL_ROWS = 40
SSM_LANE_GROUPS = 128 // SSM_GROUP
SSM_POW_ROWS = 24

VMEM_LIMIT_BYTES = 56 * 1024 * 1024

INPROJ_TM = 1024
INPROJ_TN = 1024
ROW_TM = 512
MERGE_TM = 512


def _params(sem):
    return pltpu.CompilerParams(dimension_semantics=sem, vmem_limit_bytes=VMEM_LIMIT_BYTES)


def _layernorm_f32(x):
    mu = jnp.mean(x, axis=-1, keepdims=True)
    xc = x - mu
    var = jnp.mean(xc * xc, axis=-1, keepdims=True)
    return xc * lax.rsqrt(var + LN_EPS)


def _ada_kernel(cond_ref, w_ref, b_ref, o_ref):
    c = cond_ref[...]
    a = (c * jax.nn.sigmoid(c)).astype(BF16)
    o_ref[...] = jnp.dot(a, w_ref[...].astype(BF16), preferred_element_type=F32) + b_ref[...]


def _ada_mod(cond, w_ada, b_ada):
    depth, d, n = w_ada.shape
    r = cond.shape[0]
    tn = 512
    return pl.pallas_call(
        _ada_kernel,
        out_shape=jax.ShapeDtypeStruct((depth, r, n), F32),
        grid=(depth, n // tn),
        in_specs=[
            pl.BlockSpec((r, d), lambda l, j: (0, 0)),
            pl.BlockSpec((None, d, tn), lambda l, j: (l, 0, j)),
            pl.BlockSpec((None, 1, tn), lambda l, j: (l, 0, j)),
        ],
        out_specs=pl.BlockSpec((None, r, tn), lambda l, j: (l, 0, j)),
        compiler_params=_params(("arbitrary", "arbitrary")),
        name="ada_mod",
    )(cond, w_ada, b_ada.reshape(depth, 1, n))


def _inproj_kernel(x_ref, scale_ref, shift_ref, w_ref, proj_ref, kv_ref, xa_ref, h_scr, *, kv_tile, xa_tile):
    j = pl.program_id(1)

    @pl.when(j == 0)
    def _():
        h = _layernorm_f32(x_ref[...]) * (1.0 + scale_ref[0]) + shift_ref[0]
        h_scr[...] = h.astype(BF16)

    acc = jnp.dot(h_scr[...], w_ref[...], preferred_element_type=F32)
    proj_ref[...] = acc.astype(BF16)

    @pl.when(j == kv_tile)
    def _():
        kv_ref[...] = acc[:, 0:2 * D_KV]

    @pl.when(j == xa_tile)
    def _():
        xa_ref[...] = acc[:, 0:D_SSM]


def _inproj(x2, scale, shift, w_bf16, layer, seq_len):
    t, d = x2.shape
    n = w_bf16.shape[2]
    tm, tn = INPROJ_TM, INPROJ_TN
    assert t % tm == 0 and n % tn == 0 and COL_K % tn == 0 and COL_XA % tn == 0
    assert 2 * D_KV <= tn and D_SSM <= tn and W_IN_ROTATE % tn == 0
    n_tiles = n // tn
    rot_tiles = W_IN_ROTATE // tn
    nb = scale.shape[0]
    if nb == 1:
        mod_map = lambda i, j: (0, 0, 0)
    else:
        assert seq_len % tm == 0
        tiles_per_seq = seq_len // tm
        mod_map = lambda i, j: (i // tiles_per_seq, 0, 0)
    return pl.pallas_call(
        functools.partial(_inproj_kernel, kv_tile=COL_K // tn, xa_tile=COL_XA // tn),
        out_shape=(jax.ShapeDtypeStruct((t, n), BF16), jax.ShapeDtypeStruct((t, 2 * D_KV), F32),
                   jax.ShapeDtypeStruct((t, D_SSM), F32)),
        grid=(t // tm, n // tn),
        in_specs=[
            pl.BlockSpec((tm, d), lambda i, j: (i, 0)),
            pl.BlockSpec((1, 1, d), mod_map),
            pl.BlockSpec((1, 1, d), mod_map),
            pl.BlockSpec((None, d, tn), lambda i, j: (layer, 0, (j + rot_tiles) % n_tiles)),
        ],
        out_specs=(
            pl.BlockSpec((tm, tn), lambda i, j: (i, j)),
            pl.BlockSpec((tm, 2 * D_KV), lambda i, j: (i, 0)),
            pl.BlockSpec((tm, D_SSM), lambda i, j: (i, 0)),
        ),
        scratch_shapes=[pltpu.VMEM((tm, d), BF16)],
        compiler_params=_params(("arbitrary", "arbitrary")),
        name="ln_inproj",
    )(x2, scale, shift, w_bf16)


def _zoh(lr, li, ls):
    dt = jnp.exp(ls)
    mag = jnp.exp(lr * dt)
    ar = mag * jnp.cos(li * dt)
    ai = mag * jnp.sin(li * dt)
    den = lr * lr + li * li
    nr = ar - 1.0
    return ar, ai, (nr * lr + ai * li) / den, (ai * lr - nr * li) / den


def _cmul(xr, xi, yr, yi):
    return xr * yr - xi * yi, xr * yi + xi * yr


def _cpow(br, bi, expo, n_bits):
    shape = jnp.broadcast_shapes(br.shape, expo.shape)
    pr = jnp.ones(shape, F32)
    pi = jnp.zeros(shape, F32)
    for bit in range(n_bits):
        sel = ((expo >> bit) & 1) == 1
        nr, ni = _cmul(pr, pi, br, bi)
        pr = jnp.where(sel, nr, pr)
        pi = jnp.where(sel, ni, pi)
        if bit + 1 < n_bits:
            br, bi = _cmul(br, bi, br, bi)
    return pr, pi


def _ssm_tables_kernel(laml_ref, lams_ref, bt_ref, ct_ref, ec_ref, ek_ref, wi_ref, ws_ref, we_ref, small_ref):
    p_n, t_n, c_n = SSM_STATE, SSM_T, SSM_GROUP
    p2 = 2 * p_n
    tc = t_n * c_n
    n_bits = 5
    assert max(t_n, SSM_NC) < 2 ** n_bits and SSM_POW_ROWS <= 2 ** n_bits

    def dot(a, b):
        return jnp.dot(a, b, precision=lax.Precision.HIGHEST, preferred_element_type=F32)

    ar, ai, f_re, f_im = _zoh(laml_ref[0:1, :], laml_ref[1:2, :], laml_ref[2:3, :])
    is_fwd = lax.broadcasted_iota(jnp.int32, (1, p2), 1) < p_n
    bbt_re, bbt_im = _cmul(f_re, f_im, bt_ref[:, 0:p2], bt_ref[:, p2:2 * p2])
    krow = lax.broadcasted_iota(jnp.int32, (SSM_POW_ROWS, 1), 0)
    pw_re, pw_im = _cpow(ar, ai, krow, n_bits)
    for j in range(t_n):
        pm_re = jnp.where(is_fwd, pw_re[t_n - 1 - j:t_n - j], pw_re[j:j + 1])
        pm_im = jnp.where(is_fwd, pw_im[t_n - 1 - j:t_n - j], pw_im[j:j + 1])
        s_re, s_im = _cmul(bbt_re, bbt_im, pm_re, pm_im)
        ws_ref[j * c_n:(j + 1) * c_n, 0:p2] = s_re.astype(BF16)
        ws_ref[j * c_n:(j + 1) * c_n, p2:2 * p2] = s_im.astype(BF16)
    a1r = pw_re[t_n:t_n + 1]
    a1i = pw_im[t_n:t_n + 1]
    q_re, q_im = _cpow(a1r, a1i, jnp.where(is_fwd, krow, jnp.maximum(SSM_NC - 1 - krow, 0)), n_bits)
    as_re, as_im = _cpow(a1r, a1i, jnp.full((1, 1), SSM_NC, jnp.int32), n_bits)
    small_ref[0:SSM_NC, :] = q_re[0:SSM_NC]
    small_ref[SSM_NC:2 * SSM_NC, :] = q_im[0:SSM_NC]
    small_ref[2 * SSM_NC:SSM_SMALL_ROWS, :] = jnp.concatenate(
        [a1r, a1i, as_re, as_im, jnp.zeros((SSM_SMALL_ROWS - 2 * SSM_NC - 4, p2), F32)], axis=0)

    klane = jnp.minimum(lax.broadcasted_iota(jnp.int32, (1, ct_ref.shape[1]), 1), t_n)
    cpt = []
    for d in range(2):
        ar_s, ai_s, _, _ = _zoh(lams_ref[:, 3 * d:3 * d + 1], lams_ref[:, 3 * d + 1:3 * d + 2],
                                lams_ref[:, 3 * d + 2:3 * d + 3])
        pt_re, pt_im = _cpow(ar_s, ai_s, klane, n_bits)
        c_re = dot(ct_ref[...], ec_ref[d])
        c_im = dot(ct_ref[...], ec_ref[2 + d])
        cpt.append(_cmul(c_re, c_im, dot(pt_re, ek_ref[d]), dot(pt_im, ek_ref[d])))
        e_re, e_im = _cmul(c_re, c_im, dot(pt_re, ek_ref[2 + d]), dot(pt_im, ek_ref[2 + d]))
        we_ref[d * p_n:(d + 1) * p_n, :] = e_re.astype(BF16)
        we_ref[p2 + d * p_n:p2 + (d + 1) * p_n, :] = (-e_im).astype(BF16)
    cpt_re = jnp.concatenate([cpt[0][0], cpt[1][0]], axis=0)
    cpt_im = jnp.concatenate([cpt[0][1], cpt[1][1]], axis=0)
    taps = []
    for sel in (is_fwd, jnp.logical_not(is_fwd)):
        taps.append(dot(jnp.where(sel, bbt_re, 0.0), cpt_re) - dot(jnp.where(sel, bbt_im, 0.0), cpt_im))
    lane = lax.broadcasted_iota(jnp.int32, (c_n, tc), 1)
    for j in range(t_n):
        fwd = jnp.where(lane >= j * c_n, pltpu.roll(taps[0], j * c_n, 1), 0.0)
        bwd = jnp.where(lane < (j + 1) * c_n, pltpu.roll(taps[1], (tc - (t_n - 1 - j) * c_n) % tc, 1), 0.0)
        wi_ref[j * c_n:(j + 1) * c_n, :] = (fwd + bwd).astype(BF16)


def _ssm_table_constants():
    t_n, c_n = SSM_T, SSM_GROUP
    rows = np.arange(128)[:, None]
    tok = (np.arange(t_n * c_n) // c_n)[None, :]
    ch = (np.arange(t_n * c_n) % c_n)[None, :]
    ec = np.stack([(rows == q * c_n + ch) for q in range(4)], axis=0)
    ek = np.stack([rows == tok, rows == t_n - 1 - tok, rows == tok + 1, rows == t_n - tok], axis=0)
    return jnp.asarray(ec.astype(np.float32)), jnp.asarray(ek.astype(np.float32))


def _ssm_tables(lam_re, lam_im, log_step, b_re, b_im, c_re, c_im):
    l_n, _, g_n, p_n = lam_re.shape
    c_n, t_n = SSM_GROUP, SSM_T
    tc = t_n * c_n
    p2, p4 = 2 * p_n, 4 * p_n
    ls = jnp.broadcast_to(log_step.astype(F32)[..., None], lam_re.shape)

    def lanes(x):
        return x.astype(F32).transpose(0, 2, 1, 3).reshape(l_n, g_n, 1, p2)

    laml = jnp.concatenate([lanes(lam_re), lanes(lam_im), lanes(ls), jnp.zeros((l_n, g_n, 5, p2), F32)], axis=2)
    cols = [x.astype(F32)[:, d] for d in range(2) for x in (lam_re, lam_im, ls)]
    lams = jnp.stack(cols + [jnp.zeros_like(cols[0])] * 2, axis=-1)

    def b_lanes(x):
        return x.astype(F32).transpose(0, 2, 4, 1, 3).reshape(l_n, g_n, c_n, p2)

    bt = jnp.concatenate([b_lanes(b_re), b_lanes(b_im)], axis=-1)

    def c_lanes(x):
        return x.astype(F32).transpose(0, 2, 4, 1, 3).reshape(l_n, g_n, p_n, 2 * c_n)

    ct = jnp.concatenate([c_lanes(c_re), c_lanes(c_im), jnp.zeros((l_n, g_n, p_n, 128 - 4 * c_n), F32)], axis=-1)
    ec, ek = _ssm_table_constants()

    def per_group(*dims):
        return pl.BlockSpec((None, None) + dims, lambda l, g: (l, g) + (0,) * len(dims))

    def const(shape):
        return pl.BlockSpec(shape, lambda l, g: (0,) * len(shape))

    return pl.pallas_call(
        _ssm_tables_kernel,
        out_shape=(jax.ShapeDtypeStruct((l_n, g_n, tc, tc), BF16), jax.ShapeDtypeStruct((l_n, g_n, tc, p4), BF16),
                   jax.ShapeDtypeStruct((l_n, g_n, p4, tc), BF16),
                   jax.ShapeDtypeStruct((l_n, g_n, SSM_SMALL_ROWS, p2), F32)),
        grid=(l_n, g_n),
        in_specs=[per_group(8, p2), per_group(p_n, 8), per_group(c_n, p4), per_group(p_n, 128),
                  const(ec.shape), const(ek.shape)],
        out_specs=(per_group(tc, tc), per_group(tc, p4), per_group(p4, tc), per_group(SSM_SMALL_ROWS, p2)),
        compiler_params=_params(("arbitrary", "arbitrary")),
        name="ssm_tables",
    )(laml, lams, bt, ct, ec, ek)


def _ssm_fold_matrix():
    n = SSM_T * SSM_LANE_GROUPS * SSM_GROUP
    src = np.arange(n)
    j = src // (SSM_LANE_GROUPS * SSM_GROUP)
    g = (src // SSM_GROUP) % SSM_LANE_GROUPS
    c = src % SSM_GROUP
    dst = g * SSM_TC + j * SSM_GROUP + c
    return jnp.asarray((dst[:, None] == np.arange(n)[None, :]).astype(BF16))


def _ssm_kernel(xa_ref, fold_ref, wi_ref, ws_ref, we_ref, small_ref, s0_ref, y_ref, fin_ref,
                v_scr, s_scr, yy_scr, fre_scr, fim_scr, efre_scr, efim_scr, ebre_scr, ebim_scr,
                *, n_seq, n_seg):
    p2 = 2 * SSM_STATE
    p4 = 4 * SSM_STATE
    nbp = n_seq * n_seg
    lanes = SSM_LANE_GROUPS * SSM_GROUP
    nt = (((1,), (1,)), ((), ()))

    blocks = []
    for c in range(SSM_NC):
        toks = [xa_ref[pl.ds(c * SSM_T + j, nbp, stride=SSM_SEG), :].astype(BF16) for j in range(SSM_T)]
        blocks.append(jnp.concatenate(toks, axis=1))
    xx = jnp.concatenate(blocks, axis=0)
    uu = jnp.dot(xx, fold_ref[...], preferred_element_type=F32).astype(BF16)

    lane = lax.broadcasted_iota(jnp.int32, (1, p2), 1)
    is_fwd = lane < SSM_STATE
    for g in range(SSM_LANE_GROUPS):
        gc = slice(g * SSM_TC, (g + 1) * SSM_TC)
        u = uu[:, gc]
        yy_scr[:, gc] = jnp.dot(u, wi_ref[g], preferred_element_type=F32)
        v_scr[:, gc] = jnp.dot(u, ws_ref[g], preferred_element_type=F32)

    for g in range(SSM_LANE_GROUPS):
        o = g * p4
        a1r = small_ref[g, 2 * SSM_NC:2 * SSM_NC + 1, :]
        a1i = small_ref[g, 2 * SSM_NC + 1:2 * SSM_NC + 2, :]
        asr = small_ref[g, 2 * SSM_NC + 2:2 * SSM_NC + 3, :]
        asi = small_ref[g, 2 * SSM_NC + 3:2 * SSM_NC + 4, :]

        st_re = jnp.zeros((nbp, p2), F32)
        st_im = jnp.zeros((nbp, p2), F32)
        for c in range(SSM_NC):
            cb = SSM_NC - 1 - c
            rf = slice(c * nbp, (c + 1) * nbp)
            rb = slice(cb * nbp, (cb + 1) * nbp)
            s_scr[rf, o:o + SSM_STATE] = st_re[:, 0:SSM_STATE]
            s_scr[rb, o + SSM_STATE:o + p2] = st_re[:, SSM_STATE:p2]
            s_scr[rf, o + p2:o + p2 + SSM_STATE] = st_im[:, 0:SSM_STATE]
            s_scr[rb, o + p2 + SSM_STATE:o + p4] = st_im[:, SSM_STATE:p2]
            v_re = jnp.where(is_fwd, v_scr[rf, o:o + p2], v_scr[rb, o:o + p2])
            v_im = jnp.where(is_fwd, v_scr[rf, o + p2:o + p4], v_scr[rb, o + p2:o + p4])
            st_re, st_im = (a1r * st_re - a1i * st_im + v_re, a1r * st_im + a1i * st_re + v_im)

        fre_scr[...] = st_re
        fim_scr[...] = st_im
        s0 = s0_ref[g]
        e_re = s0[:, 0:p2]
        e_im = s0[:, p2:p4]
        for k in range(n_seg):
            kb = n_seg - 1 - k
            rows_f = pl.ds(k, n_seq, stride=n_seg)
            rows_b = pl.ds(kb, n_seq, stride=n_seg)
            efre_scr[rows_f, :] = e_re
            efim_scr[rows_f, :] = e_im
            ebre_scr[rows_b, :] = e_re
            ebim_scr[rows_b, :] = e_im
            f_re = jnp.where(is_fwd, fre_scr[rows_f, :], fre_scr[rows_b, :])
            f_im = jnp.where(is_fwd, fim_scr[rows_f, :], fim_scr[rows_b, :])
            e_re, e_im = (asr * e_re - asi * e_im + f_re, asr * e_im + asi * e_re + f_im)
        fin_ref[g] = jnp.concatenate([e_re, e_im], axis=-1)

        en_re = jnp.where(is_fwd, efre_scr[...], ebre_scr[...])
        en_im = jnp.where(is_fwd, efim_scr[...], ebim_scr[...])
        for c in range(SSM_NC):
            rows = slice(c * nbp, (c + 1) * nbp)
            cr = small_ref[g, c:c + 1, :]
            ci = small_ref[g, SSM_NC + c:SSM_NC + c + 1, :]
            s_scr[rows, o:o + p2] = s_scr[rows, o:o + p2] + (cr * en_re - ci * en_im)
            s_scr[rows, o + p2:o + p4] = s_scr[rows, o + p2:o + p4] + (cr * en_im + ci * en_re)

    for g in range(SSM_LANE_GROUPS):
        gc = slice(g * SSM_TC, (g + 1) * SSM_TC)
        yy_scr[:, gc] += jnp.dot(s_scr[:, g * p4:(g + 1) * p4].astype(BF16), we_ref[g],
                                 preferred_element_type=F32)

    zz = lax.dot_general(yy_scr[...].astype(BF16), fold_ref[...], nt, preferred_element_type=F32)
    for c in range(SSM_NC):
        for j in range(SSM_T):
            y_ref[pl.ds(c * SSM_T + j, nbp, stride=SSM_SEG), :] = zz[c * nbp:(c + 1) * nbp,
                                                                      j * lanes:(j + 1) * lanes]


def _ssm_scan(xa, fold, w_intra, w_state, w_enter, small, layer, s0, n_seq, n_seg):
    t = xa.shape[0]
    g_n = N_SSM_GROUPS
    lg = SSM_LANE_GROUPS
    lanes = lg * SSM_GROUP
    p4 = 4 * SSM_STATE
    nbp = n_seq * n_seg
    r = SSM_NC * nbp
    assert t == r * SSM_T and fold.shape == (SSM_T * lanes, lg * SSM_TC)
    const = lambda shape: pl.BlockSpec(shape, lambda i: (0,) * len(shape), pipeline_mode=pl.Buffered(1))
    return pl.pallas_call(
        functools.partial(_ssm_kernel, n_seq=n_seq, n_seg=n_seg),
        out_shape=(jax.ShapeDtypeStruct((t, D_SSM), F32), jax.ShapeDtypeStruct((g_n, n_seq, p4), F32)),
        grid=(g_n // lg,),
        in_specs=[
            pl.BlockSpec((t, lanes), lambda i: (0, i)),
            const(fold.shape),
            pl.BlockSpec((None, lg, SSM_TC, SSM_TC), lambda i: (layer, i, 0, 0)),
            pl.BlockSpec((None, lg, SSM_TC, p4), lambda i: (layer, i, 0, 0)),
            pl.BlockSpec((None, lg, p4, SSM_TC), lambda i: (layer, i, 0, 0)),
            pl.BlockSpec((None, lg, SSM_SMALL_ROWS, 2 * SSM_STATE), lambda i: (layer, i, 0, 0)),
            pl.BlockSpec((lg, n_seq, p4), lambda i: (i, 0, 0)),
        ],
        out_specs=(
            pl.BlockSpec((t, lanes), lambda i: (0, i)),
            pl.BlockSpec((lg, n_seq, p4), lambda i: (i, 0, 0)),
        ),
        scratch_shapes=[pltpu.VMEM((r, lg * p4), F32), pltpu.VMEM((r, lg * p4), F32),
                        pltpu.VMEM((r, lg * SSM_TC), F32)]
        + [pltpu.VMEM((nbp, 2 * SSM_STATE), F32)] * 6,
        compiler_params=_params(("arbitrary",)),
        name="ssm_scan",
    )(xa, fold, w_intra, w_state, w_enter, small, s0)


def _ssm_post_kernel(xa_ref, za_ref, ys_ref, d_ref, w_ref, b_ref, o_ref):
    y = xa_ref[...].astype(F32) * d_ref[...] + ys_ref[...]
    y = jax.nn.gelu(y)
    gl = jnp.dot(y.astype(BF16), w_ref[...], preferred_element_type=F32) + b_ref[...]
    y = y * jax.nn.sigmoid(gl)
    o_ref[...] = (y * jax.nn.silu(za_ref[...].astype(F32))).astype(BF16)


def _ssm_post(proj, yssm, d_skip, w_glu_bf16, b_glu):
    t = proj.shape[0]
    tm = ROW_TM
    return pl.pallas_call(
        _ssm_post_kernel,
        out_shape=jax.ShapeDtypeStruct((t, D_SSM), BF16),
        grid=(t // tm,),
        in_specs=[
            pl.BlockSpec((tm, D_SSM), lambda i: (i, COL_XA // D_SSM)),
            pl.BlockSpec((tm, D_SSM), lambda i: (i, COL_ZA // D_SSM)),
            pl.BlockSpec((tm, D_SSM), lambda i: (i, 0)),
            pl.BlockSpec((1, D_SSM), lambda i: (0, 0)),
            pl.BlockSpec((D_SSM, D_SSM), lambda i: (0, 0)),
            pl.BlockSpec((1, D_SSM), lambda i: (0, 0)),
        ],
        out_specs=pl.BlockSpec((tm, D_SSM), lambda i: (i, 0)),
        compiler_params=_params(("arbitrary",)),
        name="ssm_post",
    )(proj, proj, yssm, d_skip.reshape(1, D_SSM), w_glu_bf16, b_glu.reshape(1, D_SSM))


def _softmax_pv(scores, sink, values):
    m = sink
    for s in scores:
        m = jnp.maximum(m, jnp.max(s, axis=-1, keepdims=True))
    den = jnp.exp(sink - m)
    o = None
    for s, v in zip(scores, values):
        p = jnp.exp(s - m)
        den = den + jnp.sum(p, axis=-1, keepdims=True)
        pv = jnp.dot(p.astype(BF16), v, preferred_element_type=F32)
        o = pv if o is None else o + pv
    return o / den


def _softmax_pv_wide(s, sink, v_bf16):
    hd = v_bf16.shape[1]
    m = jnp.maximum(jnp.max(s, axis=-1, keepdims=True), sink)
    p = jnp.exp(s - m).astype(BF16)
    v_ones = jnp.concatenate([v_bf16, jnp.ones_like(v_bf16)], axis=1)
    o = jnp.dot(p, v_ones, preferred_element_type=F32)
    den = o[:, hd:] + jnp.exp(sink - m)
    return o[:, :hd] / den


def _head_gate(zb_refs, h):
    per = N_HEADS // 2
    z = zb_refs[h // per][:, (h % per) * HEAD_DIM:(h % per + 1) * HEAD_DIM].astype(F32)
    return jax.nn.silu(z)


def _ctx_attn_kernel(sink_ref, q_ref, kv_ref, zb0_ref, zb1_ref, o_ref):
    scale = HEAD_DIM ** -0.5
    nt = (((1,), (1,)), ((), ()))
    for kvh in range(N_KV_HEADS):
        k = (kv_ref[:, kvh * HEAD_DIM:(kvh + 1) * HEAD_DIM] * scale).astype(BF16)
        v = kv_ref[:, D_KV + kvh * HEAD_DIM:D_KV + (kvh + 1) * HEAD_DIM].astype(BF16)
        for g in range(Q_PER_KV):
            h = kvh * Q_PER_KV + g
            hs = slice(h * HEAD_DIM, (h + 1) * HEAD_DIM)
            s = lax.dot_general(q_ref[:, hs], k, nt, preferred_element_type=F32)
            o = _softmax_pv([s], sink_ref[h], [v])
            o_ref[:, hs] = (o * _head_gate((zb0_ref, zb1_ref), h)).astype(BF16)


def _ctx_attention(proj, kv, sink, seq_len):
    t = proj.shape[0]
    return pl.pallas_call(
        _ctx_attn_kernel,
        out_shape=jax.ShapeDtypeStruct((t, D_ATTN), BF16),
        grid=(t // seq_len,),
        in_specs=[
            pl.BlockSpec(memory_space=pltpu.SMEM),
            pl.BlockSpec((seq_len, D_ATTN), lambda b: (b, COL_Q // D_ATTN)),
            pl.BlockSpec((seq_len, 2 * D_KV), lambda b: (b, 0)),
            pl.BlockSpec((seq_len, D_ATTN // 2), lambda b: (b, COL_ZB // (D_ATTN // 2))),
            pl.BlockSpec((seq_len, D_ATTN // 2), lambda b: (b, COL_ZB // (D_ATTN // 2) + 1)),
        ],
        out_specs=pl.BlockSpec((seq_len, D_ATTN), lambda b: (b, 0)),
        compiler_params=_params(("arbitrary",)),
        name="ctx_attention",
    )(sink, proj, kv, proj, proj)


def _rope_partner(x):
    lane = lax.broadcasted_iota(jnp.int32, x.shape, 1)
    quarter = HEAD_DIM // 4
    first = (lane % (2 * quarter)) < quarter
    return jnp.where(first, pltpu.roll(x, HEAD_DIM - quarter, 1), pltpu.roll(x, quarter, 1))


def _rope(x, tab_ref):
    return x * tab_ref[:, 0:HEAD_DIM] + _rope_partner(x) * tab_ref[:, HEAD_DIM:2 * HEAD_DIM]


def _rope_kv_kernel(kv_ref, tab_ref, o_ref):
    for kvh in range(N_KV_HEADS):
        ks = slice(kvh * HEAD_DIM, (kvh + 1) * HEAD_DIM)
        o_ref[:, ks] = _rope(kv_ref[:, ks], tab_ref).astype(BF16)
    o_ref[:, D_KV:2 * D_KV] = kv_ref[:, D_KV:2 * D_KV].astype(BF16)


def _rope_kv(kv, tab_k, seq_len):
    t = kv.shape[0]
    tm = ROW_TM
    assert seq_len % tm == 0
    tiles_per_seq = seq_len // tm
    return pl.pallas_call(
        _rope_kv_kernel,
        out_shape=jax.ShapeDtypeStruct((t, 2 * D_KV), BF16),
        grid=(t // tm,),
        in_specs=[
            pl.BlockSpec((tm, 2 * D_KV), lambda i: (i, 0)),
            pl.BlockSpec((tm, 2 * HEAD_DIM), lambda i: (i % tiles_per_seq, 0)),
        ],
        out_specs=pl.BlockSpec((tm, 2 * D_KV), lambda i: (i, 0)),
        compiler_params=_params(("arbitrary",)),
        name="rope_kv",
    )(kv, tab_k)


def _lat_attn_kernel(sink_ref, q_ref, kvp_ref, kvc_ref, kvn_ref, ck_ref, cv_ref, zb0_ref, zb1_ref,
                     tabq_ref, swap_ref, bias_ref, o_ref):
    nt = (((1,), (1,)), ((), ()))
    blk = ATTN_BLOCK
    cos_q = tabq_ref[:, 0:HEAD_DIM]
    sin_q = tabq_ref[:, HEAD_DIM:2 * HEAD_DIM]
    for kvh in range(N_KV_HEADS):
        ks = slice(kvh * HEAD_DIM, (kvh + 1) * HEAD_DIM)
        vs = slice(D_KV + kvh * HEAD_DIM, D_KV + (kvh + 1) * HEAD_DIM)
        keys = [jnp.concatenate([kvp_ref[:, ks], kvc_ref[:, ks], kvn_ref[:, ks]], axis=0),
                ck_ref[:, ks].astype(BF16)]
        values = [jnp.concatenate([kvp_ref[:, vs], kvc_ref[:, vs], kvn_ref[:, vs]], axis=0),
                  cv_ref[:, ks].astype(BF16)]
        qx = jnp.concatenate(
            [q_ref[:, (kvh * Q_PER_KV + g) * HEAD_DIM:(kvh * Q_PER_KV + g + 1) * HEAD_DIM]
             for g in range(Q_PER_KV)], axis=0)
        partner = jnp.dot(qx, swap_ref[...], preferred_element_type=F32)
        qf = qx.astype(F32).reshape(Q_PER_KV, blk, HEAD_DIM)
        q4 = (qf * cos_q[None] + partner.reshape(Q_PER_KV, blk, HEAD_DIM) * sin_q[None])
        q4 = q4.reshape(Q_PER_KV * blk, HEAD_DIM).astype(BF16)
        s_loc = lax.dot_general(q4, keys[0], nt, preferred_element_type=F32)
        s_loc = (s_loc.reshape(Q_PER_KV, blk, 3 * blk) + bias_ref[...][None]).reshape(Q_PER_KV * blk, 3 * blk)
        s_ctx = lax.dot_general(q4, keys[1], nt, preferred_element_type=F32)
        sink_rows = jnp.concatenate(
            [jnp.full((blk, 1), sink_ref[kvh * Q_PER_KV + g], F32) for g in range(Q_PER_KV)], axis=0)
        o = _softmax_pv_wide(jnp.concatenate([s_loc, s_ctx], axis=-1), sink_rows,
                             jnp.concatenate(values, axis=0))
        for g in range(Q_PER_KV):
            h = kvh * Q_PER_KV + g
            o_ref[:, h * HEAD_DIM:(h + 1) * HEAD_DIM] = (
                o[g * blk:(g + 1) * blk] * _head_gate((zb0_ref, zb1_ref), h)).astype(BF16)


def _lat_attention(proj, kv, ctx_k, ctx_v, sink, tab_q, tab_k, bias, swap, n_seq, seq_len):
    kv = _rope_kv(kv, tab_k, seq_len)
    t = proj.shape[0]
    blk = ATTN_BLOCK
    assert blk - 1 <= WINDOW <= blk
    nb = seq_len // blk
    lc = ctx_k.shape[1]

    def row(b, i):
        return b * nb + i

    def prev(i):
        return jnp.maximum(i - 1, 0)

    def nxt(i):
        return jnp.minimum(i + 1, nb - 1)

    def bias_variant(b, i):
        return ((i > 0).astype(jnp.int32) * 2 + (i < nb - 1).astype(jnp.int32), 0, 0)

    return pl.pallas_call(
        _lat_attn_kernel,
        out_shape=jax.ShapeDtypeStruct((t, D_ATTN), BF16),
        grid=(n_seq, nb),
        in_specs=[
            pl.BlockSpec(memory_space=pltpu.SMEM),
            pl.BlockSpec((blk, D_ATTN), lambda b, i: (row(b, i), COL_Q // D_ATTN)),
            pl.BlockSpec((blk, 2 * D_KV), lambda b, i: (row(b, prev(i)), 0)),
            pl.BlockSpec((blk, 2 * D_KV), lambda b, i: (row(b, i), 0)),
            pl.BlockSpec((blk, 2 * D_KV), lambda b, i: (row(b, nxt(i)), 0)),
            pl.BlockSpec((None, lc, D_KV), lambda b, i: (b, 0, 0)),
            pl.BlockSpec((None, lc, D_KV), lambda b, i: (b, 0, 0)),
            pl.BlockSpec((blk, D_ATTN // 2), lambda b, i: (row(b, i), COL_ZB // (D_ATTN // 2))),
            pl.BlockSpec((blk, D_ATTN // 2), lambda b, i: (row(b, i), COL_ZB // (D_ATTN // 2) + 1)),
            pl.BlockSpec((blk, 2 * HEAD_DIM), lambda b, i: (i, 0)),
            pl.BlockSpec((HEAD_DIM, HEAD_DIM), lambda b, i: (0, 0)),
            pl.BlockSpec((None, blk, 3 * blk), bias_variant),
        ],
        out_specs=pl.BlockSpec((blk, D_ATTN), lambda b, i: (row(b, i), 0)),
        compiler_params=_params(("arbitrary", "arbitrary")),
        name="lat_attention",
    )(sink, proj, kv, kv, kv, ctx_k, ctx_v, proj, proj, tab_q, swap, bias)


def _rope_tables(seq_len):
    rows = seq_len // GRID_W
    row = np.repeat(np.arange(rows), GRID_W).astype(np.float64)
    col = np.tile(np.arange(GRID_W), rows).astype(np.float64)
    half = HEAD_DIM // 2
    inv = ROPE_BASE ** (-np.arange(0, half, 2, dtype=np.float64) / half)
    ang_r = row[:, None] * inv[None, :]
    ang_c = col[:, None] * inv[None, :]
    cos_t = np.concatenate([np.cos(ang_r)] * 2 + [np.cos(ang_c)] * 2, axis=-1)
    sin_t = np.concatenate([-np.sin(ang_r), np.sin(ang_r), -np.sin(ang_c), np.sin(ang_c)], axis=-1)
    tab_k = np.concatenate([cos_t, sin_t], axis=-1)
    tab_q = tab_k * HEAD_DIM ** -0.5
    return jnp.asarray(tab_q.astype(np.float32)), jnp.asarray(tab_k.astype(np.float32))


def _rope_swap_matrix():
    quarter = HEAD_DIM // 4
    d = np.arange(HEAD_DIM)
    src = np.where(d % (2 * quarter) < quarter, d + quarter, d - quarter)
    return jnp.asarray((np.arange(HEAD_DIM)[:, None] == src[None, :]).astype(BF16))


def _window_bias():
    blk = ATTN_BLOCK
    r = np.arange(blk)[:, None]
    s = np.arange(3 * blk)[None, :]
    band = np.abs(s - blk - r) <= WINDOW
    out = []
    for has_prev in (False, True):
        for has_next in (False, True):
            ok = band & ((s >= blk) | has_prev) & ((s < 2 * blk) | has_next)
            out.append(np.where(ok, 0.0, -np.inf).astype(np.float32))
    return jnp.asarray(np.stack(out, axis=0))


def _sgu_kernel(u_ref, v_ref, z_ref, g_ref, b_ref, ws_ref, bs_ref, o_ref, *, n_chunks):
    for n in range(n_chunks):
        rows = slice(n * SGU_CHUNK, (n + 1) * SGU_CHUNK)
        v = _layernorm_f32(jax.nn.gelu(v_ref[rows, :].astype(F32))) * g_ref[...] + b_ref[...]
        v = v.astype(BF16)
        for g in range(N_SGU_GROUPS):
            cols = slice(g * SGU_GROUP_CH, (g + 1) * SGU_GROUP_CH)
            vm = jnp.dot(ws_ref[g], v[:, cols], preferred_element_type=F32) + bs_ref[:, g:g + 1]
            u = jax.nn.gelu(u_ref[rows, cols].astype(F32))
            z = z_ref[rows, cols].astype(F32)
            o_ref[rows, cols] = (u * vm * jax.nn.silu(z)).astype(BF16)


def _sgu(proj, ln_g, ln_b, w_s_bf16, b_s_t):
    t = proj.shape[0]
    tm = ROW_TM
    return pl.pallas_call(
        functools.partial(_sgu_kernel, n_chunks=tm // SGU_CHUNK),
        out_shape=jax.ShapeDtypeStruct((t, D_SGU), BF16),
        grid=(t // tm,),
        in_specs=[
            pl.BlockSpec((tm, D_SGU), lambda i: (i, COL_U // D_SGU)),
            pl.BlockSpec((tm, D_SGU), lambda i: (i, COL_VS // D_SGU)),
            pl.BlockSpec((tm, D_SGU), lambda i: (i, COL_ZC // D_SGU)),
            pl.BlockSpec((1, D_SGU), lambda i: (0, 0)),
            pl.BlockSpec((1, D_SGU), lambda i: (0, 0)),
            pl.BlockSpec((N_SGU_GROUPS, SGU_CHUNK, SGU_CHUNK), lambda i: (0, 0, 0)),
            pl.BlockSpec((SGU_CHUNK, N_SGU_GROUPS), lambda i: (0, 0)),
        ],
        out_specs=pl.BlockSpec((tm, D_SGU), lambda i: (i, 0)),
        compiler_params=_params(("arbitrary",)),
        name="sgu",
    )(proj, proj, proj, ln_g.reshape(1, D_SGU), ln_b.reshape(1, D_SGU), w_s_bf16, b_s_t)


def _merge_kernel(mg_ref, ya_ref, yb_ref, yc_ref, x_ref, gate_ref, wpa_ref, wpb_ref, wpc_ref, wout_ref,
                  lng_ref, lnb_ref, y_ref, *, alpha):
    d = D_MODEL

    def branch(k, y_r, w_r):
        gate = jax.nn.sigmoid(mg_ref[:, k * d:(k + 1) * d].astype(F32))
        return gate * jnp.dot(y_r[...], w_r[...], preferred_element_type=F32)

    merged = branch(0, ya_ref, wpa_ref) + branch(1, yb_ref, wpb_ref) + branch(2, yc_ref, wpc_ref)
    out = jnp.dot(merged.astype(BF16), wout_ref[...], preferred_element_type=F32)
    z = alpha * x_ref[...] + gate_ref[0] * out
    y_ref[...] = _layernorm_f32(z) * lng_ref[...] + lnb_ref[...]


def _merge(proj, ya, yb, yc, x2, gate, wpa, wpb, wpc, wout, ln_g, ln_b, seq_len, alpha):
    t, d = x2.shape
    tm = MERGE_TM
    if gate.shape[0] == 1:
        mod_map = lambda i: (0, 0, 0)
    else:
        assert seq_len % tm == 0
        tiles_per_seq = seq_len // tm
        mod_map = lambda i: (i // tiles_per_seq, 0, 0)

    def const(shape):
        return pl.BlockSpec(shape, lambda i: (0,) * len(shape), pipeline_mode=pl.Buffered(1))

    return pl.pallas_call(
        functools.partial(_merge_kernel, alpha=alpha),
        out_shape=jax.ShapeDtypeStruct((t, d), F32),
        grid=(t // tm,),
        in_specs=[
            pl.BlockSpec((tm, D_MG), lambda i: (i, 0)),
            pl.BlockSpec((tm, D_SSM), lambda i: (i, 0)),
            pl.BlockSpec((tm, D_ATTN), lambda i: (i, 0)),
            pl.BlockSpec((tm, D_SGU), lambda i: (i, 0)),
            pl.BlockSpec((tm, d), lambda i: (i, 0)),
            pl.BlockSpec((1, 1, d), mod_map),
            const((D_SSM, d)), const((D_ATTN, d)), const((D_SGU, d)), const((d, d)),
            const((1, d)), const((1, d)),
        ],
        out_specs=pl.BlockSpec((tm, d), lambda i: (i, 0)),
        compiler_params=_params(("arbitrary",)),
        name="merge_out",
    )(proj, ya, yb, yc, x2, gate, wpa, wpb, wpc, wout, ln_g.reshape(1, d), ln_b.reshape(1, d))


def _layer(x2, n_seq, seq_len, mod, lp, ctx_k, ctx_v, s0, rope, alpha):
    d = D_MODEL
    nbm = mod.shape[0]
    shift = mod[:, 0:d].reshape(nbm, 1, d)
    scale = mod[:, d:2 * d].reshape(nbm, 1, d)
    gate = mod[:, 2 * d:3 * d].reshape(nbm, 1, d)
    proj, kv, xa = _inproj(x2, scale, shift, lp['w_in'], lp['layer'], seq_len)

    n_seg = seq_len // SSM_SEG
    yssm, fin = _ssm_scan(xa, lp['fold'], lp['w_intra'], lp['w_state'], lp['w_enter'], lp['ssm_small'],
                          lp['layer'], s0, n_seq, n_seg)
    ya = _ssm_post(proj, yssm, lp['d_skip'], lp['w_glu'], lp['b_glu'])

    if ctx_k is None:
        yb = _ctx_attention(proj, kv, lp['sink'], seq_len)
    else:
        yb = _lat_attention(proj, kv, ctx_k, ctx_v, lp['sink'], *rope, n_seq, seq_len)

    yc = _sgu(proj, lp['sgu_g'], lp['sgu_b'], lp['w_s'], lp['b_s_t'])

    y = _merge(proj, ya, yb, yc, x2, gate, lp['w_pa'], lp['w_pb'], lp['w_pc'], lp['w_out'],
               lp['ln_g'], lp['ln_b'], seq_len, alpha)
    return y, kv, fin


def _states_to_lanes(s):
    b = s.shape[0]
    return s.astype(F32).transpose(3, 0, 2, 1, 4).reshape(N_SSM_GROUPS, b, 4 * SSM_STATE)


def _lanes_to_states(f):
    b = f.shape[1]
    return f.reshape(N_SSM_GROUPS, b, 2, 2, SSM_STATE).transpose(1, 3, 2, 0, 4)


def kernel(x_prompt, x_sample, cache_k, cache_v, state_ssm, c, c_ctx, w_ada, b_ada, w_in, ssm_lam_re, ssm_lam_im, ssm_log_step, ssm_b_re, ssm_b_im, ssm_c_re, ssm_c_im, ssm_d, w_glu, b_glu, attn_sink, sgu_ln_g, sgu_ln_b, w_spatial, b_spatial, w_proj_a, w_proj_b, w_proj_c, w_out, ln_g, ln_b):
    depth = w_in.shape[0]
    batch, seq, d = x_prompt.shape
    dec_batch, dec_seq, _ = x_sample.shape
    past_len = cache_k.shape[2]
    alpha = (2 * depth) ** 0.25

    n_cond = 1 + dec_batch
    cond_rows = -(-n_cond // 8) * 8
    cond = jnp.concatenate([c_ctx[None, :], c, jnp.zeros((cond_rows - n_cond, d), F32)], axis=0)
    mod = _ada_mod(cond, w_ada, b_ada)

    w_in_bf16 = w_in.astype(BF16)
    fold = _ssm_fold_matrix()
    tables = _ssm_tables(ssm_lam_re, ssm_lam_im, ssm_log_step, ssm_b_re, ssm_b_im, ssm_c_re, ssm_c_im)
    layers = []
    for l in range(depth):
        w_intra, w_state, w_enter, small = tables
        layers.append({
            'w_in': w_in_bf16, 'layer': l,
            'fold': fold, 'w_intra': w_intra, 'w_state': w_state, 'w_enter': w_enter, 'ssm_small': small,
            'd_skip': ssm_d[l], 'w_glu': w_glu[l].astype(BF16), 'b_glu': b_glu[l],
            'sink': attn_sink[l].astype(F32),
            'sgu_g': sgu_ln_g[l], 'sgu_b': sgu_ln_b[l],
            'w_s': w_spatial[l].astype(BF16), 'b_s_t': b_spatial[l].T.astype(F32),
            'w_pa': w_proj_a[l].astype(BF16), 'w_pb': w_proj_b[l].astype(BF16),
            'w_pc': w_proj_c[l].astype(BF16), 'w_out': w_out[l].astype(BF16),
            'ln_g': ln_g[l], 'ln_b': ln_b[l],
        })

    h = x_prompt.reshape(batch * seq, d)
    zero_state = jnp.zeros((N_SSM_GROUPS, batch, 4 * SSM_STATE), F32)
    ks, vs, ss = [], [], []
    for l in range(depth):
        h, kv, fin = _layer(h, batch, seq, mod[l, 0:1], layers[l], None, None, zero_state, None, alpha)
        ks.append(kv[:, :D_KV].reshape(batch, seq, N_KV_HEADS, HEAD_DIM))
        vs.append(kv[:, D_KV:].reshape(batch, seq, N_KV_HEADS, HEAD_DIM))
        ss.append(_lanes_to_states(fin))
    y_prompt = h.reshape(batch, seq, d)
    new_cache_k = jnp.stack(ks, axis=1)
    new_cache_v = jnp.stack(vs, axis=1)
    new_state_ssm = jnp.stack(ss, axis=1)

    rope = _rope_tables(dec_seq) + (_window_bias(), _rope_swap_matrix())
    z = x_sample.reshape(dec_batch * dec_seq, d)
    for l in range(depth):
        ctx_k = cache_k[:, l].reshape(dec_batch, past_len, D_KV).astype(F32)
        ctx_v = cache_v[:, l].reshape(dec_batch, past_len, D_KV).astype(F32)
        z, _, _ = _layer(z, dec_batch, dec_seq, mod[l, 1:1 + dec_batch], layers[l], ctx_k, ctx_v,
                         _states_to_lanes(state_ssm[:, l]), rope, alpha)
    y_sample = z.reshape(dec_batch, dec_seq, d)
    return (y_prompt, y_sample, new_cache_k, new_cache_v, new_state_ssm)
```

```python
import functools
import math

import jax
import jax.numpy as jnp
import numpy as np
from jax import lax
from jax.experimental import pallas as pl
from jax.experimental.pallas import tpu as pltpu

F32 = jnp.float32
BF16 = jnp.bfloat16

D_MODEL = 2048
GRID_W = 64
D_SSM = 512
SSM_GROUP = 16
N_SSM_GROUPS = D_SSM // SSM_GROUP
SSM_STATE = 64
HEAD_DIM = 128
N_HEADS = 8
N_KV_HEADS = 2
Q_PER_KV = N_HEADS // N_KV_HEADS
D_ATTN = N_HEADS * HEAD_DIM
D_KV = N_KV_HEADS * HEAD_DIM
WINDOW = 128
ATTN_BLOCK = 128
ROPE_BASE = 10000.0
D_SGU = 512
SGU_CHUNK = 128
SGU_GROUP_CH = 128
N_SGU_GROUPS = D_SGU // SGU_GROUP_CH
N_BRANCH = 3
LN_EPS = 1e-5

D_MG = N_BRANCH * D_MODEL
COL_MG = 0
COL_XA = COL_MG + D_MG
COL_ZA = COL_XA + D_SSM
COL_Q = COL_ZA + D_SSM
COL_K = COL_Q + D_ATTN
COL_V = COL_K + D_KV
COL_ZB = COL_V + D_KV
COL_U = COL_ZB + D_ATTN
COL_VS = COL_U + D_SGU
COL_ZC = COL_VS + D_SGU
D_IN = COL_ZC + D_SGU
W_IN_ROTATE = D_IN - D_MG

SSM_T = 16
SSM_NC = 16
SSM_TC = SSM_T * SSM_GROUP
SSM_SEG = SSM_T * SSM_NC
SSM_SMALL_ROWS = 40
SSM_LANE_GROUPS = 128 // SSM_GROUP
SSM_POW_ROWS = 24

VMEM_LIMIT_BYTES = 56 * 1024 * 1024

INPROJ_TM = 1024
INPROJ_TN = 1024
ROW_TM = 512
MERGE_TM = 512


def _params(sem):
    return pltpu.CompilerParams(dimension_semantics=sem, vmem_limit_bytes=VMEM_LIMIT_BYTES)


def _layernorm_f32(x):
    mu = jnp.mean(x, axis=-1, keepdims=True)
    xc = x - mu
    var = jnp.mean(xc * xc, axis=-1, keepdims=True)
    return xc * lax.rsqrt(var + LN_EPS)


def _ada_kernel(cond_ref, w_ref, b_ref, o_ref):
    c = cond_ref[...]
    a = (c * jax.nn.sigmoid(c)).astype(BF16)
    o_ref[...] = jnp.dot(a, w_ref[...].astype(BF16), preferred_element_type=F32) + b_ref[...]


def _ada_mod(cond, w_ada, b_ada):
    depth, d, n = w_ada.shape
    r = cond.shape[0]
    tn = 512
    return pl.pallas_call(
        _ada_kernel,
        out_shape=jax.ShapeDtypeStruct((depth, r, n), F32),
        grid=(depth, n // tn),
        in_specs=[
            pl.BlockSpec((r, d), lambda l, j: (0, 0)),
            pl.BlockSpec((None, d, tn), lambda l, j: (l, 0, j)),
            pl.BlockSpec((None, 1, tn), lambda l, j: (l, 0, j)),
        ],
        out_specs=pl.BlockSpec((None, r, tn), lambda l, j: (l, 0, j)),
        compiler_params=_params(("arbitrary", "arbitrary")),
        name="ada_mod",
    )(cond, w_ada, b_ada.reshape(depth, 1, n))


def _inproj_kernel(x_ref, scale_ref, shift_ref, w_ref, k_in_ref, v_in_ref, proj_ref, k_ref, v_ref, xa_ref, h_scr,
                   *, kv_tile, xa_tile):
    del k_in_ref, v_in_ref
    j = pl.program_id(1)

    @pl.when(j == 0)
    def _():
        h = _layernorm_f32(x_ref[...]) * (1.0 + scale_ref[0]) + shift_ref[0]
        h_scr[...] = h.astype(BF16)

    acc = jnp.dot(h_scr[...], w_ref[...], preferred_element_type=F32)
    proj_ref[...] = acc.astype(BF16)

    @pl.when(j == kv_tile)
    def _():
        k_ref[...] = acc[:, 0:D_KV].reshape(k_ref.shape)
        v_ref[...] = acc[:, D_KV:2 * D_KV].reshape(v_ref.shape)

    @pl.when(j == xa_tile)
    def _():
        xa_ref[...] = acc[:, 0:D_SSM]


def _inproj(x2, scale, shift, w_bf16, layer, n_seq, seq_len, k_arr, v_arr, slot):
    t, d = x2.shape
    n = w_bf16.shape[2]
    tm, tn = INPROJ_TM, INPROJ_TN
    assert t % tm == 0 and n % tn == 0 and COL_K % tn == 0 and COL_XA % tn == 0
    assert 2 * D_KV <= tn and D_SSM <= tn and W_IN_ROTATE % tn == 0
    n_tiles = n // tn
    rot_tiles = W_IN_ROTATE // tn
    if seq_len >= tm:
        assert seq_len % tm == 0
        tiles_per_seq = seq_len // tm
        seq_of = lambda i: i // tiles_per_seq
        kv_spec = pl.BlockSpec((None, None, tm, D_KV), lambda i, j: (i // tiles_per_seq, slot, i % tiles_per_seq, 0))
    else:
        assert tm % seq_len == 0
        seq_of = lambda i: i * (tm // seq_len)
        kv_spec = pl.BlockSpec((tm // seq_len, None, seq_len, D_KV), lambda i, j: (i, slot, 0, 0))
    if scale.shape[0] == 1:
        mod_map = lambda i, j: (0, 0, 0)
    else:
        assert seq_len >= tm
        mod_map = lambda i, j: (seq_of(i), 0, 0)
    any_spec = pl.BlockSpec(memory_space=pl.ANY)
    return pl.pallas_call(
        functools.partial(_inproj_kernel, kv_tile=COL_K // tn, xa_tile=COL_XA // tn),
        out_shape=(jax.ShapeDtypeStruct((t, n), BF16), jax.ShapeDtypeStruct(k_arr.shape, F32),
                   jax.ShapeDtypeStruct(v_arr.shape, F32), jax.ShapeDtypeStruct((t, D_SSM), F32)),
        grid=(t // tm, n // tn),
        in_specs=[
            pl.BlockSpec((tm, d), lambda i, j: (i, 0)),
            pl.BlockSpec((1, 1, d), mod_map),
            pl.BlockSpec((1, 1, d), mod_map),
            pl.BlockSpec((None, d, tn), lambda i, j: (layer, 0, (j + rot_tiles) % n_tiles)),
            any_spec, any_spec,
        ],
        out_specs=(
            pl.BlockSpec((tm, tn), lambda i, j: (i, j)),
            kv_spec, kv_spec,
            pl.BlockSpec((tm, D_SSM), lambda i, j: (i, 0)),
        ),
        input_output_aliases={4: 1, 5: 2},
        scratch_shapes=[pltpu.VMEM((tm, d), BF16)],
        compiler_params=_params(("arbitrary", "arbitrary")),
        name="ln_inproj",
    )(x2, scale, shift, w_bf16, k_arr, v_arr)


def _zoh(lr, li, ls):
    dt = jnp.exp(ls)
    mag = jnp.exp(lr * dt)
    ar = mag * jnp.cos(li * dt)
    ai = mag * jnp.sin(li * dt)
    den = lr * lr + li * li
    nr = ar - 1.0
    return ar, ai, (nr * lr + ai * li) / den, (ai * lr - nr * li) / den


def _cmul(xr, xi, yr, yi):
    return xr * yr - xi * yi, xr * yi + xi * yr


def _cpow(br, bi, expo, n_bits):
    shape = jnp.broadcast_shapes(br.shape, expo.shape)
    pr = jnp.ones(shape, F32)
    pi = jnp.zeros(shape, F32)
    for bit in range(n_bits):
        sel = ((expo >> bit) & 1) == 1
        nr, ni = _cmul(pr, pi, br, bi)
        pr = jnp.where(sel, nr, pr)
        pi = jnp.where(sel, ni, pi)
        if bit + 1 < n_bits:
            br, bi = _cmul(br, bi, br, bi)
    return pr, pi


def _ssm_tables_kernel(laml_ref, bt_ref, ct_ref, ec_ref, ek_ref, wi_ref, ws_ref, we_ref, small_ref):
    p_n, t_n, c_n = SSM_STATE, SSM_T, SSM_GROUP
    p2 = 2 * p_n
    tc = t_n * c_n
    n_bits = 5
    assert max(t_n, SSM_NC) < 2 ** n_bits and SSM_POW_ROWS <= 2 ** n_bits

    def dot(a, b):
        return jnp.dot(a, b, precision=lax.Precision.HIGHEST, preferred_element_type=F32)

    ar, ai, f_re, f_im = _zoh(laml_ref[0:1, :], laml_ref[1:2, :], laml_ref[2:3, :])
    is_fwd = lax.broadcasted_iota(jnp.int32, (1, p2), 1) < p_n
    bbt_re, bbt_im = _cmul(f_re, f_im, bt_ref[:, 0:p2], bt_ref[:, p2:2 * p2])
    krow = lax.broadcasted_iota(jnp.int32, (SSM_POW_ROWS, 1), 0)
    n_lane = ct_ref.shape[1]
    kclamp = jnp.minimum(lax.broadcasted_iota(jnp.int32, (n_lane, 1), 0), t_n)
    pw_re, pw_im = _cpow(ar, ai, kclamp, n_bits)
    for j in range(t_n):
        pm_re = jnp.where(is_fwd, pw_re[t_n - 1 - j:t_n - j], pw_re[j:j + 1])
        pm_im = jnp.where(is_fwd, pw_im[t_n - 1 - j:t_n - j], pw_im[j:j + 1])
        s_re, s_im = _cmul(bbt_re, bbt_im, pm_re, pm_im)
        ws_ref[j * c_n:(j + 1) * c_n, 0:p2] = s_re.astype(BF16)
        ws_ref[j * c_n:(j + 1) * c_n, p2:2 * p2] = s_im.astype(BF16)
    a1r = pw_re[t_n:t_n + 1]
    a1i = pw_im[t_n:t_n + 1]
    q_re, q_im = _cpow(a1r, a1i, jnp.where(is_fwd, krow, jnp.maximum(SSM_NC - 1 - krow, 0)), n_bits)
    as_re, as_im = _cpow(a1r, a1i, jnp.full((1, 1), SSM_NC, jnp.int32), n_bits)
    small_ref[0:SSM_NC, :] = q_re[0:SSM_NC]
    small_ref[SSM_NC:2 * SSM_NC, :] = q_im[0:SSM_NC]
    small_ref[2 * SSM_NC:SSM_SMALL_ROWS, :] = jnp.concatenate(
        [a1r, a1i, as_re, as_im, jnp.zeros((SSM_SMALL_ROWS - 2 * SSM_NC - 4, p2), F32)], axis=0)

    pt_re_all = pw_re.T
    pt_im_all = pw_im.T
    cpt = []
    for d in range(2):
        pt_re = pt_re_all[d * p_n:(d + 1) * p_n, :]
        pt_im = pt_im_all[d * p_n:(d + 1) * p_n, :]
        c_re = dot(ct_ref[...], ec_ref[d])
        c_im = dot(ct_ref[...], ec_ref[2 + d])
        cpt.append(_cmul(c_re, c_im, dot(pt_re, ek_ref[d]), dot(pt_im, ek_ref[d])))
        e_re, e_im = _cmul(c_re, c_im, dot(pt_re, ek_ref[2 + d]), dot(pt_im, ek_ref[2 + d]))
        we_ref[d * p_n:(d + 1) * p_n, :] = e_re.astype(BF16)
        we_ref[p2 + d * p_n:p2 + (d + 1) * p_n, :] = (-e_im).astype(BF16)
    cpt_re = jnp.concatenate([cpt[0][0], cpt[1][0]], axis=0)
    cpt_im = jnp.concatenate([cpt[0][1], cpt[1][1]], axis=0)
    taps = []
    for sel in (is_fwd, jnp.logical_not(is_fwd)):
        taps.append(dot(jnp.where(sel, bbt_re, 0.0), cpt_re) - dot(jnp.where(sel, bbt_im, 0.0), cpt_im))
    lane = lax.broadcasted_iota(jnp.int32, (c_n, tc), 1)
    for j in range(t_n):
        fwd = jnp.where(lane >= j * c_n, pltpu.roll(taps[0], j * c_n, 1), 0.0)
        bwd = jnp.where(lane < (j + 1) * c_n, pltpu.roll(taps[1], (tc - (t_n - 1 - j) * c_n) % tc, 1), 0.0)
        wi_ref[j * c_n:(j + 1) * c_n, :] = (fwd + bwd).astype(BF16)


def _ssm_table_constants():
    t_n, c_n = SSM_T, SSM_GROUP
    rows = np.arange(128)[:, None]
    tok = (np.arange(t_n * c_n) // c_n)[None, :]
    ch = (np.arange(t_n * c_n) % c_n)[None, :]
    ec = np.stack([(rows == q * c_n + ch) for q in range(4)], axis=0)
    ek = np.stack([rows == tok, rows == t_n - 1 - tok, rows == tok + 1, rows == t_n - tok], axis=0)
    return jnp.asarray(ec.astype(np.float32)), jnp.asarray(ek.astype(np.float32))


def _ssm_tables(lam_re, lam_im, log_step, b_re, b_im, c_re, c_im):
    l_n, _, g_n, p_n = lam_re.shape
    c_n, t_n = SSM_GROUP, SSM_T
    tc = t_n * c_n
    p2, p4 = 2 * p_n, 4 * p_n
    ls = jnp.broadcast_to(log_step.astype(F32)[..., None], lam_re.shape)

    def lanes(x):
        return x.astype(F32).transpose(0, 2, 1, 3).reshape(l_n, g_n, 1, p2)

    laml = jnp.concatenate([lanes(lam_re), lanes(lam_im), lanes(ls), jnp.zeros((l_n, g_n, 5, p2), F32)], axis=2)

    def b_lanes(x):
        return x.astype(F32).transpose(0, 2, 4, 1, 3).reshape(l_n, g_n, c_n, p2)

    bt = jnp.concatenate([b_lanes(b_re), b_lanes(b_im)], axis=-1)

    def c_lanes(x):
        return x.astype(F32).transpose(0, 2, 4, 1, 3).reshape(l_n, g_n, p_n, 2 * c_n)

    ct = jnp.concatenate([c_lanes(c_re), c_lanes(c_im), jnp.zeros((l_n, g_n, p_n, 128 - 4 * c_n), F32)], axis=-1)
    ec, ek = _ssm_table_constants()

    def per_group(*dims):
        return pl.BlockSpec((None, None) + dims, lambda l, g: (l, g) + (0,) * len(dims))

    def const(shape):
        return pl.BlockSpec(shape, lambda l, g: (0,) * len(shape))

    return pl.pallas_call(
        _ssm_tables_kernel,
        out_shape=(jax.ShapeDtypeStruct((l_n, g_n, tc, tc), BF16), jax.ShapeDtypeStruct((l_n, g_n, tc, p4), BF16),
                   jax.ShapeDtypeStruct((l_n, g_n, p4, tc), BF16),
                   jax.ShapeDtypeStruct((l_n, g_n, SSM_SMALL_ROWS, p2), F32)),
        grid=(l_n, g_n),
        in_specs=[per_group(8, p2), per_group(c_n, p4), per_group(p_n, 128),
                  const(ec.shape), const(ek.shape)],
        out_specs=(per_group(tc, tc), per_group(tc, p4), per_group(p4, tc), per_group(SSM_SMALL_ROWS, p2)),
        compiler_params=_params(("arbitrary", "arbitrary")),
        name="ssm_tables",
    )(laml, bt, ct, ec, ek)


def _ssm_fold_matrix():
    half_t = SSM_T // 2
    n = half_t * SSM_LANE_GROUPS * SSM_GROUP
    src = np.arange(n)
    j = src // (SSM_LANE_GROUPS * SSM_GROUP)
    g = (src // SSM_GROUP) % SSM_LANE_GROUPS
    c = src % SSM_GROUP
    dst = g * (half_t * SSM_GROUP) + j * SSM_GROUP + c
    return jnp.asarray((dst[:, None] == np.arange(n)[None, :]).astype(BF16))


def _ssm_kernel(xa_ref, fold_ref, wi_ref, ws_ref, we_ref, small_ref, s0_ref, y_ref, fin_ref,
                v_scr, s_scr, yy_scr, fre_scr, fim_scr, efre_scr, efim_scr, ebre_scr, ebim_scr,
                *, n_seq, n_seg):
    p2 = 2 * SSM_STATE
    p4 = 4 * SSM_STATE
    nbp = n_seq * n_seg
    lanes = SSM_LANE_GROUPS * SSM_GROUP
    nt = (((1,), (1,)), ((), ()))

    blocks = []
    for c in range(SSM_NC):
        toks = [xa_ref[pl.ds(c * SSM_T + j, nbp, stride=SSM_SEG), :].astype(BF16) for j in range(SSM_T)]
        blocks.append(jnp.concatenate(toks, axis=1))
    xx = jnp.concatenate(blocks, axis=0)
    hw = fold_ref.shape[0]
    uu = [jnp.dot(xx[:, h * hw:(h + 1) * hw], fold_ref[...], preferred_element_type=F32).astype(BF16)
          for h in range(2)]

    lane = lax.broadcasted_iota(jnp.int32, (1, p2), 1)
    is_fwd = lane < SSM_STATE
    for g in range(SSM_LANE_GROUPS):
        gc = slice(g * SSM_TC, (g + 1) * SSM_TC)
        u = jnp.concatenate([uu[h][:, g * lanes:(g + 1) * lanes] for h in range(2)], axis=1)
        yy_scr[:, gc] = jnp.dot(u, wi_ref[g], preferred_element_type=F32)
        v_scr[:, gc] = jnp.dot(u, ws_ref[g], preferred_element_type=F32)

    for g in range(SSM_LANE_GROUPS):
        o = g * p4
        a1r = small_ref[g, 2 * SSM_NC:2 * SSM_NC + 1, :]
        a1i = small_ref[g, 2 * SSM_NC + 1:2 * SSM_NC + 2, :]
        asr = small_ref[g, 2 * SSM_NC + 2:2 * SSM_NC + 3, :]
        asi = small_ref[g, 2 * SSM_NC + 3:2 * SSM_NC + 4, :]

        st_re = jnp.zeros((nbp, p2), F32)
        st_im = jnp.zeros((nbp, p2), F32)
        for c in range(SSM_NC):
            cb = SSM_NC - 1 - c
            rf = slice(c * nbp, (c + 1) * nbp)
            rb = slice(cb * nbp, (cb + 1) * nbp)
            s_scr[rf, o:o + SSM_STATE] = st_re[:, 0:SSM_STATE]
            s_scr[rb, o + SSM_STATE:o + p2] = st_re[:, SSM_STATE:p2]
            s_scr[rf, o + p2:o + p2 + SSM_STATE] = st_im[:, 0:SSM_STATE]
            s_scr[rb, o + p2 + SSM_STATE:o + p4] = st_im[:, SSM_STATE:p2]
            v_re = jnp.where(is_fwd, v_scr[rf, o:o + p2], v_scr[rb, o:o + p2])
            v_im = jnp.where(is_fwd, v_scr[rf, o + p2:o + p4], v_scr[rb, o + p2:o + p4])
            st_re, st_im = (a1r * st_re - a1i * st_im + v_re, a1r * st_im + a1i * st_re + v_im)

        fre_scr[...] = st_re
        fim_scr[...] = st_im
        s0 = s0_ref[g]
        e_re = s0[:, 0:p2]
        e_im = s0[:, p2:p4]
        for k in range(n_seg):
            kb = n_seg - 1 - k
            rows_f = pl.ds(k, n_seq, stride=n_seg)
            rows_b = pl.ds(kb, n_seq, stride=n_seg)
            efre_scr[rows_f, :] = e_re
            efim_scr[rows_f, :] = e_im
            ebre_scr[rows_b, :] = e_re
            ebim_scr[rows_b, :] = e_im
            f_re = jnp.where(is_fwd, fre_scr[rows_f, :], fre_scr[rows_b, :])
            f_im = jnp.where(is_fwd, fim_scr[rows_f, :], fim_scr[rows_b, :])
            e_re, e_im = (asr * e_re - asi * e_im + f_re, asr * e_im + asi * e_re + f_im)
        fin_ref[g] = jnp.concatenate([e_re, e_im], axis=-1)

        en_re = jnp.where(is_fwd, efre_scr[...], ebre_scr[...])
        en_im = jnp.where(is_fwd, efim_scr[...], ebim_scr[...])
        for c in range(SSM_NC):
            rows = slice(c * nbp, (c + 1) * nbp)
            cr = small_ref[g, c:c + 1, :]
            ci = small_ref[g, SSM_NC + c:SSM_NC + c + 1, :]
            s_scr[rows, o:o + p2] = s_scr[rows, o:o + p2] + (cr * en_re - ci * en_im)
            s_scr[rows, o + p2:o + p4] = s_scr[rows, o + p2:o + p4] + (cr * en_im + ci * en_re)

    for g in range(SSM_LANE_GROUPS):
        gc = slice(g * SSM_TC, (g + 1) * SSM_TC)
        yy_scr[:, gc] += jnp.dot(s_scr[:, g * p4:(g + 1) * p4].astype(BF16), we_ref[g],
                                 preferred_element_type=F32)

    half_t = SSM_T // 2
    for h in range(2):
        yh = jnp.concatenate([yy_scr[:, g * SSM_TC + h * lanes:g * SSM_TC + (h + 1) * lanes]
                              for g in range(SSM_LANE_GROUPS)], axis=1).astype(BF16)
        zz = lax.dot_general(yh, fold_ref[...], nt, preferred_element_type=F32)
        for c in range(SSM_NC):
            for j in range(half_t):
                y_ref[pl.ds(c * SSM_T + h * half_t + j, nbp, stride=SSM_SEG), :] = zz[c * nbp:(c + 1) * nbp,
                                                                                       j * lanes:(j + 1) * lanes]


def _ssm_scan(xa, fold, w_intra, w_state, w_enter, small, layer, s0, n_seq, n_seg):
    t = xa.shape[0]
    g_n = N_SSM_GROUPS
    lg = SSM_LANE_GROUPS
    lanes = lg * SSM_GROUP
    p4 = 4 * SSM_STATE
    nbp = n_seq * n_seg
    r = SSM_NC * nbp
    assert t == r * SSM_T and (SSM_T // 2) * SSM_GROUP == lanes and fold.shape == (lg * lanes, lg * lanes)
    const = lambda shape: pl.BlockSpec(shape, lambda i: (0,) * len(shape), pipeline_mode=pl.Buffered(1))
    return pl.pallas_call(
        functools.partial(_ssm_kernel, n_seq=n_seq, n_seg=n_seg),
        out_shape=(jax.ShapeDtypeStruct((t, D_SSM), F32), jax.ShapeDtypeStruct((g_n, n_seq, p4), F32)),
        grid=(g_n // lg,),
        in_specs=[
            pl.BlockSpec((t, lanes), lambda i: (0, i)),
            const(fold.shape),
            pl.BlockSpec((None, lg, SSM_TC, SSM_TC), lambda i: (layer, i, 0, 0)),
            pl.BlockSpec((None, lg, SSM_TC, p4), lambda i: (layer, i, 0, 0)),
            pl.BlockSpec((None, lg, p4, SSM_TC), lambda i: (layer, i, 0, 0)),
            pl.BlockSpec((None, lg, SSM_SMALL_ROWS, 2 * SSM_STATE), lambda i: (layer, i, 0, 0)),
            pl.BlockSpec((lg, n_seq, p4), lambda i: (i, 0, 0)),
        ],
        out_specs=(
            pl.BlockSpec((t, lanes), lambda i: (0, i)),
            pl.BlockSpec((lg, n_seq, p4), lambda i: (i, 0, 0)),
        ),
        scratch_shapes=[pltpu.VMEM((r, lg * p4), F32), pltpu.VMEM((r, lg * p4), F32),
                        pltpu.VMEM((r, lg * SSM_TC), F32)]
        + [pltpu.VMEM((nbp, 2 * SSM_STATE), F32)] * 6,
        compiler_params=_params(("arbitrary",)),
        name="ssm_scan",
    )(xa, fold, w_intra, w_state, w_enter, small, s0)


def _ssm_post_kernel(xa_ref, za_ref, ys_ref, d_ref, w_ref, b_ref, o_ref):
    y = xa_ref[...].astype(F32) * d_ref[...] + ys_ref[...]
    y = jax.nn.gelu(y)
    gl = jnp.dot(y.astype(BF16), w_ref[...], preferred_element_type=F32) + b_ref[...]
    y = y * jax.nn.sigmoid(gl)
    o_ref[...] = (y * jax.nn.silu(za_ref[...].astype(F32))).astype(BF16)


def _ssm_post(proj, yssm, d_skip, w_glu_bf16, b_glu):
    t = proj.shape[0]
    tm = ROW_TM
    return pl.pallas_call(
        _ssm_post_kernel,
        out_shape=jax.ShapeDtypeStruct((t, D_SSM), BF16),
        grid=(t // tm,),
        in_specs=[
            pl.BlockSpec((tm, D_SSM), lambda i: (i, COL_XA // D_SSM)),
            pl.BlockSpec((tm, D_SSM), lambda i: (i, COL_ZA // D_SSM)),
            pl.BlockSpec((tm, D_SSM), lambda i: (i, 0)),
            pl.BlockSpec((1, D_SSM), lambda i: (0, 0)),
            pl.BlockSpec((D_SSM, D_SSM), lambda i: (0, 0)),
            pl.BlockSpec((1, D_SSM), lambda i: (0, 0)),
        ],
        out_specs=pl.BlockSpec((tm, D_SSM), lambda i: (i, 0)),
        compiler_params=_params(("arbitrary",)),
        name="ssm_post",
    )(proj, proj, yssm, d_skip.reshape(1, D_SSM), w_glu_bf16, b_glu.reshape(1, D_SSM))


def _softmax_pv(scores, sink, values):
    m = sink
    for s in scores:
        m = jnp.maximum(m, jnp.max(s, axis=-1, keepdims=True))
    den = jnp.exp(sink - m)
    o = None
    for s, v in zip(scores, values):
        p = jnp.exp(s - m)
        den = den + jnp.sum(p, axis=-1, keepdims=True)
        pv = jnp.dot(p.astype(BF16), v, preferred_element_type=F32)
        o = pv if o is None else o + pv
    return o / den


def _softmax_pv_wide(s, sink, v_bf16):
    hd = v_bf16.shape[1]
    m = jnp.maximum(jnp.max(s, axis=-1, keepdims=True), sink)
    p = jnp.exp(s - m).astype(BF16)
    v_ones = jnp.concatenate([v_bf16, jnp.ones_like(v_bf16)], axis=1)
    o = jnp.dot(p, v_ones, preferred_element_type=F32)
    den = o[:, hd:] + jnp.exp(sink - m)
    return o[:, :hd] / den


def _head_gate(zb_refs, h):
    per = N_HEADS // 2
    z = zb_refs[h // per][:, (h % per) * HEAD_DIM:(h % per + 1) * HEAD_DIM].astype(F32)
    return jax.nn.silu(z)


def _ctx_attn_kernel(sink_ref, q_ref, k_ref, v_ref, zb0_ref, zb1_ref, o_ref):
    scale = HEAD_DIM ** -0.5
    nt = (((1,), (1,)), ((), ()))
    for kvh in range(N_KV_HEADS):
        k = (k_ref[:, kvh * HEAD_DIM:(kvh + 1) * HEAD_DIM] * scale).astype(BF16)
        v = v_ref[:, kvh * HEAD_DIM:(kvh + 1) * HEAD_DIM].astype(BF16)
        for g in range(Q_PER_KV):
            h = kvh * Q_PER_KV + g
            hs = slice(h * HEAD_DIM, (h + 1) * HEAD_DIM)
            s = lax.dot_general(q_ref[:, hs], k, nt, preferred_element_type=F32)
            o = _softmax_pv([s], sink_ref[h], [v])
            o_ref[:, hs] = (o * _head_gate((zb0_ref, zb1_ref), h)).astype(BF16)


def _ctx_attention(proj, k_arr, v_arr, slot, sink, seq_len):
    t = proj.shape[0]
    kv_spec = pl.BlockSpec((None, None, seq_len, D_KV), lambda b: (b, slot, 0, 0))
    return pl.pallas_call(
        _ctx_attn_kernel,
        out_shape=jax.ShapeDtypeStruct((t, D_ATTN), BF16),
        grid=(t // seq_len,),
        in_specs=[
            pl.BlockSpec(memory_space=pltpu.SMEM),
            pl.BlockSpec((seq_len, D_ATTN), lambda b: (b, COL_Q // D_ATTN)),
            kv_spec, kv_spec,
            pl.BlockSpec((seq_len, D_ATTN // 2), lambda b: (b, COL_ZB // (D_ATTN // 2))),
            pl.BlockSpec((seq_len, D_ATTN // 2), lambda b: (b, COL_ZB // (D_ATTN // 2) + 1)),
        ],
        out_specs=pl.BlockSpec((seq_len, D_ATTN), lambda b: (b, 0)),
        compiler_params=_params(("arbitrary",)),
        name="ctx_attention",
    )(sink, proj, k_arr, v_arr, proj, proj)


def _rope_partner(x):
    lane = lax.broadcasted_iota(jnp.int32, x.shape, 1)
    quarter = HEAD_DIM // 4
    first = (lane % (2 * quarter)) < quarter
    return jnp.where(first, pltpu.roll(x, HEAD_DIM - quarter, 1), pltpu.roll(x, quarter, 1))


def _rope(x, tab_ref):
    return x * tab_ref[:, 0:HEAD_DIM] + _rope_partner(x) * tab_ref[:, HEAD_DIM:2 * HEAD_DIM]


def _rope_kv_kernel(k_ref, v_ref, tab_ref, o_ref):
    for kvh in range(N_KV_HEADS):
        ks = slice(kvh * HEAD_DIM, (kvh + 1) * HEAD_DIM)
        o_ref[:, ks] = _rope(k_ref[:, ks], tab_ref).astype(BF16)
    o_ref[:, D_KV:2 * D_KV] = v_ref[...].astype(BF16)


def _rope_kv(k_arr, v_arr, slot, tab_k):
    n_seq, _, seq_len, _ = k_arr.shape
    tm = ROW_TM
    assert seq_len % tm == 0
    tiles_per_seq = seq_len // tm
    kv_spec = pl.BlockSpec((None, None, tm, D_KV), lambda i: (i // tiles_per_seq, slot, i % tiles_per_seq, 0))
    return pl.pallas_call(
        _rope_kv_kernel,
        out_shape=jax.ShapeDtypeStruct((n_seq * seq_len, 2 * D_KV), BF16),
        grid=(n_seq * tiles_per_seq,),
        in_specs=[
            kv_spec, kv_spec,
            pl.BlockSpec((tm, 2 * HEAD_DIM), lambda i: (i % tiles_per_seq, 0)),
        ],
        out_specs=pl.BlockSpec((tm, 2 * D_KV), lambda i: (i, 0)),
        compiler_params=_params(("arbitrary",)),
        name="rope_kv",
    )(k_arr, v_arr, tab_k)


def _lat_attn_kernel(sink_ref, q_ref, kvp_ref, kvc_ref, kvn_ref, ck_ref, cv_ref, zb0_ref, zb1_ref,
                     tabq_ref, swap_ref, bias_ref, o_ref):
    nt = (((1,), (1,)), ((), ()))
    blk = ATTN_BLOCK
    cos_q = tabq_ref[:, 0:HEAD_DIM]
    sin_q = tabq_ref[:, HEAD_DIM:2 * HEAD_DIM]
    for kvh in range(N_KV_HEADS):
        ks = slice(kvh * HEAD_DIM, (kvh + 1) * HEAD_DIM)
        vs = slice(D_KV + kvh * HEAD_DIM, D_KV + (kvh + 1) * HEAD_DIM)
        keys = [jnp.concatenate([kvp_ref[:, ks], kvc_ref[:, ks], kvn_ref[:, ks]], axis=0),
                ck_ref[:, ks].astype(BF16)]
        values = [jnp.concatenate([kvp_ref[:, vs], kvc_ref[:, vs], kvn_ref[:, vs]], axis=0),
                  cv_ref[:, ks].astype(BF16)]
        qx = jnp.concatenate(
            [q_ref[:, (kvh * Q_PER_KV + g) * HEAD_DIM:(kvh * Q_PER_KV + g + 1) * HEAD_DIM]
             for g in range(Q_PER_KV)], axis=0)
        partner = jnp.dot(qx, swap_ref[...], preferred_element_type=F32)
        qf = qx.astype(F32).reshape(Q_PER_KV, blk, HEAD_DIM)
        q4 = (qf * cos_q[None] + partner.reshape(Q_PER_KV, blk, HEAD_DIM) * sin_q[None])
        q4 = q4.reshape(Q_PER_KV * blk, HEAD_DIM).astype(BF16)
        s_loc = lax.dot_general(q4, keys[0], nt, preferred_element_type=F32)
        s_loc = (s_loc.reshape(Q_PER_KV, blk, 3 * blk) + bias_ref[...][None]).reshape(Q_PER_KV * blk, 3 * blk)
        s_ctx = lax.dot_general(q4, keys[1], nt, preferred_element_type=F32)
        sink_rows = jnp.concatenate(
            [jnp.full((blk, 1), sink_ref[kvh * Q_PER_KV + g], F32) for g in range(Q_PER_KV)], axis=0)
        o = _softmax_pv_wide(jnp.concatenate([s_loc, s_ctx], axis=-1), sink_rows,
                             jnp.concatenate(values, axis=0))
        for g in range(Q_PER_KV):
            h = kvh * Q_PER_KV + g
            o_ref[:, h * HEAD_DIM:(h + 1) * HEAD_DIM] = (
                o[g * blk:(g + 1) * blk] * _head_gate((zb0_ref, zb1_ref), h)).astype(BF16)


def _lat_attention(proj, k_arr, v_arr, slot, ctx_k, ctx_v, sink, tab_q, tab_k, bias, swap, n_seq, seq_len):
    kv = _rope_kv(k_arr, v_arr, slot, tab_k)
    t = proj.shape[0]
    blk = ATTN_BLOCK
    assert blk - 1 <= WINDOW <= blk
    nb = seq_len // blk
    lc = ctx_k.shape[1]

    def row(b, i):
        return b * nb + i

    def prev(i):
        return jnp.maximum(i - 1, 0)

    def nxt(i):
        return jnp.minimum(i + 1, nb - 1)

    def bias_variant(b, i):
        return ((i > 0).astype(jnp.int32) * 2 + (i < nb - 1).astype(jnp.int32), 0, 0)

    return pl.pallas_call(
        _lat_attn_kernel,
        out_shape=jax.ShapeDtypeStruct((t, D_ATTN), BF16),
        grid=(n_seq, nb),
        in_specs=[
            pl.BlockSpec(memory_space=pltpu.SMEM),
            pl.BlockSpec((blk, D_ATTN), lambda b, i: (row(b, i), COL_Q // D_ATTN)),
            pl.BlockSpec((blk, 2 * D_KV), lambda b, i: (row(b, prev(i)), 0)),
            pl.BlockSpec((blk, 2 * D_KV), lambda b, i: (row(b, i), 0)),
            pl.BlockSpec((blk, 2 * D_KV), lambda b, i: (row(b, nxt(i)), 0)),
            pl.BlockSpec((None, lc, D_KV), lambda b, i: (b, 0, 0)),
            pl.BlockSpec((None, lc, D_KV), lambda b, i: (b, 0, 0)),
            pl.BlockSpec((blk, D_ATTN // 2), lambda b, i: (row(b, i), COL_ZB // (D_ATTN // 2))),
            pl.BlockSpec((blk, D_ATTN // 2), lambda b, i: (row(b, i), COL_ZB // (D_ATTN // 2) + 1)),
            pl.BlockSpec((blk, 2 * HEAD_DIM), lambda b, i: (i, 0)),
            pl.BlockSpec((HEAD_DIM, HEAD_DIM), lambda b, i: (0, 0)),
            pl.BlockSpec((None, blk, 3 * blk), bias_variant),
        ],
        out_specs=pl.BlockSpec((blk, D_ATTN), lambda b, i: (row(b, i), 0)),
        compiler_params=_params(("arbitrary", "arbitrary")),
        name="lat_attention",
    )(sink, proj, kv, kv, kv, ctx_k, ctx_v, proj, proj, tab_q, swap, bias)


def _rope_tables(seq_len):
    rows = seq_len // GRID_W
    row = np.repeat(np.arange(rows), GRID_W).astype(np.float64)
    col = np.tile(np.arange(GRID_W), rows).astype(np.float64)
    half = HEAD_DIM // 2
    inv = ROPE_BASE ** (-np.arange(0, half, 2, dtype=np.float64) / half)
    ang_r = row[:, None] * inv[None, :]
    ang_c = col[:, None] * inv[None, :]
    cos_t = np.concatenate([np.cos(ang_r)] * 2 + [np.cos(ang_c)] * 2, axis=-1)
    sin_t = np.concatenate([-np.sin(ang_r), np.sin(ang_r), -np.sin(ang_c), np.sin(ang_c)], axis=-1)
    tab_k = np.concatenate([cos_t, sin_t], axis=-1)
    tab_q = tab_k * HEAD_DIM ** -0.5
    return jnp.asarray(tab_q.astype(np.float32)), jnp.asarray(tab_k.astype(np.float32))


def _rope_swap_matrix():
    quarter = HEAD_DIM // 4
    d = np.arange(HEAD_DIM)
    src = np.where(d % (2 * quarter) < quarter, d + quarter, d - quarter)
    return jnp.asarray((np.arange(HEAD_DIM)[:, None] == src[None, :]).astype(BF16))


def _window_bias():
    blk = ATTN_BLOCK
    r = np.arange(blk)[:, None]
    s = np.arange(3 * blk)[None, :]
    band = np.abs(s - blk - r) <= WINDOW
    out = []
    for has_prev in (False, True):
        for has_next in (False, True):
            ok = band & ((s >= blk) | has_prev) & ((s < 2 * blk) | has_next)
            out.append(np.where(ok, 0.0, -np.inf).astype(np.float32))
    return jnp.asarray(np.stack(out, axis=0))


def _sgu_kernel(u_ref, v_ref, z_ref, g_ref, b_ref, ws_ref, bs_ref, o_ref, *, n_chunks):
    for n in range(n_chunks):
        rows = slice(n * SGU_CHUNK, (n + 1) * SGU_CHUNK)
        v = _layernorm_f32(jax.nn.gelu(v_ref[rows, :].astype(F32))) * g_ref[...] + b_ref[...]
        v = v.astype(BF16)
        for g in range(N_SGU_GROUPS):
            cols = slice(g * SGU_GROUP_CH, (g + 1) * SGU_GROUP_CH)
            vm = jnp.dot(ws_ref[g], v[:, cols], preferred_element_type=F32) + bs_ref[:, g:g + 1]
            u = jax.nn.gelu(u_ref[rows, cols].astype(F32))
            z = z_ref[rows, cols].astype(F32)
            o_ref[rows, cols] = (u * vm * jax.nn.silu(z)).astype(BF16)


def _sgu(proj, ln_g, ln_b, w_s_bf16, b_s_t):
    t = proj.shape[0]
    tm = ROW_TM
    return pl.pallas_call(
        functools.partial(_sgu_kernel, n_chunks=tm // SGU_CHUNK),
        out_shape=jax.ShapeDtypeStruct((t, D_SGU), BF16),
        grid=(t // tm,),
        in_specs=[
            pl.BlockSpec((tm, D_SGU), lambda i: (i, COL_U // D_SGU)),
            pl.BlockSpec((tm, D_SGU), lambda i: (i, COL_VS // D_SGU)),
            pl.BlockSpec((tm, D_SGU), lambda i: (i, COL_ZC // D_SGU)),
            pl.BlockSpec((1, D_SGU), lambda i: (0, 0)),
            pl.BlockSpec((1, D_SGU), lambda i: (0, 0)),
            pl.BlockSpec((N_SGU_GROUPS, SGU_CHUNK, SGU_CHUNK), lambda i: (0, 0, 0)),
            pl.BlockSpec((SGU_CHUNK, N_SGU_GROUPS), lambda i: (0, 0)),
        ],
        out_specs=pl.BlockSpec((tm, D_SGU), lambda i: (i, 0)),
        compiler_params=_params(("arbitrary",)),
        name="sgu",
    )(proj, proj, proj, ln_g.reshape(1, D_SGU), ln_b.reshape(1, D_SGU), w_s_bf16, b_s_t)


def _merge_kernel(mg_ref, ya_ref, yb_ref, yc_ref, x_ref, gate_ref, wpa_ref, wpb_ref, wpc_ref, wout_ref,
                  lng_ref, lnb_ref, y_ref, *, alpha):
    d = D_MODEL

    def branch(k, y_r, w_r):
        gate = jax.nn.sigmoid(mg_ref[:, k * d:(k + 1) * d].astype(F32))
        return gate * jnp.dot(y_r[...], w_r[...], preferred_element_type=F32)

    merged = branch(0, ya_ref, wpa_ref) + branch(1, yb_ref, wpb_ref) + branch(2, yc_ref, wpc_ref)
    out = jnp.dot(merged.astype(BF16), wout_ref[...], preferred_element_type=F32)
    z = alpha * x_ref[...] + gate_ref[0] * out
    y_ref[...] = _layernorm_f32(z) * lng_ref[...] + lnb_ref[...]


def _merge(proj, ya, yb, yc, x2, gate, wpa, wpb, wpc, wout, ln_g, ln_b, seq_len, alpha):
    t, d = x2.shape
    tm = MERGE_TM
    if gate.shape[0] == 1:
        mod_map = lambda i: (0, 0, 0)
    else:
        assert seq_len % tm == 0
        tiles_per_seq = seq_len // tm
        mod_map = lambda i: (i // tiles_per_seq, 0, 0)

    def const(shape):
        return pl.BlockSpec(shape, lambda i: (0,) * len(shape), pipeline_mode=pl.Buffered(1))

    return pl.pallas_call(
        functools.partial(_merge_kernel, alpha=alpha),
        out_shape=jax.ShapeDtypeStruct((t, d), F32),
        grid=(t // tm,),
        in_specs=[
            pl.BlockSpec((tm, D_MG), lambda i: (i, 0)),
            pl.BlockSpec((tm, D_SSM), lambda i: (i, 0)),
            pl.BlockSpec((tm, D_ATTN), lambda i: (i, 0)),
            pl.BlockSpec((tm, D_SGU), lambda i: (i, 0)),
            pl.BlockSpec((tm, d), lambda i: (i, 0)),
            pl.BlockSpec((1, 1, d), mod_map),
            const((D_SSM, d)), const((D_ATTN, d)), const((D_SGU, d)), const((d, d)),
            const((1, d)), const((1, d)),
        ],
        out_specs=pl.BlockSpec((tm, d), lambda i: (i, 0)),
        compiler_params=_params(("arbitrary",)),
        name="merge_out",
    )(proj, ya, yb, yc, x2, gate, wpa, wpb, wpc, wout, ln_g.reshape(1, d), ln_b.reshape(1, d))


def _layer(x2, n_seq, seq_len, mod, lp, k_arr, v_arr, slot, ctx_k, ctx_v, s0, rope, alpha):
    d = D_MODEL
    nbm = mod.shape[0]
    shift = mod[:, 0:d].reshape(nbm, 1, d)
    scale = mod[:, d:2 * d].reshape(nbm, 1, d)
    gate = mod[:, 2 * d:3 * d].reshape(nbm, 1, d)
    proj, k_arr, v_arr, xa = _inproj(x2, scale, shift, lp['w_in'], lp['layer'], n_seq, seq_len, k_arr, v_arr, slot)

    n_seg = seq_len // SSM_SEG
    yssm, fin = _ssm_scan(xa, lp['fold'], lp['w_intra'], lp['w_state'], lp['w_enter'], lp['ssm_small'],
                          lp['layer'], s0, n_seq, n_seg)
    ya = _ssm_post(proj, yssm, lp['d_skip'], lp['w_glu'], lp['b_glu'])

    if ctx_k is None:
        yb = _ctx_attention(proj, k_arr, v_arr, slot, lp['sink'], seq_len)
    else:
        yb = _lat_attention(proj, k_arr, v_arr, slot, ctx_k, ctx_v, lp['sink'], *rope, n_seq, seq_len)

    yc = _sgu(proj, lp['sgu_g'], lp['sgu_b'], lp['w_s'], lp['b_s_t'])

    y = _merge(proj, ya, yb, yc, x2, gate, lp['w_pa'], lp['w_pb'], lp['w_pc'], lp['w_out'],
               lp['ln_g'], lp['ln_b'], seq_len, alpha)
    return y, k_arr, v_arr, fin


def _states_to_lanes(s):
    b = s.shape[0]
    return s.astype(F32).transpose(3, 0, 2, 1, 4).reshape(N_SSM_GROUPS, b, 4 * SSM_STATE)


def _lanes_to_states(f):
    b = f.shape[1]
    return f.reshape(N_SSM_GROUPS, b, 2, 2, SSM_STATE).transpose(1, 3, 2, 0, 4)


def kernel(x_prompt, x_sample, cache_k, cache_v, state_ssm, c, c_ctx, w_ada, b_ada, w_in, ssm_lam_re, ssm_lam_im, ssm_log_step, ssm_b_re, ssm_b_im, ssm_c_re, ssm_c_im, ssm_d, w_glu, b_glu, attn_sink, sgu_ln_g, sgu_ln_b, w_spatial, b_spatial, w_proj_a, w_proj_b, w_proj_c, w_out, ln_g, ln_b):
    depth = w_in.shape[0]
    batch, seq, d = x_prompt.shape
    dec_batch, dec_seq, _ = x_sample.shape
    past_len = cache_k.shape[2]
    alpha = (2 * depth) ** 0.25

    n_cond = 1 + dec_batch
    cond_rows = -(-n_cond // 8) * 8
    cond = jnp.concatenate([c_ctx[None, :], c, jnp.zeros((cond_rows - n_cond, d), F32)], axis=0)
    mod = _ada_mod(cond, w_ada, b_ada)

    w_in_bf16 = w_in.astype(BF16)
    fold = _ssm_fold_matrix()
    tables = _ssm_tables(ssm_lam_re, ssm_lam_im, ssm_log_step, ssm_b_re, ssm_b_im, ssm_c_re, ssm_c_im)
    layers = []
    for l in range(depth):
        w_intra, w_state, w_enter, small = tables
        layers.append({
            'w_in': w_in_bf16, 'layer': l,
            'fold': fold, 'w_intra': w_intra, 'w_state': w_state, 'w_enter': w_enter, 'ssm_small': small,
            'd_skip': ssm_d[l], 'w_glu': w_glu[l].astype(BF16), 'b_glu': b_glu[l],
            'sink': attn_sink[l].astype(F32),
            'sgu_g': sgu_ln_g[l], 'sgu_b': sgu_ln_b[l],
            'w_s': w_spatial[l].astype(BF16), 'b_s_t': b_spatial[l].T.astype(F32),
            'w_pa': w_proj_a[l].astype(BF16), 'w_pb': w_proj_b[l].astype(BF16),
            'w_pc': w_proj_c[l].astype(BF16), 'w_out': w_out[l].astype(BF16),
            'ln_g': ln_g[l], 'ln_b': ln_b[l],
        })

    h = x_prompt.reshape(batch * seq, d)
    zero_state = jnp.zeros((N_SSM_GROUPS, batch, 4 * SSM_STATE), F32)
    k_new = jnp.zeros((batch, depth, seq, D_KV), F32)
    v_new = jnp.zeros((batch, depth, seq, D_KV), F32)
    ss = []
    for l in range(depth):
        h, k_new, v_new, fin = _layer(h, batch, seq, mod[l, 0:1], layers[l], k_new, v_new, l,
                                      None, None, zero_state, None, alpha)
        ss.append(_lanes_to_states(fin))
    y_prompt = h.reshape(batch, seq, d)
    new_cache_k = k_new.reshape(batch, depth, seq, N_KV_HEADS, HEAD_DIM)
    new_cache_v = v_new.reshape(batch, depth, seq, N_KV_HEADS, HEAD_DIM)
    new_state_ssm = jnp.stack(ss, axis=1)

    rope = _rope_tables(dec_seq) + (_window_bias(), _rope_swap_matrix())
    z = x_sample.reshape(dec_batch * dec_seq, d)
    k_lat = jnp.zeros((dec_batch, 1, dec_seq, D_KV), F32)
    v_lat = jnp.zeros((dec_batch, 1, dec_seq, D_KV), F32)
    for l in range(depth):
        ctx_k = cache_k[:, l].reshape(dec_batch, past_len, D_KV).astype(F32)
        ctx_v = cache_v[:, l].reshape(dec_batch, past_len, D_KV).astype(F32)
        z, k_lat, v_lat, _ = _layer(z, dec_batch, dec_seq, mod[l, 1:1 + dec_batch], layers[l], k_lat, v_lat, 0,
                                    ctx_k, ctx_v, _states_to_lanes(state_ssm[:, l]), rope, alpha)
    y_sample = z.reshape(dec_batch, dec_seq, d)
    return (y_prompt, y_sample, new_cache_k, new_cache_v, new_state_ssm)
```

```python
import functools
import math

import jax
import jax.numpy as jnp
import numpy as np
from jax import lax
from jax.experimental import pallas as pl
from jax.experimental.pallas import tpu as pltpu

F32 = jnp.float32
BF16 = jnp.bfloat16

D_MODEL = 2048
GRID_W = 64
D_SSM = 512
SSM_GROUP = 16
N_SSM_GROUPS = D_SSM // SSM_GROUP
SSM_STATE = 64
HEAD_DIM = 128
N_HEADS = 8
N_KV_HEADS = 2
Q_PER_KV = N_HEADS // N_KV_HEADS
D_ATTN = N_HEADS * HEAD_DIM
D_KV = N_KV_HEADS * HEAD_DIM
WINDOW = 128
ATTN_BLOCK = 128
ROPE_BASE = 10000.0
D_SGU = 512
SGU_CHUNK = 128
SGU_GROUP_CH = 128
N_SGU_GROUPS = D_SGU // SGU_GROUP_CH
N_BRANCH = 3
LN_EPS = 1e-5

D_MG = N_BRANCH * D_MODEL
COL_MG = 0
COL_XA = COL_MG + D_MG
COL_ZA = COL_XA + D_SSM
COL_Q = COL_ZA + D_SSM
COL_K = COL_Q + D_ATTN
COL_V = COL_K + D_KV
COL_ZB = COL_V + D_KV
COL_U = COL_ZB + D_ATTN
COL_VS = COL_U + D_SGU
COL_ZC = COL_VS + D_SGU
D_IN = COL_ZC + D_SGU
W_IN_ROTATE = D_IN - D_MG

SSM_T = 16
SSM_NC = 16
SSM_TC = SSM_T * SSM_GROUP
SSM_SEG = SSM_T * SSM_NC
SSM_SMALL_ROWS = 40
SSM_LANE_GROUPS = 128 // SSM_GROUP
SSM_POW_ROWS = 24

VMEM_LIMIT_BYTES = 56 * 1024 * 1024

INPROJ_TM = 1024
INPROJ_TN = 1024
INPROJ_SUB = 256
ROW_TM = 512
MERGE_TM = 512
MERGE_SUB = 256


def _params(sem):
    return pltpu.CompilerParams(dimension_semantics=sem, vmem_limit_bytes=VMEM_LIMIT_BYTES)


def _layernorm_f32(x):
    mu = jnp.mean(x, axis=-1, keepdims=True)
    xc = x - mu
    var = jnp.mean(xc * xc, axis=-1, keepdims=True)
    return xc * lax.rsqrt(var + LN_EPS)


def _ada_kernel(cond_ref, w_ref, b_ref, o_ref):
    c = cond_ref[...]
    a = (c * jax.nn.sigmoid(c)).astype(BF16)
    o_ref[...] = jnp.dot(a, w_ref[...].astype(BF16), preferred_element_type=F32) + b_ref[...]


def _ada_mod(cond, w_ada, b_ada):
    depth, d, n = w_ada.shape
    r = cond.shape[0]
    tn = 512
    return pl.pallas_call(
        _ada_kernel,
        out_shape=jax.ShapeDtypeStruct((depth, r, n), F32),
        grid=(depth, n // tn),
        in_specs=[
            pl.BlockSpec((r, d), lambda l, j: (0, 0)),
            pl.BlockSpec((None, d, tn), lambda l, j: (l, 0, j)),
            pl.BlockSpec((None, 1, tn), lambda l, j: (l, 0, j)),
        ],
        out_specs=pl.BlockSpec((None, r, tn), lambda l, j: (l, 0, j)),
        compiler_params=_params(("arbitrary", "arbitrary")),
        name="ada_mod",
    )(cond, w_ada, b_ada.reshape(depth, 1, n))


def _inproj_kernel(x_ref, scale_ref, shift_ref, w_ref, k_in_ref, v_in_ref, proj_ref, k_ref, v_ref, xa_ref, h_scr,
                   *, kv_tile, xa_tile):
    del k_in_ref, v_in_ref
    j = pl.program_id(1)
    tm = x_ref.shape[0]

    assert kv_tile != 0 and xa_tile != 0

    @pl.when(j == 0)
    def _():
        for r0 in range(0, tm, INPROJ_SUB):
            rows = slice(r0, r0 + INPROJ_SUB)
            h = (_layernorm_f32(x_ref[rows, :]) * (1.0 + scale_ref[0]) + shift_ref[0]).astype(BF16)
            h_scr[rows, :] = h
            proj_ref[rows, :] = jnp.dot(h, w_ref[...], preferred_element_type=F32).astype(BF16)

    @pl.when(j != 0)
    def _():
        acc = jnp.dot(h_scr[...], w_ref[...], preferred_element_type=F32)
        proj_ref[...] = acc.astype(BF16)

        @pl.when(j == kv_tile)
        def _():
            k_ref[...] = acc[:, 0:D_KV].reshape(k_ref.shape)
            v_ref[...] = acc[:, D_KV:2 * D_KV].reshape(v_ref.shape)

        @pl.when(j == xa_tile)
        def _():
            xa_ref[...] = acc[:, 0:D_SSM]


def _inproj(x2, scale, shift, w_bf16, layer, n_seq, seq_len, k_arr, v_arr, slot):
    t, d = x2.shape
    n = w_bf16.shape[2]
    tm, tn = INPROJ_TM, INPROJ_TN
    assert t % tm == 0 and n % tn == 0 and COL_K % tn == 0 and COL_XA % tn == 0
    assert 2 * D_KV <= tn and D_SSM <= tn and W_IN_ROTATE % tn == 0
    n_tiles = n // tn
    rot_tiles = W_IN_ROTATE // tn
    if seq_len >= tm:
        assert seq_len % tm == 0
        tiles_per_seq = seq_len // tm
        seq_of = lambda i: i // tiles_per_seq
        kv_spec = pl.BlockSpec((None, None, tm, D_KV), lambda i, j: (i // tiles_per_seq, slot, i % tiles_per_seq, 0))
    else:
        assert tm % seq_len == 0
        seq_of = lambda i: i * (tm // seq_len)
        kv_spec = pl.BlockSpec((tm // seq_len, None, seq_len, D_KV), lambda i, j: (i, slot, 0, 0))
    if scale.shape[0] == 1:
        mod_map = lambda i, j: (0, 0, 0)
    else:
        assert seq_len >= tm
        mod_map = lambda i, j: (seq_of(i), 0, 0)
    any_spec = pl.BlockSpec(memory_space=pl.ANY)
    return pl.pallas_call(
        functools.partial(_inproj_kernel, kv_tile=COL_K // tn, xa_tile=COL_XA // tn),
        out_shape=(jax.ShapeDtypeStruct((t, n), BF16), jax.ShapeDtypeStruct(k_arr.shape, F32),
                   jax.ShapeDtypeStruct(v_arr.shape, F32), jax.ShapeDtypeStruct((t, D_SSM), F32)),
        grid=(t // tm, n // tn),
        in_specs=[
            pl.BlockSpec((tm, d), lambda i, j: (i, 0)),
            pl.BlockSpec((1, 1, d), mod_map),
            pl.BlockSpec((1, 1, d), mod_map),
            pl.BlockSpec((None, d, tn), lambda i, j: (layer, 0, (j + rot_tiles) % n_tiles)),
            any_spec, any_spec,
        ],
        out_specs=(
            pl.BlockSpec((tm, tn), lambda i, j: (i, j)),
            kv_spec, kv_spec,
            pl.BlockSpec((tm, D_SSM), lambda i, j: (i, 0)),
        ),
        input_output_aliases={4: 1, 5: 2},
        scratch_shapes=[pltpu.VMEM((tm, d), BF16)],
        compiler_params=_params(("arbitrary", "arbitrary")),
        name="ln_inproj",
    )(x2, scale, shift, w_bf16, k_arr, v_arr)


def _zoh(lr, li, ls):
    dt = jnp.exp(ls)
    mag = jnp.exp(lr * dt)
    ar = mag * jnp.cos(li * dt)
    ai = mag * jnp.sin(li * dt)
    den = lr * lr + li * li
    nr = ar - 1.0
    return ar, ai, (nr * lr + ai * li) / den, (ai * lr - nr * li) / den


def _cmul(xr, xi, yr, yi):
    return xr * yr - xi * yi, xr * yi + xi * yr


def _cpow(br, bi, expo, n_bits):
    shape = jnp.broadcast_shapes(br.shape, expo.shape)
    pr = jnp.ones(shape, F32)
    pi = jnp.zeros(shape, F32)
    for bit in range(n_bits):
        sel = ((expo >> bit) & 1) == 1
        nr, ni = _cmul(pr, pi, br, bi)
        pr = jnp.where(sel, nr, pr)
        pi = jnp.where(sel, ni, pi)
        if bit + 1 < n_bits:
            br, bi = _cmul(br, bi, br, bi)
    return pr, pi


def _ssm_tables_kernel(laml_ref, bt_ref, ct_ref, ec_ref, ek_ref, wi_ref, ws_ref, we_ref, small_ref):
    p_n, t_n, c_n = SSM_STATE, SSM_T, SSM_GROUP
    p2 = 2 * p_n
    tc = t_n * c_n
    n_bits = 5
    assert max(t_n, SSM_NC) < 2 ** n_bits and SSM_POW_ROWS <= 2 ** n_bits

    def dot(a, b):
        return jnp.dot(a, b, precision=lax.Precision.HIGHEST, preferred_element_type=F32)

    ar, ai, f_re, f_im = _zoh(laml_ref[0:1, :], laml_ref[1:2, :], laml_ref[2:3, :])
    is_fwd = lax.broadcasted_iota(jnp.int32, (1, p2), 1) < p_n
    bbt_re, bbt_im = _cmul(f_re, f_im, bt_ref[:, 0:p2], bt_ref[:, p2:2 * p2])
    krow = lax.broadcasted_iota(jnp.int32, (SSM_POW_ROWS, 1), 0)
    n_lane = ct_ref.shape[1]
    kclamp = jnp.minimum(lax.broadcasted_iota(jnp.int32, (n_lane, 1), 0), t_n)
    pw_re, pw_im = _cpow(ar, ai, kclamp, n_bits)
    for j in range(t_n):
        pm_re = jnp.where(is_fwd, pw_re[t_n - 1 - j:t_n - j], pw_re[j:j + 1])
        pm_im = jnp.where(is_fwd, pw_im[t_n - 1 - j:t_n - j], pw_im[j:j + 1])
        s_re, s_im = _cmul(bbt_re, bbt_im, pm_re, pm_im)
        ws_ref[j * c_n:(j + 1) * c_n, 0:p2] = s_re.astype(BF16)
        ws_ref[j * c_n:(j + 1) * c_n, p2:2 * p2] = s_im.astype(BF16)
    a1r = pw_re[t_n:t_n + 1]
    a1i = pw_im[t_n:t_n + 1]
    q_re, q_im = _cpow(a1r, a1i, jnp.where(is_fwd, krow, jnp.maximum(SSM_NC - 1 - krow, 0)), n_bits)
    as_re, as_im = _cpow(a1r, a1i, jnp.full((1, 1), SSM_NC, jnp.int32), n_bits)
    small_ref[0:SSM_NC, :] = q_re[0:SSM_NC]
    small_ref[SSM_NC:2 * SSM_NC, :] = q_im[0:SSM_NC]
    small_ref[2 * SSM_NC:SSM_SMALL_ROWS, :] = jnp.concatenate(
        [a1r, a1i, as_re, as_im, jnp.zeros((SSM_SMALL_ROWS - 2 * SSM_NC - 4, p2), F32)], axis=0)

    pt_re_all = pw_re.T
    pt_im_all = pw_im.T
    cpt = []
    for d in range(2):
        pt_re = pt_re_all[d * p_n:(d + 1) * p_n, :]
        pt_im = pt_im_all[d * p_n:(d + 1) * p_n, :]
        c_re = dot(ct_ref[...], ec_ref[d])
        c_im = dot(ct_ref[...], ec_ref[2 + d])
        cpt.append(_cmul(c_re, c_im, dot(pt_re, ek_ref[d]), dot(pt_im, ek_ref[d])))
        e_re, e_im = _cmul(c_re, c_im, dot(pt_re, ek_ref[2 + d]), dot(pt_im, ek_ref[2 + d]))
        we_ref[d * p_n:(d + 1) * p_n, :] = e_re.astype(BF16)
        we_ref[p2 + d * p_n:p2 + (d + 1) * p_n, :] = (-e_im).astype(BF16)
    cpt_re = jnp.concatenate([cpt[0][0], cpt[1][0]], axis=0)
    cpt_im = jnp.concatenate([cpt[0][1], cpt[1][1]], axis=0)
    taps = []
    for sel in (is_fwd, jnp.logical_not(is_fwd)):
        taps.append(dot(jnp.where(sel, bbt_re, 0.0), cpt_re) - dot(jnp.where(sel, bbt_im, 0.0), cpt_im))
    lane = lax.broadcasted_iota(jnp.int32, (c_n, tc), 1)
    for j in range(t_n):
        fwd = jnp.where(lane >= j * c_n, pltpu.roll(taps[0], j * c_n, 1), 0.0)
        bwd = jnp.where(lane < (j + 1) * c_n, pltpu.roll(taps[1], (tc - (t_n - 1 - j) * c_n) % tc, 1), 0.0)
        wi_ref[j * c_n:(j + 1) * c_n, :] = (fwd + bwd).astype(BF16)


def _ssm_table_constants():
    t_n, c_n = SSM_T, SSM_GROUP
    rows = np.arange(128)[:, None]
    tok = (np.arange(t_n * c_n) // c_n)[None, :]
    ch = (np.arange(t_n * c_n) % c_n)[None, :]
    ec = np.stack([(rows == q * c_n + ch) for q in range(4)], axis=0)
    ek = np.stack([rows == tok, rows == t_n - 1 - tok, rows == tok + 1, rows == t_n - tok], axis=0)
    return jnp.asarray(ec.astype(np.float32)), jnp.asarray(ek.astype(np.float32))


def _ssm_tables(lam_re, lam_im, log_step, b_re, b_im, c_re, c_im):
    l_n, _, g_n, p_n = lam_re.shape
    c_n, t_n = SSM_GROUP, SSM_T
    tc = t_n * c_n
    p2, p4 = 2 * p_n, 4 * p_n
    ls = jnp.broadcast_to(log_step.astype(F32)[..., None], lam_re.shape)

    def lanes(x):
        return x.astype(F32).transpose(0, 2, 1, 3).reshape(l_n, g_n, 1, p2)

    laml = jnp.concatenate([lanes(lam_re), lanes(lam_im), lanes(ls), jnp.zeros((l_n, g_n, 5, p2), F32)], axis=2)

    def b_lanes(x):
        return x.astype(F32).transpose(0, 2, 4, 1, 3).reshape(l_n, g_n, c_n, p2)

    bt = jnp.concatenate([b_lanes(b_re), b_lanes(b_im)], axis=-1)

    def c_lanes(x):
        return x.astype(F32).transpose(0, 2, 4, 1, 3).reshape(l_n, g_n, p_n, 2 * c_n)

    ct = jnp.concatenate([c_lanes(c_re), c_lanes(c_im), jnp.zeros((l_n, g_n, p_n, 128 - 4 * c_n), F32)], axis=-1)
    ec, ek = _ssm_table_constants()

    def per_group(*dims):
        return pl.BlockSpec((None, None) + dims, lambda l, g: (l, g) + (0,) * len(dims))

    def const(shape):
        return pl.BlockSpec(shape, lambda l, g: (0,) * len(shape))

    return pl.pallas_call(
        _ssm_tables_kernel,
        out_shape=(jax.ShapeDtypeStruct((l_n, g_n, tc, tc), BF16), jax.ShapeDtypeStruct((l_n, g_n, tc, p4), BF16),
                   jax.ShapeDtypeStruct((l_n, g_n, p4, tc), BF16),
                   jax.ShapeDtypeStruct((l_n, g_n, SSM_SMALL_ROWS, p2), F32)),
        grid=(l_n, g_n),
        in_specs=[per_group(8, p2), per_group(c_n, p4), per_group(p_n, 128),
                  const(ec.shape), const(ek.shape)],
        out_specs=(per_group(tc, tc), per_group(tc, p4), per_group(p4, tc), per_group(SSM_SMALL_ROWS, p2)),
        compiler_params=_params(("arbitrary", "arbitrary")),
        name="ssm_tables",
    )(laml, bt, ct, ec, ek)


def _ssm_fold_matrix():
    half_t = SSM_T // 2
    n = half_t * SSM_LANE_GROUPS * SSM_GROUP
    src = np.arange(n)
    j = src // (SSM_LANE_GROUPS * SSM_GROUP)
    g = (src // SSM_GROUP) % SSM_LANE_GROUPS
    c = src % SSM_GROUP
    dst = g * (half_t * SSM_GROUP) + j * SSM_GROUP + c
    return jnp.asarray((dst[:, None] == np.arange(n)[None, :]).astype(BF16))


def _ssm_kernel(xa_ref, fold_ref, wi_ref, ws_ref, we_ref, small_ref, s0_ref, y_ref, fin_ref,
                v_scr, s_scr, yy_scr, fre_scr, fim_scr, efre_scr, efim_scr, ebre_scr, ebim_scr,
                *, n_seq, n_seg):
    p2 = 2 * SSM_STATE
    p4 = 4 * SSM_STATE
    nbp = n_seq * n_seg
    lanes = SSM_LANE_GROUPS * SSM_GROUP
    nt = (((1,), (1,)), ((), ()))

    blocks = []
    for c in range(SSM_NC):
        toks = [xa_ref[pl.ds(c * SSM_T + j, nbp, stride=SSM_SEG), :].astype(BF16) for j in range(SSM_T)]
        blocks.append(jnp.concatenate(toks, axis=1))
    xx = jnp.concatenate(blocks, axis=0)
    hw = fold_ref.shape[0]
    uu = [jnp.dot(xx[:, h * hw:(h + 1) * hw], fold_ref[...], preferred_element_type=F32).astype(BF16)
          for h in range(2)]

    lane = lax.broadcasted_iota(jnp.int32, (1, p2), 1)
    is_fwd = lane < SSM_STATE
    for g in range(SSM_LANE_GROUPS):
        gc = slice(g * SSM_TC, (g + 1) * SSM_TC)
        u = jnp.concatenate([uu[h][:, g * lanes:(g + 1) * lanes] for h in range(2)], axis=1)
        yy_scr[:, gc] = jnp.dot(u, wi_ref[g], preferred_element_type=F32)
        v_scr[:, gc] = jnp.dot(u, ws_ref[g], preferred_element_type=F32)

    for g in range(SSM_LANE_GROUPS):
        o = g * p4
        a1r = small_ref[g, 2 * SSM_NC:2 * SSM_NC + 1, :]
        a1i = small_ref[g, 2 * SSM_NC + 1:2 * SSM_NC + 2, :]
        asr = small_ref[g, 2 * SSM_NC + 2:2 * SSM_NC + 3, :]
        asi = small_ref[g, 2 * SSM_NC + 3:2 * SSM_NC + 4, :]

        st_re = jnp.zeros((nbp, p2), F32)
        st_im = jnp.zeros((nbp, p2), F32)
        for c in range(SSM_NC):
            cb = SSM_NC - 1 - c
            rf = slice(c * nbp, (c + 1) * nbp)
            rb = slice(cb * nbp, (cb + 1) * nbp)
            s_scr[rf, o:o + SSM_STATE] = st_re[:, 0:SSM_STATE]
            s_scr[rb, o + SSM_STATE:o + p2] = st_re[:, SSM_STATE:p2]
            s_scr[rf, o + p2:o + p2 + SSM_STATE] = st_im[:, 0:SSM_STATE]
            s_scr[rb, o + p2 + SSM_STATE:o + p4] = st_im[:, SSM_STATE:p2]
            v_re = jnp.where(is_fwd, v_scr[rf, o:o + p2], v_scr[rb, o:o + p2])
            v_im = jnp.where(is_fwd, v_scr[rf, o + p2:o + p4], v_scr[rb, o + p2:o + p4])
            st_re, st_im = (a1r * st_re - a1i * st_im + v_re, a1r * st_im + a1i * st_re + v_im)

        fre_scr[...] = st_re
        fim_scr[...] = st_im
        s0 = s0_ref[g]
        e_re = s0[:, 0:p2]
        e_im = s0[:, p2:p4]
        for k in range(n_seg):
            kb = n_seg - 1 - k
            rows_f = pl.ds(k, n_seq, stride=n_seg)
            rows_b = pl.ds(kb, n_seq, stride=n_seg)
            efre_scr[rows_f, :] = e_re
            efim_scr[rows_f, :] = e_im
            ebre_scr[rows_b, :] = e_re
            ebim_scr[rows_b, :] = e_im
            f_re = jnp.where(is_fwd, fre_scr[rows_f, :], fre_scr[rows_b, :])
            f_im = jnp.where(is_fwd, fim_scr[rows_f, :], fim_scr[rows_b, :])
            e_re, e_im = (asr * e_re - asi * e_im + f_re, asr * e_im + asi * e_re + f_im)
        fin_ref[g] = jnp.concatenate([e_re, e_im], axis=-1)

        en_re = jnp.where(is_fwd, efre_scr[...], ebre_scr[...])
        en_im = jnp.where(is_fwd, efim_scr[...], ebim_scr[...])
        for c in range(SSM_NC):
            rows = slice(c * nbp, (c + 1) * nbp)
            cr = small_ref[g, c:c + 1, :]
            ci = small_ref[g, SSM_NC + c:SSM_NC + c + 1, :]
            s_scr[rows, o:o + p2] = s_scr[rows, o:o + p2] + (cr * en_re - ci * en_im)
            s_scr[rows, o + p2:o + p4] = s_scr[rows, o + p2:o + p4] + (cr * en_im + ci * en_re)

    for g in range(SSM_LANE_GROUPS):
        gc = slice(g * SSM_TC, (g + 1) * SSM_TC)
        yy_scr[:, gc] += jnp.dot(s_scr[:, g * p4:(g + 1) * p4].astype(BF16), we_ref[g],
                                 preferred_element_type=F32)

    half_t = SSM_T // 2
    for h in range(2):
        yh = jnp.concatenate([yy_scr[:, g * SSM_TC + h * lanes:g * SSM_TC + (h + 1) * lanes]
                              for g in range(SSM_LANE_GROUPS)], axis=1).astype(BF16)
        zz = lax.dot_general(yh, fold_ref[...], nt, preferred_element_type=F32)
        for c in range(SSM_NC):
            for j in range(half_t):
                y_ref[pl.ds(c * SSM_T + h * half_t + j, nbp, stride=SSM_SEG), :] = zz[c * nbp:(c + 1) * nbp,
                                                                                       j * lanes:(j + 1) * lanes]


def _ssm_scan(xa, fold, w_intra, w_state, w_enter, small, layer, s0, n_seq, n_seg):
    t = xa.shape[0]
    g_n = N_SSM_GROUPS
    lg = SSM_LANE_GROUPS
    lanes = lg * SSM_GROUP
    p4 = 4 * SSM_STATE
    nbp = n_seq * n_seg
    r = SSM_NC * nbp
    assert t == r * SSM_T and (SSM_T // 2) * SSM_GROUP == lanes and fold.shape == (lg * lanes, lg * lanes)
    const = lambda shape: pl.BlockSpec(shape, lambda i: (0,) * len(shape), pipeline_mode=pl.Buffered(1))
    return pl.pallas_call(
        functools.partial(_ssm_kernel, n_seq=n_seq, n_seg=n_seg),
        out_shape=(jax.ShapeDtypeStruct((t, D_SSM), F32), jax.ShapeDtypeStruct((g_n, n_seq, p4), F32)),
        grid=(g_n // lg,),
        in_specs=[
            pl.BlockSpec((t, lanes), lambda i: (0, i)),
            const(fold.shape),
            pl.BlockSpec((None, lg, SSM_TC, SSM_TC), lambda i: (layer, i, 0, 0)),
            pl.BlockSpec((None, lg, SSM_TC, p4), lambda i: (layer, i, 0, 0)),
            pl.BlockSpec((None, lg, p4, SSM_TC), lambda i: (layer, i, 0, 0)),
            pl.BlockSpec((None, lg, SSM_SMALL_ROWS, 2 * SSM_STATE), lambda i: (layer, i, 0, 0)),
            pl.BlockSpec((lg, n_seq, p4), lambda i: (i, 0, 0)),
        ],
        out_specs=(
            pl.BlockSpec((t, lanes), lambda i: (0, i)),
            pl.BlockSpec((lg, n_seq, p4), lambda i: (i, 0, 0)),
        ),
        scratch_shapes=[pltpu.VMEM((r, lg * p4), F32), pltpu.VMEM((r, lg * p4), F32),
                        pltpu.VMEM((r, lg * SSM_TC), F32)]
        + [pltpu.VMEM((nbp, 2 * SSM_STATE), F32)] * 6,
        compiler_params=_params(("arbitrary",)),
        name="ssm_scan",
    )(xa, fold, w_intra, w_state, w_enter, small, s0)


def _ssm_post_kernel(xa_ref, za_ref, ys_ref, d_ref, w_ref, b_ref, o_ref):
    y = xa_ref[...].astype(F32) * d_ref[...] + ys_ref[...]
    y = jax.nn.gelu(y)
    gl = jnp.dot(y.astype(BF16), w_ref[...], preferred_element_type=F32) + b_ref[...]
    y = y * jax.nn.sigmoid(gl)
    o_ref[...] = (y * jax.nn.silu(za_ref[...].astype(F32))).astype(BF16)


def _ssm_post(proj, yssm, d_skip, w_glu_bf16, b_glu):
    t = proj.shape[0]
    tm = ROW_TM
    return pl.pallas_call(
        _ssm_post_kernel,
        out_shape=jax.ShapeDtypeStruct((t, D_SSM), BF16),
        grid=(t // tm,),
        in_specs=[
            pl.BlockSpec((tm, D_SSM), lambda i: (i, COL_XA // D_SSM)),
            pl.BlockSpec((tm, D_SSM), lambda i: (i, COL_ZA // D_SSM)),
            pl.BlockSpec((tm, D_SSM), lambda i: (i, 0)),
            pl.BlockSpec((1, D_SSM), lambda i: (0, 0)),
            pl.BlockSpec((D_SSM, D_SSM), lambda i: (0, 0)),
            pl.BlockSpec((1, D_SSM), lambda i: (0, 0)),
        ],
        out_specs=pl.BlockSpec((tm, D_SSM), lambda i: (i, 0)),
        compiler_params=_params(("arbitrary",)),
        name="ssm_post",
    )(proj, proj, yssm, d_skip.reshape(1, D_SSM), w_glu_bf16, b_glu.reshape(1, D_SSM))


def _softmax_pv(scores, sink, values):
    m = sink
    for s in scores:
        m = jnp.maximum(m, jnp.max(s, axis=-1, keepdims=True))
    den = jnp.exp(sink - m)
    o = None
    for s, v in zip(scores, values):
        p = jnp.exp(s - m)
        den = den + jnp.sum(p, axis=-1, keepdims=True)
        pv = jnp.dot(p.astype(BF16), v, preferred_element_type=F32)
        o = pv if o is None else o + pv
    return o / den


def _softmax_pv_wide(s, sink, v_bf16):
    hd = v_bf16.shape[1]
    m = jnp.maximum(jnp.max(s, axis=-1, keepdims=True), sink)
    p = jnp.exp(s - m).astype(BF16)
    v_ones = jnp.concatenate([v_bf16, jnp.ones_like(v_bf16)], axis=1)
    o = jnp.dot(p, v_ones, preferred_element_type=F32)
    den = o[:, hd:] + jnp.exp(sink - m)
    return o[:, :hd] / den


def _head_gate(zb_refs, h):
    per = N_HEADS // 2
    z = zb_refs[h // per][:, (h % per) * HEAD_DIM:(h % per + 1) * HEAD_DIM].astype(F32)
    return jax.nn.silu(z)


def _ctx_attn_kernel(sink_ref, q_ref, k_ref, v_ref, zb0_ref, zb1_ref, o_ref):
    scale = HEAD_DIM ** -0.5
    nt = (((1,), (1,)), ((), ()))
    for kvh in range(N_KV_HEADS):
        k = (k_ref[:, kvh * HEAD_DIM:(kvh + 1) * HEAD_DIM] * scale).astype(BF16)
        v = v_ref[:, kvh * HEAD_DIM:(kvh + 1) * HEAD_DIM].astype(BF16)
        for g in range(Q_PER_KV):
            h = kvh * Q_PER_KV + g
            hs = slice(h * HEAD_DIM, (h + 1) * HEAD_DIM)
            s = lax.dot_general(q_ref[:, hs], k, nt, preferred_element_type=F32)
            o = _softmax_pv([s], sink_ref[h], [v])
            o_ref[:, hs] = (o * _head_gate((zb0_ref, zb1_ref), h)).astype(BF16)


def _ctx_attention(proj, k_arr, v_arr, slot, sink, seq_len):
    t = proj.shape[0]
    kv_spec = pl.BlockSpec((None, None, seq_len, D_KV), lambda b: (b, slot, 0, 0))
    return pl.pallas_call(
        _ctx_attn_kernel,
        out_shape=jax.ShapeDtypeStruct((t, D_ATTN), BF16),
        grid=(t // seq_len,),
        in_specs=[
            pl.BlockSpec(memory_space=pltpu.SMEM),
            pl.BlockSpec((seq_len, D_ATTN), lambda b: (b, COL_Q // D_ATTN)),
            kv_spec, kv_spec,
            pl.BlockSpec((seq_len, D_ATTN // 2), lambda b: (b, COL_ZB // (D_ATTN // 2))),
            pl.BlockSpec((seq_len, D_ATTN // 2), lambda b: (b, COL_ZB // (D_ATTN // 2) + 1)),
        ],
        out_specs=pl.BlockSpec((seq_len, D_ATTN), lambda b: (b, 0)),
        compiler_params=_params(("arbitrary",)),
        name="ctx_attention",
    )(sink, proj, k_arr, v_arr, proj, proj)


def _rope_partner(x):
    lane = lax.broadcasted_iota(jnp.int32, x.shape, 1)
    quarter = HEAD_DIM // 4
    first = (lane % (2 * quarter)) < quarter
    return jnp.where(first, pltpu.roll(x, HEAD_DIM - quarter, 1), pltpu.roll(x, quarter, 1))


def _rope(x, tab_ref):
    return x * tab_ref[:, 0:HEAD_DIM] + _rope_partner(x) * tab_ref[:, HEAD_DIM:2 * HEAD_DIM]


def _rope_kv_kernel(k_ref, v_ref, tab_ref, o_ref):
    for kvh in range(N_KV_HEADS):
        ks = slice(kvh * HEAD_DIM, (kvh + 1) * HEAD_DIM)
        o_ref[:, ks] = _rope(k_ref[:, ks], tab_ref).astype(BF16)
    o_ref[:, D_KV:2 * D_KV] = v_ref[...].astype(BF16)


def _rope_kv(k_arr, v_arr, slot, tab_k):
    n_seq, _, seq_len, _ = k_arr.shape
    tm = ROW_TM
    assert seq_len % tm == 0
    tiles_per_seq = seq_len // tm
    kv_spec = pl.BlockSpec((None, None, tm, D_KV), lambda i: (i // tiles_per_seq, slot, i % tiles_per_seq, 0))
    return pl.pallas_call(
        _rope_kv_kernel,
        out_shape=jax.ShapeDtypeStruct((n_seq * seq_len, 2 * D_KV), BF16),
        grid=(n_seq * tiles_per_seq,),
        in_specs=[
            kv_spec, kv_spec,
            pl.BlockSpec((tm, 2 * HEAD_DIM), lambda i: (i % tiles_per_seq, 0)),
        ],
        out_specs=pl.BlockSpec((tm, 2 * D_KV), lambda i: (i, 0)),
        compiler_params=_params(("arbitrary",)),
        name="rope_kv",
    )(k_arr, v_arr, tab_k)


def _lat_attn_kernel(sink_ref, q_ref, kvp_ref, kvc_ref, kvn_ref, ck_ref, cv_ref, zb0_ref, zb1_ref,
                     tabq_ref, swap_ref, bias_ref, o_ref):
    nt = (((1,), (1,)), ((), ()))
    blk = ATTN_BLOCK
    cos_q = tabq_ref[:, 0:HEAD_DIM]
    sin_q = tabq_ref[:, HEAD_DIM:2 * HEAD_DIM]
    for kvh in range(N_KV_HEADS):
        ks = slice(kvh * HEAD_DIM, (kvh + 1) * HEAD_DIM)
        vs = slice(D_KV + kvh * HEAD_DIM, D_KV + (kvh + 1) * HEAD_DIM)
        keys = [jnp.concatenate([kvp_ref[:, ks], kvc_ref[:, ks], kvn_ref[:, ks]], axis=0),
                ck_ref[:, ks].astype(BF16)]
        values = [jnp.concatenate([kvp_ref[:, vs], kvc_ref[:, vs], kvn_ref[:, vs]], axis=0),
                  cv_ref[:, ks].astype(BF16)]
        qx = jnp.concatenate(
            [q_ref[:, (kvh * Q_PER_KV + g) * HEAD_DIM:(kvh * Q_PER_KV + g + 1) * HEAD_DIM]
             for g in range(Q_PER_KV)], axis=0)
        partner = jnp.dot(qx, swap_ref[...], preferred_element_type=F32)
        qf = qx.astype(F32).reshape(Q_PER_KV, blk, HEAD_DIM)
        q4 = (qf * cos_q[None] + partner.reshape(Q_PER_KV, blk, HEAD_DIM) * sin_q[None])
        q4 = q4.reshape(Q_PER_KV * blk, HEAD_DIM).astype(BF16)
        s_loc = lax.dot_general(q4, keys[0], nt, preferred_element_type=F32)
        s_loc = (s_loc.reshape(Q_PER_KV, blk, 3 * blk) + bias_ref[...][None]).reshape(Q_PER_KV * blk, 3 * blk)
        s_ctx = lax.dot_general(q4, keys[1], nt, preferred_element_type=F32)
        sink_rows = jnp.concatenate(
            [jnp.full((blk, 1), sink_ref[kvh * Q_PER_KV + g], F32) for g in range(Q_PER_KV)], axis=0)
        o = _softmax_pv_wide(jnp.concatenate([s_loc, s_ctx], axis=-1), sink_rows,
                             jnp.concatenate(values, axis=0))
        for g in range(Q_PER_KV):
            h = kvh * Q_PER_KV + g
            o_ref[:, h * HEAD_DIM:(h + 1) * HEAD_DIM] = (
                o[g * blk:(g + 1) * blk] * _head_gate((zb0_ref, zb1_ref), h)).astype(BF16)


def _lat_attention(proj, k_arr, v_arr, slot, ctx_k, ctx_v, sink, tab_q, tab_k, bias, swap, n_seq, seq_len):
    kv = _rope_kv(k_arr, v_arr, slot, tab_k)
    t = proj.shape[0]
    blk = ATTN_BLOCK
    assert blk - 1 <= WINDOW <= blk
    nb = seq_len // blk
    lc = ctx_k.shape[1]

    def row(b, i):
        return b * nb + i

    def prev(i):
        return jnp.maximum(i - 1, 0)

    def nxt(i):
        return jnp.minimum(i + 1, nb - 1)

    def bias_variant(b, i):
        return ((i > 0).astype(jnp.int32) * 2 + (i < nb - 1).astype(jnp.int32), 0, 0)

    return pl.pallas_call(
        _lat_attn_kernel,
        out_shape=jax.ShapeDtypeStruct((t, D_ATTN), BF16),
        grid=(n_seq, nb),
        in_specs=[
            pl.BlockSpec(memory_space=pltpu.SMEM),
            pl.BlockSpec((blk, D_ATTN), lambda b, i: (row(b, i), COL_Q // D_ATTN)),
            pl.BlockSpec((blk, 2 * D_KV), lambda b, i: (row(b, prev(i)), 0)),
            pl.BlockSpec((blk, 2 * D_KV), lambda b, i: (row(b, i), 0)),
            pl.BlockSpec((blk, 2 * D_KV), lambda b, i: (row(b, nxt(i)), 0)),
            pl.BlockSpec((None, lc, D_KV), lambda b, i: (b, 0, 0)),
            pl.BlockSpec((None, lc, D_KV), lambda b, i: (b, 0, 0)),
            pl.BlockSpec((blk, D_ATTN // 2), lambda b, i: (row(b, i), COL_ZB // (D_ATTN // 2))),
            pl.BlockSpec((blk, D_ATTN // 2), lambda b, i: (row(b, i), COL_ZB // (D_ATTN // 2) + 1)),
            pl.BlockSpec((blk, 2 * HEAD_DIM), lambda b, i: (i, 0)),
            pl.BlockSpec((HEAD_DIM, HEAD_DIM), lambda b, i: (0, 0)),
            pl.BlockSpec((None, blk, 3 * blk), bias_variant),
        ],
        out_specs=pl.BlockSpec((blk, D_ATTN), lambda b, i: (row(b, i), 0)),
        compiler_params=_params(("arbitrary", "arbitrary")),
        name="lat_attention",
    )(sink, proj, kv, kv, kv, ctx_k, ctx_v, proj, proj, tab_q, swap, bias)


def _rope_tables(seq_len):
    rows = seq_len // GRID_W
    row = np.repeat(np.arange(rows), GRID_W).astype(np.float64)
    col = np.tile(np.arange(GRID_W), rows).astype(np.float64)
    half = HEAD_DIM // 2
    inv = ROPE_BASE ** (-np.arange(0, half, 2, dtype=np.float64) / half)
    ang_r = row[:, None] * inv[None, :]
    ang_c = col[:, None] * inv[None, :]
    cos_t = np.concatenate([np.cos(ang_r)] * 2 + [np.cos(ang_c)] * 2, axis=-1)
    sin_t = np.concatenate([-np.sin(ang_r), np.sin(ang_r), -np.sin(ang_c), np.sin(ang_c)], axis=-1)
    tab_k = np.concatenate([cos_t, sin_t], axis=-1)
    tab_q = tab_k * HEAD_DIM ** -0.5
    return jnp.asarray(tab_q.astype(np.float32)), jnp.asarray(tab_k.astype(np.float32))


def _rope_swap_matrix():
    quarter = HEAD_DIM // 4
    d = np.arange(HEAD_DIM)
    src = np.where(d % (2 * quarter) < quarter, d + quarter, d - quarter)
    return jnp.asarray((np.arange(HEAD_DIM)[:, None] == src[None, :]).astype(BF16))


def _window_bias():
    blk = ATTN_BLOCK
    r = np.arange(blk)[:, None]
    s = np.arange(3 * blk)[None, :]
    band = np.abs(s - blk - r) <= WINDOW
    out = []
    for has_prev in (False, True):
        for has_next in (False, True):
            ok = band & ((s >= blk) | has_prev) & ((s < 2 * blk) | has_next)
            out.append(np.where(ok, 0.0, -np.inf).astype(np.float32))
    return jnp.asarray(np.stack(out, axis=0))


def _sgu_kernel(u_ref, v_ref, z_ref, g_ref, b_ref, ws_ref, bs_ref, o_ref, *, n_chunks):
    for n in range(n_chunks):
        rows = slice(n * SGU_CHUNK, (n + 1) * SGU_CHUNK)
        v = _layernorm_f32(jax.nn.gelu(v_ref[rows, :].astype(F32))) * g_ref[...] + b_ref[...]
        v = v.astype(BF16)
        for g in range(N_SGU_GROUPS):
            cols = slice(g * SGU_GROUP_CH, (g + 1) * SGU_GROUP_CH)
            vm = jnp.dot(ws_ref[g], v[:, cols], preferred_element_type=F32) + bs_ref[:, g:g + 1]
            u = jax.nn.gelu(u_ref[rows, cols].astype(F32))
            z = z_ref[rows, cols].astype(F32)
            o_ref[rows, cols] = (u * vm * jax.nn.silu(z)).astype(BF16)


def _sgu(proj, ln_g, ln_b, w_s_bf16, b_s_t):
    t = proj.shape[0]
    tm = ROW_TM
    return pl.pallas_call(
        functools.partial(_sgu_kernel, n_chunks=tm // SGU_CHUNK),
        out_shape=jax.ShapeDtypeStruct((t, D_SGU), BF16),
        grid=(t // tm,),
        in_specs=[
            pl.BlockSpec((tm, D_SGU), lambda i: (i, COL_U // D_SGU)),
            pl.BlockSpec((tm, D_SGU), lambda i: (i, COL_VS // D_SGU)),
            pl.BlockSpec((tm, D_SGU), lambda i: (i, COL_ZC // D_SGU)),
            pl.BlockSpec((1, D_SGU), lambda i: (0, 0)),
            pl.BlockSpec((1, D_SGU), lambda i: (0, 0)),
            pl.BlockSpec((N_SGU_GROUPS, SGU_CHUNK, SGU_CHUNK), lambda i: (0, 0, 0)),
            pl.BlockSpec((SGU_CHUNK, N_SGU_GROUPS), lambda i: (0, 0)),
        ],
        out_specs=pl.BlockSpec((tm, D_SGU), lambda i: (i, 0)),
        compiler_params=_params(("arbitrary",)),
        name="sgu",
    )(proj, proj, proj, ln_g.reshape(1, D_SGU), ln_b.reshape(1, D_SGU), w_s_bf16, b_s_t)


def _merge_kernel(mg_ref, ya_ref, yb_ref, yc_ref, x_ref, gate_ref, wpa_ref, wpb_ref, wpc_ref, wout_ref,
                  lng_ref, lnb_ref, y_ref, *, alpha):
    d = D_MODEL
    tm = y_ref.shape[0]
    for r0 in range(0, tm, MERGE_SUB):
        rows = slice(r0, r0 + MERGE_SUB)

        def branch(k, y_r, w_r):
            gate = jax.nn.sigmoid(mg_ref[rows, k * d:(k + 1) * d].astype(F32))
            return gate * jnp.dot(y_r[rows, :], w_r[...], preferred_element_type=F32)

        merged = branch(0, ya_ref, wpa_ref) + branch(1, yb_ref, wpb_ref) + branch(2, yc_ref, wpc_ref)
        out = jnp.dot(merged.astype(BF16), wout_ref[...], preferred_element_type=F32)
        z = alpha * x_ref[rows, :] + gate_ref[0] * out
        y_ref[rows, :] = _layernorm_f32(z) * lng_ref[...] + lnb_ref[...]


def _merge(proj, ya, yb, yc, x2, gate, wpa, wpb, wpc, wout, ln_g, ln_b, seq_len, alpha):
    t, d = x2.shape
    tm = MERGE_TM
    if gate.shape[0] == 1:
        mod_map = lambda i: (0, 0, 0)
    else:
        assert seq_len % tm == 0
        tiles_per_seq = seq_len // tm
        mod_map = lambda i: (i // tiles_per_seq, 0, 0)

    def const(shape):
        return pl.BlockSpec(shape, lambda i: (0,) * len(shape), pipeline_mode=pl.Buffered(1))

    return pl.pallas_call(
        functools.partial(_merge_kernel, alpha=alpha),
        out_shape=jax.ShapeDtypeStruct((t, d), F32),
        grid=(t // tm,),
        in_specs=[
            pl.BlockSpec((tm, D_MG), lambda i: (i, 0)),
            pl.BlockSpec((tm, D_SSM), lambda i: (i, 0)),
            pl.BlockSpec((tm, D_ATTN), lambda i: (i, 0)),
            pl.BlockSpec((tm, D_SGU), lambda i: (i, 0)),
            pl.BlockSpec((tm, d), lambda i: (i, 0)),
            pl.BlockSpec((1, 1, d), mod_map),
            const((D_SSM, d)), const((D_ATTN, d)), const((D_SGU, d)), const((d, d)),
            const((1, d)), const((1, d)),
        ],
        out_specs=pl.BlockSpec((tm, d), lambda i: (i, 0)),
        compiler_params=_params(("arbitrary",)),
        name="merge_out",
    )(proj, ya, yb, yc, x2, gate, wpa, wpb, wpc, wout, ln_g.reshape(1, d), ln_b.reshape(1, d))


def _layer(x2, n_seq, seq_len, mod, lp, k_arr, v_arr, slot, ctx_k, ctx_v, s0, rope, alpha):
    d = D_MODEL
    nbm = mod.shape[0]
    shift = mod[:, 0:d].reshape(nbm, 1, d)
    scale = mod[:, d:2 * d].reshape(nbm, 1, d)
    gate = mod[:, 2 * d:3 * d].reshape(nbm, 1, d)
    proj, k_arr, v_arr, xa = _inproj(x2, scale, shift, lp['w_in'], lp['layer'], n_seq, seq_len, k_arr, v_arr, slot)

    n_seg = seq_len // SSM_SEG
    yssm, fin = _ssm_scan(xa, lp['fold'], lp['w_intra'], lp['w_state'], lp['w_enter'], lp['ssm_small'],
                          lp['layer'], s0, n_seq, n_seg)
    ya = _ssm_post(proj, yssm, lp['d_skip'], lp['w_glu'], lp['b_glu'])

    if ctx_k is None:
        yb = _ctx_attention(proj, k_arr, v_arr, slot, lp['sink'], seq_len)
    else:
        yb = _lat_attention(proj, k_arr, v_arr, slot, ctx_k, ctx_v, lp['sink'], *rope, n_seq, seq_len)

    yc = _sgu(proj, lp['sgu_g'], lp['sgu_b'], lp['w_s'], lp['b_s_t'])

    y = _merge(proj, ya, yb, yc, x2, gate, lp['w_pa'], lp['w_pb'], lp['w_pc'], lp['w_out'],
               lp['ln_g'], lp['ln_b'], seq_len, alpha)
    return y, k_arr, v_arr, fin


def _states_to_lanes(s):
    b = s.shape[0]
    return s.astype(F32).transpose(3, 0, 2, 1, 4).reshape(N_SSM_GROUPS, b, 4 * SSM_STATE)


def _lanes_to_states(f):
    b = f.shape[1]
    return f.reshape(N_SSM_GROUPS, b, 2, 2, SSM_STATE).transpose(1, 3, 2, 0, 4)


def kernel(x_prompt, x_sample, cache_k, cache_v, state_ssm, c, c_ctx, w_ada, b_ada, w_in, ssm_lam_re, ssm_lam_im, ssm_log_step, ssm_b_re, ssm_b_im, ssm_c_re, ssm_c_im, ssm_d, w_glu, b_glu, attn_sink, sgu_ln_g, sgu_ln_b, w_spatial, b_spatial, w_proj_a, w_proj_b, w_proj_c, w_out, ln_g, ln_b):
    depth = w_in.shape[0]
    batch, seq, d = x_prompt.shape
    dec_batch, dec_seq, _ = x_sample.shape
    past_len = cache_k.shape[2]
    alpha = (2 * depth) ** 0.25

    n_cond = 1 + dec_batch
    cond_rows = -(-n_cond // 8) * 8
    cond = jnp.concatenate([c_ctx[None, :], c, jnp.zeros((cond_rows - n_cond, d), F32)], axis=0)
    mod = _ada_mod(cond, w_ada, b_ada)

    w_in_bf16 = w_in.astype(BF16)
    fold = _ssm_fold_matrix()
    tables = _ssm_tables(ssm_lam_re, ssm_lam_im, ssm_log_step, ssm_b_re, ssm_b_im, ssm_c_re, ssm_c_im)
    layers = []
    for l in range(depth):
        w_intra, w_state, w_enter, small = tables
        layers.append({
            'w_in': w_in_bf16, 'layer': l,
            'fold': fold, 'w_intra': w_intra, 'w_state': w_state, 'w_enter': w_enter, 'ssm_small': small,
            'd_skip': ssm_d[l], 'w_glu': w_glu[l].astype(BF16), 'b_glu': b_glu[l],
            'sink': attn_sink[l].astype(F32),
            'sgu_g': sgu_ln_g[l], 'sgu_b': sgu_ln_b[l],
            'w_s': w_spatial[l].astype(BF16), 'b_s_t': b_spatial[l].T.astype(F32),
            'w_pa': w_proj_a[l].astype(BF16), 'w_pb': w_proj_b[l].astype(BF16),
            'w_pc': w_proj_c[l].astype(BF16), 'w_out': w_out[l].astype(BF16),
            'ln_g': ln_g[l], 'ln_b': ln_b[l],
        })

    h = x_prompt.reshape(batch * seq, d)
    zero_state = jnp.zeros((N_SSM_GROUPS, batch, 4 * SSM_STATE), F32)
    k_new = jnp.zeros((batch, depth, seq, D_KV), F32)
    v_new = jnp.zeros((batch, depth, seq, D_KV), F32)
    ss = []
    for l in range(depth):
        h, k_new, v_new, fin = _layer(h, batch, seq, mod[l, 0:1], layers[l], k_new, v_new, l,
                                      None, None, zero_state, None, alpha)
        ss.append(_lanes_to_states(fin))
    y_prompt = h.reshape(batch, seq, d)
    new_cache_k = k_new.reshape(batch, depth, seq, N_KV_HEADS, HEAD_DIM)
    new_cache_v = v_new.reshape(batch, depth, seq, N_KV_HEADS, HEAD_DIM)
    new_state_ssm = jnp.stack(ss, axis=1)

    rope = _rope_tables(dec_seq) + (_window_bias(), _rope_swap_matrix())
    z = x_sample.reshape(dec_batch * dec_seq, d)
    k_lat = jnp.zeros((dec_batch, 1, dec_seq, D_KV), F32)
    v_lat = jnp.zeros((dec_batch, 1, dec_seq, D_KV), F32)
    for l in range(depth):
        ctx_k = cache_k[:, l].reshape(dec_batch, past_len, D_KV).astype(F32)
        ctx_v = cache_v[:, l].reshape(dec_batch, past_len, D_KV).astype(F32)
        z, k_lat, v_lat, _ = _layer(z, dec_batch, dec_seq, mod[l, 1:1 + dec_batch], layers[l], k_lat, v_lat, 0,
                                    ctx_k, ctx_v, _states_to_lanes(state_ssm[:, l]), rope, alpha)
    y_sample = z.reshape(dec_batch, dec_seq, d)
    return (y_prompt, y_sample, new_cache_k, new_cache_v, new_state_ssm)
```

```python
import functools
import math

import jax
import jax.numpy as jnp
import numpy as np
from jax import lax
from jax.experimental import pallas as pl
from jax.experimental.pallas import tpu as pltpu

F32 = jnp.float32
BF16 = jnp.bfloat16

D_MODEL = 2048
GRID_W = 64
D_SSM = 512
SSM_GROUP = 16
N_SSM_GROUPS = D_SSM // SSM_GROUP
SSM_STATE = 64
HEAD_DIM = 128
N_HEADS = 8
N_KV_HEADS = 2
Q_PER_KV = N_HEADS // N_KV_HEADS
D_ATTN = N_HEADS * HEAD_DIM
D_KV = N_KV_HEADS * HEAD_DIM
WINDOW = 128
ATTN_BLOCK = 128
ROPE_BASE = 10000.0
D_SGU = 512
SGU_CHUNK = 128
SGU_GROUP_CH = 128
N_SGU_GROUPS = D_SGU // SGU_GROUP_CH
N_BRANCH = 3
LN_EPS = 1e-5

D_MG = N_BRANCH * D_MODEL
COL_MG = 0
COL_XA = COL_MG + D_MG
COL_ZA = COL_XA + D_SSM
COL_Q = COL_ZA + D_SSM
COL_K = COL_Q + D_ATTN
COL_V = COL_K + D_KV
COL_ZB = COL_V + D_KV
COL_U = COL_ZB + D_ATTN
COL_VS = COL_U + D_SGU
COL_ZC = COL_VS + D_SGU
D_IN = COL_ZC + D_SGU
W_IN_ROTATE = D_IN - D_MG

SSM_T = 16
SSM_NC = 16
SSM_TC = SSM_T * SSM_GROUP
SSM_SEG = SSM_T * SSM_NC
SSM_SMALL_ROWS = 40
SSM_LANE_GROUPS = 128 // SSM_GROUP
SSM_POW_ROWS = 24

VMEM_LIMIT_BYTES = 56 * 1024 * 1024

INPROJ_TM = 1024
INPROJ_TN = 1024
INPROJ_SUB = 256
ROW_TM = 512
MERGE_TM = 512
MERGE_SUB = 256


def _params(sem):
    return pltpu.CompilerParams(dimension_semantics=sem, vmem_limit_bytes=VMEM_LIMIT_BYTES)


def _layernorm_f32(x):
    mu = jnp.mean(x, axis=-1, keepdims=True)
    xc = x - mu
    var = jnp.mean(xc * xc, axis=-1, keepdims=True)
    return xc * lax.rsqrt(var + LN_EPS)


def _ada_kernel(cond_ref, w_ref, b_ref, o_ref):
    c = cond_ref[...]
    a = (c * jax.nn.sigmoid(c)).astype(BF16)
    o_ref[...] = jnp.dot(a, w_ref[...].astype(BF16), preferred_element_type=F32) + b_ref[...]


def _ada_mod(cond, w_ada, b_ada):
    depth, d, n = w_ada.shape
    r = cond.shape[0]
    tn = 512
    return pl.pallas_call(
        _ada_kernel,
        out_shape=jax.ShapeDtypeStruct((depth, r, n), F32),
        grid=(depth, n // tn),
        in_specs=[
            pl.BlockSpec((r, d), lambda l, j: (0, 0)),
            pl.BlockSpec((None, d, tn), lambda l, j: (l, 0, j)),
            pl.BlockSpec((None, 1, tn), lambda l, j: (l, 0, j)),
        ],
        out_specs=pl.BlockSpec((None, r, tn), lambda l, j: (l, 0, j)),
        compiler_params=_params(("arbitrary", "arbitrary")),
        name="ada_mod",
    )(cond, w_ada, b_ada.reshape(depth, 1, n))


def _inproj_kernel(x_ref, scale_ref, shift_ref, w_ref, k_in_ref, v_in_ref, proj_ref, k_ref, v_ref, xa_ref, h_scr,
                   *, kv_tile, xa_tile):
    del k_in_ref, v_in_ref
    j = pl.program_id(1)
    tm = x_ref.shape[0]

    assert kv_tile != 0 and xa_tile != 0

    @pl.when(j == 0)
    def _():
        for r0 in range(0, tm, INPROJ_SUB):
            rows = slice(r0, r0 + INPROJ_SUB)
            h = (_layernorm_f32(x_ref[rows, :]) * (1.0 + scale_ref[0]) + shift_ref[0]).astype(BF16)
            h_scr[rows, :] = h
            proj_ref[rows, :] = jnp.dot(h, w_ref[...], preferred_element_type=F32).astype(BF16)

    takes_f32 = (j == kv_tile) | (j == xa_tile)

    @pl.when((j != 0) & jnp.logical_not(takes_f32))
    def _():
        for r0 in range(0, tm, INPROJ_SUB):
            rows = slice(r0, r0 + INPROJ_SUB)
            proj_ref[rows, :] = jnp.dot(h_scr[rows, :], w_ref[...], preferred_element_type=F32).astype(BF16)

    @pl.when(takes_f32)
    def _():
        acc = jnp.dot(h_scr[...], w_ref[...], preferred_element_type=F32)
        proj_ref[...] = acc.astype(BF16)

        @pl.when(j == kv_tile)
        def _():
            k_ref[...] = acc[:, 0:D_KV].reshape(k_ref.shape)
            v_ref[...] = acc[:, D_KV:2 * D_KV].reshape(v_ref.shape)

        @pl.when(j == xa_tile)
        def _():
            xa_ref[...] = acc[:, 0:D_SSM]


def _inproj(x2, scale, shift, w_bf16, layer, n_seq, seq_len, k_arr, v_arr, slot):
    t, d = x2.shape
    n = w_bf16.shape[2]
    tm, tn = INPROJ_TM, INPROJ_TN
    assert t % tm == 0 and n % tn == 0 and COL_K % tn == 0 and COL_XA % tn == 0
    assert 2 * D_KV <= tn and D_SSM <= tn and W_IN_ROTATE % tn == 0
    n_tiles = n // tn
    rot_tiles = W_IN_ROTATE // tn
    if seq_len >= tm:
        assert seq_len % tm == 0
        tiles_per_seq = seq_len // tm
        seq_of = lambda i: i // tiles_per_seq
        kv_spec = pl.BlockSpec((None, None, tm, D_KV), lambda i, j: (i // tiles_per_seq, slot, i % tiles_per_seq, 0))
    else:
        assert tm % seq_len == 0
        seq_of = lambda i: i * (tm // seq_len)
        kv_spec = pl.BlockSpec((tm // seq_len, None, seq_len, D_KV), lambda i, j: (i, slot, 0, 0))
    if scale.shape[0] == 1:
        mod_map = lambda i, j: (0, 0, 0)
    else:
        assert seq_len >= tm
        mod_map = lambda i, j: (seq_of(i), 0, 0)
    any_spec = pl.BlockSpec(memory_space=pl.ANY)
    return pl.pallas_call(
        functools.partial(_inproj_kernel, kv_tile=COL_K // tn, xa_tile=COL_XA // tn),
        out_shape=(jax.ShapeDtypeStruct((t, n), BF16), jax.ShapeDtypeStruct(k_arr.shape, F32),
                   jax.ShapeDtypeStruct(v_arr.shape, F32), jax.ShapeDtypeStruct((t, D_SSM), F32)),
        grid=(t // tm, n // tn),
        in_specs=[
            pl.BlockSpec((tm, d), lambda i, j: (i, 0)),
            pl.BlockSpec((1, 1, d), mod_map),
            pl.BlockSpec((1, 1, d), mod_map),
            pl.BlockSpec((None, d, tn), lambda i, j: (layer, 0, (j + rot_tiles) % n_tiles)),
            any_spec, any_spec,
        ],
        out_specs=(
            pl.BlockSpec((tm, tn), lambda i, j: (i, j)),
            kv_spec, kv_spec,
            pl.BlockSpec((tm, D_SSM), lambda i, j: (i, 0)),
        ),
        input_output_aliases={4: 1, 5: 2},
        scratch_shapes=[pltpu.VMEM((tm, d), BF16)],
        compiler_params=_params(("arbitrary", "arbitrary")),
        name="ln_inproj",
    )(x2, scale, shift, w_bf16, k_arr, v_arr)


def _zoh(lr, li, ls):
    dt = jnp.exp(ls)
    mag = jnp.exp(lr * dt)
    ar = mag * jnp.cos(li * dt)
    ai = mag * jnp.sin(li * dt)
    den = lr * lr + li * li
    nr = ar - 1.0
    return ar, ai, (nr * lr + ai * li) / den, (ai * lr - nr * li) / den


def _cmul(xr, xi, yr, yi):
    return xr * yr - xi * yi, xr * yi + xi * yr


def _cpow(br, bi, expo, n_bits):
    shape = jnp.broadcast_shapes(br.shape, expo.shape)
    pr = jnp.ones(shape, F32)
    pi = jnp.zeros(shape, F32)
    for bit in range(n_bits):
        sel = ((expo >> bit) & 1) == 1
        nr, ni = _cmul(pr, pi, br, bi)
        pr = jnp.where(sel, nr, pr)
        pi = jnp.where(sel, ni, pi)
        if bit + 1 < n_bits:
            br, bi = _cmul(br, bi, br, bi)
    return pr, pi


def _ssm_tables_kernel(laml_ref, bt_ref, ct_ref, ec_ref, ek_ref, wi_ref, ws_ref, we_ref, small_ref):
    p_n, t_n, c_n = SSM_STATE, SSM_T, SSM_GROUP
    p2 = 2 * p_n
    tc = t_n * c_n
    n_bits = 5
    assert max(t_n, SSM_NC) < 2 ** n_bits and SSM_POW_ROWS <= 2 ** n_bits

    def dot(a, b):
        return jnp.dot(a, b, precision=lax.Precision.HIGHEST, preferred_element_type=F32)

    ar, ai, f_re, f_im = _zoh(laml_ref[0:1, :], laml_ref[1:2, :], laml_ref[2:3, :])
    is_fwd = lax.broadcasted_iota(jnp.int32, (1, p2), 1) < p_n
    bbt_re, bbt_im = _cmul(f_re, f_im, bt_ref[:, 0:p2], bt_ref[:, p2:2 * p2])
    krow = lax.broadcasted_iota(jnp.int32, (SSM_POW_ROWS, 1), 0)
    n_lane = ct_ref.shape[1]
    kclamp = jnp.minimum(lax.broadcasted_iota(jnp.int32, (n_lane, 1), 0), t_n)
    pw_re, pw_im = _cpow(ar, ai, kclamp, n_bits)
    for j in range(t_n):
        pm_re = jnp.where(is_fwd, pw_re[t_n - 1 - j:t_n - j], pw_re[j:j + 1])
        pm_im = jnp.where(is_fwd, pw_im[t_n - 1 - j:t_n - j], pw_im[j:j + 1])
        s_re, s_im = _cmul(bbt_re, bbt_im, pm_re, pm_im)
        ws_ref[j * c_n:(j + 1) * c_n, 0:p2] = s_re.astype(BF16)
        ws_ref[j * c_n:(j + 1) * c_n, p2:2 * p2] = s_im.astype(BF16)
    a1r = pw_re[t_n:t_n + 1]
    a1i = pw_im[t_n:t_n + 1]
    q_re, q_im = _cpow(a1r, a1i, jnp.where(is_fwd, krow, jnp.maximum(SSM_NC - 1 - krow, 0)), n_bits)
    as_re, as_im = _cpow(a1r, a1i, jnp.full((1, 1), SSM_NC, jnp.int32), n_bits)
    small_ref[0:SSM_NC, :] = q_re[0:SSM_NC]
    small_ref[SSM_NC:2 * SSM_NC, :] = q_im[0:SSM_NC]
    small_ref[2 * SSM_NC:SSM_SMALL_ROWS, :] = jnp.concatenate(
        [a1r, a1i, as_re, as_im, jnp.zeros((SSM_SMALL_ROWS - 2 * SSM_NC - 4, p2), F32)], axis=0)

    pt_re_all = pw_re.T
    pt_im_all = pw_im.T
    cpt = []
    for d in range(2):
        pt_re = pt_re_all[d * p_n:(d + 1) * p_n, :]
        pt_im = pt_im_all[d * p_n:(d + 1) * p_n, :]
        c_re = dot(ct_ref[...], ec_ref[d])
        c_im = dot(ct_ref[...], ec_ref[2 + d])
        cpt.append(_cmul(c_re, c_im, dot(pt_re, ek_ref[d]), dot(pt_im, ek_ref[d])))
        e_re, e_im = _cmul(c_re, c_im, dot(pt_re, ek_ref[2 + d]), dot(pt_im, ek_ref[2 + d]))
        we_ref[d * p_n:(d + 1) * p_n, :] = e_re.astype(BF16)
        we_ref[p2 + d * p_n:p2 + (d + 1) * p_n, :] = (-e_im).astype(BF16)
    cpt_re = jnp.concatenate([cpt[0][0], cpt[1][0]], axis=0)
    cpt_im = jnp.concatenate([cpt[0][1], cpt[1][1]], axis=0)
    taps = []
    for sel in (is_fwd, jnp.logical_not(is_fwd)):
        taps.append(dot(jnp.where(sel, bbt_re, 0.0), cpt_re) - dot(jnp.where(sel, bbt_im, 0.0), cpt_im))
    lane = lax.broadcasted_iota(jnp.int32, (c_n, tc), 1)
    for j in range(t_n):
        fwd = jnp.where(lane >= j * c_n, pltpu.roll(taps[0], j * c_n, 1), 0.0)
        bwd = jnp.where(lane < (j + 1) * c_n, pltpu.roll(taps[1], (tc - (t_n - 1 - j) * c_n) % tc, 1), 0.0)
        wi_ref[j * c_n:(j + 1) * c_n, :] = (fwd + bwd).astype(BF16)


def _ssm_table_constants():
    t_n, c_n = SSM_T, SSM_GROUP
    rows = np.arange(128)[:, None]
    tok = (np.arange(t_n * c_n) // c_n)[None, :]
    ch = (np.arange(t_n * c_n) % c_n)[None, :]
    ec = np.stack([(rows == q * c_n + ch) for q in range(4)], axis=0)
    ek = np.stack([rows == tok, rows == t_n - 1 - tok, rows == tok + 1, rows == t_n - tok], axis=0)
    return jnp.asarray(ec.astype(np.float32)), jnp.asarray(ek.astype(np.float32))


def _ssm_tables(lam_re, lam_im, log_step, b_re, b_im, c_re, c_im):
    l_n, _, g_n, p_n = lam_re.shape
    c_n, t_n = SSM_GROUP, SSM_T
    tc = t_n * c_n
    p2, p4 = 2 * p_n, 4 * p_n
    ls = jnp.broadcast_to(log_step.astype(F32)[..., None], lam_re.shape)

    def lanes(x):
        return x.astype(F32).transpose(0, 2, 1, 3).reshape(l_n, g_n, 1, p2)

    laml = jnp.concatenate([lanes(lam_re), lanes(lam_im), lanes(ls), jnp.zeros((l_n, g_n, 5, p2), F32)], axis=2)

    def b_lanes(x):
        return x.astype(F32).transpose(0, 2, 4, 1, 3).reshape(l_n, g_n, c_n, p2)

    bt = jnp.concatenate([b_lanes(b_re), b_lanes(b_im)], axis=-1)

    def c_lanes(x):
        return x.astype(F32).transpose(0, 2, 4, 1, 3).reshape(l_n, g_n, p_n, 2 * c_n)

    ct = jnp.concatenate([c_lanes(c_re), c_lanes(c_im), jnp.zeros((l_n, g_n, p_n, 128 - 4 * c_n), F32)], axis=-1)
    ec, ek = _ssm_table_constants()

    def per_group(*dims):
        return pl.BlockSpec((None, None) + dims, lambda l, g: (l, g) + (0,) * len(dims))

    def const(shape):
        return pl.BlockSpec(shape, lambda l, g: (0,) * len(shape))

    return pl.pallas_call(
        _ssm_tables_kernel,
        out_shape=(jax.ShapeDtypeStruct((l_n, g_n, tc, tc), BF16), jax.ShapeDtypeStruct((l_n, g_n, tc, p4), BF16),
                   jax.ShapeDtypeStruct((l_n, g_n, p4, tc), BF16),
                   jax.ShapeDtypeStruct((l_n, g_n, SSM_SMALL_ROWS, p2), F32)),
        grid=(l_n, g_n),
        in_specs=[per_group(8, p2), per_group(c_n, p4), per_group(p_n, 128),
                  const(ec.shape), const(ek.shape)],
        out_specs=(per_group(tc, tc), per_group(tc, p4), per_group(p4, tc), per_group(SSM_SMALL_ROWS, p2)),
        compiler_params=_params(("arbitrary", "arbitrary")),
        name="ssm_tables",
    )(laml, bt, ct, ec, ek)


def _ssm_fold_matrix():
    half_t = SSM_T // 2
    n = half_t * SSM_LANE_GROUPS * SSM_GROUP
    src = np.arange(n)
    j = src // (SSM_LANE_GROUPS * SSM_GROUP)
    g = (src // SSM_GROUP) % SSM_LANE_GROUPS
    c = src % SSM_GROUP
    dst = g * (half_t * SSM_GROUP) + j * SSM_GROUP + c
    return jnp.asarray((dst[:, None] == np.arange(n)[None, :]).astype(BF16))


def _ssm_kernel(xa_ref, fold_ref, wi_ref, ws_ref, we_ref, small_ref, s0_ref, y_ref, fin_ref,
                v_scr, s_scr, yy_scr, fre_scr, fim_scr, efre_scr, efim_scr, ebre_scr, ebim_scr,
                *, n_seq, n_seg):
    p2 = 2 * SSM_STATE
    p4 = 4 * SSM_STATE
    nbp = n_seq * n_seg
    lanes = SSM_LANE_GROUPS * SSM_GROUP
    nt = (((1,), (1,)), ((), ()))

    blocks = []
    for c in range(SSM_NC):
        toks = [xa_ref[pl.ds(c * SSM_T + j, nbp, stride=SSM_SEG), :].astype(BF16) for j in range(SSM_T)]
        blocks.append(jnp.concatenate(toks, axis=1))
    xx = jnp.concatenate(blocks, axis=0)
    hw = fold_ref.shape[0]
    uu = [jnp.dot(xx[:, h * hw:(h + 1) * hw], fold_ref[...], preferred_element_type=F32).astype(BF16)
          for h in range(2)]

    lane = lax.broadcasted_iota(jnp.int32, (1, p2), 1)
    is_fwd = lane < SSM_STATE
    for g in range(SSM_LANE_GROUPS):
        gc = slice(g * SSM_TC, (g + 1) * SSM_TC)
        u = jnp.concatenate([uu[h][:, g * lanes:(g + 1) * lanes] for h in range(2)], axis=1)
        yy_scr[:, gc] = jnp.dot(u, wi_ref[g], preferred_element_type=F32)
        v_scr[:, gc] = jnp.dot(u, ws_ref[g], preferred_element_type=F32)

    for g in range(SSM_LANE_GROUPS):
        o = g * p4
        a1r = small_ref[g, 2 * SSM_NC:2 * SSM_NC + 1, :]
        a1i = small_ref[g, 2 * SSM_NC + 1:2 * SSM_NC + 2, :]
        asr = small_ref[g, 2 * SSM_NC + 2:2 * SSM_NC + 3, :]
        asi = small_ref[g, 2 * SSM_NC + 3:2 * SSM_NC + 4, :]

        st_re = jnp.zeros((nbp, p2), F32)
        st_im = jnp.zeros((nbp, p2), F32)
        for c in range(SSM_NC):
            cb = SSM_NC - 1 - c
            rf = slice(c * nbp, (c + 1) * nbp)
            rb = slice(cb * nbp, (cb + 1) * nbp)
            s_scr[rf, o:o + SSM_STATE] = st_re[:, 0:SSM_STATE]
            s_scr[rb, o + SSM_STATE:o + p2] = st_re[:, SSM_STATE:p2]
            s_scr[rf, o + p2:o + p2 + SSM_STATE] = st_im[:, 0:SSM_STATE]
            s_scr[rb, o + p2 + SSM_STATE:o + p4] = st_im[:, SSM_STATE:p2]
            v_re = jnp.where(is_fwd, v_scr[rf, o:o + p2], v_scr[rb, o:o + p2])
            v_im = jnp.where(is_fwd, v_scr[rf, o + p2:o + p4], v_scr[rb, o + p2:o + p4])
            st_re, st_im = (a1r * st_re - a1i * st_im + v_re, a1r * st_im + a1i * st_re + v_im)

        fre_scr[...] = st_re
        fim_scr[...] = st_im
        s0 = s0_ref[g]
        e_re = s0[:, 0:p2]
        e_im = s0[:, p2:p4]
        for k in range(n_seg):
            kb = n_seg - 1 - k
            rows_f = pl.ds(k, n_seq, stride=n_seg)
            rows_b = pl.ds(kb, n_seq, stride=n_seg)
            efre_scr[rows_f, :] = e_re
            efim_scr[rows_f, :] = e_im
            ebre_scr[rows_b, :] = e_re
            ebim_scr[rows_b, :] = e_im
            f_re = jnp.where(is_fwd, fre_scr[rows_f, :], fre_scr[rows_b, :])
            f_im = jnp.where(is_fwd, fim_scr[rows_f, :], fim_scr[rows_b, :])
            e_re, e_im = (asr * e_re - asi * e_im + f_re, asr * e_im + asi * e_re + f_im)
        fin_ref[g] = jnp.concatenate([e_re, e_im], axis=-1)

        en_re = jnp.where(is_fwd, efre_scr[...], ebre_scr[...])
        en_im = jnp.where(is_fwd, efim_scr[...], ebim_scr[...])
        for c in range(SSM_NC):
            rows = slice(c * nbp, (c + 1) * nbp)
            cr = small_ref[g, c:c + 1, :]
            ci = small_ref[g, SSM_NC + c:SSM_NC + c + 1, :]
            s_scr[rows, o:o + p2] = s_scr[rows, o:o + p2] + (cr * en_re - ci * en_im)
            s_scr[rows, o + p2:o + p4] = s_scr[rows, o + p2:o + p4] + (cr * en_im + ci * en_re)

    for g in range(SSM_LANE_GROUPS):
        gc = slice(g * SSM_TC, (g + 1) * SSM_TC)
        yy_scr[:, gc] += jnp.dot(s_scr[:, g * p4:(g + 1) * p4].astype(BF16), we_ref[g],
                                 preferred_element_type=F32)

    half_t = SSM_T // 2
    for h in range(2):
        yh = jnp.concatenate([yy_scr[:, g * SSM_TC + h * lanes:g * SSM_TC + (h + 1) * lanes]
                              for g in range(SSM_LANE_GROUPS)], axis=1).astype(BF16)
        zz = lax.dot_general(yh, fold_ref[...], nt, preferred_element_type=F32)
        for c in range(SSM_NC):
            for j in range(half_t):
                y_ref[pl.ds(c * SSM_T + h * half_t + j, nbp, stride=SSM_SEG), :] = zz[c * nbp:(c + 1) * nbp,
                                                                                       j * lanes:(j + 1) * lanes]


def _ssm_scan(xa, fold, w_intra, w_state, w_enter, small, layer, s0, n_seq, n_seg):
    t = xa.shape[0]
    g_n = N_SSM_GROUPS
    lg = SSM_LANE_GROUPS
    lanes = lg * SSM_GROUP
    p4 = 4 * SSM_STATE
    nbp = n_seq * n_seg
    r = SSM_NC * nbp
    assert t == r * SSM_T and (SSM_T // 2) * SSM_GROUP == lanes and fold.shape == (lg * lanes, lg * lanes)
    const = lambda shape: pl.BlockSpec(shape, lambda i: (0,) * len(shape), pipeline_mode=pl.Buffered(1))
    return pl.pallas_call(
        functools.partial(_ssm_kernel, n_seq=n_seq, n_seg=n_seg),
        out_shape=(jax.ShapeDtypeStruct((t, D_SSM), F32), jax.ShapeDtypeStruct((g_n, n_seq, p4), F32)),
        grid=(g_n // lg,),
        in_specs=[
            pl.BlockSpec((t, lanes), lambda i: (0, i)),
            const(fold.shape),
            pl.BlockSpec((None, lg, SSM_TC, SSM_TC), lambda i: (layer, i, 0, 0)),
            pl.BlockSpec((None, lg, SSM_TC, p4), lambda i: (layer, i, 0, 0)),
            pl.BlockSpec((None, lg, p4, SSM_TC), lambda i: (layer, i, 0, 0)),
            pl.BlockSpec((None, lg, SSM_SMALL_ROWS, 2 * SSM_STATE), lambda i: (layer, i, 0, 0)),
            pl.BlockSpec((lg, n_seq, p4), lambda i: (i, 0, 0)),
        ],
        out_specs=(
            pl.BlockSpec((t, lanes), lambda i: (0, i)),
            pl.BlockSpec((lg, n_seq, p4), lambda i: (i, 0, 0)),
        ),
        scratch_shapes=[pltpu.VMEM((r, lg * p4), F32), pltpu.VMEM((r, lg * p4), F32),
                        pltpu.VMEM((r, lg * SSM_TC), F32)]
        + [pltpu.VMEM((nbp, 2 * SSM_STATE), F32)] * 6,
        compiler_params=_params(("arbitrary",)),
        name="ssm_scan",
    )(xa, fold, w_intra, w_state, w_enter, small, s0)


def _ssm_post_kernel(xa_ref, za_ref, ys_ref, d_ref, w_ref, b_ref, o_ref):
    y = xa_ref[...].astype(F32) * d_ref[...] + ys_ref[...]
    y = jax.nn.gelu(y)
    gl = jnp.dot(y.astype(BF16), w_ref[...], preferred_element_type=F32) + b_ref[...]
    y = y * jax.nn.sigmoid(gl)
    o_ref[...] = (y * jax.nn.silu(za_ref[...].astype(F32))).astype(BF16)


def _ssm_post(proj, yssm, d_skip, w_glu_bf16, b_glu):
    t = proj.shape[0]
    tm = ROW_TM
    return pl.pallas_call(
        _ssm_post_kernel,
        out_shape=jax.ShapeDtypeStruct((t, D_SSM), BF16),
        grid=(t // tm,),
        in_specs=[
            pl.BlockSpec((tm, D_SSM), lambda i: (i, COL_XA // D_SSM)),
            pl.BlockSpec((tm, D_SSM), lambda i: (i, COL_ZA // D_SSM)),
            pl.BlockSpec((tm, D_SSM), lambda i: (i, 0)),
            pl.BlockSpec((1, D_SSM), lambda i: (0, 0)),
            pl.BlockSpec((D_SSM, D_SSM), lambda i: (0, 0)),
            pl.BlockSpec((1, D_SSM), lambda i: (0, 0)),
        ],
        out_specs=pl.BlockSpec((tm, D_SSM), lambda i: (i, 0)),
        compiler_params=_params(("arbitrary",)),
        name="ssm_post",
    )(proj, proj, yssm, d_skip.reshape(1, D_SSM), w_glu_bf16, b_glu.reshape(1, D_SSM))


def _softmax_pv(scores, sink, values):
    m = sink
    for s in scores:
        m = jnp.maximum(m, jnp.max(s, axis=-1, keepdims=True))
    den = jnp.exp(sink - m)
    o = None
    for s, v in zip(scores, values):
        p = jnp.exp(s - m)
        den = den + jnp.sum(p, axis=-1, keepdims=True)
        pv = jnp.dot(p.astype(BF16), v, preferred_element_type=F32)
        o = pv if o is None else o + pv
    return o / den


def _softmax_pv_wide(s, sink, v_bf16):
    hd = v_bf16.shape[1]
    m = jnp.maximum(jnp.max(s, axis=-1, keepdims=True), sink)
    p = jnp.exp(s - m).astype(BF16)
    v_ones = jnp.concatenate([v_bf16, jnp.ones_like(v_bf16)], axis=1)
    o = jnp.dot(p, v_ones, preferred_element_type=F32)
    den = o[:, hd:] + jnp.exp(sink - m)
    return o[:, :hd] / den


def _head_gate(zb_refs, h):
    per = N_HEADS // 2
    z = zb_refs[h // per][:, (h % per) * HEAD_DIM:(h % per + 1) * HEAD_DIM].astype(F32)
    return jax.nn.silu(z)


def _ctx_attn_kernel(sink_ref, q_ref, k_ref, v_ref, zb0_ref, zb1_ref, o_ref):
    scale = HEAD_DIM ** -0.5
    nt = (((1,), (1,)), ((), ()))
    for kvh in range(N_KV_HEADS):
        k = (k_ref[:, kvh * HEAD_DIM:(kvh + 1) * HEAD_DIM] * scale).astype(BF16)
        v = v_ref[:, kvh * HEAD_DIM:(kvh + 1) * HEAD_DIM].astype(BF16)
        for g in range(Q_PER_KV):
            h = kvh * Q_PER_KV + g
            hs = slice(h * HEAD_DIM, (h + 1) * HEAD_DIM)
            s = lax.dot_general(q_ref[:, hs], k, nt, preferred_element_type=F32)
            o = _softmax_pv([s], sink_ref[h], [v])
            o_ref[:, hs] = (o * _head_gate((zb0_ref, zb1_ref), h)).astype(BF16)


def _ctx_attention(proj, k_arr, v_arr, slot, sink, seq_len):
    t = proj.shape[0]
    kv_spec = pl.BlockSpec((None, None, seq_len, D_KV), lambda b: (b, slot, 0, 0))
    return pl.pallas_call(
        _ctx_attn_kernel,
        out_shape=jax.ShapeDtypeStruct((t, D_ATTN), BF16),
        grid=(t // seq_len,),
        in_specs=[
            pl.BlockSpec(memory_space=pltpu.SMEM),
            pl.BlockSpec((seq_len, D_ATTN), lambda b: (b, COL_Q // D_ATTN)),
            kv_spec, kv_spec,
            pl.BlockSpec((seq_len, D_ATTN // 2), lambda b: (b, COL_ZB // (D_ATTN // 2))),
            pl.BlockSpec((seq_len, D_ATTN // 2), lambda b: (b, COL_ZB // (D_ATTN // 2) + 1)),
        ],
        out_specs=pl.BlockSpec((seq_len, D_ATTN), lambda b: (b, 0)),
        compiler_params=_params(("arbitrary",)),
        name="ctx_attention",
    )(sink, proj, k_arr, v_arr, proj, proj)


def _rope_partner(x):
    lane = lax.broadcasted_iota(jnp.int32, x.shape, 1)
    quarter = HEAD_DIM // 4
    first = (lane % (2 * quarter)) < quarter
    return jnp.where(first, pltpu.roll(x, HEAD_DIM - quarter, 1), pltpu.roll(x, quarter, 1))


def _rope(x, tab_ref):
    return x * tab_ref[:, 0:HEAD_DIM] + _rope_partner(x) * tab_ref[:, HEAD_DIM:2 * HEAD_DIM]


def _rope_kv_kernel(k_ref, v_ref, tab_ref, o_ref):
    for kvh in range(N_KV_HEADS):
        ks = slice(kvh * HEAD_DIM, (kvh + 1) * HEAD_DIM)
        o_ref[:, ks] = _rope(k_ref[:, ks], tab_ref).astype(BF16)
    o_ref[:, D_KV:2 * D_KV] = v_ref[...].astype(BF16)


def _rope_kv(k_arr, v_arr, slot, tab_k):
    n_seq, _, seq_len, _ = k_arr.shape
    tm = ROW_TM
    assert seq_len % tm == 0
    tiles_per_seq = seq_len // tm
    kv_spec = pl.BlockSpec((None, None, tm, D_KV), lambda i: (i // tiles_per_seq, slot, i % tiles_per_seq, 0))
    return pl.pallas_call(
        _rope_kv_kernel,
        out_shape=jax.ShapeDtypeStruct((n_seq * seq_len, 2 * D_KV), BF16),
        grid=(n_seq * tiles_per_seq,),
        in_specs=[
            kv_spec, kv_spec,
            pl.BlockSpec((tm, 2 * HEAD_DIM), lambda i: (i % tiles_per_seq, 0)),
        ],
        out_specs=pl.BlockSpec((tm, 2 * D_KV), lambda i: (i, 0)),
        compiler_params=_params(("arbitrary",)),
        name="rope_kv",
    )(k_arr, v_arr, tab_k)


def _lat_attn_kernel(sink_ref, q_ref, kvp_ref, kvc_ref, kvn_ref, ck_ref, cv_ref, zb0_ref, zb1_ref,
                     tabq_ref, swap_ref, bias_ref, o_ref):
    nt = (((1,), (1,)), ((), ()))
    blk = ATTN_BLOCK
    cos_q = tabq_ref[:, 0:HEAD_DIM]
    sin_q = tabq_ref[:, HEAD_DIM:2 * HEAD_DIM]
    for kvh in range(N_KV_HEADS):
        ks = slice(kvh * HEAD_DIM, (kvh + 1) * HEAD_DIM)
        vs = slice(D_KV + kvh * HEAD_DIM, D_KV + (kvh + 1) * HEAD_DIM)
        keys = [jnp.concatenate([kvp_ref[:, ks], kvc_ref[:, ks], kvn_ref[:, ks]], axis=0),
                ck_ref[:, ks].astype(BF16)]
        values = [jnp.concatenate([kvp_ref[:, vs], kvc_ref[:, vs], kvn_ref[:, vs]], axis=0),
                  cv_ref[:, ks].astype(BF16)]
        qx = jnp.concatenate(
            [q_ref[:, (kvh * Q_PER_KV + g) * HEAD_DIM:(kvh * Q_PER_KV + g + 1) * HEAD_DIM]
             for g in range(Q_PER_KV)], axis=0)
        partner = jnp.dot(qx, swap_ref[...], preferred_element_type=F32)
        qf = qx.astype(F32).reshape(Q_PER_KV, blk, HEAD_DIM)
        q4 = (qf * cos_q[None] + partner.reshape(Q_PER_KV, blk, HEAD_DIM) * sin_q[None])
        q4 = q4.reshape(Q_PER_KV * blk, HEAD_DIM).astype(BF16)
        s_loc = lax.dot_general(q4, keys[0], nt, preferred_element_type=F32)
        s_loc = (s_loc.reshape(Q_PER_KV, blk, 3 * blk) + bias_ref[...][None]).reshape(Q_PER_KV * blk, 3 * blk)
        s_ctx = lax.dot_general(q4, keys[1], nt, preferred_element_type=F32)
        sink_rows = jnp.concatenate(
            [jnp.full((blk, 1), sink_ref[kvh * Q_PER_KV + g], F32) for g in range(Q_PER_KV)], axis=0)
        o = _softmax_pv_wide(jnp.concatenate([s_loc, s_ctx], axis=-1), sink_rows,
                             jnp.concatenate(values, axis=0))
        for g in range(Q_PER_KV):
            h = kvh * Q_PER_KV + g
            o_ref[:, h * HEAD_DIM:(h + 1) * HEAD_DIM] = (
                o[g * blk:(g + 1) * blk] * _head_gate((zb0_ref, zb1_ref), h)).astype(BF16)


def _lat_attention(proj, k_arr, v_arr, slot, ctx_k, ctx_v, sink, tab_q, tab_k, bias, swap, n_seq, seq_len):
    kv = _rope_kv(k_arr, v_arr, slot, tab_k)
    t = proj.shape[0]
    blk = ATTN_BLOCK
    assert blk - 1 <= WINDOW <= blk
    nb = seq_len // blk
    lc = ctx_k.shape[1]

    def row(b, i):
        return b * nb + i

    def prev(i):
        return jnp.maximum(i - 1, 0)

    def nxt(i):
        return jnp.minimum(i + 1, nb - 1)

    def bias_variant(b, i):
        return ((i > 0).astype(jnp.int32) * 2 + (i < nb - 1).astype(jnp.int32), 0, 0)

    return pl.pallas_call(
        _lat_attn_kernel,
        out_shape=jax.ShapeDtypeStruct((t, D_ATTN), BF16),
        grid=(n_seq, nb),
        in_specs=[
            pl.BlockSpec(memory_space=pltpu.SMEM),
            pl.BlockSpec((blk, D_ATTN), lambda b, i: (row(b, i), COL_Q // D_ATTN)),
            pl.BlockSpec((blk, 2 * D_KV), lambda b, i: (row(b, prev(i)), 0)),
            pl.BlockSpec((blk, 2 * D_KV), lambda b, i: (row(b, i), 0)),
            pl.BlockSpec((blk, 2 * D_KV), lambda b, i: (row(b, nxt(i)), 0)),
            pl.BlockSpec((None, lc, D_KV), lambda b, i: (b, 0, 0)),
            pl.BlockSpec((None, lc, D_KV), lambda b, i: (b, 0, 0)),
            pl.BlockSpec((blk, D_ATTN // 2), lambda b, i: (row(b, i), COL_ZB // (D_ATTN // 2))),
            pl.BlockSpec((blk, D_ATTN // 2), lambda b, i: (row(b, i), COL_ZB // (D_ATTN // 2) + 1)),
            pl.BlockSpec((blk, 2 * HEAD_DIM), lambda b, i: (i, 0)),
            pl.BlockSpec((HEAD_DIM, HEAD_DIM), lambda b, i: (0, 0)),
            pl.BlockSpec((None, blk, 3 * blk), bias_variant),
        ],
        out_specs=pl.BlockSpec((blk, D_ATTN), lambda b, i: (row(b, i), 0)),
        compiler_params=_params(("arbitrary", "arbitrary")),
        name="lat_attention",
    )(sink, proj, kv, kv, kv, ctx_k, ctx_v, proj, proj, tab_q, swap, bias)


def _rope_tables(seq_len):
    rows = seq_len // GRID_W
    row = np.repeat(np.arange(rows), GRID_W).astype(np.float64)
    col = np.tile(np.arange(GRID_W), rows).astype(np.float64)
    half = HEAD_DIM // 2
    inv = ROPE_BASE ** (-np.arange(0, half, 2, dtype=np.float64) / half)
    ang_r = row[:, None] * inv[None, :]
    ang_c = col[:, None] * inv[None, :]
    cos_t = np.concatenate([np.cos(ang_r)] * 2 + [np.cos(ang_c)] * 2, axis=-1)
    sin_t = np.concatenate([-np.sin(ang_r), np.sin(ang_r), -np.sin(ang_c), np.sin(ang_c)], axis=-1)
    tab_k = np.concatenate([cos_t, sin_t], axis=-1)
    tab_q = tab_k * HEAD_DIM ** -0.5
    return jnp.asarray(tab_q.astype(np.float32)), jnp.asarray(tab_k.astype(np.float32))


def _rope_swap_matrix():
    quarter = HEAD_DIM // 4
    d = np.arange(HEAD_DIM)
    src = np.where(d % (2 * quarter) < quarter, d + quarter, d - quarter)
    return jnp.asarray((np.arange(HEAD_DIM)[:, None] == src[None, :]).astype(BF16))


def _window_bias():
    blk = ATTN_BLOCK
    r = np.arange(blk)[:, None]
    s = np.arange(3 * blk)[None, :]
    band = np.abs(s - blk - r) <= WINDOW
    out = []
    for has_prev in (False, True):
        for has_next in (False, True):
            ok = band & ((s >= blk) | has_prev) & ((s < 2 * blk) | has_next)
            out.append(np.where(ok, 0.0, -np.inf).astype(np.float32))
    return jnp.asarray(np.stack(out, axis=0))


def _sgu_kernel(u_ref, v_ref, z_ref, g_ref, b_ref, ws_ref, bs_ref, o_ref, *, n_chunks):
    for n in range(n_chunks):
        rows = slice(n * SGU_CHUNK, (n + 1) * SGU_CHUNK)
        v = _layernorm_f32(jax.nn.gelu(v_ref[rows, :].astype(F32))) * g_ref[...] + b_ref[...]
        v = v.astype(BF16)
        for g in range(N_SGU_GROUPS):
            cols = slice(g * SGU_GROUP_CH, (g + 1) * SGU_GROUP_CH)
            vm = jnp.dot(ws_ref[g], v[:, cols], preferred_element_type=F32) + bs_ref[:, g:g + 1]
            u = jax.nn.gelu(u_ref[rows, cols].astype(F32))
            z = z_ref[rows, cols].astype(F32)
            o_ref[rows, cols] = (u * vm * jax.nn.silu(z)).astype(BF16)


def _sgu(proj, ln_g, ln_b, w_s_bf16, b_s_t):
    t = proj.shape[0]
    tm = ROW_TM
    return pl.pallas_call(
        functools.partial(_sgu_kernel, n_chunks=tm // SGU_CHUNK),
        out_shape=jax.ShapeDtypeStruct((t, D_SGU), BF16),
        grid=(t // tm,),
        in_specs=[
            pl.BlockSpec((tm, D_SGU), lambda i: (i, COL_U // D_SGU)),
            pl.BlockSpec((tm, D_SGU), lambda i: (i, COL_VS // D_SGU)),
            pl.BlockSpec((tm, D_SGU), lambda i: (i, COL_ZC // D_SGU)),
            pl.BlockSpec((1, D_SGU), lambda i: (0, 0)),
            pl.BlockSpec((1, D_SGU), lambda i: (0, 0)),
            pl.BlockSpec((N_SGU_GROUPS, SGU_CHUNK, SGU_CHUNK), lambda i: (0, 0, 0)),
            pl.BlockSpec((SGU_CHUNK, N_SGU_GROUPS), lambda i: (0, 0)),
        ],
        out_specs=pl.BlockSpec((tm, D_SGU), lambda i: (i, 0)),
        compiler_params=_params(("arbitrary",)),
        name="sgu",
    )(proj, proj, proj, ln_g.reshape(1, D_SGU), ln_b.reshape(1, D_SGU), w_s_bf16, b_s_t)


def _merge_kernel(mg_ref, ya_ref, yb_ref, yc_ref, x_ref, gate_ref, wpa_ref, wpb_ref, wpc_ref, wout_ref,
                  lng_ref, lnb_ref, y_ref, *, alpha):
    d = D_MODEL
    tm = y_ref.shape[0]
    for r0 in range(0, tm, MERGE_SUB):
        rows = slice(r0, r0 + MERGE_SUB)

        def branch(k, y_r, w_r):
            gate = jax.nn.sigmoid(mg_ref[rows, k * d:(k + 1) * d].astype(F32))
            return gate * jnp.dot(y_r[rows, :], w_r[...], preferred_element_type=F32)

        merged = branch(0, ya_ref, wpa_ref) + branch(1, yb_ref, wpb_ref) + branch(2, yc_ref, wpc_ref)
        out = jnp.dot(merged.astype(BF16), wout_ref[...], preferred_element_type=F32)
        z = alpha * x_ref[rows, :] + gate_ref[0] * out
        y_ref[rows, :] = _layernorm_f32(z) * lng_ref[...] + lnb_ref[...]


def _merge(proj, ya, yb, yc, x2, gate, wpa, wpb, wpc, wout, ln_g, ln_b, seq_len, alpha):
    t, d = x2.shape
    tm = MERGE_TM
    if gate.shape[0] == 1:
        mod_map = lambda i: (0, 0, 0)
    else:
        assert seq_len % tm == 0
        tiles_per_seq = seq_len // tm
        mod_map = lambda i: (i // tiles_per_seq, 0, 0)

    def const(shape):
        return pl.BlockSpec(shape, lambda i: (0,) * len(shape), pipeline_mode=pl.Buffered(1))

    return pl.pallas_call(
        functools.partial(_merge_kernel, alpha=alpha),
        out_shape=jax.ShapeDtypeStruct((t, d), F32),
        grid=(t // tm,),
        in_specs=[
            pl.BlockSpec((tm, D_MG), lambda i: (i, 0)),
            pl.BlockSpec((tm, D_SSM), lambda i: (i, 0)),
            pl.BlockSpec((tm, D_ATTN), lambda i: (i, 0)),
            pl.BlockSpec((tm, D_SGU), lambda i: (i, 0)),
            pl.BlockSpec((tm, d), lambda i: (i, 0)),
            pl.BlockSpec((1, 1, d), mod_map),
            const((D_SSM, d)), const((D_ATTN, d)), const((D_SGU, d)), const((d, d)),
            const((1, d)), const((1, d)),
        ],
        out_specs=pl.BlockSpec((tm, d), lambda i: (i, 0)),
        compiler_params=_params(("arbitrary",)),
        name="merge_out",
    )(proj, ya, yb, yc, x2, gate, wpa, wpb, wpc, wout, ln_g.reshape(1, d), ln_b.reshape(1, d))


def _layer(x2, n_seq, seq_len, mod, lp, k_arr, v_arr, slot, ctx_k, ctx_v, s0, rope, alpha):
    d = D_MODEL
    nbm = mod.shape[0]
    shift = mod[:, 0:d].reshape(nbm, 1, d)
    scale = mod[:, d:2 * d].reshape(nbm, 1, d)
    gate = mod[:, 2 * d:3 * d].reshape(nbm, 1, d)
    proj, k_arr, v_arr, xa = _inproj(x2, scale, shift, lp['w_in'], lp['layer'], n_seq, seq_len, k_arr, v_arr, slot)

    n_seg = seq_len // SSM_SEG
    yssm, fin = _ssm_scan(xa, lp['fold'], lp['w_intra'], lp['w_state'], lp['w_enter'], lp['ssm_small'],
                          lp['layer'], s0, n_seq, n_seg)
    ya = _ssm_post(proj, yssm, lp['d_skip'], lp['w_glu'], lp['b_glu'])

    if ctx_k is None:
        yb = _ctx_attention(proj, k_arr, v_arr, slot, lp['sink'], seq_len)
    else:
        yb = _lat_attention(proj, k_arr, v_arr, slot, ctx_k, ctx_v, lp['sink'], *rope, n_seq, seq_len)

    yc = _sgu(proj, lp['sgu_g'], lp['sgu_b'], lp['w_s'], lp['b_s_t'])

    y = _merge(proj, ya, yb, yc, x2, gate, lp['w_pa'], lp['w_pb'], lp['w_pc'], lp['w_out'],
               lp['ln_g'], lp['ln_b'], seq_len, alpha)
    return y, k_arr, v_arr, fin


def _states_to_lanes(s):
    b = s.shape[0]
    return s.astype(F32).transpose(3, 0, 2, 1, 4).reshape(N_SSM_GROUPS, b, 4 * SSM_STATE)


def _lanes_to_states(f):
    b = f.shape[1]
    return f.reshape(N_SSM_GROUPS, b, 2, 2, SSM_STATE).transpose(1, 3, 2, 0, 4)


def kernel(x_prompt, x_sample, cache_k, cache_v, state_ssm, c, c_ctx, w_ada, b_ada, w_in, ssm_lam_re, ssm_lam_im, ssm_log_step, ssm_b_re, ssm_b_im, ssm_c_re, ssm_c_im, ssm_d, w_glu, b_glu, attn_sink, sgu_ln_g, sgu_ln_b, w_spatial, b_spatial, w_proj_a, w_proj_b, w_proj_c, w_out, ln_g, ln_b):
    depth = w_in.shape[0]
    batch, seq, d = x_prompt.shape
    dec_batch, dec_seq, _ = x_sample.shape
    past_len = cache_k.shape[2]
    alpha = (2 * depth) ** 0.25

    n_cond = 1 + dec_batch
    cond_rows = -(-n_cond // 8) * 8
    cond = jnp.concatenate([c_ctx[None, :], c, jnp.zeros((cond_rows - n_cond, d), F32)], axis=0)
    mod = _ada_mod(cond, w_ada, b_ada)

    w_in_bf16 = w_in.astype(BF16)
    fold = _ssm_fold_matrix()
    tables = _ssm_tables(ssm_lam_re, ssm_lam_im, ssm_log_step, ssm_b_re, ssm_b_im, ssm_c_re, ssm_c_im)
    layers = []
    for l in range(depth):
        w_intra, w_state, w_enter, small = tables
        layers.append({
            'w_in': w_in_bf16, 'layer': l,
            'fold': fold, 'w_intra': w_intra, 'w_state': w_state, 'w_enter': w_enter, 'ssm_small': small,
            'd_skip': ssm_d[l], 'w_glu': w_glu[l].astype(BF16), 'b_glu': b_glu[l],
            'sink': attn_sink[l].astype(F32),
            'sgu_g': sgu_ln_g[l], 'sgu_b': sgu_ln_b[l],
            'w_s': w_spatial[l].astype(BF16), 'b_s_t': b_spatial[l].T.astype(F32),
            'w_pa': w_proj_a[l].astype(BF16), 'w_pb': w_proj_b[l].astype(BF16),
            'w_pc': w_proj_c[l].astype(BF16), 'w_out': w_out[l].astype(BF16),
            'ln_g': ln_g[l], 'ln_b': ln_b[l],
        })

    h = x_prompt.reshape(batch * seq, d)
    zero_state = jnp.zeros((N_SSM_GROUPS, batch, 4 * SSM_STATE), F32)
    k_new = jnp.zeros((batch, depth, seq, D_KV), F32)
    v_new = jnp.zeros((batch, depth, seq, D_KV), F32)
    ss = []
    for l in range(depth):
        h, k_new, v_new, fin = _layer(h, batch, seq, mod[l, 0:1], layers[l], k_new, v_new, l,
                                      None, None, zero_state, None, alpha)
        ss.append(_lanes_to_states(fin))
    y_prompt = h.reshape(batch, seq, d)
    new_cache_k = k_new.reshape(batch, depth, seq, N_KV_HEADS, HEAD_DIM)
    new_cache_v = v_new.reshape(batch, depth, seq, N_KV_HEADS, HEAD_DIM)
    new_state_ssm = jnp.stack(ss, axis=1)

    rope = _rope_tables(dec_seq) + (_window_bias(), _rope_swap_matrix())
    z = x_sample.reshape(dec_batch * dec_seq, d)
    k_lat = jnp.zeros((dec_batch, 1, dec_seq, D_KV), F32)
    v_lat = jnp.zeros((dec_batch, 1, dec_seq, D_KV), F32)
    for l in range(depth):
        ctx_k = cache_k[:, l].reshape(dec_batch, past_len, D_KV).astype(F32)
        ctx_v = cache_v[:, l].reshape(dec_batch, past_len, D_KV).astype(F32)
        z, k_lat, v_lat, _ = _layer(z, dec_batch, dec_seq, mod[l, 1:1 + dec_batch], layers[l], k_lat, v_lat, 0,
                                    ctx_k, ctx_v, _states_to_lanes(state_ssm[:, l]), rope, alpha)
    y_sample = z.reshape(dec_batch, dec_seq, d)
    return (y_prompt, y_sample, new_cache_k, new_cache_v, new_state_ssm)
```

```python
import functools
import math

import jax
import jax.numpy as jnp
import numpy as np
from jax import lax
from jax.experimental import pallas as pl
from jax.experimental.pallas import tpu as pltpu

F32 = jnp.float32
BF16 = jnp.bfloat16

D_MODEL = 2048
GRID_W = 64
D_SSM = 512
SSM_GROUP = 16
N_SSM_GROUPS = D_SSM // SSM_GROUP
SSM_STATE = 64
HEAD_DIM = 128
N_HEADS = 8
N_KV_HEADS = 2
Q_PER_KV = N_HEADS // N_KV_HEADS
D_ATTN = N_HEADS * HEAD_DIM
D_KV = N_KV_HEADS * HEAD_DIM
WINDOW = 128
ATTN_BLOCK = 128
ROPE_BASE = 10000.0
D_SGU = 512
SGU_CHUNK = 128
SGU_GROUP_CH = 128
N_SGU_GROUPS = D_SGU // SGU_GROUP_CH
N_BRANCH = 3
LN_EPS = 1e-5

D_MG = N_BRANCH * D_MODEL
COL_MG = 0
COL_XA = COL_MG + D_MG
COL_ZA = COL_XA + D_SSM
COL_Q = COL_ZA + D_SSM
COL_K = COL_Q + D_ATTN
COL_V = COL_K + D_KV
COL_ZB = COL_V + D_KV
COL_U = COL_ZB + D_ATTN
COL_VS = COL_U + D_SGU
COL_ZC = COL_VS + D_SGU
D_IN = COL_ZC + D_SGU
W_IN_ROTATE = D_IN - D_MG

SSM_T = 16
SSM_NC = 16
SSM_TC = SSM_T * SSM_GROUP
SSM_SEG = SSM_T * SSM_NC
SSM_SMALL_ROWS = 40
SSM_LANE_GROUPS = 128 // SSM_GROUP
SSM_POW_ROWS = 24

VMEM_LIMIT_BYTES = 56 * 1024 * 1024

INPROJ_TM = 1024
INPROJ_TN = 1024
INPROJ_SUB = 256
ROW_TM = 512
MERGE_TM = 512
MERGE_SUB = 256


def _params(sem):
    return pltpu.CompilerParams(dimension_semantics=sem, vmem_limit_bytes=VMEM_LIMIT_BYTES)


def _layernorm_f32(x):
    mu = jnp.mean(x, axis=-1, keepdims=True)
    xc = x - mu
    var = jnp.mean(xc * xc, axis=-1, keepdims=True)
    return xc * lax.rsqrt(var + LN_EPS)


def _ada_kernel(cond_ref, w_ref, b_ref, o_ref):
    c = cond_ref[...]
    a = (c * jax.nn.sigmoid(c)).astype(BF16)
    o_ref[...] = jnp.dot(a, w_ref[...].astype(BF16), preferred_element_type=F32) + b_ref[...]


def _ada_mod(cond, w_ada, b_ada):
    depth, d, n = w_ada.shape
    r = cond.shape[0]
    tn = 512
    return pl.pallas_call(
        _ada_kernel,
        out_shape=jax.ShapeDtypeStruct((depth, r, n), F32),
        grid=(depth, n // tn),
        in_specs=[
            pl.BlockSpec((r, d), lambda l, j: (0, 0)),
            pl.BlockSpec((None, d, tn), lambda l, j: (l, 0, j)),
            pl.BlockSpec((None, 1, tn), lambda l, j: (l, 0, j)),
        ],
        out_specs=pl.BlockSpec((None, r, tn), lambda l, j: (l, 0, j)),
        compiler_params=_params(("arbitrary", "arbitrary")),
        name="ada_mod",
    )(cond, w_ada, b_ada.reshape(depth, 1, n))


def _inproj_kernel(x_ref, scale_ref, shift_ref, w_ref, k_in_ref, v_in_ref, proj_ref, k_ref, v_ref, xa_ref, h_scr,
                   *, kv_tile, xa_tile):
    del k_in_ref, v_in_ref
    j = pl.program_id(1)
    tm = x_ref.shape[0]

    assert kv_tile != 0 and xa_tile != 0

    @pl.when(j == 0)
    def _():
        for r0 in range(0, tm, INPROJ_SUB):
            rows = slice(r0, r0 + INPROJ_SUB)
            h = (_layernorm_f32(x_ref[rows, :]) * (1.0 + scale_ref[0]) + shift_ref[0]).astype(BF16)
            h_scr[rows, :] = h
            proj_ref[rows, :] = jnp.dot(h, w_ref[...], preferred_element_type=F32).astype(BF16)

    takes_f32 = (j == kv_tile) | (j == xa_tile)

    @pl.when((j != 0) & jnp.logical_not(takes_f32))
    def _():
        for r0 in range(0, tm, INPROJ_SUB):
            rows = slice(r0, r0 + INPROJ_SUB)
            proj_ref[rows, :] = jnp.dot(h_scr[rows, :], w_ref[...], preferred_element_type=F32).astype(BF16)

    @pl.when(takes_f32)
    def _():
        acc = jnp.dot(h_scr[...], w_ref[...], preferred_element_type=F32)
        proj_ref[...] = acc.astype(BF16)

        @pl.when(j == kv_tile)
        def _():
            k_ref[...] = acc[:, 0:D_KV].reshape(k_ref.shape)
            v_ref[...] = acc[:, D_KV:2 * D_KV].reshape(v_ref.shape)

        @pl.when(j == xa_tile)
        def _():
            xa_ref[...] = acc[:, 0:D_SSM]


def _inproj(x2, scale, shift, w_bf16, layer, n_seq, seq_len, k_arr, v_arr, slot):
    t, d = x2.shape
    tm, tn = INPROJ_TM, INPROJ_TN
    n_tiles = w_bf16.shape[1]
    n = n_tiles * tn
    assert w_bf16.shape[2:] == (d, tn)
    assert t % tm == 0 and COL_K % tn == 0 and COL_XA % tn == 0
    assert 2 * D_KV <= tn and D_SSM <= tn and W_IN_ROTATE % tn == 0
    rot_tiles = W_IN_ROTATE // tn
    if seq_len >= tm:
        assert seq_len % tm == 0
        tiles_per_seq = seq_len // tm
        seq_of = lambda i: i // tiles_per_seq
        kv_spec = pl.BlockSpec((None, None, tm, D_KV), lambda i, j: (i // tiles_per_seq, slot, i % tiles_per_seq, 0))
    else:
        assert tm % seq_len == 0
        seq_of = lambda i: i * (tm // seq_len)
        kv_spec = pl.BlockSpec((tm // seq_len, None, seq_len, D_KV), lambda i, j: (i, slot, 0, 0))
    if scale.shape[0] == 1:
        mod_map = lambda i, j: (0, 0, 0)
    else:
        assert seq_len >= tm
        mod_map = lambda i, j: (seq_of(i), 0, 0)
    any_spec = pl.BlockSpec(memory_space=pl.ANY)
    return pl.pallas_call(
        functools.partial(_inproj_kernel, kv_tile=COL_K // tn, xa_tile=COL_XA // tn),
        out_shape=(jax.ShapeDtypeStruct((t, n), BF16), jax.ShapeDtypeStruct(k_arr.shape, F32),
                   jax.ShapeDtypeStruct(v_arr.shape, F32), jax.ShapeDtypeStruct((t, D_SSM), F32)),
        grid=(t // tm, n // tn),
        in_specs=[
            pl.BlockSpec((tm, d), lambda i, j: (i, 0)),
            pl.BlockSpec((1, 1, d), mod_map),
            pl.BlockSpec((1, 1, d), mod_map),
            pl.BlockSpec((None, None, d, tn), lambda i, j: (layer, (j + rot_tiles) % n_tiles, 0, 0)),
            any_spec, any_spec,
        ],
        out_specs=(
            pl.BlockSpec((tm, tn), lambda i, j: (i, j)),
            kv_spec, kv_spec,
            pl.BlockSpec((tm, D_SSM), lambda i, j: (i, 0)),
        ),
        input_output_aliases={4: 1, 5: 2},
        scratch_shapes=[pltpu.VMEM((tm, d), BF16)],
        compiler_params=_params(("arbitrary", "arbitrary")),
        name="ln_inproj",
    )(x2, scale, shift, w_bf16, k_arr, v_arr)


def _zoh(lr, li, ls):
    dt = jnp.exp(ls)
    mag = jnp.exp(lr * dt)
    ar = mag * jnp.cos(li * dt)
    ai = mag * jnp.sin(li * dt)
    den = lr * lr + li * li
    nr = ar - 1.0
    return ar, ai, (nr * lr + ai * li) / den, (ai * lr - nr * li) / den


def _cmul(xr, xi, yr, yi):
    return xr * yr - xi * yi, xr * yi + xi * yr


def _cpow(br, bi, expo, n_bits):
    shape = jnp.broadcast_shapes(br.shape, expo.shape)
    pr = jnp.ones(shape, F32)
    pi = jnp.zeros(shape, F32)
    for bit in range(n_bits):
        sel = ((expo >> bit) & 1) == 1
        nr, ni = _cmul(pr, pi, br, bi)
        pr = jnp.where(sel, nr, pr)
        pi = jnp.where(sel, ni, pi)
        if bit + 1 < n_bits:
            br, bi = _cmul(br, bi, br, bi)
    return pr, pi


def _ssm_tables_kernel(laml_ref, bt_ref, ct_ref, ec_ref, ek_ref, wi_ref, ws_ref, we_ref, small_ref):
    p_n, t_n, c_n = SSM_STATE, SSM_T, SSM_GROUP
    p2 = 2 * p_n
    tc = t_n * c_n
    n_bits = 5
    assert max(t_n, SSM_NC) < 2 ** n_bits and SSM_POW_ROWS <= 2 ** n_bits

    def dot(a, b):
        return jnp.dot(a, b, precision=lax.Precision.HIGHEST, preferred_element_type=F32)

    ar, ai, f_re, f_im = _zoh(laml_ref[0:1, :], laml_ref[1:2, :], laml_ref[2:3, :])
    is_fwd = lax.broadcasted_iota(jnp.int32, (1, p2), 1) < p_n
    bbt_re, bbt_im = _cmul(f_re, f_im, bt_ref[:, 0:p2], bt_ref[:, p2:2 * p2])
    krow = lax.broadcasted_iota(jnp.int32, (SSM_POW_ROWS, 1), 0)
    n_lane = ct_ref.shape[1]
    kclamp = jnp.minimum(lax.broadcasted_iota(jnp.int32, (n_lane, 1), 0), t_n)
    pw_re, pw_im = _cpow(ar, ai, kclamp, n_bits)
    for j in range(t_n):
        pm_re = jnp.where(is_fwd, pw_re[t_n - 1 - j:t_n - j], pw_re[j:j + 1])
        pm_im = jnp.where(is_fwd, pw_im[t_n - 1 - j:t_n - j], pw_im[j:j + 1])
        s_re, s_im = _cmul(bbt_re, bbt_im, pm_re, pm_im)
        ws_ref[j * c_n:(j + 1) * c_n, 0:p2] = s_re.astype(BF16)
        ws_ref[j * c_n:(j + 1) * c_n, p2:2 * p2] = s_im.astype(BF16)
    a1r = pw_re[t_n:t_n + 1]
    a1i = pw_im[t_n:t_n + 1]
    q_re, q_im = _cpow(a1r, a1i, jnp.where(is_fwd, krow, jnp.maximum(SSM_NC - 1 - krow, 0)), n_bits)
    as_re, as_im = _cpow(a1r, a1i, jnp.full((1, 1), SSM_NC, jnp.int32), n_bits)
    small_ref[0:SSM_NC, :] = q_re[0:SSM_NC]
    small_ref[SSM_NC:2 * SSM_NC, :] = q_im[0:SSM_NC]
    small_ref[2 * SSM_NC:SSM_SMALL_ROWS, :] = jnp.concatenate(
        [a1r, a1i, as_re, as_im, jnp.zeros((SSM_SMALL_ROWS - 2 * SSM_NC - 4, p2), F32)], axis=0)

    pt_re_all = pw_re.T
    pt_im_all = pw_im.T
    cpt = []
    for d in range(2):
        pt_re = pt_re_all[d * p_n:(d + 1) * p_n, :]
        pt_im = pt_im_all[d * p_n:(d + 1) * p_n, :]
        c_re = dot(ct_ref[...], ec_ref[d])
        c_im = dot(ct_ref[...], ec_ref[2 + d])
        cpt.append(_cmul(c_re, c_im, dot(pt_re, ek_ref[d]), dot(pt_im, ek_ref[d])))
        e_re, e_im = _cmul(c_re, c_im, dot(pt_re, ek_ref[2 + d]), dot(pt_im, ek_ref[2 + d]))
        we_ref[d * p_n:(d + 1) * p_n, :] = e_re.astype(BF16)
        we_ref[p2 + d * p_n:p2 + (d + 1) * p_n, :] = (-e_im).astype(BF16)
    cpt_re = jnp.concatenate([cpt[0][0], cpt[1][0]], axis=0)
    cpt_im = jnp.concatenate([cpt[0][1], cpt[1][1]], axis=0)
    taps = []
    for sel in (is_fwd, jnp.logical_not(is_fwd)):
        taps.append(dot(jnp.where(sel, bbt_re, 0.0), cpt_re) - dot(jnp.where(sel, bbt_im, 0.0), cpt_im))
    lane = lax.broadcasted_iota(jnp.int32, (c_n, tc), 1)
    for j in range(t_n):
        fwd = jnp.where(lane >= j * c_n, pltpu.roll(taps[0], j * c_n, 1), 0.0)
        bwd = jnp.where(lane < (j + 1) * c_n, pltpu.roll(taps[1], (tc - (t_n - 1 - j) * c_n) % tc, 1), 0.0)
        wi_ref[j * c_n:(j + 1) * c_n, :] = (fwd + bwd).astype(BF16)


def _ssm_table_constants():
    t_n, c_n = SSM_T, SSM_GROUP
    rows = np.arange(128)[:, None]
    tok = (np.arange(t_n * c_n) // c_n)[None, :]
    ch = (np.arange(t_n * c_n) % c_n)[None, :]
    ec = np.stack([(rows == q * c_n + ch) for q in range(4)], axis=0)
    ek = np.stack([rows == tok, rows == t_n - 1 - tok, rows == tok + 1, rows == t_n - tok], axis=0)
    return jnp.asarray(ec.astype(np.float32)), jnp.asarray(ek.astype(np.float32))


def _ssm_tables(lam_re, lam_im, log_step, b_re, b_im, c_re, c_im):
    l_n, _, g_n, p_n = lam_re.shape
    c_n, t_n = SSM_GROUP, SSM_T
    tc = t_n * c_n
    p2, p4 = 2 * p_n, 4 * p_n
    ls = jnp.broadcast_to(log_step.astype(F32)[..., None], lam_re.shape)

    def lanes(x):
        return x.astype(F32).transpose(0, 2, 1, 3).reshape(l_n, g_n, 1, p2)

    laml = jnp.concatenate([lanes(lam_re), lanes(lam_im), lanes(ls), jnp.zeros((l_n, g_n, 5, p2), F32)], axis=2)

    def b_lanes(x):
        return x.astype(F32).transpose(0, 2, 4, 1, 3).reshape(l_n, g_n, c_n, p2)

    bt = jnp.concatenate([b_lanes(b_re), b_lanes(b_im)], axis=-1)

    def c_lanes(x):
        return x.astype(F32).transpose(0, 2, 4, 1, 3).reshape(l_n, g_n, p_n, 2 * c_n)

    ct = jnp.concatenate([c_lanes(c_re), c_lanes(c_im), jnp.zeros((l_n, g_n, p_n, 128 - 4 * c_n), F32)], axis=-1)
    ec, ek = _ssm_table_constants()

    def per_group(*dims):
        return pl.BlockSpec((None, None) + dims, lambda l, g: (l, g) + (0,) * len(dims))

    def const(shape):
        return pl.BlockSpec(shape, lambda l, g: (0,) * len(shape))

    return pl.pallas_call(
        _ssm_tables_kernel,
        out_shape=(jax.ShapeDtypeStruct((l_n, g_n, tc, tc), BF16), jax.ShapeDtypeStruct((l_n, g_n, tc, p4), BF16),
                   jax.ShapeDtypeStruct((l_n, g_n, p4, tc), BF16),
                   jax.ShapeDtypeStruct((l_n, g_n, SSM_SMALL_ROWS, p2), F32)),
        grid=(l_n, g_n),
        in_specs=[per_group(8, p2), per_group(c_n, p4), per_group(p_n, 128),
                  const(ec.shape), const(ek.shape)],
        out_specs=(per_group(tc, tc), per_group(tc, p4), per_group(p4, tc), per_group(SSM_SMALL_ROWS, p2)),
        compiler_params=_params(("arbitrary", "arbitrary")),
        name="ssm_tables",
    )(laml, bt, ct, ec, ek)


def _ssm_fold_matrix():
    half_t = SSM_T // 2
    n = half_t * SSM_LANE_GROUPS * SSM_GROUP
    src = np.arange(n)
    j = src // (SSM_LANE_GROUPS * SSM_GROUP)
    g = (src // SSM_GROUP) % SSM_LANE_GROUPS
    c = src % SSM_GROUP
    dst = g * (half_t * SSM_GROUP) + j * SSM_GROUP + c
    return jnp.asarray((dst[:, None] == np.arange(n)[None, :]).astype(BF16))


def _ssm_kernel(xa_ref, fold_ref, wi_ref, ws_ref, we_ref, small_ref, s0_ref, y_ref, fin_ref,
                v_scr, s_scr, yy_scr, fre_scr, fim_scr, efre_scr, efim_scr, ebre_scr, ebim_scr,
                *, n_seq, n_seg):
    p2 = 2 * SSM_STATE
    p4 = 4 * SSM_STATE
    nbp = n_seq * n_seg
    lanes = SSM_LANE_GROUPS * SSM_GROUP
    nt = (((1,), (1,)), ((), ()))

    blocks = []
    for c in range(SSM_NC):
        toks = [xa_ref[pl.ds(c * SSM_T + j, nbp, stride=SSM_SEG), :].astype(BF16) for j in range(SSM_T)]
        blocks.append(jnp.concatenate(toks, axis=1))
    xx = jnp.concatenate(blocks, axis=0)
    hw = fold_ref.shape[0]
    uu = [jnp.dot(xx[:, h * hw:(h + 1) * hw], fold_ref[...], preferred_element_type=F32).astype(BF16)
          for h in range(2)]

    lane = lax.broadcasted_iota(jnp.int32, (1, p2), 1)
    is_fwd = lane < SSM_STATE
    for g in range(SSM_LANE_GROUPS):
        gc = slice(g * SSM_TC, (g + 1) * SSM_TC)
        u = jnp.concatenate([uu[h][:, g * lanes:(g + 1) * lanes] for h in range(2)], axis=1)
        yy_scr[:, gc] = jnp.dot(u, wi_ref[g], preferred_element_type=F32)
        v_scr[:, gc] = jnp.dot(u, ws_ref[g], preferred_element_type=F32)

    for g in range(SSM_LANE_GROUPS):
        o = g * p4
        a1r = small_ref[g, 2 * SSM_NC:2 * SSM_NC + 1, :]
        a1i = small_ref[g, 2 * SSM_NC + 1:2 * SSM_NC + 2, :]
        asr = small_ref[g, 2 * SSM_NC + 2:2 * SSM_NC + 3, :]
        asi = small_ref[g, 2 * SSM_NC + 3:2 * SSM_NC + 4, :]

        st_re = jnp.zeros((nbp, p2), F32)
        st_im = jnp.zeros((nbp, p2), F32)
        for c in range(SSM_NC):
            cb = SSM_NC - 1 - c
            rf = slice(c * nbp, (c + 1) * nbp)
            rb = slice(cb * nbp, (cb + 1) * nbp)
            s_scr[rf, o:o + SSM_STATE] = st_re[:, 0:SSM_STATE]
            s_scr[rb, o + SSM_STATE:o + p2] = st_re[:, SSM_STATE:p2]
            s_scr[rf, o + p2:o + p2 + SSM_STATE] = st_im[:, 0:SSM_STATE]
            s_scr[rb, o + p2 + SSM_STATE:o + p4] = st_im[:, SSM_STATE:p2]
            v_re = jnp.where(is_fwd, v_scr[rf, o:o + p2], v_scr[rb, o:o + p2])
            v_im = jnp.where(is_fwd, v_scr[rf, o + p2:o + p4], v_scr[rb, o + p2:o + p4])
            st_re, st_im = (a1r * st_re - a1i * st_im + v_re, a1r * st_im + a1i * st_re + v_im)

        fre_scr[...] = st_re
        fim_scr[...] = st_im
        s0 = s0_ref[g]
        e_re = s0[:, 0:p2]
        e_im = s0[:, p2:p4]
        for k in range(n_seg):
            kb = n_seg - 1 - k
            rows_f = pl.ds(k, n_seq, stride=n_seg)
            rows_b = pl.ds(kb, n_seq, stride=n_seg)
            efre_scr[rows_f, :] = e_re
            efim_scr[rows_f, :] = e_im
            ebre_scr[rows_b, :] = e_re
            ebim_scr[rows_b, :] = e_im
            f_re = jnp.where(is_fwd, fre_scr[rows_f, :], fre_scr[rows_b, :])
            f_im = jnp.where(is_fwd, fim_scr[rows_f, :], fim_scr[rows_b, :])
            e_re, e_im = (asr * e_re - asi * e_im + f_re, asr * e_im + asi * e_re + f_im)
        fin_ref[g] = jnp.concatenate([e_re, e_im], axis=-1)

        en_re = jnp.where(is_fwd, efre_scr[...], ebre_scr[...])
        en_im = jnp.where(is_fwd, efim_scr[...], ebim_scr[...])
        for c in range(SSM_NC):
            rows = slice(c * nbp, (c + 1) * nbp)
            cr = small_ref[g, c:c + 1, :]
            ci = small_ref[g, SSM_NC + c:SSM_NC + c + 1, :]
            s_scr[rows, o:o + p2] = s_scr[rows, o:o + p2] + (cr * en_re - ci * en_im)
            s_scr[rows, o + p2:o + p4] = s_scr[rows, o + p2:o + p4] + (cr * en_im + ci * en_re)

    for g in range(SSM_LANE_GROUPS):
        gc = slice(g * SSM_TC, (g + 1) * SSM_TC)
        yy_scr[:, gc] += jnp.dot(s_scr[:, g * p4:(g + 1) * p4].astype(BF16), we_ref[g],
                                 preferred_element_type=F32)

    half_t = SSM_T // 2
    for h in range(2):
        yh = jnp.concatenate([yy_scr[:, g * SSM_TC + h * lanes:g * SSM_TC + (h + 1) * lanes]
                              for g in range(SSM_LANE_GROUPS)], axis=1).astype(BF16)
        zz = lax.dot_general(yh, fold_ref[...], nt, preferred_element_type=F32)
        for c in range(SSM_NC):
            for j in range(half_t):
                y_ref[pl.ds(c * SSM_T + h * half_t + j, nbp, stride=SSM_SEG), :] = zz[c * nbp:(c + 1) * nbp,
                                                                                       j * lanes:(j + 1) * lanes]


def _ssm_scan(xa, fold, w_intra, w_state, w_enter, small, layer, s0, n_seq, n_seg):
    t = xa.shape[0]
    g_n = N_SSM_GROUPS
    lg = SSM_LANE_GROUPS
    lanes = lg * SSM_GROUP
    p4 = 4 * SSM_STATE
    nbp = n_seq * n_seg
    r = SSM_NC * nbp
    assert t == r * SSM_T and (SSM_T // 2) * SSM_GROUP == lanes and fold.shape == (lg * lanes, lg * lanes)
    const = lambda shape: pl.BlockSpec(shape, lambda i: (0,) * len(shape), pipeline_mode=pl.Buffered(1))
    return pl.pallas_call(
        functools.partial(_ssm_kernel, n_seq=n_seq, n_seg=n_seg),
        out_shape=(jax.ShapeDtypeStruct((t, D_SSM), F32), jax.ShapeDtypeStruct((g_n, n_seq, p4), F32)),
        grid=(g_n // lg,),
        in_specs=[
            pl.BlockSpec((t, lanes), lambda i: (0, i)),
            const(fold.shape),
            pl.BlockSpec((None, lg, SSM_TC, SSM_TC), lambda i: (layer, i, 0, 0)),
            pl.BlockSpec((None, lg, SSM_TC, p4), lambda i: (layer, i, 0, 0)),
            pl.BlockSpec((None, lg, p4, SSM_TC), lambda i: (layer, i, 0, 0)),
            pl.BlockSpec((None, lg, SSM_SMALL_ROWS, 2 * SSM_STATE), lambda i: (layer, i, 0, 0)),
            pl.BlockSpec((lg, n_seq, p4), lambda i: (i, 0, 0)),
        ],
        out_specs=(
            pl.BlockSpec((t, lanes), lambda i: (0, i)),
            pl.BlockSpec((lg, n_seq, p4), lambda i: (i, 0, 0)),
        ),
        scratch_shapes=[pltpu.VMEM((r, lg * p4), F32), pltpu.VMEM((r, lg * p4), F32),
                        pltpu.VMEM((r, lg * SSM_TC), F32)]
        + [pltpu.VMEM((nbp, 2 * SSM_STATE), F32)] * 6,
        compiler_params=_params(("arbitrary",)),
        name="ssm_scan",
    )(xa, fold, w_intra, w_state, w_enter, small, s0)


def _ssm_post_kernel(xa_ref, za_ref, ys_ref, d_ref, w_ref, b_ref, o_ref):
    y = xa_ref[...].astype(F32) * d_ref[...] + ys_ref[...]
    y = jax.nn.gelu(y)
    gl = jnp.dot(y.astype(BF16), w_ref[...], preferred_element_type=F32) + b_ref[...]
    y = y * jax.nn.sigmoid(gl)
    o_ref[...] = (y * jax.nn.silu(za_ref[...].astype(F32))).astype(BF16)


def _ssm_post(proj, yssm, d_skip, w_glu_bf16, b_glu):
    t = proj.shape[0]
    tm = ROW_TM
    return pl.pallas_call(
        _ssm_post_kernel,
        out_shape=jax.ShapeDtypeStruct((t, D_SSM), BF16),
        grid=(t // tm,),
        in_specs=[
            pl.BlockSpec((tm, D_SSM), lambda i: (i, COL_XA // D_SSM)),
            pl.BlockSpec((tm, D_SSM), lambda i: (i, COL_ZA // D_SSM)),
            pl.BlockSpec((tm, D_SSM), lambda i: (i, 0)),
            pl.BlockSpec((1, D_SSM), lambda i: (0, 0)),
            pl.BlockSpec((D_SSM, D_SSM), lambda i: (0, 0)),
            pl.BlockSpec((1, D_SSM), lambda i: (0, 0)),
        ],
        out_specs=pl.BlockSpec((tm, D_SSM), lambda i: (i, 0)),
        compiler_params=_params(("arbitrary",)),
        name="ssm_post",
    )(proj, proj, yssm, d_skip.reshape(1, D_SSM), w_glu_bf16, b_glu.reshape(1, D_SSM))


def _softmax_pv(scores, sink, values):
    m = sink
    for s in scores:
        m = jnp.maximum(m, jnp.max(s, axis=-1, keepdims=True))
    den = jnp.exp(sink - m)
    o = None
    for s, v in zip(scores, values):
        p = jnp.exp(s - m)
        den = den + jnp.sum(p, axis=-1, keepdims=True)
        pv = jnp.dot(p.astype(BF16), v, preferred_element_type=F32)
        o = pv if o is None else o + pv
    return o / den


def _softmax_pv_wide(s, sink, v_bf16):
    hd = v_bf16.shape[1]
    m = jnp.maximum(jnp.max(s, axis=-1, keepdims=True), sink)
    p = jnp.exp(s - m).astype(BF16)
    v_ones = jnp.concatenate([v_bf16, jnp.ones_like(v_bf16)], axis=1)
    o = jnp.dot(p, v_ones, preferred_element_type=F32)
    den = o[:, hd:] + jnp.exp(sink - m)
    return o[:, :hd] / den


def _head_gate(zb_refs, h):
    per = N_HEADS // 2
    z = zb_refs[h // per][:, (h % per) * HEAD_DIM:(h % per + 1) * HEAD_DIM].astype(F32)
    return jax.nn.silu(z)


def _ctx_attn_kernel(sink_ref, q_ref, k_ref, v_ref, zb0_ref, zb1_ref, o_ref):
    scale = HEAD_DIM ** -0.5
    nt = (((1,), (1,)), ((), ()))
    for kvh in range(N_KV_HEADS):
        k = (k_ref[:, kvh * HEAD_DIM:(kvh + 1) * HEAD_DIM] * scale).astype(BF16)
        v = v_ref[:, kvh * HEAD_DIM:(kvh + 1) * HEAD_DIM].astype(BF16)
        for g in range(Q_PER_KV):
            h = kvh * Q_PER_KV + g
            hs = slice(h * HEAD_DIM, (h + 1) * HEAD_DIM)
            s = lax.dot_general(q_ref[:, hs], k, nt, preferred_element_type=F32)
            o = _softmax_pv([s], sink_ref[h], [v])
            o_ref[:, hs] = (o * _head_gate((zb0_ref, zb1_ref), h)).astype(BF16)


def _ctx_attention(proj, k_arr, v_arr, slot, sink, seq_len):
    t = proj.shape[0]
    kv_spec = pl.BlockSpec((None, None, seq_len, D_KV), lambda b: (b, slot, 0, 0))
    return pl.pallas_call(
        _ctx_attn_kernel,
        out_shape=jax.ShapeDtypeStruct((t, D_ATTN), BF16),
        grid=(t // seq_len,),
        in_specs=[
            pl.BlockSpec(memory_space=pltpu.SMEM),
            pl.BlockSpec((seq_len, D_ATTN), lambda b: (b, COL_Q // D_ATTN)),
            kv_spec, kv_spec,
            pl.BlockSpec((seq_len, D_ATTN // 2), lambda b: (b, COL_ZB // (D_ATTN // 2))),
            pl.BlockSpec((seq_len, D_ATTN // 2), lambda b: (b, COL_ZB // (D_ATTN // 2) + 1)),
        ],
        out_specs=pl.BlockSpec((seq_len, D_ATTN), lambda b: (b, 0)),
        compiler_params=_params(("arbitrary",)),
        name="ctx_attention",
    )(sink, proj, k_arr, v_arr, proj, proj)


def _rope_partner(x):
    lane = lax.broadcasted_iota(jnp.int32, x.shape, 1)
    quarter = HEAD_DIM // 4
    first = (lane % (2 * quarter)) < quarter
    return jnp.where(first, pltpu.roll(x, HEAD_DIM - quarter, 1), pltpu.roll(x, quarter, 1))


def _rope(x, tab_ref):
    return x * tab_ref[:, 0:HEAD_DIM] + _rope_partner(x) * tab_ref[:, HEAD_DIM:2 * HEAD_DIM]


def _rope_kv_kernel(k_ref, v_ref, tab_ref, o_ref):
    for kvh in range(N_KV_HEADS):
        ks = slice(kvh * HEAD_DIM, (kvh + 1) * HEAD_DIM)
        o_ref[:, ks] = _rope(k_ref[:, ks], tab_ref).astype(BF16)
    o_ref[:, D_KV:2 * D_KV] = v_ref[...].astype(BF16)


def _rope_kv(k_arr, v_arr, slot, tab_k):
    n_seq, _, seq_len, _ = k_arr.shape
    tm = ROW_TM
    assert seq_len % tm == 0
    tiles_per_seq = seq_len // tm
    kv_spec = pl.BlockSpec((None, None, tm, D_KV), lambda i: (i // tiles_per_seq, slot, i % tiles_per_seq, 0))
    return pl.pallas_call(
        _rope_kv_kernel,
        out_shape=jax.ShapeDtypeStruct((n_seq * seq_len, 2 * D_KV), BF16),
        grid=(n_seq * tiles_per_seq,),
        in_specs=[
            kv_spec, kv_spec,
            pl.BlockSpec((tm, 2 * HEAD_DIM), lambda i: (i % tiles_per_seq, 0)),
        ],
        out_specs=pl.BlockSpec((tm, 2 * D_KV), lambda i: (i, 0)),
        compiler_params=_params(("arbitrary",)),
        name="rope_kv",
    )(k_arr, v_arr, tab_k)


def _lat_attn_kernel(sink_ref, q_ref, kvp_ref, kvc_ref, kvn_ref, ck_ref, cv_ref, zb0_ref, zb1_ref,
                     tabq_ref, swap_ref, bias_ref, o_ref):
    nt = (((1,), (1,)), ((), ()))
    blk = ATTN_BLOCK
    cos_q = tabq_ref[:, 0:HEAD_DIM]
    sin_q = tabq_ref[:, HEAD_DIM:2 * HEAD_DIM]
    for kvh in range(N_KV_HEADS):
        ks = slice(kvh * HEAD_DIM, (kvh + 1) * HEAD_DIM)
        vs = slice(D_KV + kvh * HEAD_DIM, D_KV + (kvh + 1) * HEAD_DIM)
        keys = [jnp.concatenate([kvp_ref[:, ks], kvc_ref[:, ks], kvn_ref[:, ks]], axis=0),
                ck_ref[:, ks].astype(BF16)]
        values = [jnp.concatenate([kvp_ref[:, vs], kvc_ref[:, vs], kvn_ref[:, vs]], axis=0),
                  cv_ref[:, ks].astype(BF16)]
        qx = jnp.concatenate(
            [q_ref[:, (kvh * Q_PER_KV + g) * HEAD_DIM:(kvh * Q_PER_KV + g + 1) * HEAD_DIM]
             for g in range(Q_PER_KV)], axis=0)
        partner = jnp.dot(qx, swap_ref[...], preferred_element_type=F32)
        qf = qx.astype(F32).reshape(Q_PER_KV, blk, HEAD_DIM)
        q4 = (qf * cos_q[None] + partner.reshape(Q_PER_KV, blk, HEAD_DIM) * sin_q[None])
        q4 = q4.reshape(Q_PER_KV * blk, HEAD_DIM).astype(BF16)
        s_loc = lax.dot_general(q4, keys[0], nt, preferred_element_type=F32)
        s_loc = (s_loc.reshape(Q_PER_KV, blk, 3 * blk) + bias_ref[...][None]).reshape(Q_PER_KV * blk, 3 * blk)
        s_ctx = lax.dot_general(q4, keys[1], nt, preferred_element_type=F32)
        sink_rows = jnp.concatenate(
            [jnp.full((blk, 1), sink_ref[kvh * Q_PER_KV + g], F32) for g in range(Q_PER_KV)], axis=0)
        o = _softmax_pv_wide(jnp.concatenate([s_loc, s_ctx], axis=-1), sink_rows,
                             jnp.concatenate(values, axis=0))
        for g in range(Q_PER_KV):
            h = kvh * Q_PER_KV + g
            o_ref[:, h * HEAD_DIM:(h + 1) * HEAD_DIM] = (
                o[g * blk:(g + 1) * blk] * _head_gate((zb0_ref, zb1_ref), h)).astype(BF16)


def _lat_attention(proj, k_arr, v_arr, slot, ctx_k, ctx_v, sink, tab_q, tab_k, bias, swap, n_seq, seq_len):
    kv = _rope_kv(k_arr, v_arr, slot, tab_k)
    t = proj.shape[0]
    blk = ATTN_BLOCK
    assert blk - 1 <= WINDOW <= blk
    nb = seq_len // blk
    lc = ctx_k.shape[1]

    def row(b, i):
        return b * nb + i

    def prev(i):
        return jnp.maximum(i - 1, 0)

    def nxt(i):
        return jnp.minimum(i + 1, nb - 1)

    def bias_variant(b, i):
        return ((i > 0).astype(jnp.int32) * 2 + (i < nb - 1).astype(jnp.int32), 0, 0)

    return pl.pallas_call(
        _lat_attn_kernel,
        out_shape=jax.ShapeDtypeStruct((t, D_ATTN), BF16),
        grid=(n_seq, nb),
        in_specs=[
            pl.BlockSpec(memory_space=pltpu.SMEM),
            pl.BlockSpec((blk, D_ATTN), lambda b, i: (row(b, i), COL_Q // D_ATTN)),
            pl.BlockSpec((blk, 2 * D_KV), lambda b, i: (row(b, prev(i)), 0)),
            pl.BlockSpec((blk, 2 * D_KV), lambda b, i: (row(b, i), 0)),
            pl.BlockSpec((blk, 2 * D_KV), lambda b, i: (row(b, nxt(i)), 0)),
            pl.BlockSpec((None, lc, D_KV), lambda b, i: (b, 0, 0)),
            pl.BlockSpec((None, lc, D_KV), lambda b, i: (b, 0, 0)),
            pl.BlockSpec((blk, D_ATTN // 2), lambda b, i: (row(b, i), COL_ZB // (D_ATTN // 2))),
            pl.BlockSpec((blk, D_ATTN // 2), lambda b, i: (row(b, i), COL_ZB // (D_ATTN // 2) + 1)),
            pl.BlockSpec((blk, 2 * HEAD_DIM), lambda b, i: (i, 0)),
            pl.BlockSpec((HEAD_DIM, HEAD_DIM), lambda b, i: (0, 0)),
            pl.BlockSpec((None, blk, 3 * blk), bias_variant),
        ],
        out_specs=pl.BlockSpec((blk, D_ATTN), lambda b, i: (row(b, i), 0)),
        compiler_params=_params(("arbitrary", "arbitrary")),
        name="lat_attention",
    )(sink, proj, kv, kv, kv, ctx_k, ctx_v, proj, proj, tab_q, swap, bias)


def _rope_tables(seq_len):
    rows = seq_len // GRID_W
    row = np.repeat(np.arange(rows), GRID_W).astype(np.float64)
    col = np.tile(np.arange(GRID_W), rows).astype(np.float64)
    half = HEAD_DIM // 2
    inv = ROPE_BASE ** (-np.arange(0, half, 2, dtype=np.float64) / half)
    ang_r = row[:, None] * inv[None, :]
    ang_c = col[:, None] * inv[None, :]
    cos_t = np.concatenate([np.cos(ang_r)] * 2 + [np.cos(ang_c)] * 2, axis=-1)
    sin_t = np.concatenate([-np.sin(ang_r), np.sin(ang_r), -np.sin(ang_c), np.sin(ang_c)], axis=-1)
    tab_k = np.concatenate([cos_t, sin_t], axis=-1)
    tab_q = tab_k * HEAD_DIM ** -0.5
    return jnp.asarray(tab_q.astype(np.float32)), jnp.asarray(tab_k.astype(np.float32))


def _rope_swap_matrix():
    quarter = HEAD_DIM // 4
    d = np.arange(HEAD_DIM)
    src = np.where(d % (2 * quarter) < quarter, d + quarter, d - quarter)
    return jnp.asarray((np.arange(HEAD_DIM)[:, None] == src[None, :]).astype(BF16))


def _window_bias():
    blk = ATTN_BLOCK
    r = np.arange(blk)[:, None]
    s = np.arange(3 * blk)[None, :]
    band = np.abs(s - blk - r) <= WINDOW
    out = []
    for has_prev in (False, True):
        for has_next in (False, True):
            ok = band & ((s >= blk) | has_prev) & ((s < 2 * blk) | has_next)
            out.append(np.where(ok, 0.0, -np.inf).astype(np.float32))
    return jnp.asarray(np.stack(out, axis=0))


def _sgu_kernel(u_ref, v_ref, z_ref, g_ref, b_ref, ws_ref, bs_ref, o_ref, *, n_chunks):
    for n in range(n_chunks):
        rows = slice(n * SGU_CHUNK, (n + 1) * SGU_CHUNK)
        v = _layernorm_f32(jax.nn.gelu(v_ref[rows, :].astype(F32))) * g_ref[...] + b_ref[...]
        v = v.astype(BF16)
        for g in range(N_SGU_GROUPS):
            cols = slice(g * SGU_GROUP_CH, (g + 1) * SGU_GROUP_CH)
            vm = jnp.dot(ws_ref[g], v[:, cols], preferred_element_type=F32) + bs_ref[:, g:g + 1]
            u = jax.nn.gelu(u_ref[rows, cols].astype(F32))
            z = z_ref[rows, cols].astype(F32)
            o_ref[rows, cols] = (u * vm * jax.nn.silu(z)).astype(BF16)


def _sgu(proj, ln_g, ln_b, w_s_bf16, b_s_t):
    t = proj.shape[0]
    tm = ROW_TM
    return pl.pallas_call(
        functools.partial(_sgu_kernel, n_chunks=tm // SGU_CHUNK),
        out_shape=jax.ShapeDtypeStruct((t, D_SGU), BF16),
        grid=(t // tm,),
        in_specs=[
            pl.BlockSpec((tm, D_SGU), lambda i: (i, COL_U // D_SGU)),
            pl.BlockSpec((tm, D_SGU), lambda i: (i, COL_VS // D_SGU)),
            pl.BlockSpec((tm, D_SGU), lambda i: (i, COL_ZC // D_SGU)),
            pl.BlockSpec((1, D_SGU), lambda i: (0, 0)),
            pl.BlockSpec((1, D_SGU), lambda i: (0, 0)),
            pl.BlockSpec((N_SGU_GROUPS, SGU_CHUNK, SGU_CHUNK), lambda i: (0, 0, 0)),
            pl.BlockSpec((SGU_CHUNK, N_SGU_GROUPS), lambda i: (0, 0)),
        ],
        out_specs=pl.BlockSpec((tm, D_SGU), lambda i: (i, 0)),
        compiler_params=_params(("arbitrary",)),
        name="sgu",
    )(proj, proj, proj, ln_g.reshape(1, D_SGU), ln_b.reshape(1, D_SGU), w_s_bf16, b_s_t)


def _merge_kernel(mg_ref, ya_ref, yb_ref, yc_ref, x_ref, gate_ref, wpa_ref, wpb_ref, wpc_ref, wout_ref,
                  lng_ref, lnb_ref, y_ref, *, alpha):
    d = D_MODEL
    tm = y_ref.shape[0]
    for r0 in range(0, tm, MERGE_SUB):
        rows = slice(r0, r0 + MERGE_SUB)

        def branch(k, y_r, w_r):
            gate = jax.nn.sigmoid(mg_ref[rows, k * d:(k + 1) * d].astype(F32))
            return gate * jnp.dot(y_r[rows, :], w_r[...], preferred_element_type=F32)

        merged = branch(0, ya_ref, wpa_ref) + branch(1, yb_ref, wpb_ref) + branch(2, yc_ref, wpc_ref)
        out = jnp.dot(merged.astype(BF16), wout_ref[...], preferred_element_type=F32)
        z = alpha * x_ref[rows, :] + gate_ref[0] * out
        y_ref[rows, :] = _layernorm_f32(z) * lng_ref[...] + lnb_ref[...]


def _merge(proj, ya, yb, yc, x2, gate, wpa, wpb, wpc, wout, ln_g, ln_b, seq_len, alpha):
    t, d = x2.shape
    tm = MERGE_TM
    if gate.shape[0] == 1:
        mod_map = lambda i: (0, 0, 0)
    else:
        assert seq_len % tm == 0
        tiles_per_seq = seq_len // tm
        mod_map = lambda i: (i // tiles_per_seq, 0, 0)

    def const(shape):
        return pl.BlockSpec(shape, lambda i: (0,) * len(shape), pipeline_mode=pl.Buffered(1))

    return pl.pallas_call(
        functools.partial(_merge_kernel, alpha=alpha),
        out_shape=jax.ShapeDtypeStruct((t, d), F32),
        grid=(t // tm,),
        in_specs=[
            pl.BlockSpec((tm, D_MG), lambda i: (i, 0)),
            pl.BlockSpec((tm, D_SSM), lambda i: (i, 0)),
            pl.BlockSpec((tm, D_ATTN), lambda i: (i, 0)),
            pl.BlockSpec((tm, D_SGU), lambda i: (i, 0)),
            pl.BlockSpec((tm, d), lambda i: (i, 0)),
            pl.BlockSpec((1, 1, d), mod_map),
            const((D_SSM, d)), const((D_ATTN, d)), const((D_SGU, d)), const((d, d)),
            const((1, d)), const((1, d)),
        ],
        out_specs=pl.BlockSpec((tm, d), lambda i: (i, 0)),
        compiler_params=_params(("arbitrary",)),
        name="merge_out",
    )(proj, ya, yb, yc, x2, gate, wpa, wpb, wpc, wout, ln_g.reshape(1, d), ln_b.reshape(1, d))


def _layer(x2, n_seq, seq_len, mod, lp, k_arr, v_arr, slot, ctx_k, ctx_v, s0, rope, alpha):
    d = D_MODEL
    nbm = mod.shape[0]
    shift = mod[:, 0:d].reshape(nbm, 1, d)
    scale = mod[:, d:2 * d].reshape(nbm, 1, d)
    gate = mod[:, 2 * d:3 * d].reshape(nbm, 1, d)
    proj, k_arr, v_arr, xa = _inproj(x2, scale, shift, lp['w_in'], lp['layer'], n_seq, seq_len, k_arr, v_arr, slot)

    n_seg = seq_len // SSM_SEG
    yssm, fin = _ssm_scan(xa, lp['fold'], lp['w_intra'], lp['w_state'], lp['w_enter'], lp['ssm_small'],
                          lp['layer'], s0, n_seq, n_seg)
    ya = _ssm_post(proj, yssm, lp['d_skip'], lp['w_glu'], lp['b_glu'])

    if ctx_k is None:
        yb = _ctx_attention(proj, k_arr, v_arr, slot, lp['sink'], seq_len)
    else:
        yb = _lat_attention(proj, k_arr, v_arr, slot, ctx_k, ctx_v, lp['sink'], *rope, n_seq, seq_len)

    yc = _sgu(proj, lp['sgu_g'], lp['sgu_b'], lp['w_s'], lp['b_s_t'])

    y = _merge(proj, ya, yb, yc, x2, gate, lp['w_pa'], lp['w_pb'], lp['w_pc'], lp['w_out'],
               lp['ln_g'], lp['ln_b'], seq_len, alpha)
    return y, k_arr, v_arr, fin


def _states_to_lanes(s):
    b = s.shape[0]
    return s.astype(F32).transpose(3, 0, 2, 1, 4).reshape(N_SSM_GROUPS, b, 4 * SSM_STATE)


def _lanes_to_states(f):
    b = f.shape[1]
    return f.reshape(N_SSM_GROUPS, b, 2, 2, SSM_STATE).transpose(1, 3, 2, 0, 4)


def kernel(x_prompt, x_sample, cache_k, cache_v, state_ssm, c, c_ctx, w_ada, b_ada, w_in, ssm_lam_re, ssm_lam_im, ssm_log_step, ssm_b_re, ssm_b_im, ssm_c_re, ssm_c_im, ssm_d, w_glu, b_glu, attn_sink, sgu_ln_g, sgu_ln_b, w_spatial, b_spatial, w_proj_a, w_proj_b, w_proj_c, w_out, ln_g, ln_b):
    depth = w_in.shape[0]
    batch, seq, d = x_prompt.shape
    dec_batch, dec_seq, _ = x_sample.shape
    past_len = cache_k.shape[2]
    alpha = (2 * depth) ** 0.25

    n_cond = 1 + dec_batch
    cond_rows = -(-n_cond // 8) * 8
    cond = jnp.concatenate([c_ctx[None, :], c, jnp.zeros((cond_rows - n_cond, d), F32)], axis=0)
    mod = _ada_mod(cond, w_ada, b_ada)

    w_in_bf16 = w_in.astype(BF16).reshape(depth, d, w_in.shape[2] // INPROJ_TN, INPROJ_TN).transpose(0, 2, 1, 3)
    fold = _ssm_fold_matrix()
    tables = _ssm_tables(ssm_lam_re, ssm_lam_im, ssm_log_step, ssm_b_re, ssm_b_im, ssm_c_re, ssm_c_im)
    layers = []
    for l in range(depth):
        w_intra, w_state, w_enter, small = tables
        layers.append({
            'w_in': w_in_bf16, 'layer': l,
            'fold': fold, 'w_intra': w_intra, 'w_state': w_state, 'w_enter': w_enter, 'ssm_small': small,
            'd_skip': ssm_d[l], 'w_glu': w_glu[l].astype(BF16), 'b_glu': b_glu[l],
            'sink': attn_sink[l].astype(F32),
            'sgu_g': sgu_ln_g[l], 'sgu_b': sgu_ln_b[l],
            'w_s': w_spatial[l].astype(BF16), 'b_s_t': b_spatial[l].T.astype(F32),
            'w_pa': w_proj_a[l].astype(BF16), 'w_pb': w_proj_b[l].astype(BF16),
            'w_pc': w_proj_c[l].astype(BF16), 'w_out': w_out[l].astype(BF16),
            'ln_g': ln_g[l], 'ln_b': ln_b[l],
        })

    h = x_prompt.reshape(batch * seq, d)
    zero_state = jnp.zeros((N_SSM_GROUPS, batch, 4 * SSM_STATE), F32)
    k_new = jnp.zeros((batch, depth, seq, D_KV), F32)
    v_new = jnp.zeros((batch, depth, seq, D_KV), F32)
    ss = []
    for l in range(depth):
        h, k_new, v_new, fin = _layer(h, batch, seq, mod[l, 0:1], layers[l], k_new, v_new, l,
                                      None, None, zero_state, None, alpha)
        ss.append(_lanes_to_states(fin))
    y_prompt = h.reshape(batch, seq, d)
    new_cache_k = k_new.reshape(batch, depth, seq, N_KV_HEADS, HEAD_DIM)
    new_cache_v = v_new.reshape(batch, depth, seq, N_KV_HEADS, HEAD_DIM)
    new_state_ssm = jnp.stack(ss, axis=1)

    rope = _rope_tables(dec_seq) + (_window_bias(), _rope_swap_matrix())
    z = x_sample.reshape(dec_batch * dec_seq, d)
    k_lat = jnp.zeros((dec_batch, 1, dec_seq, D_KV), F32)
    v_lat = jnp.zeros((dec_batch, 1, dec_seq, D_KV), F32)
    for l in range(depth):
        ctx_k = cache_k[:, l].reshape(dec_batch, past_len, D_KV).astype(F32)
        ctx_v = cache_v[:, l].reshape(dec_batch, past_len, D_KV).astype(F32)
        z, k_lat, v_lat, _ = _layer(z, dec_batch, dec_seq, mod[l, 1:1 + dec_batch], layers[l], k_lat, v_lat, 0,
                                    ctx_k, ctx_v, _states_to_lanes(state_ssm[:, l]), rope, alpha)
    y_sample = z.reshape(dec_batch, dec_seq, d)
    return (y_prompt, y_sample, new_cache_k, new_cache_v, new_state_ssm)
```

```python
import functools

import jax
import jax.numpy as jnp
import numpy as np
from jax import lax
from jax.experimental import pallas as pl
from jax.experimental.pallas import tpu as pltpu

F32 = jnp.float32
BF16 = jnp.bfloat16

D_MODEL = 2048
GRID_W = 64
D_SSM = 512
SSM_GROUP = 16
N_SSM_GROUPS = D_SSM // SSM_GROUP
SSM_STATE = 64
HEAD_DIM = 128
N_HEADS = 8
N_KV_HEADS = 2
Q_PER_KV = N_HEADS // N_KV_HEADS
D_ATTN = N_HEADS * HEAD_DIM
D_KV = N_KV_HEADS * HEAD_DIM
WINDOW = 128
ATTN_BLOCK = 128
ROPE_BASE = 10000.0
D_SGU = 512
SGU_CHUNK = 128
SGU_GROUP_CH = 128
N_SGU_GROUPS = D_SGU // SGU_GROUP_CH
N_BRANCH = 3
LN_EPS = 1e-5

D_MG = N_BRANCH * D_MODEL
COL_MG = 0
COL_XA = COL_MG + D_MG
COL_ZA = COL_XA + D_SSM
COL_Q = COL_ZA + D_SSM
COL_K = COL_Q + D_ATTN
COL_V = COL_K + D_KV
COL_ZB = COL_V + D_KV
COL_U = COL_ZB + D_ATTN
COL_VS = COL_U + D_SGU
COL_ZC = COL_VS + D_SGU
D_IN = COL_ZC + D_SGU
W_IN_ROTATE = D_IN - D_MG

SSM_T = 16
SSM_NC = 16
SSM_TC = SSM_T * SSM_GROUP
SSM_SEG = SSM_T * SSM_NC
SSM_SMALL_ROWS = 40
SSM_LANE_GROUPS = 128 // SSM_GROUP
SSM_POW_ROWS = 24

VMEM_LIMIT_BYTES = 56 * 1024 * 1024

INPROJ_TM = 1024
INPROJ_TN = 1024
INPROJ_SUB = 256
ROW_TM = 512
MERGE_TM = 512
MERGE_SUB = 256


def _params(sem):
    return pltpu.CompilerParams(dimension_semantics=sem, vmem_limit_bytes=VMEM_LIMIT_BYTES)


def _layernorm_f32(x):
    mu = jnp.mean(x, axis=-1, keepdims=True)
    xc = x - mu
    var = jnp.mean(xc * xc, axis=-1, keepdims=True)
    return xc * lax.rsqrt(var + LN_EPS)


def _ada_kernel(cond_ref, w_ref, b_ref, o_ref):
    c = cond_ref[...]
    a = (c * jax.nn.sigmoid(c)).astype(BF16)
    o_ref[...] = jnp.dot(a, w_ref[...].astype(BF16), preferred_element_type=F32) + b_ref[...]


def _ada_mod(cond, w_ada, b_ada):
    depth, d, n = w_ada.shape
    r = cond.shape[0]
    tn = 512
    return pl.pallas_call(
        _ada_kernel,
        out_shape=jax.ShapeDtypeStruct((depth, r, n), F32),
        grid=(depth, n // tn),
        in_specs=[
            pl.BlockSpec((r, d), lambda l, j: (0, 0)),
            pl.BlockSpec((None, d, tn), lambda l, j: (l, 0, j)),
            pl.BlockSpec((None, 1, tn), lambda l, j: (l, 0, j)),
        ],
        out_specs=pl.BlockSpec((None, r, tn), lambda l, j: (l, 0, j)),
        compiler_params=_params(("arbitrary", "arbitrary")),
        name="ada_mod",
    )(cond, w_ada, b_ada.reshape(depth, 1, n))


def _inproj_kernel(x_ref, scale_ref, shift_ref, w_ref, k_in_ref, v_in_ref, proj_ref, k_ref, v_ref, xa_ref, h_scr,
                   *, kv_tile, xa_tile):
    del k_in_ref, v_in_ref
    j = pl.program_id(1)
    tm = x_ref.shape[0]

    assert kv_tile != 0 and xa_tile != 0

    @pl.when(j == 0)
    def _():
        for r0 in range(0, tm, INPROJ_SUB):
            rows = slice(r0, r0 + INPROJ_SUB)
            h = (_layernorm_f32(x_ref[rows, :]) * (1.0 + scale_ref[0]) + shift_ref[0]).astype(BF16)
            h_scr[rows, :] = h
            proj_ref[rows, :] = jnp.dot(h, w_ref[...], preferred_element_type=F32).astype(BF16)

    @pl.when(j != 0)
    def _():
        acc = jnp.dot(h_scr[...], w_ref[...], preferred_element_type=F32)
        proj_ref[...] = acc.astype(BF16)

        @pl.when(j == kv_tile)
        def _():
            k_ref[...] = acc[:, 0:D_KV].reshape(k_ref.shape)
            v_ref[...] = acc[:, D_KV:2 * D_KV].reshape(v_ref.shape)

        @pl.when(j == xa_tile)
        def _():
            xa_ref[...] = acc[:, 0:D_SSM]


def _inproj(x2, scale, shift, w_bf16, layer, n_seq, seq_len, k_arr, v_arr, slot):
    t, d = x2.shape
    n = w_bf16.shape[2]
    tm, tn = INPROJ_TM, INPROJ_TN
    assert t % tm == 0 and n % tn == 0 and COL_K % tn == 0 and COL_XA % tn == 0
    assert 2 * D_KV <= tn and D_SSM <= tn and W_IN_ROTATE % tn == 0
    n_tiles = n // tn
    rot_tiles = W_IN_ROTATE // tn
    if seq_len >= tm:
        assert seq_len % tm == 0
        tiles_per_seq = seq_len // tm
        seq_of = lambda i: i // tiles_per_seq
        kv_spec = pl.BlockSpec((None, None, tm, D_KV), lambda i, j: (i // tiles_per_seq, slot, i % tiles_per_seq, 0))
    else:
        assert tm % seq_len == 0
        seq_of = lambda i: i * (tm // seq_len)
        kv_spec = pl.BlockSpec((tm // seq_len, None, seq_len, D_KV), lambda i, j: (i, slot, 0, 0))
    if scale.shape[0] == 1:
        mod_map = lambda i, j: (0, 0, 0)
    else:
        assert seq_len >= tm
        mod_map = lambda i, j: (seq_of(i), 0, 0)
    any_spec = pl.BlockSpec(memory_space=pl.ANY)
    return pl.pallas_call(
        functools.partial(_inproj_kernel, kv_tile=COL_K // tn, xa_tile=COL_XA // tn),
        out_shape=(jax.ShapeDtypeStruct((t, n), BF16), jax.ShapeDtypeStruct(k_arr.shape, F32),
                   jax.ShapeDtypeStruct(v_arr.shape, F32), jax.ShapeDtypeStruct((t, D_SSM), F32)),
        grid=(t // tm, n // tn),
        in_specs=[
            pl.BlockSpec((tm, d), lambda i, j: (i, 0)),
            pl.BlockSpec((1, 1, d), mod_map),
            pl.BlockSpec((1, 1, d), mod_map),
            pl.BlockSpec((None, d, tn), lambda i, j: (layer, 0, (j + rot_tiles) % n_tiles)),
            any_spec, any_spec,
        ],
        out_specs=(
            pl.BlockSpec((tm, tn), lambda i, j: (i, j)),
            kv_spec, kv_spec,
            pl.BlockSpec((tm, D_SSM), lambda i, j: (i, 0)),
        ),
        input_output_aliases={4: 1, 5: 2},
        scratch_shapes=[pltpu.VMEM((tm, d), BF16)],
        compiler_params=_params(("arbitrary", "arbitrary")),
        name="ln_inproj",
    )(x2, scale, shift, w_bf16, k_arr, v_arr)


def _zoh(lr, li, ls):
    dt = jnp.exp(ls)
    mag = jnp.exp(lr * dt)
    ar = mag * jnp.cos(li * dt)
    ai = mag * jnp.sin(li * dt)
    den = lr * lr + li * li
    nr = ar - 1.0
    return ar, ai, (nr * lr + ai * li) / den, (ai * lr - nr * li) / den


def _cmul(xr, xi, yr, yi):
    return xr * yr - xi * yi, xr * yi + xi * yr


def _cpow(br, bi, expo, n_bits):
    shape = jnp.broadcast_shapes(br.shape, expo.shape)
    pr = jnp.ones(shape, F32)
    pi = jnp.zeros(shape, F32)
    for bit in range(n_bits):
        sel = ((expo >> bit) & 1) == 1
        nr, ni = _cmul(pr, pi, br, bi)
        pr = jnp.where(sel, nr, pr)
        pi = jnp.where(sel, ni, pi)
        if bit + 1 < n_bits:
            br, bi = _cmul(br, bi, br, bi)
    return pr, pi


def _ssm_tables_kernel(laml_ref, bt_ref, ct_ref, ec_ref, ek_ref, wi_ref, ws_ref, we_ref, small_ref):
    p_n, t_n, c_n = SSM_STATE, SSM_T, SSM_GROUP
    p2 = 2 * p_n
    tc = t_n * c_n
    n_bits = 5
    assert max(t_n, SSM_NC) < 2 ** n_bits and SSM_POW_ROWS <= 2 ** n_bits

    def dot(a, b):
        return jnp.dot(a, b, precision=lax.Precision.HIGHEST, preferred_element_type=F32)

    def spread(x, onehot):
        hi = x.astype(BF16)
        rest = x - hi.astype(F32)
        mid = rest.astype(BF16)
        lo = (rest - mid.astype(F32)).astype(BF16)
        return (jnp.dot(hi, onehot, preferred_element_type=F32) + jnp.dot(mid, onehot, preferred_element_type=F32)
                + jnp.dot(lo, onehot, preferred_element_type=F32))

    ar, ai, f_re, f_im = _zoh(laml_ref[0:1, :], laml_ref[1:2, :], laml_ref[2:3, :])
    is_fwd = lax.broadcasted_iota(jnp.int32, (1, p2), 1) < p_n
    bbt_re, bbt_im = _cmul(f_re, f_im, bt_ref[:, 0:p2], bt_ref[:, p2:2 * p2])
    krow = lax.broadcasted_iota(jnp.int32, (SSM_POW_ROWS, 1), 0)
    n_lane = ct_ref.shape[1]
    kclamp = jnp.minimum(lax.broadcasted_iota(jnp.int32, (n_lane, 1), 0), t_n)
    pw_re, pw_im = _cpow(ar, ai, kclamp, n_bits)
    for j in range(t_n):
        pm_re = jnp.where(is_fwd, pw_re[t_n - 1 - j:t_n - j], pw_re[j:j + 1])
        pm_im = jnp.where(is_fwd, pw_im[t_n - 1 - j:t_n - j], pw_im[j:j + 1])
        s_re, s_im = _cmul(bbt_re, bbt_im, pm_re, pm_im)
        ws_ref[j * c_n:(j + 1) * c_n, 0:p2] = s_re.astype(BF16)
        ws_ref[j * c_n:(j + 1) * c_n, p2:2 * p2] = s_im.astype(BF16)
    a1r = pw_re[t_n:t_n + 1]
    a1i = pw_im[t_n:t_n + 1]
    q_re, q_im = _cpow(a1r, a1i, jnp.where(is_fwd, krow, jnp.maximum(SSM_NC - 1 - krow, 0)), n_bits)
    as_re, as_im = _cpow(a1r, a1i, jnp.full((1, 1), SSM_NC, jnp.int32), n_bits)
    small_ref[0:SSM_NC, :] = q_re[0:SSM_NC]
    small_ref[SSM_NC:2 * SSM_NC, :] = q_im[0:SSM_NC]
    small_ref[2 * SSM_NC:SSM_SMALL_ROWS, :] = jnp.concatenate(
        [a1r, a1i, as_re, as_im, jnp.zeros((SSM_SMALL_ROWS - 2 * SSM_NC - 4, p2), F32)], axis=0)

    pt_re_all = pw_re.T
    pt_im_all = pw_im.T
    cpt = []
    for d in range(2):
        pt_re = pt_re_all[d * p_n:(d + 1) * p_n, :]
        pt_im = pt_im_all[d * p_n:(d + 1) * p_n, :]
        c_re = spread(ct_ref[...], ec_ref[d])
        c_im = spread(ct_ref[...], ec_ref[2 + d])
        cpt.append(_cmul(c_re, c_im, spread(pt_re, ek_ref[d]), spread(pt_im, ek_ref[d])))
        e_re, e_im = _cmul(c_re, c_im, spread(pt_re, ek_ref[2 + d]), spread(pt_im, ek_ref[2 + d]))
        we_ref[d * p_n:(d + 1) * p_n, :] = e_re.astype(BF16)
        we_ref[p2 + d * p_n:p2 + (d + 1) * p_n, :] = (-e_im).astype(BF16)
    cpt_re = jnp.concatenate([cpt[0][0], cpt[1][0]], axis=0)
    cpt_im = jnp.concatenate([cpt[0][1], cpt[1][1]], axis=0)
    taps = []
    for sel in (is_fwd, jnp.logical_not(is_fwd)):
        taps.append(dot(jnp.where(sel, bbt_re, 0.0), cpt_re) - dot(jnp.where(sel, bbt_im, 0.0), cpt_im))
    lane = lax.broadcasted_iota(jnp.int32, (c_n, tc), 1)
    for j in range(t_n):
        fwd = jnp.where(lane >= j * c_n, pltpu.roll(taps[0], j * c_n, 1), 0.0)
        bwd = jnp.where(lane < (j + 1) * c_n, pltpu.roll(taps[1], (tc - (t_n - 1 - j) * c_n) % tc, 1), 0.0)
        wi_ref[j * c_n:(j + 1) * c_n, :] = (fwd + bwd).astype(BF16)


def _ssm_table_constants():
    t_n, c_n = SSM_T, SSM_GROUP
    rows = np.arange(128)[:, None]
    tok = (np.arange(t_n * c_n) // c_n)[None, :]
    ch = (np.arange(t_n * c_n) % c_n)[None, :]
    ec = np.stack([(rows == q * c_n + ch) for q in range(4)], axis=0)
    ek = np.stack([rows == tok, rows == t_n - 1 - tok, rows == tok + 1, rows == t_n - tok], axis=0)
    return jnp.asarray(ec.astype(BF16)), jnp.asarray(ek.astype(BF16))


def _ssm_tables(lam_re, lam_im, log_step, b_re, b_im, c_re, c_im):
    l_n, _, g_n, p_n = lam_re.shape
    c_n, t_n = SSM_GROUP, SSM_T
    tc = t_n * c_n
    p2, p4 = 2 * p_n, 4 * p_n
    ls = jnp.broadcast_to(log_step.astype(F32)[..., None], lam_re.shape)

    def lanes(x):
        return x.astype(F32).transpose(0, 2, 1, 3).reshape(l_n, g_n, 1, p2)

    laml = jnp.concatenate([lanes(lam_re), lanes(lam_im), lanes(ls), jnp.zeros((l_n, g_n, 5, p2), F32)], axis=2)

    def b_lanes(x):
        return x.astype(F32).transpose(0, 2, 4, 1, 3).reshape(l_n, g_n, c_n, p2)

    bt = jnp.concatenate([b_lanes(b_re), b_lanes(b_im)], axis=-1)

    def c_lanes(x):
        return x.astype(F32).transpose(0, 2, 4, 1, 3).reshape(l_n, g_n, p_n, 2 * c_n)

    ct = jnp.concatenate([c_lanes(c_re), c_lanes(c_im), jnp.zeros((l_n, g_n, p_n, 128 - 4 * c_n), F32)], axis=-1)
    ec, ek = _ssm_table_constants()

    def per_group(*dims):
        return pl.BlockSpec((None, None) + dims, lambda l, g: (l, g) + (0,) * len(dims))

    def const(shape):
        return pl.BlockSpec(shape, lambda l, g: (0,) * len(shape))

    return pl.pallas_call(
        _ssm_tables_kernel,
        out_shape=(jax.ShapeDtypeStruct((l_n, g_n, tc, tc), BF16), jax.ShapeDtypeStruct((l_n, g_n, tc, p4), BF16),
                   jax.ShapeDtypeStruct((l_n, g_n, p4, tc), BF16),
                   jax.ShapeDtypeStruct((l_n, g_n, SSM_SMALL_ROWS, p2), F32)),
        grid=(l_n, g_n),
        in_specs=[per_group(8, p2), per_group(c_n, p4), per_group(p_n, 128),
                  const(ec.shape), const(ek.shape)],
        out_specs=(per_group(tc, tc), per_group(tc, p4), per_group(p4, tc), per_group(SSM_SMALL_ROWS, p2)),
        compiler_params=_params(("arbitrary", "arbitrary")),
        name="ssm_tables",
    )(laml, bt, ct, ec, ek)


def _ssm_fold_matrix():
    half_t = SSM_T // 2
    n = half_t * SSM_LANE_GROUPS * SSM_GROUP
    src = np.arange(n)
    j = src // (SSM_LANE_GROUPS * SSM_GROUP)
    g = (src // SSM_GROUP) % SSM_LANE_GROUPS
    c = src % SSM_GROUP
    dst = g * (half_t * SSM_GROUP) + j * SSM_GROUP + c
    return jnp.asarray((dst[:, None] == np.arange(n)[None, :]).astype(BF16))


def _ssm_kernel(xa_ref, fold_ref, wi_ref, ws_ref, we_ref, small_ref, s0_ref, y_ref, fin_ref,
                v_scr, s_scr, yy_scr, fre_scr, fim_scr, efre_scr, efim_scr, ebre_scr, ebim_scr,
                *, n_seq, n_seg):
    p2 = 2 * SSM_STATE
    p4 = 4 * SSM_STATE
    nbp = n_seq * n_seg
    lanes = SSM_LANE_GROUPS * SSM_GROUP
    nt = (((1,), (1,)), ((), ()))

    blocks = []
    for c in range(SSM_NC):
        toks = [xa_ref[pl.ds(c * SSM_T + j, nbp, stride=SSM_SEG), :].astype(BF16) for j in range(SSM_T)]
        blocks.append(jnp.concatenate(toks, axis=1))
    xx = jnp.concatenate(blocks, axis=0)
    hw = fold_ref.shape[0]
    uu = [jnp.dot(xx[:, h * hw:(h + 1) * hw], fold_ref[...], preferred_element_type=F32).astype(BF16)
          for h in range(2)]

    lane = lax.broadcasted_iota(jnp.int32, (1, p2), 1)
    is_fwd = lane < SSM_STATE
    for g in range(SSM_LANE_GROUPS):
        gc = slice(g * SSM_TC, (g + 1) * SSM_TC)
        u = jnp.concatenate([uu[h][:, g * lanes:(g + 1) * lanes] for h in range(2)], axis=1)
        yy_scr[:, gc] = jnp.dot(u, wi_ref[g], preferred_element_type=F32)
        v_scr[:, gc] = jnp.dot(u, ws_ref[g], preferred_element_type=F32)

    for g in range(SSM_LANE_GROUPS):
        o = g * p4
        a1r = small_ref[g, 2 * SSM_NC:2 * SSM_NC + 1, :]
        a1i = small_ref[g, 2 * SSM_NC + 1:2 * SSM_NC + 2, :]
        asr = small_ref[g, 2 * SSM_NC + 2:2 * SSM_NC + 3, :]
        asi = small_ref[g, 2 * SSM_NC + 3:2 * SSM_NC + 4, :]

        st_re = jnp.zeros((nbp, p2), F32)
        st_im = jnp.zeros((nbp, p2), F32)
        for c in range(SSM_NC):
            cb = SSM_NC - 1 - c
            rf = slice(c * nbp, (c + 1) * nbp)
            rb = slice(cb * nbp, (cb + 1) * nbp)
            s_scr[rf, o:o + SSM_STATE] = st_re[:, 0:SSM_STATE]
            s_scr[rb, o + SSM_STATE:o + p2] = st_re[:, SSM_STATE:p2]
            s_scr[rf, o + p2:o + p2 + SSM_STATE] = st_im[:, 0:SSM_STATE]
            s_scr[rb, o + p2 + SSM_STATE:o + p4] = st_im[:, SSM_STATE:p2]
            v_re = jnp.where(is_fwd, v_scr[rf, o:o + p2], v_scr[rb, o:o + p2])
            v_im = jnp.where(is_fwd, v_scr[rf, o + p2:o + p4], v_scr[rb, o + p2:o + p4])
            st_re, st_im = (a1r * st_re - a1i * st_im + v_re, a1r * st_im + a1i * st_re + v_im)

        fre_scr[...] = st_re
        fim_scr[...] = st_im
        s0 = s0_ref[g]
        e_re = s0[:, 0:p2]
        e_im = s0[:, p2:p4]
        for k in range(n_seg):
            kb = n_seg - 1 - k
            rows_f = pl.ds(k, n_seq, stride=n_seg)
            rows_b = pl.ds(kb, n_seq, stride=n_seg)
            efre_scr[rows_f, :] = e_re
            efim_scr[rows_f, :] = e_im
            ebre_scr[rows_b, :] = e_re
            ebim_scr[rows_b, :] = e_im
            f_re = jnp.where(is_fwd, fre_scr[rows_f, :], fre_scr[rows_b, :])
            f_im = jnp.where(is_fwd, fim_scr[rows_f, :], fim_scr[rows_b, :])
            e_re, e_im = (asr * e_re - asi * e_im + f_re, asr * e_im + asi * e_re + f_im)
        fin_ref[g] = jnp.concatenate([e_re, e_im], axis=-1)

        en_re = jnp.where(is_fwd, efre_scr[...], ebre_scr[...])
        en_im = jnp.where(is_fwd, efim_scr[...], ebim_scr[...])
        for c in range(SSM_NC):
            rows = slice(c * nbp, (c + 1) * nbp)
            cr = small_ref[g, c:c + 1, :]
            ci = small_ref[g, SSM_NC + c:SSM_NC + c + 1, :]
            s_scr[rows, o:o + p2] = s_scr[rows, o:o + p2] + (cr * en_re - ci * en_im)
            s_scr[rows, o + p2:o + p4] = s_scr[rows, o + p2:o + p4] + (cr * en_im + ci * en_re)

    for g in range(SSM_LANE_GROUPS):
        gc = slice(g * SSM_TC, (g + 1) * SSM_TC)
        yy_scr[:, gc] += jnp.dot(s_scr[:, g * p4:(g + 1) * p4].astype(BF16), we_ref[g],
                                 preferred_element_type=F32)

    half_t = SSM_T // 2
    for h in range(2):
        yh = jnp.concatenate([yy_scr[:, g * SSM_TC + h * lanes:g * SSM_TC + (h + 1) * lanes]
                              for g in range(SSM_LANE_GROUPS)], axis=1).astype(BF16)
        zz = lax.dot_general(yh, fold_ref[...], nt, preferred_element_type=F32)
        for c in range(SSM_NC):
            for j in range(half_t):
                y_ref[pl.ds(c * SSM_T + h * half_t + j, nbp, stride=SSM_SEG), :] = zz[c * nbp:(c + 1) * nbp,
                                                                                       j * lanes:(j + 1) * lanes]


def _ssm_scan(xa, fold, w_intra, w_state, w_enter, small, layer, s0, n_seq, n_seg):
    t = xa.shape[0]
    g_n = N_SSM_GROUPS
    lg = SSM_LANE_GROUPS
    lanes = lg * SSM_GROUP
    p4 = 4 * SSM_STATE
    nbp = n_seq * n_seg
    r = SSM_NC * nbp
    assert t == r * SSM_T and (SSM_T // 2) * SSM_GROUP == lanes and fold.shape == (lg * lanes, lg * lanes)
    const = lambda shape: pl.BlockSpec(shape, lambda i: (0,) * len(shape), pipeline_mode=pl.Buffered(1))
    return pl.pallas_call(
        functools.partial(_ssm_kernel, n_seq=n_seq, n_seg=n_seg),
        out_shape=(jax.ShapeDtypeStruct((t, D_SSM), F32), jax.ShapeDtypeStruct((g_n, n_seq, p4), F32)),
        grid=(g_n // lg,),
        in_specs=[
            pl.BlockSpec((t, lanes), lambda i: (0, i)),
            const(fold.shape),
            pl.BlockSpec((None, lg, SSM_TC, SSM_TC), lambda i: (layer, i, 0, 0)),
            pl.BlockSpec((None, lg, SSM_TC, p4), lambda i: (layer, i, 0, 0)),
            pl.BlockSpec((None, lg, p4, SSM_TC), lambda i: (layer, i, 0, 0)),
            pl.BlockSpec((None, lg, SSM_SMALL_ROWS, 2 * SSM_STATE), lambda i: (layer, i, 0, 0)),
            pl.BlockSpec((lg, n_seq, p4), lambda i: (i, 0, 0)),
        ],
        out_specs=(
            pl.BlockSpec((t, lanes), lambda i: (0, i)),
            pl.BlockSpec((lg, n_seq, p4), lambda i: (i, 0, 0)),
        ),
        scratch_shapes=[pltpu.VMEM((r, lg * p4), F32), pltpu.VMEM((r, lg * p4), F32),
                        pltpu.VMEM((r, lg * SSM_TC), F32)]
        + [pltpu.VMEM((nbp, 2 * SSM_STATE), F32)] * 6,
        compiler_params=_params(("arbitrary",)),
        name="ssm_scan",
    )(xa, fold, w_intra, w_state, w_enter, small, s0)


def _ssm_post_kernel(xa_ref, za_ref, ys_ref, d_ref, w_ref, b_ref, o_ref):
    y = xa_ref[...].astype(F32) * d_ref[...] + ys_ref[...]
    y = jax.nn.gelu(y)
    gl = jnp.dot(y.astype(BF16), w_ref[...], preferred_element_type=F32) + b_ref[...]
    y = y * jax.nn.sigmoid(gl)
    o_ref[...] = (y * jax.nn.silu(za_ref[...].astype(F32))).astype(BF16)


def _ssm_post(proj, yssm, d_skip, w_glu_bf16, b_glu):
    t = proj.shape[0]
    tm = ROW_TM
    return pl.pallas_call(
        _ssm_post_kernel,
        out_shape=jax.ShapeDtypeStruct((t, D_SSM), BF16),
        grid=(t // tm,),
        in_specs=[
            pl.BlockSpec((tm, D_SSM), lambda i: (i, COL_XA // D_SSM)),
            pl.BlockSpec((tm, D_SSM), lambda i: (i, COL_ZA // D_SSM)),
            pl.BlockSpec((tm, D_SSM), lambda i: (i, 0)),
            pl.BlockSpec((1, D_SSM), lambda i: (0, 0)),
            pl.BlockSpec((D_SSM, D_SSM), lambda i: (0, 0)),
            pl.BlockSpec((1, D_SSM), lambda i: (0, 0)),
        ],
        out_specs=pl.BlockSpec((tm, D_SSM), lambda i: (i, 0)),
        compiler_params=_params(("arbitrary",)),
        name="ssm_post",
    )(proj, proj, yssm, d_skip.reshape(1, D_SSM), w_glu_bf16, b_glu.reshape(1, D_SSM))


def _softmax_pv(scores, sink, values):
    m = sink
    for s in scores:
        m = jnp.maximum(m, jnp.max(s, axis=-1, keepdims=True))
    den = jnp.exp(sink - m)
    o = None
    for s, v in zip(scores, values):
        p = jnp.exp(s - m)
        den = den + jnp.sum(p, axis=-1, keepdims=True)
        pv = jnp.dot(p.astype(BF16), v, preferred_element_type=F32)
        o = pv if o is None else o + pv
    return o / den


def _softmax_pv_wide(s, sink, v_bf16):
    hd = v_bf16.shape[1]
    m = jnp.maximum(jnp.max(s, axis=-1, keepdims=True), sink)
    p = jnp.exp(s - m).astype(BF16)
    v_ones = jnp.concatenate([v_bf16, jnp.ones_like(v_bf16)], axis=1)
    o = jnp.dot(p, v_ones, preferred_element_type=F32)
    den = o[:, hd:] + jnp.exp(sink - m)
    return o[:, :hd] / den


def _head_gate(zb_refs, h):
    per = N_HEADS // 2
    z = zb_refs[h // per][:, (h % per) * HEAD_DIM:(h % per + 1) * HEAD_DIM].astype(F32)
    return jax.nn.silu(z)


def _ctx_attn_kernel(sink_ref, q_ref, k_ref, v_ref, zb0_ref, zb1_ref, o_ref):
    scale = HEAD_DIM ** -0.5
    nt = (((1,), (1,)), ((), ()))
    for kvh in range(N_KV_HEADS):
        k = (k_ref[:, kvh * HEAD_DIM:(kvh + 1) * HEAD_DIM] * scale).astype(BF16)
        v = v_ref[:, kvh * HEAD_DIM:(kvh + 1) * HEAD_DIM].astype(BF16)
        for g in range(Q_PER_KV):
            h = kvh * Q_PER_KV + g
            hs = slice(h * HEAD_DIM, (h + 1) * HEAD_DIM)
            s = lax.dot_general(q_ref[:, hs], k, nt, preferred_element_type=F32)
            o = _softmax_pv([s], sink_ref[h], [v])
            o_ref[:, hs] = (o * _head_gate((zb0_ref, zb1_ref), h)).astype(BF16)


def _ctx_attention(proj, k_arr, v_arr, slot, sink, seq_len):
    t = proj.shape[0]
    kv_spec = pl.BlockSpec((None, None, seq_len, D_KV), lambda b: (b, slot, 0, 0))
    return pl.pallas_call(
        _ctx_attn_kernel,
        out_shape=jax.ShapeDtypeStruct((t, D_ATTN), BF16),
        grid=(t // seq_len,),
        in_specs=[
            pl.BlockSpec(memory_space=pltpu.SMEM),
            pl.BlockSpec((seq_len, D_ATTN), lambda b: (b, COL_Q // D_ATTN)),
            kv_spec, kv_spec,
            pl.BlockSpec((seq_len, D_ATTN // 2), lambda b: (b, COL_ZB // (D_ATTN // 2))),
            pl.BlockSpec((seq_len, D_ATTN // 2), lambda b: (b, COL_ZB // (D_ATTN // 2) + 1)),
        ],
        out_specs=pl.BlockSpec((seq_len, D_ATTN), lambda b: (b, 0)),
        compiler_params=_params(("arbitrary",)),
        name="ctx_attention",
    )(sink, proj, k_arr, v_arr, proj, proj)


def _rope_partner(x):
    lane = lax.broadcasted_iota(jnp.int32, x.shape, 1)
    quarter = HEAD_DIM // 4
    first = (lane % (2 * quarter)) < quarter
    return jnp.where(first, pltpu.roll(x, HEAD_DIM - quarter, 1), pltpu.roll(x, quarter, 1))


def _rope(x, tab_ref):
    return x * tab_ref[:, 0:HEAD_DIM] + _rope_partner(x) * tab_ref[:, HEAD_DIM:2 * HEAD_DIM]


def _rope_kv_kernel(k_ref, v_ref, tab_ref, o_ref):
    for kvh in range(N_KV_HEADS):
        ks = slice(kvh * HEAD_DIM, (kvh + 1) * HEAD_DIM)
        o_ref[:, ks] = _rope(k_ref[:, ks], tab_ref).astype(BF16)
    o_ref[:, D_KV:2 * D_KV] = v_ref[...].astype(BF16)


def _rope_kv(k_arr, v_arr, slot, tab_k):
    n_seq, _, seq_len, _ = k_arr.shape
    tm = ROW_TM
    assert seq_len % tm == 0
    tiles_per_seq = seq_len // tm
    kv_spec = pl.BlockSpec((None, None, tm, D_KV), lambda i: (i // tiles_per_seq, slot, i % tiles_per_seq, 0))
    return pl.pallas_call(
        _rope_kv_kernel,
        out_shape=jax.ShapeDtypeStruct((n_seq * seq_len, 2 * D_KV), BF16),
        grid=(n_seq * tiles_per_seq,),
        in_specs=[
            kv_spec, kv_spec,
            pl.BlockSpec((tm, 2 * HEAD_DIM), lambda i: (i % tiles_per_seq, 0)),
        ],
        out_specs=pl.BlockSpec((tm, 2 * D_KV), lambda i: (i, 0)),
        compiler_params=_params(("arbitrary",)),
        name="rope_kv",
    )(k_arr, v_arr, tab_k)


def _lat_attn_kernel(sink_ref, q_ref, kvp_ref, kvc_ref, kvn_ref, ck_ref, cv_ref, zb0_ref, zb1_ref,
                     tabq_ref, swap_ref, bias_ref, o_ref):
    nt = (((1,), (1,)), ((), ()))
    blk = ATTN_BLOCK
    cos_q = tabq_ref[:, 0:HEAD_DIM]
    sin_q = tabq_ref[:, HEAD_DIM:2 * HEAD_DIM]
    for kvh in range(N_KV_HEADS):
        ks = slice(kvh * HEAD_DIM, (kvh + 1) * HEAD_DIM)
        vs = slice(D_KV + kvh * HEAD_DIM, D_KV + (kvh + 1) * HEAD_DIM)
        keys = [jnp.concatenate([kvp_ref[:, ks], kvc_ref[:, ks], kvn_ref[:, ks]], axis=0),
                ck_ref[:, ks].astype(BF16)]
        values = [jnp.concatenate([kvp_ref[:, vs], kvc_ref[:, vs], kvn_ref[:, vs]], axis=0),
                  cv_ref[:, ks].astype(BF16)]
        qx = jnp.concatenate(
            [q_ref[:, (kvh * Q_PER_KV + g) * HEAD_DIM:(kvh * Q_PER_KV + g + 1) * HEAD_DIM]
             for g in range(Q_PER_KV)], axis=0)
        partner = jnp.dot(qx, swap_ref[...], preferred_element_type=F32)
        qf = qx.astype(F32).reshape(Q_PER_KV, blk, HEAD_DIM)
        q4 = (qf * cos_q[None] + partner.reshape(Q_PER_KV, blk, HEAD_DIM) * sin_q[None])
        q4 = q4.reshape(Q_PER_KV * blk, HEAD_DIM).astype(BF16)
        s_loc = lax.dot_general(q4, keys[0], nt, preferred_element_type=F32)
        s_loc = (s_loc.reshape(Q_PER_KV, blk, 3 * blk) + bias_ref[...][None]).reshape(Q_PER_KV * blk, 3 * blk)
        s_ctx = lax.dot_general(q4, keys[1], nt, preferred_element_type=F32)
        sink_rows = jnp.concatenate(
            [jnp.full((blk, 1), sink_ref[kvh * Q_PER_KV + g], F32) for g in range(Q_PER_KV)], axis=0)
        o = _softmax_pv_wide(jnp.concatenate([s_loc, s_ctx], axis=-1), sink_rows,
                             jnp.concatenate(values, axis=0))
        for g in range(Q_PER_KV):
            h = kvh * Q_PER_KV + g
            o_ref[:, h * HEAD_DIM:(h + 1) * HEAD_DIM] = (
                o[g * blk:(g + 1) * blk] * _head_gate((zb0_ref, zb1_ref), h)).astype(BF16)


def _lat_attention(proj, k_arr, v_arr, slot, ctx_k, ctx_v, sink, tab_q, tab_k, bias, swap, n_seq, seq_len):
    kv = _rope_kv(k_arr, v_arr, slot, tab_k)
    t = proj.shape[0]
    blk = ATTN_BLOCK
    assert blk - 1 <= WINDOW <= blk
    nb = seq_len // blk
    lc = ctx_k.shape[1]

    def row(b, i):
        return b * nb + i

    def prev(i):
        return jnp.maximum(i - 1, 0)

    def nxt(i):
        return jnp.minimum(i + 1, nb - 1)

    def bias_variant(b, i):
        return ((i > 0).astype(jnp.int32) * 2 + (i < nb - 1).astype(jnp.int32), 0, 0)

    return pl.pallas_call(
        _lat_attn_kernel,
        out_shape=jax.ShapeDtypeStruct((t, D_ATTN), BF16),
        grid=(n_seq, nb),
        in_specs=[
            pl.BlockSpec(memory_space=pltpu.SMEM),
            pl.BlockSpec((blk, D_ATTN), lambda b, i: (row(b, i), COL_Q // D_ATTN)),
            pl.BlockSpec((blk, 2 * D_KV), lambda b, i: (row(b, prev(i)), 0)),
            pl.BlockSpec((blk, 2 * D_KV), lambda b, i: (row(b, i), 0)),
            pl.BlockSpec((blk, 2 * D_KV), lambda b, i: (row(b, nxt(i)), 0)),
            pl.BlockSpec((None, lc, D_KV), lambda b, i: (b, 0, 0)),
            pl.BlockSpec((None, lc, D_KV), lambda b, i: (b, 0, 0)),
            pl.BlockSpec((blk, D_ATTN // 2), lambda b, i: (row(b, i), COL_ZB // (D_ATTN // 2))),
            pl.BlockSpec((blk, D_ATTN // 2), lambda b, i: (row(b, i), COL_ZB // (D_ATTN // 2) + 1)),
            pl.BlockSpec((blk, 2 * HEAD_DIM), lambda b, i: (i, 0)),
            pl.BlockSpec((HEAD_DIM, HEAD_DIM), lambda b, i: (0, 0)),
            pl.BlockSpec((None, blk, 3 * blk), bias_variant),
        ],
        out_specs=pl.BlockSpec((blk, D_ATTN), lambda b, i: (row(b, i), 0)),
        compiler_params=_params(("arbitrary", "arbitrary")),
        name="lat_attention",
    )(sink, proj, kv, kv, kv, ctx_k, ctx_v, proj, proj, tab_q, swap, bias)


def _rope_tables(seq_len):
    rows = seq_len // GRID_W
    row = np.repeat(np.arange(rows), GRID_W).astype(np.float64)
    col = np.tile(np.arange(GRID_W), rows).astype(np.float64)
    half = HEAD_DIM // 2
    inv = ROPE_BASE ** (-np.arange(0, half, 2, dtype=np.float64) / half)
    ang_r = row[:, None] * inv[None, :]
    ang_c = col[:, None] * inv[None, :]
    cos_t = np.concatenate([np.cos(ang_r)] * 2 + [np.cos(ang_c)] * 2, axis=-1)
    sin_t = np.concatenate([-np.sin(ang_r), np.sin(ang_r), -np.sin(ang_c), np.sin(ang_c)], axis=-1)
    tab_k = np.concatenate([cos_t, sin_t], axis=-1)
    tab_q = tab_k * HEAD_DIM ** -0.5
    return jnp.asarray(tab_q.astype(np.float32)), jnp.asarray(tab_k.astype(np.float32))


def _rope_swap_matrix():
    quarter = HEAD_DIM // 4
    d = np.arange(HEAD_DIM)
    src = np.where(d % (2 * quarter) < quarter, d + quarter, d - quarter)
    return jnp.asarray((np.arange(HEAD_DIM)[:, None] == src[None, :]).astype(BF16))


def _window_bias():
    blk = ATTN_BLOCK
    r = np.arange(blk)[:, None]
    s = np.arange(3 * blk)[None, :]
    band = np.abs(s - blk - r) <= WINDOW
    out = []
    for has_prev in (False, True):
        for has_next in (False, True):
            ok = band & ((s >= blk) | has_prev) & ((s < 2 * blk) | has_next)
            out.append(np.where(ok, 0.0, -np.inf).astype(np.float32))
    return jnp.asarray(np.stack(out, axis=0))


def _sgu_kernel(u_ref, v_ref, z_ref, g_ref, b_ref, ws_ref, bs_ref, o_ref, *, n_chunks):
    for n in range(n_chunks):
        rows = slice(n * SGU_CHUNK, (n + 1) * SGU_CHUNK)
        v = _layernorm_f32(jax.nn.gelu(v_ref[rows, :].astype(F32))) * g_ref[...] + b_ref[...]
        v = v.astype(BF16)
        for g in range(N_SGU_GROUPS):
            cols = slice(g * SGU_GROUP_CH, (g + 1) * SGU_GROUP_CH)
            vm = jnp.dot(ws_ref[g], v[:, cols], preferred_element_type=F32) + bs_ref[:, g:g + 1]
            u = jax.nn.gelu(u_ref[rows, cols].astype(F32))
            z = z_ref[rows, cols].astype(F32)
            o_ref[rows, cols] = (u * vm * jax.nn.silu(z)).astype(BF16)


def _sgu(proj, ln_g, ln_b, w_s_bf16, b_s_t):
    t = proj.shape[0]
    tm = ROW_TM
    return pl.pallas_call(
        functools.partial(_sgu_kernel, n_chunks=tm // SGU_CHUNK),
        out_shape=jax.ShapeDtypeStruct((t, D_SGU), BF16),
        grid=(t // tm,),
        in_specs=[
            pl.BlockSpec((tm, D_SGU), lambda i: (i, COL_U // D_SGU)),
            pl.BlockSpec((tm, D_SGU), lambda i: (i, COL_VS // D_SGU)),
            pl.BlockSpec((tm, D_SGU), lambda i: (i, COL_ZC // D_SGU)),
            pl.BlockSpec((1, D_SGU), lambda i: (0, 0)),
            pl.BlockSpec((1, D_SGU), lambda i: (0, 0)),
            pl.BlockSpec((N_SGU_GROUPS, SGU_CHUNK, SGU_CHUNK), lambda i: (0, 0, 0)),
            pl.BlockSpec((SGU_CHUNK, N_SGU_GROUPS), lambda i: (0, 0)),
        ],
        out_specs=pl.BlockSpec((tm, D_SGU), lambda i: (i, 0)),
        compiler_params=_params(("arbitrary",)),
        name="sgu",
    )(proj, proj, proj, ln_g.reshape(1, D_SGU), ln_b.reshape(1, D_SGU), w_s_bf16, b_s_t)


def _merge_kernel(mg_ref, ya_ref, yb_ref, yc_ref, x_ref, gate_ref, wpa_ref, wpb_ref, wpc_ref, wout_ref,
                  lng_ref, lnb_ref, y_ref, *, alpha):
    d = D_MODEL
    tm = y_ref.shape[0]
    for r0 in range(0, tm, MERGE_SUB):
        rows = slice(r0, r0 + MERGE_SUB)

        def branch(k, y_r, w_r):
            gate = jax.nn.sigmoid(mg_ref[rows, k * d:(k + 1) * d].astype(F32))
            return gate * jnp.dot(y_r[rows, :], w_r[...], preferred_element_type=F32)

        merged = branch(0, ya_ref, wpa_ref) + branch(1, yb_ref, wpb_ref) + branch(2, yc_ref, wpc_ref)
        out = jnp.dot(merged.astype(BF16), wout_ref[...], preferred_element_type=F32)
        z = alpha * x_ref[rows, :] + gate_ref[0] * out
        y_ref[rows, :] = _layernorm_f32(z) * lng_ref[...] + lnb_ref[...]


def _merge(proj, ya, yb, yc, x2, gate, wpa, wpb, wpc, wout, ln_g, ln_b, seq_len, alpha):
    t, d = x2.shape
    tm = MERGE_TM
    if gate.shape[0] == 1:
        mod_map = lambda i: (0, 0, 0)
    else:
        assert seq_len % tm == 0
        tiles_per_seq = seq_len // tm
        mod_map = lambda i: (i // tiles_per_seq, 0, 0)

    def const(shape):
        return pl.BlockSpec(shape, lambda i: (0,) * len(shape), pipeline_mode=pl.Buffered(1))

    return pl.pallas_call(
        functools.partial(_merge_kernel, alpha=alpha),
        out_shape=jax.ShapeDtypeStruct((t, d), F32),
        grid=(t // tm,),
        in_specs=[
            pl.BlockSpec((tm, D_MG), lambda i: (i, 0)),
            pl.BlockSpec((tm, D_SSM), lambda i: (i, 0)),
            pl.BlockSpec((tm, D_ATTN), lambda i: (i, 0)),
            pl.BlockSpec((tm, D_SGU), lambda i: (i, 0)),
            pl.BlockSpec((tm, d), lambda i: (i, 0)),
            pl.BlockSpec((1, 1, d), mod_map),
            const((D_SSM, d)), const((D_ATTN, d)), const((D_SGU, d)), const((d, d)),
            const((1, d)), const((1, d)),
        ],
        out_specs=pl.BlockSpec((tm, d), lambda i: (i, 0)),
        compiler_params=_params(("arbitrary",)),
        name="merge_out",
    )(proj, ya, yb, yc, x2, gate, wpa, wpb, wpc, wout, ln_g.reshape(1, d), ln_b.reshape(1, d))


def _layer(x2, n_seq, seq_len, mod, lp, k_arr, v_arr, slot, ctx_k, ctx_v, s0, rope, alpha):
    d = D_MODEL
    nbm = mod.shape[0]
    shift = mod[:, 0:d].reshape(nbm, 1, d)
    scale = mod[:, d:2 * d].reshape(nbm, 1, d)
    gate = mod[:, 2 * d:3 * d].reshape(nbm, 1, d)
    proj, k_arr, v_arr, xa = _inproj(x2, scale, shift, lp['w_in'], lp['layer'], n_seq, seq_len, k_arr, v_arr, slot)

    n_seg = seq_len // SSM_SEG
    yssm, fin = _ssm_scan(xa, lp['fold'], lp['w_intra'], lp['w_state'], lp['w_enter'], lp['ssm_small'],
                          lp['layer'], s0, n_seq, n_seg)
    ya = _ssm_post(proj, yssm, lp['d_skip'], lp['w_glu'], lp['b_glu'])

    if ctx_k is None:
        yb = _ctx_attention(proj, k_arr, v_arr, slot, lp['sink'], seq_len)
    else:
        yb = _lat_attention(proj, k_arr, v_arr, slot, ctx_k, ctx_v, lp['sink'], *rope, n_seq, seq_len)

    yc = _sgu(proj, lp['sgu_g'], lp['sgu_b'], lp['w_s'], lp['b_s_t'])

    y = _merge(proj, ya, yb, yc, x2, gate, lp['w_pa'], lp['w_pb'], lp['w_pc'], lp['w_out'],
               lp['ln_g'], lp['ln_b'], seq_len, alpha)
    return y, k_arr, v_arr, fin


def _states_to_lanes(s):
    b = s.shape[0]
    return s.astype(F32).transpose(3, 0, 2, 1, 4).reshape(N_SSM_GROUPS, b, 4 * SSM_STATE)


def _lanes_to_states(f):
    b = f.shape[1]
    return f.reshape(N_SSM_GROUPS, b, 2, 2, SSM_STATE).transpose(1, 3, 2, 0, 4)


def kernel(x_prompt, x_sample, cache_k, cache_v, state_ssm, c, c_ctx, w_ada, b_ada, w_in, ssm_lam_re, ssm_lam_im, ssm_log_step, ssm_b_re, ssm_b_im, ssm_c_re, ssm_c_im, ssm_d, w_glu, b_glu, attn_sink, sgu_ln_g, sgu_ln_b, w_spatial, b_spatial, w_proj_a, w_proj_b, w_proj_c, w_out, ln_g, ln_b):
    depth = w_in.shape[0]
    batch, seq, d = x_prompt.shape
    dec_batch, dec_seq, _ = x_sample.shape
    past_len = cache_k.shape[2]
    alpha = (2 * depth) ** 0.25

    n_cond = 1 + dec_batch
    cond_rows = -(-n_cond // 8) * 8
    cond = jnp.concatenate([c_ctx[None, :], c, jnp.zeros((cond_rows - n_cond, d), F32)], axis=0)
    mod = _ada_mod(cond, w_ada, b_ada)

    w_in_bf16 = w_in.astype(BF16)
    fold = _ssm_fold_matrix()
    tables = _ssm_tables(ssm_lam_re, ssm_lam_im, ssm_log_step, ssm_b_re, ssm_b_im, ssm_c_re, ssm_c_im)
    layers = []
    for l in range(depth):
        w_intra, w_state, w_enter, small = tables
        layers.append({
            'w_in': w_in_bf16, 'layer': l,
            'fold': fold, 'w_intra': w_intra, 'w_state': w_state, 'w_enter': w_enter, 'ssm_small': small,
            'd_skip': ssm_d[l], 'w_glu': w_glu[l].astype(BF16), 'b_glu': b_glu[l],
            'sink': attn_sink[l].astype(F32),
            'sgu_g': sgu_ln_g[l], 'sgu_b': sgu_ln_b[l],
            'w_s': w_spatial[l].astype(BF16), 'b_s_t': b_spatial[l].T.astype(F32),
            'w_pa': w_proj_a[l].astype(BF16), 'w_pb': w_proj_b[l].astype(BF16),
            'w_pc': w_proj_c[l].astype(BF16), 'w_out': w_out[l].astype(BF16),
            'ln_g': ln_g[l], 'ln_b': ln_b[l],
        })

    h = x_prompt.reshape(batch * seq, d)
    zero_state = jnp.zeros((N_SSM_GROUPS, batch, 4 * SSM_STATE), F32)
    k_new = jnp.zeros((batch, depth, seq, D_KV), F32)
    v_new = jnp.zeros((batch, depth, seq, D_KV), F32)
    ss = []
    for l in range(depth):
        h, k_new, v_new, fin = _layer(h, batch, seq, mod[l, 0:1], layers[l], k_new, v_new, l,
                                      None, None, zero_state, None, alpha)
        ss.append(_lanes_to_states(fin))
    y_prompt = h.reshape(batch, seq, d)
    new_cache_k = k_new.reshape(batch, depth, seq, N_KV_HEADS, HEAD_DIM)
    new_cache_v = v_new.reshape(batch, depth, seq, N_KV_HEADS, HEAD_DIM)
    new_state_ssm = jnp.stack(ss, axis=1)

    rope = _rope_tables(dec_seq) + (_window_bias(), _rope_swap_matrix())
    z = x_sample.reshape(dec_batch * dec_seq, d)
    k_lat = jnp.zeros((dec_batch, 1, dec_seq, D_KV), F32)
    v_lat = jnp.zeros((dec_batch, 1, dec_seq, D_KV), F32)
    for l in range(depth):
        ctx_k = cache_k[:, l].reshape(dec_batch, past_len, D_KV).astype(F32)
        ctx_v = cache_v[:, l].reshape(dec_batch, past_len, D_KV).astype(F32)
        z, k_lat, v_lat, _ = _layer(z, dec_batch, dec_seq, mod[l, 1:1 + dec_batch], layers[l], k_lat, v_lat, 0,
                                    ctx_k, ctx_v, _states_to_lanes(state_ssm[:, l]), rope, alpha)
    y_sample = z.reshape(dec_batch, dec_seq, d)
    return (y_prompt, y_sample, new_cache_k, new_cache_v, new_state_ssm)
```

```python
import functools

import jax
import jax.numpy as jnp
import numpy as np
from jax import lax
from jax.experimental import pallas as pl
from jax.experimental.pallas import tpu as pltpu

F32 = jnp.float32
BF16 = jnp.bfloat16

D_MODEL = 2048
GRID_W = 64
D_SSM = 512
SSM_GROUP = 16
N_SSM_GROUPS = D_SSM // SSM_GROUP
SSM_STATE = 64
HEAD_DIM = 128
N_HEADS = 8
N_KV_HEADS = 2
Q_PER_KV = N_HEADS // N_KV_HEADS
D_ATTN = N_HEADS * HEAD_DIM
D_KV = N_KV_HEADS * HEAD_DIM
WINDOW = 128
ATTN_BLOCK = 128
ROPE_BASE = 10000.0
D_SGU = 512
SGU_CHUNK = 128
SGU_GROUP_CH = 128
N_SGU_GROUPS = D_SGU // SGU_GROUP_CH
N_BRANCH = 3
LN_EPS = 1e-5

D_MG = N_BRANCH * D_MODEL
COL_MG = 0
COL_XA = COL_MG + D_MG
COL_ZA = COL_XA + D_SSM
COL_Q = COL_ZA + D_SSM
COL_K = COL_Q + D_ATTN
COL_V = COL_K + D_KV
COL_ZB = COL_V + D_KV
COL_U = COL_ZB + D_ATTN
COL_VS = COL_U + D_SGU
COL_ZC = COL_VS + D_SGU
D_IN = COL_ZC + D_SGU
W_IN_ROTATE = D_IN - D_MG

SSM_T = 16
SSM_NC = 16
SSM_TC = SSM_T * SSM_GROUP
SSM_SEG = SSM_T * SSM_NC
SSM_SMALL_ROWS = 40
SSM_LANE_GROUPS = 128 // SSM_GROUP
SSM_POW_ROWS = 24

VMEM_LIMIT_BYTES = 56 * 1024 * 1024

INPROJ_TM = 1024
INPROJ_TN = 1024
INPROJ_SUB = 256
ROW_TM = 512
MERGE_TM = 512
MERGE_SUB = 256


def _params(sem):
    return pltpu.CompilerParams(dimension_semantics=sem, vmem_limit_bytes=VMEM_LIMIT_BYTES)


def _layernorm_f32(x):
    mu = jnp.mean(x, axis=-1, keepdims=True)
    xc = x - mu
    var = jnp.mean(xc * xc, axis=-1, keepdims=True)
    return xc * lax.rsqrt(var + LN_EPS)


def _ada_kernel(cond_ref, w_ref, b_ref, o_ref):
    c = cond_ref[...]
    a = (c * jax.nn.sigmoid(c)).astype(BF16)
    o_ref[...] = jnp.dot(a, w_ref[...].astype(BF16), preferred_element_type=F32) + b_ref[...]


def _ada_mod(cond, w_ada, b_ada):
    depth, d, n = w_ada.shape
    r = cond.shape[0]
    tn = 512
    return pl.pallas_call(
        _ada_kernel,
        out_shape=jax.ShapeDtypeStruct((depth, r, n), F32),
        grid=(depth, n // tn),
        in_specs=[
            pl.BlockSpec((r, d), lambda l, j: (0, 0)),
            pl.BlockSpec((None, d, tn), lambda l, j: (l, 0, j)),
            pl.BlockSpec((None, 1, tn), lambda l, j: (l, 0, j)),
        ],
        out_specs=pl.BlockSpec((None, r, tn), lambda l, j: (l, 0, j)),
        compiler_params=_params(("arbitrary", "arbitrary")),
        name="ada_mod",
    )(cond, w_ada, b_ada.reshape(depth, 1, n))


def _inproj_kernel(x_ref, scale_ref, shift_ref, w_ref, k_in_ref, v_in_ref, proj_ref, k_ref, v_ref, xa_ref, h_scr,
                   *, kv_tile, xa_tile):
    del k_in_ref, v_in_ref
    j = pl.program_id(1)
    tm = x_ref.shape[0]

    assert kv_tile != 0 and xa_tile != 0

    @pl.when(j == 0)
    def _():
        for r0 in range(0, tm, INPROJ_SUB):
            rows = slice(r0, r0 + INPROJ_SUB)
            h = (_layernorm_f32(x_ref[rows, :]) * (1.0 + scale_ref[0]) + shift_ref[0]).astype(BF16)
            h_scr[rows, :] = h
            proj_ref[rows, :] = jnp.dot(h, w_ref[...], preferred_element_type=F32).astype(BF16)

    @pl.when(j != 0)
    def _():
        acc = jnp.dot(h_scr[...], w_ref[...], preferred_element_type=F32)
        proj_ref[...] = acc.astype(BF16)

        @pl.when(j == kv_tile)
        def _():
            k_ref[...] = acc[:, 0:D_KV].reshape(k_ref.shape)
            v_ref[...] = acc[:, D_KV:2 * D_KV].reshape(v_ref.shape)

        @pl.when(j == xa_tile)
        def _():
            xa_ref[...] = acc[:, 0:D_SSM]


def _inproj(x2, scale, shift, w_bf16, layer, n_seq, seq_len, k_arr, v_arr, slot):
    t, d = x2.shape
    n = w_bf16.shape[2]
    tm, tn = INPROJ_TM, INPROJ_TN
    assert t % tm == 0 and n % tn == 0 and COL_K % tn == 0 and COL_XA % tn == 0
    assert 2 * D_KV <= tn and D_SSM <= tn and W_IN_ROTATE % tn == 0
    n_tiles = n // tn
    rot_tiles = W_IN_ROTATE // tn
    if seq_len >= tm:
        assert seq_len % tm == 0
        tiles_per_seq = seq_len // tm
        seq_of = lambda i: i // tiles_per_seq
        kv_spec = pl.BlockSpec((None, None, tm, D_KV), lambda i, j: (i // tiles_per_seq, slot, i % tiles_per_seq, 0))
    else:
        assert tm % seq_len == 0
        seq_of = lambda i: i * (tm // seq_len)
        kv_spec = pl.BlockSpec((tm // seq_len, None, seq_len, D_KV), lambda i, j: (i, slot, 0, 0))
    if scale.shape[0] == 1:
        mod_map = lambda i, j: (0, 0, 0)
    else:
        assert seq_len >= tm
        mod_map = lambda i, j: (seq_of(i), 0, 0)
    any_spec = pl.BlockSpec(memory_space=pl.ANY)
    return pl.pallas_call(
        functools.partial(_inproj_kernel, kv_tile=COL_K // tn, xa_tile=COL_XA // tn),
        out_shape=(jax.ShapeDtypeStruct((t, n), BF16), jax.ShapeDtypeStruct(k_arr.shape, F32),
                   jax.ShapeDtypeStruct(v_arr.shape, F32), jax.ShapeDtypeStruct((t, D_SSM), F32)),
        grid=(t // tm, n // tn),
        in_specs=[
            pl.BlockSpec((tm, d), lambda i, j: (i, 0)),
            pl.BlockSpec((1, 1, d), mod_map),
            pl.BlockSpec((1, 1, d), mod_map),
            pl.BlockSpec((None, d, tn), lambda i, j: (layer, 0, (j + rot_tiles) % n_tiles)),
            any_spec, any_spec,
        ],
        out_specs=(
            pl.BlockSpec((tm, tn), lambda i, j: (i, j)),
            kv_spec, kv_spec,
            pl.BlockSpec((tm, D_SSM), lambda i, j: (i, 0)),
        ),
        input_output_aliases={4: 1, 5: 2},
        scratch_shapes=[pltpu.VMEM((tm, d), BF16)],
        compiler_params=_params(("arbitrary", "arbitrary")),
        name="ln_inproj",
    )(x2, scale, shift, w_bf16, k_arr, v_arr)


def _zoh(lr, li, ls):
    dt = jnp.exp(ls)
    mag = jnp.exp(lr * dt)
    ar = mag * jnp.cos(li * dt)
    ai = mag * jnp.sin(li * dt)
    den = lr * lr + li * li
    nr = ar - 1.0
    return ar, ai, (nr * lr + ai * li) / den, (ai * lr - nr * li) / den


def _cmul(xr, xi, yr, yi):
    return xr * yr - xi * yi, xr * yi + xi * yr


def _cpow(br, bi, expo, n_bits):
    shape = jnp.broadcast_shapes(br.shape, expo.shape)
    pr = jnp.ones(shape, F32)
    pi = jnp.zeros(shape, F32)
    for bit in range(n_bits):
        sel = ((expo >> bit) & 1) == 1
        nr, ni = _cmul(pr, pi, br, bi)
        pr = jnp.where(sel, nr, pr)
        pi = jnp.where(sel, ni, pi)
        if bit + 1 < n_bits:
            br, bi = _cmul(br, bi, br, bi)
    return pr, pi


def _ssm_tables_kernel(laml_ref, bt_ref, ct_ref, ec_ref, ek_ref, wi_ref, ws_ref, we_ref, small_ref):
    p_n, t_n, c_n = SSM_STATE, SSM_T, SSM_GROUP
    p2 = 2 * p_n
    tc = t_n * c_n
    n_bits = 5
    assert max(t_n, SSM_NC) < 2 ** n_bits and SSM_POW_ROWS <= 2 ** n_bits

    def dot(a, b):
        return jnp.dot(a, b, precision=lax.Precision.HIGHEST, preferred_element_type=F32)

    def spread(x, onehot):
        hi = x.astype(BF16)
        rest = x - hi.astype(F32)
        mid = rest.astype(BF16)
        lo = (rest - mid.astype(F32)).astype(BF16)
        return (jnp.dot(hi, onehot, preferred_element_type=F32) + jnp.dot(mid, onehot, preferred_element_type=F32)
                + jnp.dot(lo, onehot, preferred_element_type=F32))

    ar, ai, f_re, f_im = _zoh(laml_ref[0:1, :], laml_ref[1:2, :], laml_ref[2:3, :])
    is_fwd = lax.broadcasted_iota(jnp.int32, (1, p2), 1) < p_n
    bbt_re, bbt_im = _cmul(f_re, f_im, bt_ref[:, 0:p2], bt_ref[:, p2:2 * p2])
    krow = lax.broadcasted_iota(jnp.int32, (SSM_POW_ROWS, 1), 0)
    n_lane = ct_ref.shape[1]
    kclamp = jnp.minimum(lax.broadcasted_iota(jnp.int32, (n_lane, 1), 0), t_n)
    pw_re, pw_im = _cpow(ar, ai, kclamp, n_bits)
    for j in range(t_n):
        pm_re = jnp.where(is_fwd, pw_re[t_n - 1 - j:t_n - j], pw_re[j:j + 1])
        pm_im = jnp.where(is_fwd, pw_im[t_n - 1 - j:t_n - j], pw_im[j:j + 1])
        s_re, s_im = _cmul(bbt_re, bbt_im, pm_re, pm_im)
        ws_ref[j * c_n:(j + 1) * c_n, 0:p2] = s_re.astype(BF16)
        ws_ref[j * c_n:(j + 1) * c_n, p2:2 * p2] = s_im.astype(BF16)
    a1r = pw_re[t_n:t_n + 1]
    a1i = pw_im[t_n:t_n + 1]
    q_re, q_im = _cpow(a1r, a1i, jnp.where(is_fwd, krow, jnp.maximum(SSM_NC - 1 - krow, 0)), n_bits)
    as_re, as_im = _cpow(a1r, a1i, jnp.full((1, 1), SSM_NC, jnp.int32), n_bits)
    small_ref[0:SSM_NC, :] = q_re[0:SSM_NC]
    small_ref[SSM_NC:2 * SSM_NC, :] = q_im[0:SSM_NC]
    small_ref[2 * SSM_NC:SSM_SMALL_ROWS, :] = jnp.concatenate(
        [a1r, a1i, as_re, as_im, jnp.zeros((SSM_SMALL_ROWS - 2 * SSM_NC - 4, p2), F32)], axis=0)

    pt_re_all = pw_re.T
    pt_im_all = pw_im.T
    cpt = []
    for d in range(2):
        pt_re = pt_re_all[d * p_n:(d + 1) * p_n, :]
        pt_im = pt_im_all[d * p_n:(d + 1) * p_n, :]
        c_re = spread(ct_ref[...], ec_ref[d])
        c_im = spread(ct_ref[...], ec_ref[2 + d])
        cpt.append(_cmul(c_re, c_im, spread(pt_re, ek_ref[d]), spread(pt_im, ek_ref[d])))
        e_re, e_im = _cmul(c_re, c_im, spread(pt_re, ek_ref[2 + d]), spread(pt_im, ek_ref[2 + d]))
        we_ref[d * p_n:(d + 1) * p_n, :] = e_re.astype(BF16)
        we_ref[p2 + d * p_n:p2 + (d + 1) * p_n, :] = (-e_im).astype(BF16)
    cpt_re = jnp.concatenate([cpt[0][0], cpt[1][0]], axis=0)
    cpt_im = jnp.concatenate([cpt[0][1], cpt[1][1]], axis=0)
    taps = []
    for sel in (is_fwd, jnp.logical_not(is_fwd)):
        taps.append(dot(jnp.where(sel, bbt_re, 0.0), cpt_re) - dot(jnp.where(sel, bbt_im, 0.0), cpt_im))
    lane = lax.broadcasted_iota(jnp.int32, (c_n, tc), 1)
    for j in range(t_n):
        fwd = jnp.where(lane >= j * c_n, pltpu.roll(taps[0], j * c_n, 1), 0.0)
        bwd = jnp.where(lane < (j + 1) * c_n, pltpu.roll(taps[1], (tc - (t_n - 1 - j) * c_n) % tc, 1), 0.0)
        wi_ref[j * c_n:(j + 1) * c_n, :] = (fwd + bwd).astype(BF16)


def _ssm_table_constants():
    t_n, c_n = SSM_T, SSM_GROUP
    rows = np.arange(128)[:, None]
    tok = (np.arange(t_n * c_n) // c_n)[None, :]
    ch = (np.arange(t_n * c_n) % c_n)[None, :]
    ec = np.stack([(rows == q * c_n + ch) for q in range(4)], axis=0)
    ek = np.stack([rows == tok, rows == t_n - 1 - tok, rows == tok + 1, rows == t_n - tok], axis=0)
    return jnp.asarray(ec.astype(BF16)), jnp.asarray(ek.astype(BF16))


def _ssm_tables(lam_re, lam_im, log_step, b_re, b_im, c_re, c_im):
    l_n, _, g_n, p_n = lam_re.shape
    c_n, t_n = SSM_GROUP, SSM_T
    tc = t_n * c_n
    p2, p4 = 2 * p_n, 4 * p_n
    ls = jnp.broadcast_to(log_step.astype(F32)[..., None], lam_re.shape)

    def lanes(x):
        return x.astype(F32).transpose(0, 2, 1, 3).reshape(l_n, g_n, 1, p2)

    laml = jnp.concatenate([lanes(lam_re), lanes(lam_im), lanes(ls), jnp.zeros((l_n, g_n, 5, p2), F32)], axis=2)

    def b_lanes(x):
        return x.astype(F32).transpose(0, 2, 4, 1, 3).reshape(l_n, g_n, c_n, p2)

    bt = jnp.concatenate([b_lanes(b_re), b_lanes(b_im)], axis=-1)

    def c_lanes(x):
        return x.astype(F32).transpose(0, 2, 4, 1, 3).reshape(l_n, g_n, p_n, 2 * c_n)

    ct = jnp.concatenate([c_lanes(c_re), c_lanes(c_im), jnp.zeros((l_n, g_n, p_n, 128 - 4 * c_n), F32)], axis=-1)
    ec, ek = _ssm_table_constants()

    def per_group(*dims):
        return pl.BlockSpec((None, None) + dims, lambda l, g: (l, g) + (0,) * len(dims))

    def const(shape):
        return pl.BlockSpec(shape, lambda l, g: (0,) * len(shape))

    return pl.pallas_call(
        _ssm_tables_kernel,
        out_shape=(jax.ShapeDtypeStruct((l_n, g_n, tc, tc), BF16), jax.ShapeDtypeStruct((l_n, g_n, tc, p4), BF16),
                   jax.ShapeDtypeStruct((l_n, g_n, p4, tc), BF16),
                   jax.ShapeDtypeStruct((l_n, g_n, SSM_SMALL_ROWS, p2), F32)),
        grid=(l_n, g_n),
        in_specs=[per_group(8, p2), per_group(c_n, p4), per_group(p_n, 128),
                  const(ec.shape), const(ek.shape)],
        out_specs=(per_group(tc, tc), per_group(tc, p4), per_group(p4, tc), per_group(SSM_SMALL_ROWS, p2)),
        compiler_params=_params(("arbitrary", "arbitrary")),
        name="ssm_tables",
    )(laml, bt, ct, ec, ek)


def _ssm_fold_matrix():
    half_t = SSM_T // 2
    n = half_t * SSM_LANE_GROUPS * SSM_GROUP
    src = np.arange(n)
    j = src // (SSM_LANE_GROUPS * SSM_GROUP)
    g = (src // SSM_GROUP) % SSM_LANE_GROUPS
    c = src % SSM_GROUP
    dst = g * (half_t * SSM_GROUP) + j * SSM_GROUP + c
    return jnp.asarray((dst[:, None] == np.arange(n)[None, :]).astype(BF16))


def _ssm_kernel(xa_ref, fold_ref, wi_ref, ws_ref, we_ref, small_ref, s0_ref, y_ref, fin_ref,
                v_scr, s_scr, yy_scr, fre_scr, fim_scr, efre_scr, efim_scr, ebre_scr, ebim_scr,
                *, n_seq, n_seg):
    p2 = 2 * SSM_STATE
    p4 = 4 * SSM_STATE
    nbp = n_seq * n_seg
    lanes = SSM_LANE_GROUPS * SSM_GROUP
    nt = (((1,), (1,)), ((), ()))

    blocks = []
    for c in range(SSM_NC):
        toks = [xa_ref[pl.ds(c * SSM_T + j, nbp, stride=SSM_SEG), :].astype(BF16) for j in range(SSM_T)]
        blocks.append(jnp.concatenate(toks, axis=1))
    xx = jnp.concatenate(blocks, axis=0)
    hw = fold_ref.shape[0]
    uu = [jnp.dot(xx[:, h * hw:(h + 1) * hw], fold_ref[...], preferred_element_type=F32).astype(BF16)
          for h in range(2)]

    lane = lax.broadcasted_iota(jnp.int32, (1, p2), 1)
    is_fwd = lane < SSM_STATE
    for g in range(SSM_LANE_GROUPS):
        gc = slice(g * SSM_TC, (g + 1) * SSM_TC)
        u = jnp.concatenate([uu[h][:, g * lanes:(g + 1) * lanes] for h in range(2)], axis=1)
        yy_scr[:, gc] = jnp.dot(u, wi_ref[g], preferred_element_type=F32)
        v_scr[:, gc] = jnp.dot(u, ws_ref[g], preferred_element_type=F32)

    for g in range(SSM_LANE_GROUPS):
        o = g * p4
        a1r = small_ref[g, 2 * SSM_NC:2 * SSM_NC + 1, :]
        a1i = small_ref[g, 2 * SSM_NC + 1:2 * SSM_NC + 2, :]
        asr = small_ref[g, 2 * SSM_NC + 2:2 * SSM_NC + 3, :]
        asi = small_ref[g, 2 * SSM_NC + 3:2 * SSM_NC + 4, :]

        st_re = jnp.zeros((nbp, p2), F32)
        st_im = jnp.zeros((nbp, p2), F32)
        for c in range(SSM_NC):
            cb = SSM_NC - 1 - c
            rf = slice(c * nbp, (c + 1) * nbp)
            rb = slice(cb * nbp, (cb + 1) * nbp)
            s_scr[rf, o:o + SSM_STATE] = st_re[:, 0:SSM_STATE]
            s_scr[rb, o + SSM_STATE:o + p2] = st_re[:, SSM_STATE:p2]
            s_scr[rf, o + p2:o + p2 + SSM_STATE] = st_im[:, 0:SSM_STATE]
            s_scr[rb, o + p2 + SSM_STATE:o + p4] = st_im[:, SSM_STATE:p2]
            v_re = jnp.where(is_fwd, v_scr[rf, o:o + p2], v_scr[rb, o:o + p2])
            v_im = jnp.where(is_fwd, v_scr[rf, o + p2:o + p4], v_scr[rb, o + p2:o + p4])
            st_re, st_im = (a1r * st_re - a1i * st_im + v_re, a1r * st_im + a1i * st_re + v_im)

        fre_scr[...] = st_re
        fim_scr[...] = st_im
        s0 = s0_ref[g]
        e_re = s0[:, 0:p2]
        e_im = s0[:, p2:p4]
        for k in range(n_seg):
            kb = n_seg - 1 - k
            rows_f = pl.ds(k, n_seq, stride=n_seg)
            rows_b = pl.ds(kb, n_seq, stride=n_seg)
            efre_scr[rows_f, :] = e_re
            efim_scr[rows_f, :] = e_im
            ebre_scr[rows_b, :] = e_re
            ebim_scr[rows_b, :] = e_im
            f_re = jnp.where(is_fwd, fre_scr[rows_f, :], fre_scr[rows_b, :])
            f_im = jnp.where(is_fwd, fim_scr[rows_f, :], fim_scr[rows_b, :])
            e_re, e_im = (asr * e_re - asi * e_im + f_re, asr * e_im + asi * e_re + f_im)
        fin_ref[g] = jnp.concatenate([e_re, e_im], axis=-1)

        en_re = jnp.where(is_fwd, efre_scr[...], ebre_scr[...])
        en_im = jnp.where(is_fwd, efim_scr[...], ebim_scr[...])
        for c in range(SSM_NC):
            rows = slice(c * nbp, (c + 1) * nbp)
            cr = small_ref[g, c:c + 1, :]
            ci = small_ref[g, SSM_NC + c:SSM_NC + c + 1, :]
            s_scr[rows, o:o + p2] = s_scr[rows, o:o + p2] + (cr * en_re - ci * en_im)
            s_scr[rows, o + p2:o + p4] = s_scr[rows, o + p2:o + p4] + (cr * en_im + ci * en_re)

    for g in range(SSM_LANE_GROUPS):
        gc = slice(g * SSM_TC, (g + 1) * SSM_TC)
        yy_scr[:, gc] += jnp.dot(s_scr[:, g * p4:(g + 1) * p4].astype(BF16), we_ref[g],
                                 preferred_element_type=F32)

    half_t = SSM_T // 2
    for h in range(2):
        yh = jnp.concatenate([yy_scr[:, g * SSM_TC + h * lanes:g * SSM_TC + (h + 1) * lanes]
                              for g in range(SSM_LANE_GROUPS)], axis=1).astype(BF16)
        zz = lax.dot_general(yh, fold_ref[...], nt, preferred_element_type=F32)
        for c in range(SSM_NC):
            for j in range(half_t):
                y_ref[pl.ds(c * SSM_T + h * half_t + j, nbp, stride=SSM_SEG), :] = zz[c * nbp:(c + 1) * nbp,
                                                                                       j * lanes:(j + 1) * lanes]


def _ssm_scan(xa, fold, w_intra, w_state, w_enter, small, layer, s0, n_seq, n_seg):
    t = xa.shape[0]
    g_n = N_SSM_GROUPS
    lg = SSM_LANE_GROUPS
    lanes = lg * SSM_GROUP
    p4 = 4 * SSM_STATE
    nbp = n_seq * n_seg
    r = SSM_NC * nbp
    assert t == r * SSM_T and (SSM_T // 2) * SSM_GROUP == lanes and fold.shape == (lg * lanes, lg * lanes)
    const = lambda shape: pl.BlockSpec(shape, lambda i: (0,) * len(shape), pipeline_mode=pl.Buffered(1))
    return pl.pallas_call(
        functools.partial(_ssm_kernel, n_seq=n_seq, n_seg=n_seg),
        out_shape=(jax.ShapeDtypeStruct((t, D_SSM), F32), jax.ShapeDtypeStruct((g_n, n_seq, p4), F32)),
        grid=(g_n // lg,),
        in_specs=[
            pl.BlockSpec((t, lanes), lambda i: (0, i)),
            const(fold.shape),
            pl.BlockSpec((None, lg, SSM_TC, SSM_TC), lambda i: (layer, i, 0, 0)),
            pl.BlockSpec((None, lg, SSM_TC, p4), lambda i: (layer, i, 0, 0)),
            pl.BlockSpec((None, lg, p4, SSM_TC), lambda i: (layer, i, 0, 0)),
            pl.BlockSpec((None, lg, SSM_SMALL_ROWS, 2 * SSM_STATE), lambda i: (layer, i, 0, 0)),
            pl.BlockSpec((lg, n_seq, p4), lambda i: (i, 0, 0)),
        ],
        out_specs=(
            pl.BlockSpec((t, lanes), lambda i: (0, i)),
            pl.BlockSpec((lg, n_seq, p4), lambda i: (i, 0, 0)),
        ),
        scratch_shapes=[pltpu.VMEM((r, lg * p4), F32), pltpu.VMEM((r, lg * p4), F32),
                        pltpu.VMEM((r, lg * SSM_TC), F32)]
        + [pltpu.VMEM((nbp, 2 * SSM_STATE), F32)] * 6,
        compiler_params=_params(("arbitrary",)),
        name="ssm_scan",
    )(xa, fold, w_intra, w_state, w_enter, small, s0)


def _ssm_post_kernel(xa_ref, za_ref, ys_ref, d_ref, w_ref, b_ref, o_ref):
    y = xa_ref[...].astype(F32) * d_ref[...] + ys_ref[...]
    y = jax.nn.gelu(y)
    gl = jnp.dot(y.astype(BF16), w_ref[...], preferred_element_type=F32) + b_ref[...]
    y = y * jax.nn.sigmoid(gl)
    o_ref[...] = (y * jax.nn.silu(za_ref[...].astype(F32))).astype(BF16)


def _ssm_post(proj, yssm, d_skip, w_glu_bf16, b_glu):
    t = proj.shape[0]
    tm = ROW_TM
    return pl.pallas_call(
        _ssm_post_kernel,
        out_shape=jax.ShapeDtypeStruct((t, D_SSM), BF16),
        grid=(t // tm,),
        in_specs=[
            pl.BlockSpec((tm, D_SSM), lambda i: (i, COL_XA // D_SSM)),
            pl.BlockSpec((tm, D_SSM), lambda i: (i, COL_ZA // D_SSM)),
            pl.BlockSpec((tm, D_SSM), lambda i: (i, 0)),
            pl.BlockSpec((1, D_SSM), lambda i: (0, 0)),
            pl.BlockSpec((D_SSM, D_SSM), lambda i: (0, 0)),
            pl.BlockSpec((1, D_SSM), lambda i: (0, 0)),
        ],
        out_specs=pl.BlockSpec((tm, D_SSM), lambda i: (i, 0)),
        compiler_params=_params(("arbitrary",)),
        name="ssm_post",
    )(proj, proj, yssm, d_skip.reshape(1, D_SSM), w_glu_bf16, b_glu.reshape(1, D_SSM))


def _softmax_pv_wide(s, sink, v_bf16):
    hd = v_bf16.shape[1]
    m = jnp.maximum(jnp.max(s, axis=-1, keepdims=True), sink)
    p = jnp.exp(s - m).astype(BF16)
    v_ones = jnp.concatenate([v_bf16, jnp.ones_like(v_bf16)], axis=1)
    o = jnp.dot(p, v_ones, preferred_element_type=F32)
    den = o[:, hd:] + jnp.exp(sink - m)
    return o[:, :hd] / den


def _head_gate(zb_refs, h):
    per = N_HEADS // 2
    z = zb_refs[h // per][:, (h % per) * HEAD_DIM:(h % per + 1) * HEAD_DIM].astype(F32)
    return jax.nn.silu(z)


def _ctx_attn_kernel(sink_ref, q_ref, k_ref, v_ref, zb0_ref, zb1_ref, o_ref):
    scale = HEAD_DIM ** -0.5
    nt = (((1,), (1,)), ((), ()))
    for kvh in range(N_KV_HEADS):
        k = (k_ref[:, kvh * HEAD_DIM:(kvh + 1) * HEAD_DIM] * scale).astype(BF16)
        v = v_ref[:, kvh * HEAD_DIM:(kvh + 1) * HEAD_DIM].astype(BF16)
        for g in range(Q_PER_KV):
            h = kvh * Q_PER_KV + g
            hs = slice(h * HEAD_DIM, (h + 1) * HEAD_DIM)
            s = lax.dot_general(q_ref[:, hs], k, nt, preferred_element_type=F32)
            o = _softmax_pv_wide(s, sink_ref[h], v)
            o_ref[:, hs] = (o * _head_gate((zb0_ref, zb1_ref), h)).astype(BF16)


def _ctx_attention(proj, k_arr, v_arr, slot, sink, seq_len):
    t = proj.shape[0]
    kv_spec = pl.BlockSpec((None, None, seq_len, D_KV), lambda b: (b, slot, 0, 0))
    return pl.pallas_call(
        _ctx_attn_kernel,
        out_shape=jax.ShapeDtypeStruct((t, D_ATTN), BF16),
        grid=(t // seq_len,),
        in_specs=[
            pl.BlockSpec(memory_space=pltpu.SMEM),
            pl.BlockSpec((seq_len, D_ATTN), lambda b: (b, COL_Q // D_ATTN)),
            kv_spec, kv_spec,
            pl.BlockSpec((seq_len, D_ATTN // 2), lambda b: (b, COL_ZB // (D_ATTN // 2))),
            pl.BlockSpec((seq_len, D_ATTN // 2), lambda b: (b, COL_ZB // (D_ATTN // 2) + 1)),
        ],
        out_specs=pl.BlockSpec((seq_len, D_ATTN), lambda b: (b, 0)),
        compiler_params=_params(("arbitrary",)),
        name="ctx_attention",
    )(sink, proj, k_arr, v_arr, proj, proj)


def _rope_partner(x):
    lane = lax.broadcasted_iota(jnp.int32, x.shape, 1)
    quarter = HEAD_DIM // 4
    first = (lane % (2 * quarter)) < quarter
    return jnp.where(first, pltpu.roll(x, HEAD_DIM - quarter, 1), pltpu.roll(x, quarter, 1))


def _rope(x, tab_ref):
    return x * tab_ref[:, 0:HEAD_DIM] + _rope_partner(x) * tab_ref[:, HEAD_DIM:2 * HEAD_DIM]


def _rope_kv_kernel(k_ref, v_ref, tab_ref, o_ref):
    for kvh in range(N_KV_HEADS):
        ks = slice(kvh * HEAD_DIM, (kvh + 1) * HEAD_DIM)
        o_ref[:, ks] = _rope(k_ref[:, ks], tab_ref).astype(BF16)
    o_ref[:, D_KV:2 * D_KV] = v_ref[...].astype(BF16)


def _rope_kv(k_arr, v_arr, slot, tab_k):
    n_seq, _, seq_len, _ = k_arr.shape
    tm = ROW_TM
    assert seq_len % tm == 0
    tiles_per_seq = seq_len // tm
    kv_spec = pl.BlockSpec((None, None, tm, D_KV), lambda i: (i // tiles_per_seq, slot, i % tiles_per_seq, 0))
    return pl.pallas_call(
        _rope_kv_kernel,
        out_shape=jax.ShapeDtypeStruct((n_seq * seq_len, 2 * D_KV), BF16),
        grid=(n_seq * tiles_per_seq,),
        in_specs=[
            kv_spec, kv_spec,
            pl.BlockSpec((tm, 2 * HEAD_DIM), lambda i: (i % tiles_per_seq, 0)),
        ],
        out_specs=pl.BlockSpec((tm, 2 * D_KV), lambda i: (i, 0)),
        compiler_params=_params(("arbitrary",)),
        name="rope_kv",
    )(k_arr, v_arr, tab_k)


def _lat_attn_kernel(sink_ref, q_ref, kvp_ref, kvc_ref, kvn_ref, ck_ref, cv_ref, zb0_ref, zb1_ref,
                     tabq_ref, swap_ref, bias_ref, o_ref):
    nt = (((1,), (1,)), ((), ()))
    blk = ATTN_BLOCK
    cos_q = tabq_ref[:, 0:HEAD_DIM]
    sin_q = tabq_ref[:, HEAD_DIM:2 * HEAD_DIM]
    for kvh in range(N_KV_HEADS):
        ks = slice(kvh * HEAD_DIM, (kvh + 1) * HEAD_DIM)
        vs = slice(D_KV + kvh * HEAD_DIM, D_KV + (kvh + 1) * HEAD_DIM)
        keys = [jnp.concatenate([kvp_ref[:, ks], kvc_ref[:, ks], kvn_ref[:, ks]], axis=0),
                ck_ref[:, ks].astype(BF16)]
        values = [jnp.concatenate([kvp_ref[:, vs], kvc_ref[:, vs], kvn_ref[:, vs]], axis=0),
                  cv_ref[:, ks].astype(BF16)]
        qx = jnp.concatenate(
            [q_ref[:, (kvh * Q_PER_KV + g) * HEAD_DIM:(kvh * Q_PER_KV + g + 1) * HEAD_DIM]
             for g in range(Q_PER_KV)], axis=0)
        partner = jnp.dot(qx, swap_ref[...], preferred_element_type=F32)
        qf = qx.astype(F32).reshape(Q_PER_KV, blk, HEAD_DIM)
        q4 = (qf * cos_q[None] + partner.reshape(Q_PER_KV, blk, HEAD_DIM) * sin_q[None])
        q4 = q4.reshape(Q_PER_KV * blk, HEAD_DIM).astype(BF16)
        s_loc = lax.dot_general(q4, keys[0], nt, preferred_element_type=F32)
        s_loc = (s_loc.reshape(Q_PER_KV, blk, 3 * blk) + bias_ref[...][None]).reshape(Q_PER_KV * blk, 3 * blk)
        s_ctx = lax.dot_general(q4, keys[1], nt, preferred_element_type=F32)
        sink_rows = jnp.concatenate(
            [jnp.full((blk, 1), sink_ref[kvh * Q_PER_KV + g], F32) for g in range(Q_PER_KV)], axis=0)
        o = _softmax_pv_wide(jnp.concatenate([s_loc, s_ctx], axis=-1), sink_rows,
                             jnp.concatenate(values, axis=0))
        for g in range(Q_PER_KV):
            h = kvh * Q_PER_KV + g
            o_ref[:, h * HEAD_DIM:(h + 1) * HEAD_DIM] = (
                o[g * blk:(g + 1) * blk] * _head_gate((zb0_ref, zb1_ref), h)).astype(BF16)


def _lat_attention(proj, k_arr, v_arr, slot, ctx_k, ctx_v, sink, tab_q, tab_k, bias, swap, n_seq, seq_len):
    kv = _rope_kv(k_arr, v_arr, slot, tab_k)
    t = proj.shape[0]
    blk = ATTN_BLOCK
    assert blk - 1 <= WINDOW <= blk
    nb = seq_len // blk
    lc = ctx_k.shape[1]

    def row(b, i):
        return b * nb + i

    def prev(i):
        return jnp.maximum(i - 1, 0)

    def nxt(i):
        return jnp.minimum(i + 1, nb - 1)

    def bias_variant(b, i):
        return ((i > 0).astype(jnp.int32) * 2 + (i < nb - 1).astype(jnp.int32), 0, 0)

    return pl.pallas_call(
        _lat_attn_kernel,
        out_shape=jax.ShapeDtypeStruct((t, D_ATTN), BF16),
        grid=(n_seq, nb),
        in_specs=[
            pl.BlockSpec(memory_space=pltpu.SMEM),
            pl.BlockSpec((blk, D_ATTN), lambda b, i: (row(b, i), COL_Q // D_ATTN)),
            pl.BlockSpec((blk, 2 * D_KV), lambda b, i: (row(b, prev(i)), 0)),
            pl.BlockSpec((blk, 2 * D_KV), lambda b, i: (row(b, i), 0)),
            pl.BlockSpec((blk, 2 * D_KV), lambda b, i: (row(b, nxt(i)), 0)),
            pl.BlockSpec((None, lc, D_KV), lambda b, i: (b, 0, 0)),
            pl.BlockSpec((None, lc, D_KV), lambda b, i: (b, 0, 0)),
            pl.BlockSpec((blk, D_ATTN // 2), lambda b, i: (row(b, i), COL_ZB // (D_ATTN // 2))),
            pl.BlockSpec((blk, D_ATTN // 2), lambda b, i: (row(b, i), COL_ZB // (D_ATTN // 2) + 1)),
            pl.BlockSpec((blk, 2 * HEAD_DIM), lambda b, i: (i, 0)),
            pl.BlockSpec((HEAD_DIM, HEAD_DIM), lambda b, i: (0, 0)),
            pl.BlockSpec((None, blk, 3 * blk), bias_variant),
        ],
        out_specs=pl.BlockSpec((blk, D_ATTN), lambda b, i: (row(b, i), 0)),
        compiler_params=_params(("arbitrary", "arbitrary")),
        name="lat_attention",
    )(sink, proj, kv, kv, kv, ctx_k, ctx_v, proj, proj, tab_q, swap, bias)


def _rope_tables(seq_len):
    rows = seq_len // GRID_W
    row = np.repeat(np.arange(rows), GRID_W).astype(np.float64)
    col = np.tile(np.arange(GRID_W), rows).astype(np.float64)
    half = HEAD_DIM // 2
    inv = ROPE_BASE ** (-np.arange(0, half, 2, dtype=np.float64) / half)
    ang_r = row[:, None] * inv[None, :]
    ang_c = col[:, None] * inv[None, :]
    cos_t = np.concatenate([np.cos(ang_r)] * 2 + [np.cos(ang_c)] * 2, axis=-1)
    sin_t = np.concatenate([-np.sin(ang_r), np.sin(ang_r), -np.sin(ang_c), np.sin(ang_c)], axis=-1)
    tab_k = np.concatenate([cos_t, sin_t], axis=-1)
    tab_q = tab_k * HEAD_DIM ** -0.5
    return jnp.asarray(tab_q.astype(np.float32)), jnp.asarray(tab_k.astype(np.float32))


def _rope_swap_matrix():
    quarter = HEAD_DIM // 4
    d = np.arange(HEAD_DIM)
    src = np.where(d % (2 * quarter) < quarter, d + quarter, d - quarter)
    return jnp.asarray((np.arange(HEAD_DIM)[:, None] == src[None, :]).astype(BF16))


def _window_bias():
    blk = ATTN_BLOCK
    r = np.arange(blk)[:, None]
    s = np.arange(3 * blk)[None, :]
    band = np.abs(s - blk - r) <= WINDOW
    out = []
    for has_prev in (False, True):
        for has_next in (False, True):
            ok = band & ((s >= blk) | has_prev) & ((s < 2 * blk) | has_next)
            out.append(np.where(ok, 0.0, -np.inf).astype(np.float32))
    return jnp.asarray(np.stack(out, axis=0))


def _sgu_kernel(u_ref, v_ref, z_ref, g_ref, b_ref, ws_ref, bs_ref, o_ref, *, n_chunks):
    for n in range(n_chunks):
        rows = slice(n * SGU_CHUNK, (n + 1) * SGU_CHUNK)
        v = _layernorm_f32(jax.nn.gelu(v_ref[rows, :].astype(F32))) * g_ref[...] + b_ref[...]
        v = v.astype(BF16)
        for g in range(N_SGU_GROUPS):
            cols = slice(g * SGU_GROUP_CH, (g + 1) * SGU_GROUP_CH)
            vm = jnp.dot(ws_ref[g], v[:, cols], preferred_element_type=F32) + bs_ref[:, g:g + 1]
            u = jax.nn.gelu(u_ref[rows, cols].astype(F32))
            z = z_ref[rows, cols].astype(F32)
            o_ref[rows, cols] = (u * vm * jax.nn.silu(z)).astype(BF16)


def _sgu(proj, ln_g, ln_b, w_s_bf16, b_s_t):
    t = proj.shape[0]
    tm = ROW_TM
    return pl.pallas_call(
        functools.partial(_sgu_kernel, n_chunks=tm // SGU_CHUNK),
        out_shape=jax.ShapeDtypeStruct((t, D_SGU), BF16),
        grid=(t // tm,),
        in_specs=[
            pl.BlockSpec((tm, D_SGU), lambda i: (i, COL_U // D_SGU)),
            pl.BlockSpec((tm, D_SGU), lambda i: (i, COL_VS // D_SGU)),
            pl.BlockSpec((tm, D_SGU), lambda i: (i, COL_ZC // D_SGU)),
            pl.BlockSpec((1, D_SGU), lambda i: (0, 0)),
            pl.BlockSpec((1, D_SGU), lambda i: (0, 0)),
            pl.BlockSpec((N_SGU_GROUPS, SGU_CHUNK, SGU_CHUNK), lambda i: (0, 0, 0)),
            pl.BlockSpec((SGU_CHUNK, N_SGU_GROUPS), lambda i: (0, 0)),
        ],
        out_specs=pl.BlockSpec((tm, D_SGU), lambda i: (i, 0)),
        compiler_params=_params(("arbitrary",)),
        name="sgu",
    )(proj, proj, proj, ln_g.reshape(1, D_SGU), ln_b.reshape(1, D_SGU), w_s_bf16, b_s_t)


def _merge_kernel(mg_ref, ya_ref, yb_ref, yc_ref, x_ref, gate_ref, wpa_ref, wpb_ref, wpc_ref, wout_ref,
                  lng_ref, lnb_ref, y_ref, *, alpha):
    d = D_MODEL
    tm = y_ref.shape[0]
    for r0 in range(0, tm, MERGE_SUB):
        rows = slice(r0, r0 + MERGE_SUB)

        def branch(k, y_r, w_r):
            gate = jax.nn.sigmoid(mg_ref[rows, k * d:(k + 1) * d].astype(F32))
            return gate * jnp.dot(y_r[rows, :], w_r[...], preferred_element_type=F32)

        merged = branch(0, ya_ref, wpa_ref) + branch(1, yb_ref, wpb_ref) + branch(2, yc_ref, wpc_ref)
        out = jnp.dot(merged.astype(BF16), wout_ref[...], preferred_element_type=F32)
        z = alpha * x_ref[rows, :] + gate_ref[0] * out
        y_ref[rows, :] = _layernorm_f32(z) * lng_ref[...] + lnb_ref[...]


def _merge(proj, ya, yb, yc, x2, gate, wpa, wpb, wpc, wout, ln_g, ln_b, seq_len, alpha):
    t, d = x2.shape
    tm = MERGE_TM
    if gate.shape[0] == 1:
        mod_map = lambda i: (0, 0, 0)
    else:
        assert seq_len % tm == 0
        tiles_per_seq = seq_len // tm
        mod_map = lambda i: (i // tiles_per_seq, 0, 0)

    def const(shape):
        return pl.BlockSpec(shape, lambda i: (0,) * len(shape), pipeline_mode=pl.Buffered(1))

    return pl.pallas_call(
        functools.partial(_merge_kernel, alpha=alpha),
        out_shape=jax.ShapeDtypeStruct((t, d), F32),
        grid=(t // tm,),
        in_specs=[
            pl.BlockSpec((tm, D_MG), lambda i: (i, 0)),
            pl.BlockSpec((tm, D_SSM), lambda i: (i, 0)),
            pl.BlockSpec((tm, D_ATTN), lambda i: (i, 0)),
            pl.BlockSpec((tm, D_SGU), lambda i: (i, 0)),
            pl.BlockSpec((tm, d), lambda i: (i, 0)),
            pl.BlockSpec((1, 1, d), mod_map),
            const((D_SSM, d)), const((D_ATTN, d)), const((D_SGU, d)), const((d, d)),
            const((1, d)), const((1, d)),
        ],
        out_specs=pl.BlockSpec((tm, d), lambda i: (i, 0)),
        compiler_params=_params(("arbitrary",)),
        name="merge_out",
    )(proj, ya, yb, yc, x2, gate, wpa, wpb, wpc, wout, ln_g.reshape(1, d), ln_b.reshape(1, d))


def _layer(x2, n_seq, seq_len, mod, lp, k_arr, v_arr, slot, ctx_k, ctx_v, s0, rope, alpha):
    d = D_MODEL
    nbm = mod.shape[0]
    shift = mod[:, 0:d].reshape(nbm, 1, d)
    scale = mod[:, d:2 * d].reshape(nbm, 1, d)
    gate = mod[:, 2 * d:3 * d].reshape(nbm, 1, d)
    proj, k_arr, v_arr, xa = _inproj(x2, scale, shift, lp['w_in'], lp['layer'], n_seq, seq_len, k_arr, v_arr, slot)

    n_seg = seq_len // SSM_SEG
    yssm, fin = _ssm_scan(xa, lp['fold'], lp['w_intra'], lp['w_state'], lp['w_enter'], lp['ssm_small'],
                          lp['layer'], s0, n_seq, n_seg)
    ya = _ssm_post(proj, yssm, lp['d_skip'], lp['w_glu'], lp['b_glu'])

    if ctx_k is None:
        yb = _ctx_attention(proj, k_arr, v_arr, slot, lp['sink'], seq_len)
    else:
        yb = _lat_attention(proj, k_arr, v_arr, slot, ctx_k, ctx_v, lp['sink'], *rope, n_seq, seq_len)

    yc = _sgu(proj, lp['sgu_g'], lp['sgu_b'], lp['w_s'], lp['b_s_t'])

    y = _merge(proj, ya, yb, yc, x2, gate, lp['w_pa'], lp['w_pb'], lp['w_pc'], lp['w_out'],
               lp['ln_g'], lp['ln_b'], seq_len, alpha)
    return y, k_arr, v_arr, fin


def _states_to_lanes(s):
    b = s.shape[0]
    return s.astype(F32).transpose(3, 0, 2, 1, 4).reshape(N_SSM_GROUPS, b, 4 * SSM_STATE)


def _lanes_to_states(f):
    b = f.shape[1]
    return f.reshape(N_SSM_GROUPS, b, 2, 2, SSM_STATE).transpose(1, 3, 2, 0, 4)


def kernel(x_prompt, x_sample, cache_k, cache_v, state_ssm, c, c_ctx, w_ada, b_ada, w_in, ssm_lam_re, ssm_lam_im, ssm_log_step, ssm_b_re, ssm_b_im, ssm_c_re, ssm_c_im, ssm_d, w_glu, b_glu, attn_sink, sgu_ln_g, sgu_ln_b, w_spatial, b_spatial, w_proj_a, w_proj_b, w_proj_c, w_out, ln_g, ln_b):
    depth = w_in.shape[0]
    batch, seq, d = x_prompt.shape
    dec_batch, dec_seq, _ = x_sample.shape
    past_len = cache_k.shape[2]
    alpha = (2 * depth) ** 0.25

    n_cond = 1 + dec_batch
    cond_rows = -(-n_cond // 8) * 8
    cond = jnp.concatenate([c_ctx[None, :], c, jnp.zeros((cond_rows - n_cond, d), F32)], axis=0)
    mod = _ada_mod(cond, w_ada, b_ada)

    w_in_bf16 = w_in.astype(BF16)
    fold = _ssm_fold_matrix()
    tables = _ssm_tables(ssm_lam_re, ssm_lam_im, ssm_log_step, ssm_b_re, ssm_b_im, ssm_c_re, ssm_c_im)
    layers = []
    for l in range(depth):
        w_intra, w_state, w_enter, small = tables
        layers.append({
            'w_in': w_in_bf16, 'layer': l,
            'fold': fold, 'w_intra': w_intra, 'w_state': w_state, 'w_enter': w_enter, 'ssm_small': small,
            'd_skip': ssm_d[l], 'w_glu': w_glu[l].astype(BF16), 'b_glu': b_glu[l],
            'sink': attn_sink[l].astype(F32),
            'sgu_g': sgu_ln_g[l], 'sgu_b': sgu_ln_b[l],
            'w_s': w_spatial[l].astype(BF16), 'b_s_t': b_spatial[l].T.astype(F32),
            'w_pa': w_proj_a[l].astype(BF16), 'w_pb': w_proj_b[l].astype(BF16),
            'w_pc': w_proj_c[l].astype(BF16), 'w_out': w_out[l].astype(BF16),
            'ln_g': ln_g[l], 'ln_b': ln_b[l],
        })

    h = x_prompt.reshape(batch * seq, d)
    zero_state = jnp.zeros((N_SSM_GROUPS, batch, 4 * SSM_STATE), F32)
    k_new = jnp.zeros((batch, depth, seq, D_KV), F32)
    v_new = jnp.zeros((batch, depth, seq, D_KV), F32)
    ss = []
    for l in range(depth):
        h, k_new, v_new, fin = _layer(h, batch, seq, mod[l, 0:1], layers[l], k_new, v_new, l,
                                      None, None, zero_state, None, alpha)
        ss.append(_lanes_to_states(fin))
    y_prompt = h.reshape(batch, seq, d)
    new_cache_k = k_new.reshape(batch, depth, seq, N_KV_HEADS, HEAD_DIM)
    new_cache_v = v_new.reshape(batch, depth, seq, N_KV_HEADS, HEAD_DIM)
    new_state_ssm = jnp.stack(ss, axis=1)

    rope = _rope_tables(dec_seq) + (_window_bias(), _rope_swap_matrix())
    z = x_sample.reshape(dec_batch * dec_seq, d)
    k_lat = jnp.zeros((dec_batch, 1, dec_seq, D_KV), F32)
    v_lat = jnp.zeros((dec_batch, 1, dec_seq, D_KV), F32)
    for l in range(depth):
        ctx_k = cache_k[:, l].reshape(dec_batch, past_len, D_KV).astype(F32)
        ctx_v = cache_v[:, l].reshape(dec_batch, past_len, D_KV).astype(F32)
        z, k_lat, v_lat, _ = _layer(z, dec_batch, dec_seq, mod[l, 1:1 + dec_batch], layers[l], k_lat, v_lat, 0,
                                    ctx_k, ctx_v, _states_to_lanes(state_ssm[:, l]), rope, alpha)
    y_sample = z.reshape(dec_batch, dec_seq, d)
    return (y_prompt, y_sample, new_cache_k, new_cache_v, new_state_ssm)
```

```python
import functools

import jax
import jax.numpy as jnp
import numpy as np
from jax import lax
from jax.experimental import pallas as pl
from jax.experimental.pallas import tpu as pltpu

F32 = jnp.float32
BF16 = jnp.bfloat16

D_MODEL = 2048
GRID_W = 64
D_SSM = 512
SSM_GROUP = 16
N_SSM_GROUPS = D_SSM // SSM_GROUP
SSM_STATE = 64
HEAD_DIM = 128
N_HEADS = 8
N_KV_HEADS = 2
Q_PER_KV = N_HEADS // N_KV_HEADS
D_ATTN = N_HEADS * HEAD_DIM
D_KV = N_KV_HEADS * HEAD_DIM
WINDOW = 128
ATTN_BLOCK = 128
ROPE_BASE = 10000.0
D_SGU = 512
SGU_CHUNK = 128
SGU_GROUP_CH = 128
N_SGU_GROUPS = D_SGU // SGU_GROUP_CH
N_BRANCH = 3
LN_EPS = 1e-5

D_MG = N_BRANCH * D_MODEL
COL_MG = 0
COL_XA = COL_MG + D_MG
COL_ZA = COL_XA + D_SSM
COL_Q = COL_ZA + D_SSM
COL_K = COL_Q + D_ATTN
COL_V = COL_K + D_KV
COL_ZB = COL_V + D_KV
COL_U = COL_ZB + D_ATTN
COL_VS = COL_U + D_SGU
COL_ZC = COL_VS + D_SGU
D_IN = COL_ZC + D_SGU
W_IN_ROTATE = D_IN - D_MG

SSM_T = 16
SSM_NC = 16
SSM_TC = SSM_T * SSM_GROUP
SSM_SEG = SSM_T * SSM_NC
SSM_SMALL_ROWS = 40
SSM_LANE_GROUPS = 128 // SSM_GROUP
SSM_POW_ROWS = 24

VMEM_LIMIT_BYTES = 56 * 1024 * 1024

INPROJ_TM = 1024
INPROJ_TN = 1024
INPROJ_SUB = 256
ROW_TM = 1024
MERGE_TM = 512
MERGE_SUB = 256


def _params(sem):
    return pltpu.CompilerParams(dimension_semantics=sem, vmem_limit_bytes=VMEM_LIMIT_BYTES)


def _layernorm_f32(x):
    mu = jnp.mean(x, axis=-1, keepdims=True)
    xc = x - mu
    var = jnp.mean(xc * xc, axis=-1, keepdims=True)
    return xc * lax.rsqrt(var + LN_EPS)


def _ada_kernel(cond_ref, w_ref, b_ref, o_ref):
    c = cond_ref[...]
    a = (c * jax.nn.sigmoid(c)).astype(BF16)
    o_ref[...] = jnp.dot(a, w_ref[...].astype(BF16), preferred_element_type=F32) + b_ref[...]


def _ada_mod(cond, w_ada, b_ada):
    depth, d, n = w_ada.shape
    r = cond.shape[0]
    tn = 512
    return pl.pallas_call(
        _ada_kernel,
        out_shape=jax.ShapeDtypeStruct((depth, r, n), F32),
        grid=(depth, n // tn),
        in_specs=[
            pl.BlockSpec((r, d), lambda l, j: (0, 0)),
            pl.BlockSpec((None, d, tn), lambda l, j: (l, 0, j)),
            pl.BlockSpec((None, 1, tn), lambda l, j: (l, 0, j)),
        ],
        out_specs=pl.BlockSpec((None, r, tn), lambda l, j: (l, 0, j)),
        compiler_params=_params(("arbitrary", "arbitrary")),
        name="ada_mod",
    )(cond, w_ada, b_ada.reshape(depth, 1, n))


def _inproj_kernel(x_ref, scale_ref, shift_ref, w_ref, k_in_ref, v_in_ref, proj_ref, k_ref, v_ref, xa_ref, h_scr,
                   *, kv_tile, xa_tile):
    del k_in_ref, v_in_ref
    j = pl.program_id(1)
    tm = x_ref.shape[0]

    assert kv_tile != 0 and xa_tile != 0

    @pl.when(j == 0)
    def _():
        for r0 in range(0, tm, INPROJ_SUB):
            rows = slice(r0, r0 + INPROJ_SUB)
            h = (_layernorm_f32(x_ref[rows, :]) * (1.0 + scale_ref[0]) + shift_ref[0]).astype(BF16)
            h_scr[rows, :] = h
            proj_ref[rows, :] = jnp.dot(h, w_ref[...], preferred_element_type=F32).astype(BF16)

    @pl.when(j != 0)
    def _():
        acc = jnp.dot(h_scr[...], w_ref[...], preferred_element_type=F32)
        proj_ref[...] = acc.astype(BF16)

        @pl.when(j == kv_tile)
        def _():
            k_ref[...] = acc[:, 0:D_KV].reshape(k_ref.shape)
            v_ref[...] = acc[:, D_KV:2 * D_KV].reshape(v_ref.shape)

        @pl.when(j == xa_tile)
        def _():
            xa_ref[...] = acc[:, 0:D_SSM]


def _inproj(x2, scale, shift, w_bf16, layer, n_seq, seq_len, k_arr, v_arr, slot):
    t, d = x2.shape
    n = w_bf16.shape[2]
    tm, tn = INPROJ_TM, INPROJ_TN
    assert t % tm == 0 and n % tn == 0 and COL_K % tn == 0 and COL_XA % tn == 0
    assert 2 * D_KV <= tn and D_SSM <= tn and W_IN_ROTATE % tn == 0
    n_tiles = n // tn
    rot_tiles = W_IN_ROTATE // tn
    if seq_len >= tm:
        assert seq_len % tm == 0
        tiles_per_seq = seq_len // tm
        seq_of = lambda i: i // tiles_per_seq
        kv_spec = pl.BlockSpec((None, None, tm, D_KV), lambda i, j: (i // tiles_per_seq, slot, i % tiles_per_seq, 0))
    else:
        assert tm % seq_len == 0
        seq_of = lambda i: i * (tm // seq_len)
        kv_spec = pl.BlockSpec((tm // seq_len, None, seq_len, D_KV), lambda i, j: (i, slot, 0, 0))
    if scale.shape[0] == 1:
        mod_map = lambda i, j: (0, 0, 0)
    else:
        assert seq_len >= tm
        mod_map = lambda i, j: (seq_of(i), 0, 0)
    any_spec = pl.BlockSpec(memory_space=pl.ANY)
    return pl.pallas_call(
        functools.partial(_inproj_kernel, kv_tile=COL_K // tn, xa_tile=COL_XA // tn),
        out_shape=(jax.ShapeDtypeStruct((t, n), BF16), jax.ShapeDtypeStruct(k_arr.shape, F32),
                   jax.ShapeDtypeStruct(v_arr.shape, F32), jax.ShapeDtypeStruct((t, D_SSM), F32)),
        grid=(t // tm, n // tn),
        in_specs=[
            pl.BlockSpec((tm, d), lambda i, j: (i, 0)),
            pl.BlockSpec((1, 1, d), mod_map),
            pl.BlockSpec((1, 1, d), mod_map),
            pl.BlockSpec((None, d, tn), lambda i, j: (layer, 0, (j + rot_tiles) % n_tiles)),
            any_spec, any_spec,
        ],
        out_specs=(
            pl.BlockSpec((tm, tn), lambda i, j: (i, j)),
            kv_spec, kv_spec,
            pl.BlockSpec((tm, D_SSM), lambda i, j: (i, 0)),
        ),
        input_output_aliases={4: 1, 5: 2},
        scratch_shapes=[pltpu.VMEM((tm, d), BF16)],
        compiler_params=_params(("arbitrary", "arbitrary")),
        name="ln_inproj",
    )(x2, scale, shift, w_bf16, k_arr, v_arr)


def _zoh(lr, li, ls):
    dt = jnp.exp(ls)
    mag = jnp.exp(lr * dt)
    ar = mag * jnp.cos(li * dt)
    ai = mag * jnp.sin(li * dt)
    den = lr * lr + li * li
    nr = ar - 1.0
    return ar, ai, (nr * lr + ai * li) / den, (ai * lr - nr * li) / den


def _cmul(xr, xi, yr, yi):
    return xr * yr - xi * yi, xr * yi + xi * yr


def _cpow(br, bi, expo, n_bits):
    shape = jnp.broadcast_shapes(br.shape, expo.shape)
    pr = jnp.ones(shape, F32)
    pi = jnp.zeros(shape, F32)
    for bit in range(n_bits):
        sel = ((expo >> bit) & 1) == 1
        nr, ni = _cmul(pr, pi, br, bi)
        pr = jnp.where(sel, nr, pr)
        pi = jnp.where(sel, ni, pi)
        if bit + 1 < n_bits:
            br, bi = _cmul(br, bi, br, bi)
    return pr, pi


def _ssm_tables_kernel(laml_ref, bt_ref, ct_ref, ec_ref, ek_ref, wi_ref, ws_ref, we_ref, small_ref):
    p_n, t_n, c_n = SSM_STATE, SSM_T, SSM_GROUP
    p2 = 2 * p_n
    tc = t_n * c_n
    n_bits = 5
    assert max(t_n, SSM_NC) < 2 ** n_bits and SSM_POW_ROWS <= 2 ** n_bits

    def dot(a, b):
        return jnp.dot(a, b, precision=lax.Precision.HIGHEST, preferred_element_type=F32)

    def spread(x, onehot):
        hi = x.astype(BF16)
        rest = x - hi.astype(F32)
        mid = rest.astype(BF16)
        lo = (rest - mid.astype(F32)).astype(BF16)
        return (jnp.dot(hi, onehot, preferred_element_type=F32) + jnp.dot(mid, onehot, preferred_element_type=F32)
                + jnp.dot(lo, onehot, preferred_element_type=F32))

    ar, ai, f_re, f_im = _zoh(laml_ref[0:1, :], laml_ref[1:2, :], laml_ref[2:3, :])
    is_fwd = lax.broadcasted_iota(jnp.int32, (1, p2), 1) < p_n
    bbt_re, bbt_im = _cmul(f_re, f_im, bt_ref[:, 0:p2], bt_ref[:, p2:2 * p2])
    krow = lax.broadcasted_iota(jnp.int32, (SSM_POW_ROWS, 1), 0)
    n_lane = ct_ref.shape[1]
    kclamp = jnp.minimum(lax.broadcasted_iota(jnp.int32, (n_lane, 1), 0), t_n)
    pw_re, pw_im = _cpow(ar, ai, kclamp, n_bits)
    for j in range(t_n):
        pm_re = jnp.where(is_fwd, pw_re[t_n - 1 - j:t_n - j], pw_re[j:j + 1])
        pm_im = jnp.where(is_fwd, pw_im[t_n - 1 - j:t_n - j], pw_im[j:j + 1])
        s_re, s_im = _cmul(bbt_re, bbt_im, pm_re, pm_im)
        ws_ref[j * c_n:(j + 1) * c_n, 0:p2] = s_re.astype(BF16)
        ws_ref[j * c_n:(j + 1) * c_n, p2:2 * p2] = s_im.astype(BF16)
    a1r = pw_re[t_n:t_n + 1]
    a1i = pw_im[t_n:t_n + 1]
    q_re, q_im = _cpow(a1r, a1i, jnp.where(is_fwd, krow, jnp.maximum(SSM_NC - 1 - krow, 0)), n_bits)
    as_re, as_im = _cpow(a1r, a1i, jnp.full((1, 1), SSM_NC, jnp.int32), n_bits)
    small_ref[0:SSM_NC, :] = q_re[0:SSM_NC]
    small_ref[SSM_NC:2 * SSM_NC, :] = q_im[0:SSM_NC]
    small_ref[2 * SSM_NC:SSM_SMALL_ROWS, :] = jnp.concatenate(
        [a1r, a1i, as_re, as_im, jnp.zeros((SSM_SMALL_ROWS - 2 * SSM_NC - 4, p2), F32)], axis=0)

    pt_re_all = pw_re.T
    pt_im_all = pw_im.T
    cpt = []
    for d in range(2):
        pt_re = pt_re_all[d * p_n:(d + 1) * p_n, :]
        pt_im = pt_im_all[d * p_n:(d + 1) * p_n, :]
        c_re = spread(ct_ref[...], ec_ref[d])
        c_im = spread(ct_ref[...], ec_ref[2 + d])
        cpt.append(_cmul(c_re, c_im, spread(pt_re, ek_ref[d]), spread(pt_im, ek_ref[d])))
        e_re, e_im = _cmul(c_re, c_im, spread(pt_re, ek_ref[2 + d]), spread(pt_im, ek_ref[2 + d]))
        we_ref[d * p_n:(d + 1) * p_n, :] = e_re.astype(BF16)
        we_ref[p2 + d * p_n:p2 + (d + 1) * p_n, :] = (-e_im).astype(BF16)
    cpt_re = jnp.concatenate([cpt[0][0], cpt[1][0]], axis=0)
    cpt_im = jnp.concatenate([cpt[0][1], cpt[1][1]], axis=0)
    taps = []
    for sel in (is_fwd, jnp.logical_not(is_fwd)):
        taps.append(dot(jnp.where(sel, bbt_re, 0.0), cpt_re) - dot(jnp.where(sel, bbt_im, 0.0), cpt_im))
    lane = lax.broadcasted_iota(jnp.int32, (c_n, tc), 1)
    for j in range(t_n):
        fwd = jnp.where(lane >= j * c_n, pltpu.roll(taps[0], j * c_n, 1), 0.0)
        bwd = jnp.where(lane < (j + 1) * c_n, pltpu.roll(taps[1], (tc - (t_n - 1 - j) * c_n) % tc, 1), 0.0)
        wi_ref[j * c_n:(j + 1) * c_n, :] = (fwd + bwd).astype(BF16)


def _ssm_table_constants():
    t_n, c_n = SSM_T, SSM_GROUP
    rows = np.arange(128)[:, None]
    tok = (np.arange(t_n * c_n) // c_n)[None, :]
    ch = (np.arange(t_n * c_n) % c_n)[None, :]
    ec = np.stack([(rows == q * c_n + ch) for q in range(4)], axis=0)
    ek = np.stack([rows == tok, rows == t_n - 1 - tok, rows == tok + 1, rows == t_n - tok], axis=0)
    return jnp.asarray(ec.astype(BF16)), jnp.asarray(ek.astype(BF16))


def _ssm_tables(lam_re, lam_im, log_step, b_re, b_im, c_re, c_im):
    l_n, _, g_n, p_n = lam_re.shape
    c_n, t_n = SSM_GROUP, SSM_T
    tc = t_n * c_n
    p2, p4 = 2 * p_n, 4 * p_n
    ls = jnp.broadcast_to(log_step.astype(F32)[..., None], lam_re.shape)

    def lanes(x):
        return x.astype(F32).transpose(0, 2, 1, 3).reshape(l_n, g_n, 1, p2)

    laml = jnp.concatenate([lanes(lam_re), lanes(lam_im), lanes(ls), jnp.zeros((l_n, g_n, 5, p2), F32)], axis=2)

    def b_lanes(x):
        return x.astype(F32).transpose(0, 2, 4, 1, 3).reshape(l_n, g_n, c_n, p2)

    bt = jnp.concatenate([b_lanes(b_re), b_lanes(b_im)], axis=-1)

    def c_lanes(x):
        return x.astype(F32).transpose(0, 2, 4, 1, 3).reshape(l_n, g_n, p_n, 2 * c_n)

    ct = jnp.concatenate([c_lanes(c_re), c_lanes(c_im), jnp.zeros((l_n, g_n, p_n, 128 - 4 * c_n), F32)], axis=-1)
    ec, ek = _ssm_table_constants()

    def per_group(*dims):
        return pl.BlockSpec((None, None) + dims, lambda l, g: (l, g) + (0,) * len(dims))

    def const(shape):
        return pl.BlockSpec(shape, lambda l, g: (0,) * len(shape))

    return pl.pallas_call(
        _ssm_tables_kernel,
        out_shape=(jax.ShapeDtypeStruct((l_n, g_n, tc, tc), BF16), jax.ShapeDtypeStruct((l_n, g_n, tc, p4), BF16),
                   jax.ShapeDtypeStruct((l_n, g_n, p4, tc), BF16),
                   jax.ShapeDtypeStruct((l_n, g_n, SSM_SMALL_ROWS, p2), F32)),
        grid=(l_n, g_n),
        in_specs=[per_group(8, p2), per_group(c_n, p4), per_group(p_n, 128),
                  const(ec.shape), const(ek.shape)],
        out_specs=(per_group(tc, tc), per_group(tc, p4), per_group(p4, tc), per_group(SSM_SMALL_ROWS, p2)),
        compiler_params=_params(("arbitrary", "arbitrary")),
        name="ssm_tables",
    )(laml, bt, ct, ec, ek)


def _ssm_fold_matrix():
    half_t = SSM_T // 2
    n = half_t * SSM_LANE_GROUPS * SSM_GROUP
    src = np.arange(n)
    j = src // (SSM_LANE_GROUPS * SSM_GROUP)
    g = (src // SSM_GROUP) % SSM_LANE_GROUPS
    c = src % SSM_GROUP
    dst = g * (half_t * SSM_GROUP) + j * SSM_GROUP + c
    return jnp.asarray((dst[:, None] == np.arange(n)[None, :]).astype(BF16))


def _ssm_kernel(xa_ref, fold_ref, wi_ref, ws_ref, we_ref, small_ref, s0_ref, y_ref, fin_ref,
                v_scr, s_scr, yy_scr, fre_scr, fim_scr, efre_scr, efim_scr, ebre_scr, ebim_scr,
                *, n_seq, n_seg):
    p2 = 2 * SSM_STATE
    p4 = 4 * SSM_STATE
    nbp = n_seq * n_seg
    lanes = SSM_LANE_GROUPS * SSM_GROUP
    nt = (((1,), (1,)), ((), ()))

    blocks = []
    for c in range(SSM_NC):
        toks = [xa_ref[pl.ds(c * SSM_T + j, nbp, stride=SSM_SEG), :].astype(BF16) for j in range(SSM_T)]
        blocks.append(jnp.concatenate(toks, axis=1))
    xx = jnp.concatenate(blocks, axis=0)
    hw = fold_ref.shape[0]
    uu = [jnp.dot(xx[:, h * hw:(h + 1) * hw], fold_ref[...], preferred_element_type=F32).astype(BF16)
          for h in range(2)]

    lane = lax.broadcasted_iota(jnp.int32, (1, p2), 1)
    is_fwd = lane < SSM_STATE
    for g in range(SSM_LANE_GROUPS):
        gc = slice(g * SSM_TC, (g + 1) * SSM_TC)
        u = jnp.concatenate([uu[h][:, g * lanes:(g + 1) * lanes] for h in range(2)], axis=1)
        yy_scr[:, gc] = jnp.dot(u, wi_ref[g], preferred_element_type=F32)
        v_scr[:, gc] = jnp.dot(u, ws_ref[g], preferred_element_type=F32)

    for g in range(SSM_LANE_GROUPS):
        o = g * p4
        a1r = small_ref[g, 2 * SSM_NC:2 * SSM_NC + 1, :]
        a1i = small_ref[g, 2 * SSM_NC + 1:2 * SSM_NC + 2, :]
        asr = small_ref[g, 2 * SSM_NC + 2:2 * SSM_NC + 3, :]
        asi = small_ref[g, 2 * SSM_NC + 3:2 * SSM_NC + 4, :]

        st_re = jnp.zeros((nbp, p2), F32)
        st_im = jnp.zeros((nbp, p2), F32)
        for c in range(SSM_NC):
            cb = SSM_NC - 1 - c
            rf = slice(c * nbp, (c + 1) * nbp)
            rb = slice(cb * nbp, (cb + 1) * nbp)
            s_scr[rf, o:o + SSM_STATE] = st_re[:, 0:SSM_STATE]
            s_scr[rb, o + SSM_STATE:o + p2] = st_re[:, SSM_STATE:p2]
            s_scr[rf, o + p2:o + p2 + SSM_STATE] = st_im[:, 0:SSM_STATE]
            s_scr[rb, o + p2 + SSM_STATE:o + p4] = st_im[:, SSM_STATE:p2]
            v_re = jnp.where(is_fwd, v_scr[rf, o:o + p2], v_scr[rb, o:o + p2])
            v_im = jnp.where(is_fwd, v_scr[rf, o + p2:o + p4], v_scr[rb, o + p2:o + p4])
            st_re, st_im = (a1r * st_re - a1i * st_im + v_re, a1r * st_im + a1i * st_re + v_im)

        fre_scr[...] = st_re
        fim_scr[...] = st_im
        s0 = s0_ref[g]
        e_re = s0[:, 0:p2]
        e_im = s0[:, p2:p4]
        for k in range(n_seg):
            kb = n_seg - 1 - k
            rows_f = pl.ds(k, n_seq, stride=n_seg)
            rows_b = pl.ds(kb, n_seq, stride=n_seg)
            efre_scr[rows_f, :] = e_re
            efim_scr[rows_f, :] = e_im
            ebre_scr[rows_b, :] = e_re
            ebim_scr[rows_b, :] = e_im
            f_re = jnp.where(is_fwd, fre_scr[rows_f, :], fre_scr[rows_b, :])
            f_im = jnp.where(is_fwd, fim_scr[rows_f, :], fim_scr[rows_b, :])
            e_re, e_im = (asr * e_re - asi * e_im + f_re, asr * e_im + asi * e_re + f_im)
        fin_ref[g] = jnp.concatenate([e_re, e_im], axis=-1)

        en_re = jnp.where(is_fwd, efre_scr[...], ebre_scr[...])
        en_im = jnp.where(is_fwd, efim_scr[...], ebim_scr[...])
        for c in range(SSM_NC):
            rows = slice(c * nbp, (c + 1) * nbp)
            cr = small_ref[g, c:c + 1, :]
            ci = small_ref[g, SSM_NC + c:SSM_NC + c + 1, :]
            s_scr[rows, o:o + p2] = s_scr[rows, o:o + p2] + (cr * en_re - ci * en_im)
            s_scr[rows, o + p2:o + p4] = s_scr[rows, o + p2:o + p4] + (cr * en_im + ci * en_re)

    for g in range(SSM_LANE_GROUPS):
        gc = slice(g * SSM_TC, (g + 1) * SSM_TC)
        yy_scr[:, gc] += jnp.dot(s_scr[:, g * p4:(g + 1) * p4].astype(BF16), we_ref[g],
                                 preferred_element_type=F32)

    half_t = SSM_T // 2
    for h in range(2):
        yh = jnp.concatenate([yy_scr[:, g * SSM_TC + h * lanes:g * SSM_TC + (h + 1) * lanes]
                              for g in range(SSM_LANE_GROUPS)], axis=1).astype(BF16)
        zz = lax.dot_general(yh, fold_ref[...], nt, preferred_element_type=F32)
        for c in range(SSM_NC):
            for j in range(half_t):
                y_ref[pl.ds(c * SSM_T + h * half_t + j, nbp, stride=SSM_SEG), :] = zz[c * nbp:(c + 1) * nbp,
                                                                                       j * lanes:(j + 1) * lanes]


def _ssm_scan(xa, fold, w_intra, w_state, w_enter, small, layer, s0, n_seq, n_seg):
    t = xa.shape[0]
    g_n = N_SSM_GROUPS
    lg = SSM_LANE_GROUPS
    lanes = lg * SSM_GROUP
    p4 = 4 * SSM_STATE
    nbp = n_seq * n_seg
    r = SSM_NC * nbp
    assert t == r * SSM_T and (SSM_T // 2) * SSM_GROUP == lanes and fold.shape == (lg * lanes, lg * lanes)
    const = lambda shape: pl.BlockSpec(shape, lambda i: (0,) * len(shape), pipeline_mode=pl.Buffered(1))
    return pl.pallas_call(
        functools.partial(_ssm_kernel, n_seq=n_seq, n_seg=n_seg),
        out_shape=(jax.ShapeDtypeStruct((t, D_SSM), F32), jax.ShapeDtypeStruct((g_n, n_seq, p4), F32)),
        grid=(g_n // lg,),
        in_specs=[
            pl.BlockSpec((t, lanes), lambda i: (0, i)),
            const(fold.shape),
            pl.BlockSpec((None, lg, SSM_TC, SSM_TC), lambda i: (layer, i, 0, 0)),
            pl.BlockSpec((None, lg, SSM_TC, p4), lambda i: (layer, i, 0, 0)),
            pl.BlockSpec((None, lg, p4, SSM_TC), lambda i: (layer, i, 0, 0)),
            pl.BlockSpec((None, lg, SSM_SMALL_ROWS, 2 * SSM_STATE), lambda i: (layer, i, 0, 0)),
            pl.BlockSpec((lg, n_seq, p4), lambda i: (i, 0, 0)),
        ],
        out_specs=(
            pl.BlockSpec((t, lanes), lambda i: (0, i)),
            pl.BlockSpec((lg, n_seq, p4), lambda i: (i, 0, 0)),
        ),
        scratch_shapes=[pltpu.VMEM((r, lg * p4), F32), pltpu.VMEM((r, lg * p4), F32),
                        pltpu.VMEM((r, lg * SSM_TC), F32)]
        + [pltpu.VMEM((nbp, 2 * SSM_STATE), F32)] * 6,
        compiler_params=_params(("arbitrary",)),
        name="ssm_scan",
    )(xa, fold, w_intra, w_state, w_enter, small, s0)


def _ssm_post_kernel(xa_ref, za_ref, ys_ref, d_ref, w_ref, b_ref, o_ref):
    y = xa_ref[...].astype(F32) * d_ref[...] + ys_ref[...]
    y = jax.nn.gelu(y)
    gl = jnp.dot(y.astype(BF16), w_ref[...], preferred_element_type=F32) + b_ref[...]
    y = y * jax.nn.sigmoid(gl)
    o_ref[...] = (y * jax.nn.silu(za_ref[...].astype(F32))).astype(BF16)


def _ssm_post(proj, yssm, d_skip, w_glu_bf16, b_glu):
    t = proj.shape[0]
    tm = ROW_TM
    return pl.pallas_call(
        _ssm_post_kernel,
        out_shape=jax.ShapeDtypeStruct((t, D_SSM), BF16),
        grid=(t // tm,),
        in_specs=[
            pl.BlockSpec((tm, D_SSM), lambda i: (i, COL_XA // D_SSM)),
            pl.BlockSpec((tm, D_SSM), lambda i: (i, COL_ZA // D_SSM)),
            pl.BlockSpec((tm, D_SSM), lambda i: (i, 0)),
            pl.BlockSpec((1, D_SSM), lambda i: (0, 0)),
            pl.BlockSpec((D_SSM, D_SSM), lambda i: (0, 0)),
            pl.BlockSpec((1, D_SSM), lambda i: (0, 0)),
        ],
        out_specs=pl.BlockSpec((tm, D_SSM), lambda i: (i, 0)),
        compiler_params=_params(("arbitrary",)),
        name="ssm_post",
    )(proj, proj, yssm, d_skip.reshape(1, D_SSM), w_glu_bf16, b_glu.reshape(1, D_SSM))


def _softmax_pv_wide(s, sink, v_bf16):
    hd = v_bf16.shape[1]
    m = jnp.maximum(jnp.max(s, axis=-1, keepdims=True), sink)
    p = jnp.exp(s - m).astype(BF16)
    v_ones = jnp.concatenate([v_bf16, jnp.ones_like(v_bf16)], axis=1)
    o = jnp.dot(p, v_ones, preferred_element_type=F32)
    den = o[:, hd:] + jnp.exp(sink - m)
    return o[:, :hd] / den


def _head_gate(zb_refs, h):
    per = N_HEADS // 2
    z = zb_refs[h // per][:, (h % per) * HEAD_DIM:(h % per + 1) * HEAD_DIM].astype(F32)
    return jax.nn.silu(z)


def _ctx_attn_kernel(sink_ref, q_ref, k_ref, v_ref, zb0_ref, zb1_ref, o_ref):
    scale = HEAD_DIM ** -0.5
    nt = (((1,), (1,)), ((), ()))
    for kvh in range(N_KV_HEADS):
        k = (k_ref[:, kvh * HEAD_DIM:(kvh + 1) * HEAD_DIM] * scale).astype(BF16)
        v = v_ref[:, kvh * HEAD_DIM:(kvh + 1) * HEAD_DIM].astype(BF16)
        for g in range(Q_PER_KV):
            h = kvh * Q_PER_KV + g
            hs = slice(h * HEAD_DIM, (h + 1) * HEAD_DIM)
            s = lax.dot_general(q_ref[:, hs], k, nt, preferred_element_type=F32)
            o = _softmax_pv_wide(s, sink_ref[h], v)
            o_ref[:, hs] = (o * _head_gate((zb0_ref, zb1_ref), h)).astype(BF16)


def _ctx_attention(proj, k_arr, v_arr, slot, sink, seq_len):
    t = proj.shape[0]
    kv_spec = pl.BlockSpec((None, None, seq_len, D_KV), lambda b: (b, slot, 0, 0))
    return pl.pallas_call(
        _ctx_attn_kernel,
        out_shape=jax.ShapeDtypeStruct((t, D_ATTN), BF16),
        grid=(t // seq_len,),
        in_specs=[
            pl.BlockSpec(memory_space=pltpu.SMEM),
            pl.BlockSpec((seq_len, D_ATTN), lambda b: (b, COL_Q // D_ATTN)),
            kv_spec, kv_spec,
            pl.BlockSpec((seq_len, D_ATTN // 2), lambda b: (b, COL_ZB // (D_ATTN // 2))),
            pl.BlockSpec((seq_len, D_ATTN // 2), lambda b: (b, COL_ZB // (D_ATTN // 2) + 1)),
        ],
        out_specs=pl.BlockSpec((seq_len, D_ATTN), lambda b: (b, 0)),
        compiler_params=_params(("arbitrary",)),
        name="ctx_attention",
    )(sink, proj, k_arr, v_arr, proj, proj)


def _rope_partner(x):
    lane = lax.broadcasted_iota(jnp.int32, x.shape, 1)
    quarter = HEAD_DIM // 4
    first = (lane % (2 * quarter)) < quarter
    return jnp.where(first, pltpu.roll(x, HEAD_DIM - quarter, 1), pltpu.roll(x, quarter, 1))


def _rope(x, tab_ref):
    return x * tab_ref[:, 0:HEAD_DIM] + _rope_partner(x) * tab_ref[:, HEAD_DIM:2 * HEAD_DIM]


def _rope_kv_kernel(k_ref, v_ref, tab_ref, o_ref):
    for kvh in range(N_KV_HEADS):
        ks = slice(kvh * HEAD_DIM, (kvh + 1) * HEAD_DIM)
        o_ref[:, ks] = _rope(k_ref[:, ks], tab_ref).astype(BF16)
    o_ref[:, D_KV:2 * D_KV] = v_ref[...].astype(BF16)


def _rope_kv(k_arr, v_arr, slot, tab_k):
    n_seq, _, seq_len, _ = k_arr.shape
    tm = ROW_TM
    assert seq_len % tm == 0
    tiles_per_seq = seq_len // tm
    kv_spec = pl.BlockSpec((None, None, tm, D_KV), lambda i: (i // tiles_per_seq, slot, i % tiles_per_seq, 0))
    return pl.pallas_call(
        _rope_kv_kernel,
        out_shape=jax.ShapeDtypeStruct((n_seq * seq_len, 2 * D_KV), BF16),
        grid=(n_seq * tiles_per_seq,),
        in_specs=[
            kv_spec, kv_spec,
            pl.BlockSpec((tm, 2 * HEAD_DIM), lambda i: (i % tiles_per_seq, 0)),
        ],
        out_specs=pl.BlockSpec((tm, 2 * D_KV), lambda i: (i, 0)),
        compiler_params=_params(("arbitrary",)),
        name="rope_kv",
    )(k_arr, v_arr, tab_k)


def _lat_attn_kernel(sink_ref, q_ref, kvp_ref, kvc_ref, kvn_ref, ck_ref, cv_ref, zb0_ref, zb1_ref,
                     tabq_ref, swap_ref, bias_ref, o_ref):
    nt = (((1,), (1,)), ((), ()))
    blk = ATTN_BLOCK
    cos_q = tabq_ref[:, 0:HEAD_DIM]
    sin_q = tabq_ref[:, HEAD_DIM:2 * HEAD_DIM]
    for kvh in range(N_KV_HEADS):
        ks = slice(kvh * HEAD_DIM, (kvh + 1) * HEAD_DIM)
        vs = slice(D_KV + kvh * HEAD_DIM, D_KV + (kvh + 1) * HEAD_DIM)
        keys = [jnp.concatenate([kvp_ref[:, ks], kvc_ref[:, ks], kvn_ref[:, ks]], axis=0),
                ck_ref[:, ks].astype(BF16)]
        values = [jnp.concatenate([kvp_ref[:, vs], kvc_ref[:, vs], kvn_ref[:, vs]], axis=0),
                  cv_ref[:, ks].astype(BF16)]
        qx = jnp.concatenate(
            [q_ref[:, (kvh * Q_PER_KV + g) * HEAD_DIM:(kvh * Q_PER_KV + g + 1) * HEAD_DIM]
             for g in range(Q_PER_KV)], axis=0)
        partner = jnp.dot(qx, swap_ref[...], preferred_element_type=F32)
        qf = qx.astype(F32).reshape(Q_PER_KV, blk, HEAD_DIM)
        q4 = (qf * cos_q[None] + partner.reshape(Q_PER_KV, blk, HEAD_DIM) * sin_q[None])
        q4 = q4.reshape(Q_PER_KV * blk, HEAD_DIM).astype(BF16)
        s_loc = lax.dot_general(q4, keys[0], nt, preferred_element_type=F32)
        s_loc = (s_loc.reshape(Q_PER_KV, blk, 3 * blk) + bias_ref[...][None]).reshape(Q_PER_KV * blk, 3 * blk)
        s_ctx = lax.dot_general(q4, keys[1], nt, preferred_element_type=F32)
        sink_rows = jnp.concatenate(
            [jnp.full((blk, 1), sink_ref[kvh * Q_PER_KV + g], F32) for g in range(Q_PER_KV)], axis=0)
        o = _softmax_pv_wide(jnp.concatenate([s_loc, s_ctx], axis=-1), sink_rows,
                             jnp.concatenate(values, axis=0))
        for g in range(Q_PER_KV):
            h = kvh * Q_PER_KV + g
            o_ref[:, h * HEAD_DIM:(h + 1) * HEAD_DIM] = (
                o[g * blk:(g + 1) * blk] * _head_gate((zb0_ref, zb1_ref), h)).astype(BF16)


def _lat_attention(proj, k_arr, v_arr, slot, ctx_k, ctx_v, sink, tab_q, tab_k, bias, swap, n_seq, seq_len):
    kv = _rope_kv(k_arr, v_arr, slot, tab_k)
    t = proj.shape[0]
    blk = ATTN_BLOCK
    assert blk - 1 <= WINDOW <= blk
    nb = seq_len // blk
    lc = ctx_k.shape[1]

    def row(b, i):
        return b * nb + i

    def prev(i):
        return jnp.maximum(i - 1, 0)

    def nxt(i):
        return jnp.minimum(i + 1, nb - 1)

    def bias_variant(b, i):
        return ((i > 0).astype(jnp.int32) * 2 + (i < nb - 1).astype(jnp.int32), 0, 0)

    return pl.pallas_call(
        _lat_attn_kernel,
        out_shape=jax.ShapeDtypeStruct((t, D_ATTN), BF16),
        grid=(n_seq, nb),
        in_specs=[
            pl.BlockSpec(memory_space=pltpu.SMEM),
            pl.BlockSpec((blk, D_ATTN), lambda b, i: (row(b, i), COL_Q // D_ATTN)),
            pl.BlockSpec((blk, 2 * D_KV), lambda b, i: (row(b, prev(i)), 0)),
            pl.BlockSpec((blk, 2 * D_KV), lambda b, i: (row(b, i), 0)),
            pl.BlockSpec((blk, 2 * D_KV), lambda b, i: (row(b, nxt(i)), 0)),
            pl.BlockSpec((None, lc, D_KV), lambda b, i: (b, 0, 0)),
            pl.BlockSpec((None, lc, D_KV), lambda b, i: (b, 0, 0)),
            pl.BlockSpec((blk, D_ATTN // 2), lambda b, i: (row(b, i), COL_ZB // (D_ATTN // 2))),
            pl.BlockSpec((blk, D_ATTN // 2), lambda b, i: (row(b, i), COL_ZB // (D_ATTN // 2) + 1)),
            pl.BlockSpec((blk, 2 * HEAD_DIM), lambda b, i: (i, 0)),
            pl.BlockSpec((HEAD_DIM, HEAD_DIM), lambda b, i: (0, 0)),
            pl.BlockSpec((None, blk, 3 * blk), bias_variant),
        ],
        out_specs=pl.BlockSpec((blk, D_ATTN), lambda b, i: (row(b, i), 0)),
        compiler_params=_params(("arbitrary", "arbitrary")),
        name="lat_attention",
    )(sink, proj, kv, kv, kv, ctx_k, ctx_v, proj, proj, tab_q, swap, bias)


def _rope_tables(seq_len):
    rows = seq_len // GRID_W
    row = np.repeat(np.arange(rows), GRID_W).astype(np.float64)
    col = np.tile(np.arange(GRID_W), rows).astype(np.float64)
    half = HEAD_DIM // 2
    inv = ROPE_BASE ** (-np.arange(0, half, 2, dtype=np.float64) / half)
    ang_r = row[:, None] * inv[None, :]
    ang_c = col[:, None] * inv[None, :]
    cos_t = np.concatenate([np.cos(ang_r)] * 2 + [np.cos(ang_c)] * 2, axis=-1)
    sin_t = np.concatenate([-np.sin(ang_r), np.sin(ang_r), -np.sin(ang_c), np.sin(ang_c)], axis=-1)
    tab_k = np.concatenate([cos_t, sin_t], axis=-1)
    tab_q = tab_k * HEAD_DIM ** -0.5
    return jnp.asarray(tab_q.astype(np.float32)), jnp.asarray(tab_k.astype(np.float32))


def _rope_swap_matrix():
    quarter = HEAD_DIM // 4
    d = np.arange(HEAD_DIM)
    src = np.where(d % (2 * quarter) < quarter, d + quarter, d - quarter)
    return jnp.asarray((np.arange(HEAD_DIM)[:, None] == src[None, :]).astype(BF16))


def _window_bias():
    blk = ATTN_BLOCK
    r = np.arange(blk)[:, None]
    s = np.arange(3 * blk)[None, :]
    band = np.abs(s - blk - r) <= WINDOW
    out = []
    for has_prev in (False, True):
        for has_next in (False, True):
            ok = band & ((s >= blk) | has_prev) & ((s < 2 * blk) | has_next)
            out.append(np.where(ok, 0.0, -np.inf).astype(np.float32))
    return jnp.asarray(np.stack(out, axis=0))


def _sgu_kernel(u_ref, v_ref, z_ref, g_ref, b_ref, ws_ref, bs_ref, o_ref, *, n_chunks):
    for n in range(n_chunks):
        rows = slice(n * SGU_CHUNK, (n + 1) * SGU_CHUNK)
        v = _layernorm_f32(jax.nn.gelu(v_ref[rows, :].astype(F32))) * g_ref[...] + b_ref[...]
        v = v.astype(BF16)
        for g in range(N_SGU_GROUPS):
            cols = slice(g * SGU_GROUP_CH, (g + 1) * SGU_GROUP_CH)
            vm = jnp.dot(ws_ref[g], v[:, cols], preferred_element_type=F32) + bs_ref[:, g:g + 1]
            u = jax.nn.gelu(u_ref[rows, cols].astype(F32))
            z = z_ref[rows, cols].astype(F32)
            o_ref[rows, cols] = (u * vm * jax.nn.silu(z)).astype(BF16)


def _sgu(proj, ln_g, ln_b, w_s_bf16, b_s_t):
    t = proj.shape[0]
    tm = ROW_TM
    return pl.pallas_call(
        functools.partial(_sgu_kernel, n_chunks=tm // SGU_CHUNK),
        out_shape=jax.ShapeDtypeStruct((t, D_SGU), BF16),
        grid=(t // tm,),
        in_specs=[
            pl.BlockSpec((tm, D_SGU), lambda i: (i, COL_U // D_SGU)),
            pl.BlockSpec((tm, D_SGU), lambda i: (i, COL_VS // D_SGU)),
            pl.BlockSpec((tm, D_SGU), lambda i: (i, COL_ZC // D_SGU)),
            pl.BlockSpec((1, D_SGU), lambda i: (0, 0)),
            pl.BlockSpec((1, D_SGU), lambda i: (0, 0)),
            pl.BlockSpec((N_SGU_GROUPS, SGU_CHUNK, SGU_CHUNK), lambda i: (0, 0, 0)),
            pl.BlockSpec((SGU_CHUNK, N_SGU_GROUPS), lambda i: (0, 0)),
        ],
        out_specs=pl.BlockSpec((tm, D_SGU), lambda i: (i, 0)),
        compiler_params=_params(("arbitrary",)),
        name="sgu",
    )(proj, proj, proj, ln_g.reshape(1, D_SGU), ln_b.reshape(1, D_SGU), w_s_bf16, b_s_t)


def _merge_kernel(mg_ref, ya_ref, yb_ref, yc_ref, x_ref, gate_ref, wpa_ref, wpb_ref, wpc_ref, wout_ref,
                  lng_ref, lnb_ref, y_ref, *, alpha):
    d = D_MODEL
    tm = y_ref.shape[0]
    for r0 in range(0, tm, MERGE_SUB):
        rows = slice(r0, r0 + MERGE_SUB)

        def branch(k, y_r, w_r):
            gate = jax.nn.sigmoid(mg_ref[rows, k * d:(k + 1) * d].astype(F32))
            return gate * jnp.dot(y_r[rows, :], w_r[...], preferred_element_type=F32)

        merged = branch(0, ya_ref, wpa_ref) + branch(1, yb_ref, wpb_ref) + branch(2, yc_ref, wpc_ref)
        out = jnp.dot(merged.astype(BF16), wout_ref[...], preferred_element_type=F32)
        z = alpha * x_ref[rows, :] + gate_ref[0] * out
        y_ref[rows, :] = _layernorm_f32(z) * lng_ref[...] + lnb_ref[...]


def _merge(proj, ya, yb, yc, x2, gate, wpa, wpb, wpc, wout, ln_g, ln_b, seq_len, alpha):
    t, d = x2.shape
    tm = MERGE_TM
    if gate.shape[0] == 1:
        mod_map = lambda i: (0, 0, 0)
    else:
        assert seq_len % tm == 0
        tiles_per_seq = seq_len // tm
        mod_map = lambda i: (i // tiles_per_seq, 0, 0)

    def const(shape):
        return pl.BlockSpec(shape, lambda i: (0,) * len(shape), pipeline_mode=pl.Buffered(1))

    return pl.pallas_call(
        functools.partial(_merge_kernel, alpha=alpha),
        out_shape=jax.ShapeDtypeStruct((t, d), F32),
        grid=(t // tm,),
        in_specs=[
            pl.BlockSpec((tm, D_MG), lambda i: (i, 0)),
            pl.BlockSpec((tm, D_SSM), lambda i: (i, 0)),
            pl.BlockSpec((tm, D_ATTN), lambda i: (i, 0)),
            pl.BlockSpec((tm, D_SGU), lambda i: (i, 0)),
            pl.BlockSpec((tm, d), lambda i: (i, 0)),
            pl.BlockSpec((1, 1, d), mod_map),
            const((D_SSM, d)), const((D_ATTN, d)), const((D_SGU, d)), const((d, d)),
            const((1, d)), const((1, d)),
        ],
        out_specs=pl.BlockSpec((tm, d), lambda i: (i, 0)),
        compiler_params=_params(("arbitrary",)),
        name="merge_out",
    )(proj, ya, yb, yc, x2, gate, wpa, wpb, wpc, wout, ln_g.reshape(1, d), ln_b.reshape(1, d))


def _layer(x2, n_seq, seq_len, mod, lp, k_arr, v_arr, slot, ctx_k, ctx_v, s0, rope, alpha):
    d = D_MODEL
    nbm = mod.shape[0]
    shift = mod[:, 0:d].reshape(nbm, 1, d)
    scale = mod[:, d:2 * d].reshape(nbm, 1, d)
    gate = mod[:, 2 * d:3 * d].reshape(nbm, 1, d)
    proj, k_arr, v_arr, xa = _inproj(x2, scale, shift, lp['w_in'], lp['layer'], n_seq, seq_len, k_arr, v_arr, slot)

    n_seg = seq_len // SSM_SEG
    yssm, fin = _ssm_scan(xa, lp['fold'], lp['w_intra'], lp['w_state'], lp['w_enter'], lp['ssm_small'],
                          lp['layer'], s0, n_seq, n_seg)
    ya = _ssm_post(proj, yssm, lp['d_skip'], lp['w_glu'], lp['b_glu'])

    if ctx_k is None:
        yb = _ctx_attention(proj, k_arr, v_arr, slot, lp['sink'], seq_len)
    else:
        yb = _lat_attention(proj, k_arr, v_arr, slot, ctx_k, ctx_v, lp['sink'], *rope, n_seq, seq_len)

    yc = _sgu(proj, lp['sgu_g'], lp['sgu_b'], lp['w_s'], lp['b_s_t'])

    y = _merge(proj, ya, yb, yc, x2, gate, lp['w_pa'], lp['w_pb'], lp['w_pc'], lp['w_out'],
               lp['ln_g'], lp['ln_b'], seq_len, alpha)
    return y, k_arr, v_arr, fin


def _states_to_lanes(s):
    b = s.shape[0]
    return s.astype(F32).transpose(3, 0, 2, 1, 4).reshape(N_SSM_GROUPS, b, 4 * SSM_STATE)


def _lanes_to_states(f):
    b = f.shape[1]
    return f.reshape(N_SSM_GROUPS, b, 2, 2, SSM_STATE).transpose(1, 3, 2, 0, 4)


def kernel(x_prompt, x_sample, cache_k, cache_v, state_ssm, c, c_ctx, w_ada, b_ada, w_in, ssm_lam_re, ssm_lam_im, ssm_log_step, ssm_b_re, ssm_b_im, ssm_c_re, ssm_c_im, ssm_d, w_glu, b_glu, attn_sink, sgu_ln_g, sgu_ln_b, w_spatial, b_spatial, w_proj_a, w_proj_b, w_proj_c, w_out, ln_g, ln_b):
    depth = w_in.shape[0]
    batch, seq, d = x_prompt.shape
    dec_batch, dec_seq, _ = x_sample.shape
    past_len = cache_k.shape[2]
    alpha = (2 * depth) ** 0.25

    n_cond = 1 + dec_batch
    cond_rows = -(-n_cond // 8) * 8
    cond = jnp.concatenate([c_ctx[None, :], c, jnp.zeros((cond_rows - n_cond, d), F32)], axis=0)
    mod = _ada_mod(cond, w_ada, b_ada)

    w_in_bf16 = w_in.astype(BF16)
    fold = _ssm_fold_matrix()
    tables = _ssm_tables(ssm_lam_re, ssm_lam_im, ssm_log_step, ssm_b_re, ssm_b_im, ssm_c_re, ssm_c_im)
    layers = []
    for l in range(depth):
        w_intra, w_state, w_enter, small = tables
        layers.append({
            'w_in': w_in_bf16, 'layer': l,
            'fold': fold, 'w_intra': w_intra, 'w_state': w_state, 'w_enter': w_enter, 'ssm_small': small,
            'd_skip': ssm_d[l], 'w_glu': w_glu[l].astype(BF16), 'b_glu': b_glu[l],
            'sink': attn_sink[l].astype(F32),
            'sgu_g': sgu_ln_g[l], 'sgu_b': sgu_ln_b[l],
            'w_s': w_spatial[l].astype(BF16), 'b_s_t': b_spatial[l].T.astype(F32),
            'w_pa': w_proj_a[l].astype(BF16), 'w_pb': w_proj_b[l].astype(BF16),
            'w_pc': w_proj_c[l].astype(BF16), 'w_out': w_out[l].astype(BF16),
            'ln_g': ln_g[l], 'ln_b': ln_b[l],
        })

    h = x_prompt.reshape(batch * seq, d)
    zero_state = jnp.zeros((N_SSM_GROUPS, batch, 4 * SSM_STATE), F32)
    k_new = jnp.zeros((batch, depth, seq, D_KV), F32)
    v_new = jnp.zeros((batch, depth, seq, D_KV), F32)
    ss = []
    for l in range(depth):
        h, k_new, v_new, fin = _layer(h, batch, seq, mod[l, 0:1], layers[l], k_new, v_new, l,
                                      None, None, zero_state, None, alpha)
        ss.append(_lanes_to_states(fin))
    y_prompt = h.reshape(batch, seq, d)
    new_cache_k = k_new.reshape(batch, depth, seq, N_KV_HEADS, HEAD_DIM)
    new_cache_v = v_new.reshape(batch, depth, seq, N_KV_HEADS, HEAD_DIM)
    new_state_ssm = jnp.stack(ss, axis=1)

    rope = _rope_tables(dec_seq) + (_window_bias(), _rope_swap_matrix())
    z = x_sample.reshape(dec_batch * dec_seq, d)
    k_lat = jnp.zeros((dec_batch, 1, dec_seq, D_KV), F32)
    v_lat = jnp.zeros((dec_batch, 1, dec_seq, D_KV), F32)
    for l in range(depth):
        ctx_k = cache_k[:, l].reshape(dec_batch, past_len, D_KV).astype(F32)
        ctx_v = cache_v[:, l].reshape(dec_batch, past_len, D_KV).astype(F32)
        z, k_lat, v_lat, _ = _layer(z, dec_batch, dec_seq, mod[l, 1:1 + dec_batch], layers[l], k_lat, v_lat, 0,
                                    ctx_k, ctx_v, _states_to_lanes(state_ssm[:, l]), rope, alpha)
    y_sample = z.reshape(dec_batch, dec_seq, d)
    return (y_prompt, y_sample, new_cache_k, new_cache_v, new_state_ssm)
```

```python
import functools

import jax
import jax.numpy as jnp
import numpy as np
from jax import lax
from jax.experimental import pallas as pl
from jax.experimental.pallas import tpu as pltpu

F32 = jnp.float32
BF16 = jnp.bfloat16

D_MODEL = 2048
GRID_W = 64
D_SSM = 512
SSM_GROUP = 16
N_SSM_GROUPS = D_SSM // SSM_GROUP
SSM_STATE = 64
HEAD_DIM = 128
N_HEADS = 8
N_KV_HEADS = 2
Q_PER_KV = N_HEADS // N_KV_HEADS
D_ATTN = N_HEADS * HEAD_DIM
D_KV = N_KV_HEADS * HEAD_DIM
WINDOW = 128
ATTN_BLOCK = 128
ROPE_BASE = 10000.0
D_SGU = 512
SGU_CHUNK = 128
SGU_GROUP_CH = 128
N_SGU_GROUPS = D_SGU // SGU_GROUP_CH
N_BRANCH = 3
LN_EPS = 1e-5

D_MG = N_BRANCH * D_MODEL
COL_MG = 0
COL_XA = COL_MG + D_MG
COL_ZA = COL_XA + D_SSM
COL_Q = COL_ZA + D_SSM
COL_K = COL_Q + D_ATTN
COL_V = COL_K + D_KV
COL_ZB = COL_V + D_KV
COL_U = COL_ZB + D_ATTN
COL_VS = COL_U + D_SGU
COL_ZC = COL_VS + D_SGU
D_IN = COL_ZC + D_SGU
W_IN_ROTATE = D_IN - D_MG

SSM_T = 16
SSM_NC = 16
SSM_TC = SSM_T * SSM_GROUP
SSM_SEG = SSM_T * SSM_NC
SSM_SMALL_ROWS = 40
SSM_LANE_GROUPS = 128 // SSM_GROUP
SSM_POW_ROWS = 24

VMEM_LIMIT_BYTES = 56 * 1024 * 1024

INPROJ_TM = 1024
INPROJ_TN = 1024
INPROJ_SUB = 256
ROW_TM = 2048
MERGE_TM = 512
MERGE_SUB = 256


def _params(sem):
    return pltpu.CompilerParams(dimension_semantics=sem, vmem_limit_bytes=VMEM_LIMIT_BYTES)


def _layernorm_f32(x):
    mu = jnp.mean(x, axis=-1, keepdims=True)
    xc = x - mu
    var = jnp.mean(xc * xc, axis=-1, keepdims=True)
    return xc * lax.rsqrt(var + LN_EPS)


def _ada_kernel(cond_ref, w_ref, b_ref, o_ref):
    c = cond_ref[...]
    a = (c * jax.nn.sigmoid(c)).astype(BF16)
    o_ref[...] = jnp.dot(a, w_ref[...].astype(BF16), preferred_element_type=F32) + b_ref[...]


def _ada_mod(cond, w_ada, b_ada):
    depth, d, n = w_ada.shape
    r = cond.shape[0]
    tn = 512
    return pl.pallas_call(
        _ada_kernel,
        out_shape=jax.ShapeDtypeStruct((depth, r, n), F32),
        grid=(depth, n // tn),
        in_specs=[
            pl.BlockSpec((r, d), lambda l, j: (0, 0)),
            pl.BlockSpec((None, d, tn), lambda l, j: (l, 0, j)),
            pl.BlockSpec((None, 1, tn), lambda l, j: (l, 0, j)),
        ],
        out_specs=pl.BlockSpec((None, r, tn), lambda l, j: (l, 0, j)),
        compiler_params=_params(("arbitrary", "arbitrary")),
        name="ada_mod",
    )(cond, w_ada, b_ada.reshape(depth, 1, n))


def _inproj_kernel(x_ref, scale_ref, shift_ref, w_ref, k_in_ref, v_in_ref, proj_ref, k_ref, v_ref, xa_ref, h_scr,
                   *, kv_tile, xa_tile):
    del k_in_ref, v_in_ref
    j = pl.program_id(1)
    tm = x_ref.shape[0]

    assert kv_tile != 0 and xa_tile != 0

    @pl.when(j == 0)
    def _():
        for r0 in range(0, tm, INPROJ_SUB):
            rows = slice(r0, r0 + INPROJ_SUB)
            h = (_layernorm_f32(x_ref[rows, :]) * (1.0 + scale_ref[0]) + shift_ref[0]).astype(BF16)
            h_scr[rows, :] = h
            proj_ref[rows, :] = jnp.dot(h, w_ref[...], preferred_element_type=F32).astype(BF16)

    @pl.when(j != 0)
    def _():
        acc = jnp.dot(h_scr[...], w_ref[...], preferred_element_type=F32)
        proj_ref[...] = acc.astype(BF16)

        @pl.when(j == kv_tile)
        def _():
            k_ref[...] = acc[:, 0:D_KV].reshape(k_ref.shape)
            v_ref[...] = acc[:, D_KV:2 * D_KV].reshape(v_ref.shape)

        @pl.when(j == xa_tile)
        def _():
            xa_ref[...] = acc[:, 0:D_SSM]


def _inproj(x2, scale, shift, w_bf16, layer, n_seq, seq_len, k_arr, v_arr, slot):
    t, d = x2.shape
    n = w_bf16.shape[2]
    tm, tn = INPROJ_TM, INPROJ_TN
    assert t % tm == 0 and n % tn == 0 and COL_K % tn == 0 and COL_XA % tn == 0
    assert 2 * D_KV <= tn and D_SSM <= tn and W_IN_ROTATE % tn == 0
    n_tiles = n // tn
    rot_tiles = W_IN_ROTATE // tn
    if seq_len >= tm:
        assert seq_len % tm == 0
        tiles_per_seq = seq_len // tm
        seq_of = lambda i: i // tiles_per_seq
        kv_spec = pl.BlockSpec((None, None, tm, D_KV), lambda i, j: (i // tiles_per_seq, slot, i % tiles_per_seq, 0))
    else:
        assert tm % seq_len == 0
        seq_of = lambda i: i * (tm // seq_len)
        kv_spec = pl.BlockSpec((tm // seq_len, None, seq_len, D_KV), lambda i, j: (i, slot, 0, 0))
    if scale.shape[0] == 1:
        mod_map = lambda i, j: (0, 0, 0)
    else:
        assert seq_len >= tm
        mod_map = lambda i, j: (seq_of(i), 0, 0)
    any_spec = pl.BlockSpec(memory_space=pl.ANY)
    return pl.pallas_call(
        functools.partial(_inproj_kernel, kv_tile=COL_K // tn, xa_tile=COL_XA // tn),
        out_shape=(jax.ShapeDtypeStruct((t, n), BF16), jax.ShapeDtypeStruct(k_arr.shape, F32),
                   jax.ShapeDtypeStruct(v_arr.shape, F32), jax.ShapeDtypeStruct((t, D_SSM), F32)),
        grid=(t // tm, n // tn),
        in_specs=[
            pl.BlockSpec((tm, d), lambda i, j: (i, 0)),
            pl.BlockSpec((1, 1, d), mod_map),
            pl.BlockSpec((1, 1, d), mod_map),
            pl.BlockSpec((None, d, tn), lambda i, j: (layer, 0, (j + rot_tiles) % n_tiles)),
            any_spec, any_spec,
        ],
        out_specs=(
            pl.BlockSpec((tm, tn), lambda i, j: (i, j)),
            kv_spec, kv_spec,
            pl.BlockSpec((tm, D_SSM), lambda i, j: (i, 0)),
        ),
        input_output_aliases={4: 1, 5: 2},
        scratch_shapes=[pltpu.VMEM((tm, d), BF16)],
        compiler_params=_params(("arbitrary", "arbitrary")),
        name="ln_inproj",
    )(x2, scale, shift, w_bf16, k_arr, v_arr)


def _zoh(lr, li, ls):
    dt = jnp.exp(ls)
    mag = jnp.exp(lr * dt)
    ar = mag * jnp.cos(li * dt)
    ai = mag * jnp.sin(li * dt)
    den = lr * lr + li * li
    nr = ar - 1.0
    return ar, ai, (nr * lr + ai * li) / den, (ai * lr - nr * li) / den


def _cmul(xr, xi, yr, yi):
    return xr * yr - xi * yi, xr * yi + xi * yr


def _cpow(br, bi, expo, n_bits):
    shape = jnp.broadcast_shapes(br.shape, expo.shape)
    pr = jnp.ones(shape, F32)
    pi = jnp.zeros(shape, F32)
    for bit in range(n_bits):
        sel = ((expo >> bit) & 1) == 1
        nr, ni = _cmul(pr, pi, br, bi)
        pr = jnp.where(sel, nr, pr)
        pi = jnp.where(sel, ni, pi)
        if bit + 1 < n_bits:
            br, bi = _cmul(br, bi, br, bi)
    return pr, pi


def _ssm_tables_kernel(laml_ref, bt_ref, ct_ref, ec_ref, ek_ref, wi_ref, ws_ref, we_ref, small_ref):
    p_n, t_n, c_n = SSM_STATE, SSM_T, SSM_GROUP
    p2 = 2 * p_n
    tc = t_n * c_n
    n_bits = 5
    assert max(t_n, SSM_NC) < 2 ** n_bits and SSM_POW_ROWS <= 2 ** n_bits

    def dot(a, b):
        return jnp.dot(a, b, precision=lax.Precision.HIGHEST, preferred_element_type=F32)

    def spread(x, onehot):
        hi = x.astype(BF16)
        rest = x - hi.astype(F32)
        mid = rest.astype(BF16)
        lo = (rest - mid.astype(F32)).astype(BF16)
        return (jnp.dot(hi, onehot, preferred_element_type=F32) + jnp.dot(mid, onehot, preferred_element_type=F32)
                + jnp.dot(lo, onehot, preferred_element_type=F32))

    ar, ai, f_re, f_im = _zoh(laml_ref[0:1, :], laml_ref[1:2, :], laml_ref[2:3, :])
    is_fwd = lax.broadcasted_iota(jnp.int32, (1, p2), 1) < p_n
    bbt_re, bbt_im = _cmul(f_re, f_im, bt_ref[:, 0:p2], bt_ref[:, p2:2 * p2])
    krow = lax.broadcasted_iota(jnp.int32, (SSM_POW_ROWS, 1), 0)
    n_lane = ct_ref.shape[1]
    kclamp = jnp.minimum(lax.broadcasted_iota(jnp.int32, (n_lane, 1), 0), t_n)
    pw_re, pw_im = _cpow(ar, ai, kclamp, n_bits)
    for j in range(t_n):
        pm_re = jnp.where(is_fwd, pw_re[t_n - 1 - j:t_n - j], pw_re[j:j + 1])
        pm_im = jnp.where(is_fwd, pw_im[t_n - 1 - j:t_n - j], pw_im[j:j + 1])
        s_re, s_im = _cmul(bbt_re, bbt_im, pm_re, pm_im)
        ws_ref[j * c_n:(j + 1) * c_n, 0:p2] = s_re.astype(BF16)
        ws_ref[j * c_n:(j + 1) * c_n, p2:2 * p2] = s_im.astype(BF16)
    a1r = pw_re[t_n:t_n + 1]
    a1i = pw_im[t_n:t_n + 1]
    q_re, q_im = _cpow(a1r, a1i, jnp.where(is_fwd, krow, jnp.maximum(SSM_NC - 1 - krow, 0)), n_bits)
    as_re, as_im = _cpow(a1r, a1i, jnp.full((1, 1), SSM_NC, jnp.int32), n_bits)
    small_ref[0:SSM_NC, :] = q_re[0:SSM_NC]
    small_ref[SSM_NC:2 * SSM_NC, :] = q_im[0:SSM_NC]
    small_ref[2 * SSM_NC:SSM_SMALL_ROWS, :] = jnp.concatenate(
        [a1r, a1i, as_re, as_im, jnp.zeros((SSM_SMALL_ROWS - 2 * SSM_NC - 4, p2), F32)], axis=0)

    pt_re_all = pw_re.T
    pt_im_all = pw_im.T
    cpt = []
    for d in range(2):
        pt_re = pt_re_all[d * p_n:(d + 1) * p_n, :]
        pt_im = pt_im_all[d * p_n:(d + 1) * p_n, :]
        c_re = spread(ct_ref[...], ec_ref[d])
        c_im = spread(ct_ref[...], ec_ref[2 + d])
        cpt.append(_cmul(c_re, c_im, spread(pt_re, ek_ref[d]), spread(pt_im, ek_ref[d])))
        e_re, e_im = _cmul(c_re, c_im, spread(pt_re, ek_ref[2 + d]), spread(pt_im, ek_ref[2 + d]))
        we_ref[d * p_n:(d + 1) * p_n, :] = e_re.astype(BF16)
        we_ref[p2 + d * p_n:p2 + (d + 1) * p_n, :] = (-e_im).astype(BF16)
    cpt_re = jnp.concatenate([cpt[0][0], cpt[1][0]], axis=0)
    cpt_im = jnp.concatenate([cpt[0][1], cpt[1][1]], axis=0)
    taps = []
    for sel in (is_fwd, jnp.logical_not(is_fwd)):
        taps.append(dot(jnp.where(sel, bbt_re, 0.0), cpt_re) - dot(jnp.where(sel, bbt_im, 0.0), cpt_im))
    lane = lax.broadcasted_iota(jnp.int32, (c_n, tc), 1)
    for j in range(t_n):
        fwd = jnp.where(lane >= j * c_n, pltpu.roll(taps[0], j * c_n, 1), 0.0)
        bwd = jnp.where(lane < (j + 1) * c_n, pltpu.roll(taps[1], (tc - (t_n - 1 - j) * c_n) % tc, 1), 0.0)
        wi_ref[j * c_n:(j + 1) * c_n, :] = (fwd + bwd).astype(BF16)


def _ssm_table_constants():
    t_n, c_n = SSM_T, SSM_GROUP
    rows = np.arange(128)[:, None]
    tok = (np.arange(t_n * c_n) // c_n)[None, :]
    ch = (np.arange(t_n * c_n) % c_n)[None, :]
    ec = np.stack([(rows == q * c_n + ch) for q in range(4)], axis=0)
    ek = np.stack([rows == tok, rows == t_n - 1 - tok, rows == tok + 1, rows == t_n - tok], axis=0)
    return jnp.asarray(ec.astype(BF16)), jnp.asarray(ek.astype(BF16))


def _ssm_tables(lam_re, lam_im, log_step, b_re, b_im, c_re, c_im):
    l_n, _, g_n, p_n = lam_re.shape
    c_n, t_n = SSM_GROUP, SSM_T
    tc = t_n * c_n
    p2, p4 = 2 * p_n, 4 * p_n
    ls = jnp.broadcast_to(log_step.astype(F32)[..., None], lam_re.shape)

    def lanes(x):
        return x.astype(F32).transpose(0, 2, 1, 3).reshape(l_n, g_n, 1, p2)

    laml = jnp.concatenate([lanes(lam_re), lanes(lam_im), lanes(ls), jnp.zeros((l_n, g_n, 5, p2), F32)], axis=2)

    def b_lanes(x):
        return x.astype(F32).transpose(0, 2, 4, 1, 3).reshape(l_n, g_n, c_n, p2)

    bt = jnp.concatenate([b_lanes(b_re), b_lanes(b_im)], axis=-1)

    def c_lanes(x):
        return x.astype(F32).transpose(0, 2, 4, 1, 3).reshape(l_n, g_n, p_n, 2 * c_n)

    ct = jnp.concatenate([c_lanes(c_re), c_lanes(c_im), jnp.zeros((l_n, g_n, p_n, 128 - 4 * c_n), F32)], axis=-1)
    ec, ek = _ssm_table_constants()

    def per_group(*dims):
        return pl.BlockSpec((None, None) + dims, lambda l, g: (l, g) + (0,) * len(dims))

    def const(shape):
        return pl.BlockSpec(shape, lambda l, g: (0,) * len(shape))

    return pl.pallas_call(
        _ssm_tables_kernel,
        out_shape=(jax.ShapeDtypeStruct((l_n, g_n, tc, tc), BF16), jax.ShapeDtypeStruct((l_n, g_n, tc, p4), BF16),
                   jax.ShapeDtypeStruct((l_n, g_n, p4, tc), BF16),
                   jax.ShapeDtypeStruct((l_n, g_n, SSM_SMALL_ROWS, p2), F32)),
        grid=(l_n, g_n),
        in_specs=[per_group(8, p2), per_group(c_n, p4), per_group(p_n, 128),
                  const(ec.shape), const(ek.shape)],
        out_specs=(per_group(tc, tc), per_group(tc, p4), per_group(p4, tc), per_group(SSM_SMALL_ROWS, p2)),
        compiler_params=_params(("arbitrary", "arbitrary")),
        name="ssm_tables",
    )(laml, bt, ct, ec, ek)


def _ssm_fold_matrix():
    half_t = SSM_T // 2
    n = half_t * SSM_LANE_GROUPS * SSM_GROUP
    src = np.arange(n)
    j = src // (SSM_LANE_GROUPS * SSM_GROUP)
    g = (src // SSM_GROUP) % SSM_LANE_GROUPS
    c = src % SSM_GROUP
    dst = g * (half_t * SSM_GROUP) + j * SSM_GROUP + c
    return jnp.asarray((dst[:, None] == np.arange(n)[None, :]).astype(BF16))


def _ssm_kernel(xa_ref, fold_ref, wi_ref, ws_ref, we_ref, small_ref, s0_ref, y_ref, fin_ref,
                v_scr, s_scr, yy_scr, fre_scr, fim_scr, efre_scr, efim_scr, ebre_scr, ebim_scr,
                *, n_seq, n_seg):
    p2 = 2 * SSM_STATE
    p4 = 4 * SSM_STATE
    nbp = n_seq * n_seg
    lanes = SSM_LANE_GROUPS * SSM_GROUP
    nt = (((1,), (1,)), ((), ()))

    blocks = []
    for c in range(SSM_NC):
        toks = [xa_ref[pl.ds(c * SSM_T + j, nbp, stride=SSM_SEG), :].astype(BF16) for j in range(SSM_T)]
        blocks.append(jnp.concatenate(toks, axis=1))
    xx = jnp.concatenate(blocks, axis=0)
    hw = fold_ref.shape[0]
    uu = [jnp.dot(xx[:, h * hw:(h + 1) * hw], fold_ref[...], preferred_element_type=F32).astype(BF16)
          for h in range(2)]

    lane = lax.broadcasted_iota(jnp.int32, (1, p2), 1)
    is_fwd = lane < SSM_STATE
    for g in range(SSM_LANE_GROUPS):
        gc = slice(g * SSM_TC, (g + 1) * SSM_TC)
        u = jnp.concatenate([uu[h][:, g * lanes:(g + 1) * lanes] for h in range(2)], axis=1)
        yy_scr[:, gc] = jnp.dot(u, wi_ref[g], preferred_element_type=F32)
        v_scr[:, gc] = jnp.dot(u, ws_ref[g], preferred_element_type=F32)

    for g in range(SSM_LANE_GROUPS):
        o = g * p4
        a1r = small_ref[g, 2 * SSM_NC:2 * SSM_NC + 1, :]
        a1i = small_ref[g, 2 * SSM_NC + 1:2 * SSM_NC + 2, :]
        asr = small_ref[g, 2 * SSM_NC + 2:2 * SSM_NC + 3, :]
        asi = small_ref[g, 2 * SSM_NC + 3:2 * SSM_NC + 4, :]

        st_re = jnp.zeros((nbp, p2), F32)
        st_im = jnp.zeros((nbp, p2), F32)
        for c in range(SSM_NC):
            cb = SSM_NC - 1 - c
            rf = slice(c * nbp, (c + 1) * nbp)
            rb = slice(cb * nbp, (cb + 1) * nbp)
            s_scr[rf, o:o + SSM_STATE] = st_re[:, 0:SSM_STATE]
            s_scr[rb, o + SSM_STATE:o + p2] = st_re[:, SSM_STATE:p2]
            s_scr[rf, o + p2:o + p2 + SSM_STATE] = st_im[:, 0:SSM_STATE]
            s_scr[rb, o + p2 + SSM_STATE:o + p4] = st_im[:, SSM_STATE:p2]
            v_re = jnp.where(is_fwd, v_scr[rf, o:o + p2], v_scr[rb, o:o + p2])
            v_im = jnp.where(is_fwd, v_scr[rf, o + p2:o + p4], v_scr[rb, o + p2:o + p4])
            st_re, st_im = (a1r * st_re - a1i * st_im + v_re, a1r * st_im + a1i * st_re + v_im)

        fre_scr[...] = st_re
        fim_scr[...] = st_im
        s0 = s0_ref[g]
        e_re = s0[:, 0:p2]
        e_im = s0[:, p2:p4]
        for k in range(n_seg):
            kb = n_seg - 1 - k
            rows_f = pl.ds(k, n_seq, stride=n_seg)
            rows_b = pl.ds(kb, n_seq, stride=n_seg)
            efre_scr[rows_f, :] = e_re
            efim_scr[rows_f, :] = e_im
            ebre_scr[rows_b, :] = e_re
            ebim_scr[rows_b, :] = e_im
            f_re = jnp.where(is_fwd, fre_scr[rows_f, :], fre_scr[rows_b, :])
            f_im = jnp.where(is_fwd, fim_scr[rows_f, :], fim_scr[rows_b, :])
            e_re, e_im = (asr * e_re - asi * e_im + f_re, asr * e_im + asi * e_re + f_im)
        fin_ref[g] = jnp.concatenate([e_re, e_im], axis=-1)

        en_re = jnp.where(is_fwd, efre_scr[...], ebre_scr[...])
        en_im = jnp.where(is_fwd, efim_scr[...], ebim_scr[...])
        for c in range(SSM_NC):
            rows = slice(c * nbp, (c + 1) * nbp)
            cr = small_ref[g, c:c + 1, :]
            ci = small_ref[g, SSM_NC + c:SSM_NC + c + 1, :]
            s_scr[rows, o:o + p2] = s_scr[rows, o:o + p2] + (cr * en_re - ci * en_im)
            s_scr[rows, o + p2:o + p4] = s_scr[rows, o + p2:o + p4] + (cr * en_im + ci * en_re)

    for g in range(SSM_LANE_GROUPS):
        gc = slice(g * SSM_TC, (g + 1) * SSM_TC)
        yy_scr[:, gc] += jnp.dot(s_scr[:, g * p4:(g + 1) * p4].astype(BF16), we_ref[g],
                                 preferred_element_type=F32)

    half_t = SSM_T // 2
    for h in range(2):
        yh = jnp.concatenate([yy_scr[:, g * SSM_TC + h * lanes:g * SSM_TC + (h + 1) * lanes]
                              for g in range(SSM_LANE_GROUPS)], axis=1).astype(BF16)
        zz = lax.dot_general(yh, fold_ref[...], nt, preferred_element_type=F32)
        for c in range(SSM_NC):
            for j in range(half_t):
                y_ref[pl.ds(c * SSM_T + h * half_t + j, nbp, stride=SSM_SEG), :] = zz[c * nbp:(c + 1) * nbp,
                                                                                       j * lanes:(j + 1) * lanes]


def _ssm_scan(xa, fold, w_intra, w_state, w_enter, small, layer, s0, n_seq, n_seg):
    t = xa.shape[0]
    g_n = N_SSM_GROUPS
    lg = SSM_LANE_GROUPS
    lanes = lg * SSM_GROUP
    p4 = 4 * SSM_STATE
    nbp = n_seq * n_seg
    r = SSM_NC * nbp
    assert t == r * SSM_T and (SSM_T // 2) * SSM_GROUP == lanes and fold.shape == (lg * lanes, lg * lanes)
    const = lambda shape: pl.BlockSpec(shape, lambda i: (0,) * len(shape), pipeline_mode=pl.Buffered(1))
    return pl.pallas_call(
        functools.partial(_ssm_kernel, n_seq=n_seq, n_seg=n_seg),
        out_shape=(jax.ShapeDtypeStruct((t, D_SSM), F32), jax.ShapeDtypeStruct((g_n, n_seq, p4), F32)),
        grid=(g_n // lg,),
        in_specs=[
            pl.BlockSpec((t, lanes), lambda i: (0, i)),
            const(fold.shape),
            pl.BlockSpec((None, lg, SSM_TC, SSM_TC), lambda i: (layer, i, 0, 0)),
            pl.BlockSpec((None, lg, SSM_TC, p4), lambda i: (layer, i, 0, 0)),
            pl.BlockSpec((None, lg, p4, SSM_TC), lambda i: (layer, i, 0, 0)),
            pl.BlockSpec((None, lg, SSM_SMALL_ROWS, 2 * SSM_STATE), lambda i: (layer, i, 0, 0)),
            pl.BlockSpec((lg, n_seq, p4), lambda i: (i, 0, 0)),
        ],
        out_specs=(
            pl.BlockSpec((t, lanes), lambda i: (0, i)),
            pl.BlockSpec((lg, n_seq, p4), lambda i: (i, 0, 0)),
        ),
        scratch_shapes=[pltpu.VMEM((r, lg * p4), F32), pltpu.VMEM((r, lg * p4), F32),
                        pltpu.VMEM((r, lg * SSM_TC), F32)]
        + [pltpu.VMEM((nbp, 2 * SSM_STATE), F32)] * 6,
        compiler_params=_params(("arbitrary",)),
        name="ssm_scan",
    )(xa, fold, w_intra, w_state, w_enter, small, s0)


def _ssm_post_kernel(xa_ref, za_ref, ys_ref, d_ref, w_ref, b_ref, o_ref):
    y = xa_ref[...].astype(F32) * d_ref[...] + ys_ref[...]
    y = jax.nn.gelu(y)
    gl = jnp.dot(y.astype(BF16), w_ref[...], preferred_element_type=F32) + b_ref[...]
    y = y * jax.nn.sigmoid(gl)
    o_ref[...] = (y * jax.nn.silu(za_ref[...].astype(F32))).astype(BF16)


def _ssm_post(proj, yssm, d_skip, w_glu_bf16, b_glu):
    t = proj.shape[0]
    tm = ROW_TM
    return pl.pallas_call(
        _ssm_post_kernel,
        out_shape=jax.ShapeDtypeStruct((t, D_SSM), BF16),
        grid=(t // tm,),
        in_specs=[
            pl.BlockSpec((tm, D_SSM), lambda i: (i, COL_XA // D_SSM)),
            pl.BlockSpec((tm, D_SSM), lambda i: (i, COL_ZA // D_SSM)),
            pl.BlockSpec((tm, D_SSM), lambda i: (i, 0)),
            pl.BlockSpec((1, D_SSM), lambda i: (0, 0)),
            pl.BlockSpec((D_SSM, D_SSM), lambda i: (0, 0)),
            pl.BlockSpec((1, D_SSM), lambda i: (0, 0)),
        ],
        out_specs=pl.BlockSpec((tm, D_SSM), lambda i: (i, 0)),
        compiler_params=_params(("arbitrary",)),
        name="ssm_post",
    )(proj, proj, yssm, d_skip.reshape(1, D_SSM), w_glu_bf16, b_glu.reshape(1, D_SSM))


def _softmax_pv_wide(s, sink, v_bf16):
    hd = v_bf16.shape[1]
    m = jnp.maximum(jnp.max(s, axis=-1, keepdims=True), sink)
    p = jnp.exp(s - m).astype(BF16)
    v_ones = jnp.concatenate([v_bf16, jnp.ones_like(v_bf16)], axis=1)
    o = jnp.dot(p, v_ones, preferred_element_type=F32)
    den = o[:, hd:] + jnp.exp(sink - m)
    return o[:, :hd] / den


def _head_gate(zb_refs, h):
    per = N_HEADS // 2
    z = zb_refs[h // per][:, (h % per) * HEAD_DIM:(h % per + 1) * HEAD_DIM].astype(F32)
    return jax.nn.silu(z)


def _ctx_attn_kernel(sink_ref, q_ref, k_ref, v_ref, zb0_ref, zb1_ref, o_ref):
    scale = HEAD_DIM ** -0.5
    nt = (((1,), (1,)), ((), ()))
    for kvh in range(N_KV_HEADS):
        k = (k_ref[:, kvh * HEAD_DIM:(kvh + 1) * HEAD_DIM] * scale).astype(BF16)
        v = v_ref[:, kvh * HEAD_DIM:(kvh + 1) * HEAD_DIM].astype(BF16)
        for g in range(Q_PER_KV):
            h = kvh * Q_PER_KV + g
            hs = slice(h * HEAD_DIM, (h + 1) * HEAD_DIM)
            s = lax.dot_general(q_ref[:, hs], k, nt, preferred_element_type=F32)
            o = _softmax_pv_wide(s, sink_ref[h], v)
            o_ref[:, hs] = (o * _head_gate((zb0_ref, zb1_ref), h)).astype(BF16)


def _ctx_attention(proj, k_arr, v_arr, slot, sink, seq_len):
    t = proj.shape[0]
    kv_spec = pl.BlockSpec((None, None, seq_len, D_KV), lambda b: (b, slot, 0, 0))
    return pl.pallas_call(
        _ctx_attn_kernel,
        out_shape=jax.ShapeDtypeStruct((t, D_ATTN), BF16),
        grid=(t // seq_len,),
        in_specs=[
            pl.BlockSpec(memory_space=pltpu.SMEM),
            pl.BlockSpec((seq_len, D_ATTN), lambda b: (b, COL_Q // D_ATTN)),
            kv_spec, kv_spec,
            pl.BlockSpec((seq_len, D_ATTN // 2), lambda b: (b, COL_ZB // (D_ATTN // 2))),
            pl.BlockSpec((seq_len, D_ATTN // 2), lambda b: (b, COL_ZB // (D_ATTN // 2) + 1)),
        ],
        out_specs=pl.BlockSpec((seq_len, D_ATTN), lambda b: (b, 0)),
        compiler_params=_params(("arbitrary",)),
        name="ctx_attention",
    )(sink, proj, k_arr, v_arr, proj, proj)


def _rope_partner(x):
    lane = lax.broadcasted_iota(jnp.int32, x.shape, 1)
    quarter = HEAD_DIM // 4
    first = (lane % (2 * quarter)) < quarter
    return jnp.where(first, pltpu.roll(x, HEAD_DIM - quarter, 1), pltpu.roll(x, quarter, 1))


def _rope(x, tab_ref):
    return x * tab_ref[:, 0:HEAD_DIM] + _rope_partner(x) * tab_ref[:, HEAD_DIM:2 * HEAD_DIM]


def _rope_kv_kernel(k_ref, v_ref, tab_ref, o_ref):
    for kvh in range(N_KV_HEADS):
        ks = slice(kvh * HEAD_DIM, (kvh + 1) * HEAD_DIM)
        o_ref[:, ks] = _rope(k_ref[:, ks], tab_ref).astype(BF16)
    o_ref[:, D_KV:2 * D_KV] = v_ref[...].astype(BF16)


def _rope_kv(k_arr, v_arr, slot, tab_k):
    n_seq, _, seq_len, _ = k_arr.shape
    tm = ROW_TM
    assert seq_len % tm == 0
    tiles_per_seq = seq_len // tm
    kv_spec = pl.BlockSpec((None, None, tm, D_KV), lambda i: (i // tiles_per_seq, slot, i % tiles_per_seq, 0))
    return pl.pallas_call(
        _rope_kv_kernel,
        out_shape=jax.ShapeDtypeStruct((n_seq * seq_len, 2 * D_KV), BF16),
        grid=(n_seq * tiles_per_seq,),
        in_specs=[
            kv_spec, kv_spec,
            pl.BlockSpec((tm, 2 * HEAD_DIM), lambda i: (i % tiles_per_seq, 0)),
        ],
        out_specs=pl.BlockSpec((tm, 2 * D_KV), lambda i: (i, 0)),
        compiler_params=_params(("arbitrary",)),
        name="rope_kv",
    )(k_arr, v_arr, tab_k)


def _lat_attn_kernel(sink_ref, q_ref, kvp_ref, kvc_ref, kvn_ref, ck_ref, cv_ref, zb0_ref, zb1_ref,
                     tabq_ref, swap_ref, bias_ref, o_ref):
    nt = (((1,), (1,)), ((), ()))
    blk = ATTN_BLOCK
    cos_q = tabq_ref[:, 0:HEAD_DIM]
    sin_q = tabq_ref[:, HEAD_DIM:2 * HEAD_DIM]
    for kvh in range(N_KV_HEADS):
        ks = slice(kvh * HEAD_DIM, (kvh + 1) * HEAD_DIM)
        vs = slice(D_KV + kvh * HEAD_DIM, D_KV + (kvh + 1) * HEAD_DIM)
        keys = [jnp.concatenate([kvp_ref[:, ks], kvc_ref[:, ks], kvn_ref[:, ks]], axis=0),
                ck_ref[:, ks].astype(BF16)]
        values = [jnp.concatenate([kvp_ref[:, vs], kvc_ref[:, vs], kvn_ref[:, vs]], axis=0),
                  cv_ref[:, ks].astype(BF16)]
        qx = jnp.concatenate(
            [q_ref[:, (kvh * Q_PER_KV + g) * HEAD_DIM:(kvh * Q_PER_KV + g + 1) * HEAD_DIM]
             for g in range(Q_PER_KV)], axis=0)
        partner = jnp.dot(qx, swap_ref[...], preferred_element_type=F32)
        qf = qx.astype(F32).reshape(Q_PER_KV, blk, HEAD_DIM)
        q4 = (qf * cos_q[None] + partner.reshape(Q_PER_KV, blk, HEAD_DIM) * sin_q[None])
        q4 = q4.reshape(Q_PER_KV * blk, HEAD_DIM).astype(BF16)
        s_loc = lax.dot_general(q4, keys[0], nt, preferred_element_type=F32)
        s_loc = (s_loc.reshape(Q_PER_KV, blk, 3 * blk) + bias_ref[...][None]).reshape(Q_PER_KV * blk, 3 * blk)
        s_ctx = lax.dot_general(q4, keys[1], nt, preferred_element_type=F32)
        sink_rows = jnp.concatenate(
            [jnp.full((blk, 1), sink_ref[kvh * Q_PER_KV + g], F32) for g in range(Q_PER_KV)], axis=0)
        o = _softmax_pv_wide(jnp.concatenate([s_loc, s_ctx], axis=-1), sink_rows,
                             jnp.concatenate(values, axis=0))
        for g in range(Q_PER_KV):
            h = kvh * Q_PER_KV + g
            o_ref[:, h * HEAD_DIM:(h + 1) * HEAD_DIM] = (
                o[g * blk:(g + 1) * blk] * _head_gate((zb0_ref, zb1_ref), h)).astype(BF16)


def _lat_attention(proj, k_arr, v_arr, slot, ctx_k, ctx_v, sink, tab_q, tab_k, bias, swap, n_seq, seq_len):
    kv = _rope_kv(k_arr, v_arr, slot, tab_k)
    t = proj.shape[0]
    blk = ATTN_BLOCK
    assert blk - 1 <= WINDOW <= blk
    nb = seq_len // blk
    lc = ctx_k.shape[1]

    def row(b, i):
        return b * nb + i

    def prev(i):
        return jnp.maximum(i - 1, 0)

    def nxt(i):
        return jnp.minimum(i + 1, nb - 1)

    def bias_variant(b, i):
        return ((i > 0).astype(jnp.int32) * 2 + (i < nb - 1).astype(jnp.int32), 0, 0)

    return pl.pallas_call(
        _lat_attn_kernel,
        out_shape=jax.ShapeDtypeStruct((t, D_ATTN), BF16),
        grid=(n_seq, nb),
        in_specs=[
            pl.BlockSpec(memory_space=pltpu.SMEM),
            pl.BlockSpec((blk, D_ATTN), lambda b, i: (row(b, i), COL_Q // D_ATTN)),
            pl.BlockSpec((blk, 2 * D_KV), lambda b, i: (row(b, prev(i)), 0)),
            pl.BlockSpec((blk, 2 * D_KV), lambda b, i: (row(b, i), 0)),
            pl.BlockSpec((blk, 2 * D_KV), lambda b, i: (row(b, nxt(i)), 0)),
            pl.BlockSpec((None, lc, D_KV), lambda b, i: (b, 0, 0)),
            pl.BlockSpec((None, lc, D_KV), lambda b, i: (b, 0, 0)),
            pl.BlockSpec((blk, D_ATTN // 2), lambda b, i: (row(b, i), COL_ZB // (D_ATTN // 2))),
            pl.BlockSpec((blk, D_ATTN // 2), lambda b, i: (row(b, i), COL_ZB // (D_ATTN // 2) + 1)),
            pl.BlockSpec((blk, 2 * HEAD_DIM), lambda b, i: (i, 0)),
            pl.BlockSpec((HEAD_DIM, HEAD_DIM), lambda b, i: (0, 0)),
            pl.BlockSpec((None, blk, 3 * blk), bias_variant),
        ],
        out_specs=pl.BlockSpec((blk, D_ATTN), lambda b, i: (row(b, i), 0)),
        compiler_params=_params(("arbitrary", "arbitrary")),
        name="lat_attention",
    )(sink, proj, kv, kv, kv, ctx_k, ctx_v, proj, proj, tab_q, swap, bias)


def _rope_tables(seq_len):
    rows = seq_len // GRID_W
    row = np.repeat(np.arange(rows), GRID_W).astype(np.float64)
    col = np.tile(np.arange(GRID_W), rows).astype(np.float64)
    half = HEAD_DIM // 2
    inv = ROPE_BASE ** (-np.arange(0, half, 2, dtype=np.float64) / half)
    ang_r = row[:, None] * inv[None, :]
    ang_c = col[:, None] * inv[None, :]
    cos_t = np.concatenate([np.cos(ang_r)] * 2 + [np.cos(ang_c)] * 2, axis=-1)
    sin_t = np.concatenate([-np.sin(ang_r), np.sin(ang_r), -np.sin(ang_c), np.sin(ang_c)], axis=-1)
    tab_k = np.concatenate([cos_t, sin_t], axis=-1)
    tab_q = tab_k * HEAD_DIM ** -0.5
    return jnp.asarray(tab_q.astype(np.float32)), jnp.asarray(tab_k.astype(np.float32))


def _rope_swap_matrix():
    quarter = HEAD_DIM // 4
    d = np.arange(HEAD_DIM)
    src = np.where(d % (2 * quarter) < quarter, d + quarter, d - quarter)
    return jnp.asarray((np.arange(HEAD_DIM)[:, None] == src[None, :]).astype(BF16))


def _window_bias():
    blk = ATTN_BLOCK
    r = np.arange(blk)[:, None]
    s = np.arange(3 * blk)[None, :]
    band = np.abs(s - blk - r) <= WINDOW
    out = []
    for has_prev in (False, True):
        for has_next in (False, True):
            ok = band & ((s >= blk) | has_prev) & ((s < 2 * blk) | has_next)
            out.append(np.where(ok, 0.0, -np.inf).astype(np.float32))
    return jnp.asarray(np.stack(out, axis=0))


def _sgu_kernel(u_ref, v_ref, z_ref, g_ref, b_ref, ws_ref, bs_ref, o_ref, *, n_chunks):
    for n in range(n_chunks):
        rows = slice(n * SGU_CHUNK, (n + 1) * SGU_CHUNK)
        v = _layernorm_f32(jax.nn.gelu(v_ref[rows, :].astype(F32))) * g_ref[...] + b_ref[...]
        v = v.astype(BF16)
        for g in range(N_SGU_GROUPS):
            cols = slice(g * SGU_GROUP_CH, (g + 1) * SGU_GROUP_CH)
            vm = jnp.dot(ws_ref[g], v[:, cols], preferred_element_type=F32) + bs_ref[:, g:g + 1]
            u = jax.nn.gelu(u_ref[rows, cols].astype(F32))
            z = z_ref[rows, cols].astype(F32)
            o_ref[rows, cols] = (u * vm * jax.nn.silu(z)).astype(BF16)


def _sgu(proj, ln_g, ln_b, w_s_bf16, b_s_t):
    t = proj.shape[0]
    tm = ROW_TM
    return pl.pallas_call(
        functools.partial(_sgu_kernel, n_chunks=tm // SGU_CHUNK),
        out_shape=jax.ShapeDtypeStruct((t, D_SGU), BF16),
        grid=(t // tm,),
        in_specs=[
            pl.BlockSpec((tm, D_SGU), lambda i: (i, COL_U // D_SGU)),
            pl.BlockSpec((tm, D_SGU), lambda i: (i, COL_VS // D_SGU)),
            pl.BlockSpec((tm, D_SGU), lambda i: (i, COL_ZC // D_SGU)),
            pl.BlockSpec((1, D_SGU), lambda i: (0, 0)),
            pl.BlockSpec((1, D_SGU), lambda i: (0, 0)),
            pl.BlockSpec((N_SGU_GROUPS, SGU_CHUNK, SGU_CHUNK), lambda i: (0, 0, 0)),
            pl.BlockSpec((SGU_CHUNK, N_SGU_GROUPS), lambda i: (0, 0)),
        ],
        out_specs=pl.BlockSpec((tm, D_SGU), lambda i: (i, 0)),
        compiler_params=_params(("arbitrary",)),
        name="sgu",
    )(proj, proj, proj, ln_g.reshape(1, D_SGU), ln_b.reshape(1, D_SGU), w_s_bf16, b_s_t)


def _merge_kernel(mg_ref, ya_ref, yb_ref, yc_ref, x_ref, gate_ref, wpa_ref, wpb_ref, wpc_ref, wout_ref,
                  lng_ref, lnb_ref, y_ref, *, alpha):
    d = D_MODEL
    tm = y_ref.shape[0]
    for r0 in range(0, tm, MERGE_SUB):
        rows = slice(r0, r0 + MERGE_SUB)

        def branch(k, y_r, w_r):
            gate = jax.nn.sigmoid(mg_ref[rows, k * d:(k + 1) * d].astype(F32))
            return gate * jnp.dot(y_r[rows, :], w_r[...], preferred_element_type=F32)

        merged = branch(0, ya_ref, wpa_ref) + branch(1, yb_ref, wpb_ref) + branch(2, yc_ref, wpc_ref)
        out = jnp.dot(merged.astype(BF16), wout_ref[...], preferred_element_type=F32)
        z = alpha * x_ref[rows, :] + gate_ref[0] * out
        y_ref[rows, :] = _layernorm_f32(z) * lng_ref[...] + lnb_ref[...]


def _merge(proj, ya, yb, yc, x2, gate, wpa, wpb, wpc, wout, ln_g, ln_b, seq_len, alpha):
    t, d = x2.shape
    tm = MERGE_TM
    if gate.shape[0] == 1:
        mod_map = lambda i: (0, 0, 0)
    else:
        assert seq_len % tm == 0
        tiles_per_seq = seq_len // tm
        mod_map = lambda i: (i // tiles_per_seq, 0, 0)

    def const(shape):
        return pl.BlockSpec(shape, lambda i: (0,) * len(shape), pipeline_mode=pl.Buffered(1))

    return pl.pallas_call(
        functools.partial(_merge_kernel, alpha=alpha),
        out_shape=jax.ShapeDtypeStruct((t, d), F32),
        grid=(t // tm,),
        in_specs=[
            pl.BlockSpec((tm, D_MG), lambda i: (i, 0)),
            pl.BlockSpec((tm, D_SSM), lambda i: (i, 0)),
            pl.BlockSpec((tm, D_ATTN), lambda i: (i, 0)),
            pl.BlockSpec((tm, D_SGU), lambda i: (i, 0)),
            pl.BlockSpec((tm, d), lambda i: (i, 0)),
            pl.BlockSpec((1, 1, d), mod_map),
            const((D_SSM, d)), const((D_ATTN, d)), const((D_SGU, d)), const((d, d)),
            const((1, d)), const((1, d)),
        ],
        out_specs=pl.BlockSpec((tm, d), lambda i: (i, 0)),
        compiler_params=_params(("arbitrary",)),
        name="merge_out",
    )(proj, ya, yb, yc, x2, gate, wpa, wpb, wpc, wout, ln_g.reshape(1, d), ln_b.reshape(1, d))


def _layer(x2, n_seq, seq_len, mod, lp, k_arr, v_arr, slot, ctx_k, ctx_v, s0, rope, alpha):
    d = D_MODEL
    nbm = mod.shape[0]
    shift = mod[:, 0:d].reshape(nbm, 1, d)
    scale = mod[:, d:2 * d].reshape(nbm, 1, d)
    gate = mod[:, 2 * d:3 * d].reshape(nbm, 1, d)
    proj, k_arr, v_arr, xa = _inproj(x2, scale, shift, lp['w_in'], lp['layer'], n_seq, seq_len, k_arr, v_arr, slot)

    n_seg = seq_len // SSM_SEG
    yssm, fin = _ssm_scan(xa, lp['fold'], lp['w_intra'], lp['w_state'], lp['w_enter'], lp['ssm_small'],
                          lp['layer'], s0, n_seq, n_seg)
    ya = _ssm_post(proj, yssm, lp['d_skip'], lp['w_glu'], lp['b_glu'])

    if ctx_k is None:
        yb = _ctx_attention(proj, k_arr, v_arr, slot, lp['sink'], seq_len)
    else:
        yb = _lat_attention(proj, k_arr, v_arr, slot, ctx_k, ctx_v, lp['sink'], *rope, n_seq, seq_len)

    yc = _sgu(proj, lp['sgu_g'], lp['sgu_b'], lp['w_s'], lp['b_s_t'])

    y = _merge(proj, ya, yb, yc, x2, gate, lp['w_pa'], lp['w_pb'], lp['w_pc'], lp['w_out'],
               lp['ln_g'], lp['ln_b'], seq_len, alpha)
    return y, k_arr, v_arr, fin


def _states_to_lanes(s):
    b = s.shape[0]
    return s.astype(F32).transpose(3, 0, 2, 1, 4).reshape(N_SSM_GROUPS, b, 4 * SSM_STATE)


def _lanes_to_states(f):
    b = f.shape[1]
    return f.reshape(N_SSM_GROUPS, b, 2, 2, SSM_STATE).transpose(1, 3, 2, 0, 4)


def kernel(x_prompt, x_sample, cache_k, cache_v, state_ssm, c, c_ctx, w_ada, b_ada, w_in, ssm_lam_re, ssm_lam_im, ssm_log_step, ssm_b_re, ssm_b_im, ssm_c_re, ssm_c_im, ssm_d, w_glu, b_glu, attn_sink, sgu_ln_g, sgu_ln_b, w_spatial, b_spatial, w_proj_a, w_proj_b, w_proj_c, w_out, ln_g, ln_b):
    depth = w_in.shape[0]
    batch, seq, d = x_prompt.shape
    dec_batch, dec_seq, _ = x_sample.shape
    past_len = cache_k.shape[2]
    alpha = (2 * depth) ** 0.25

    n_cond = 1 + dec_batch
    cond_rows = -(-n_cond // 8) * 8
    cond = jnp.concatenate([c_ctx[None, :], c, jnp.zeros((cond_rows - n_cond, d), F32)], axis=0)
    mod = _ada_mod(cond, w_ada, b_ada)

    w_in_bf16 = w_in.astype(BF16)
    fold = _ssm_fold_matrix()
    tables = _ssm_tables(ssm_lam_re, ssm_lam_im, ssm_log_step, ssm_b_re, ssm_b_im, ssm_c_re, ssm_c_im)
    layers = []
    for l in range(depth):
        w_intra, w_state, w_enter, small = tables
        layers.append({
            'w_in': w_in_bf16, 'layer': l,
            'fold': fold, 'w_intra': w_intra, 'w_state': w_state, 'w_enter': w_enter, 'ssm_small': small,
            'd_skip': ssm_d[l], 'w_glu': w_glu[l].astype(BF16), 'b_glu': b_glu[l],
            'sink': attn_sink[l].astype(F32),
            'sgu_g': sgu_ln_g[l], 'sgu_b': sgu_ln_b[l],
            'w_s': w_spatial[l].astype(BF16), 'b_s_t': b_spatial[l].T.astype(F32),
            'w_pa': w_proj_a[l].astype(BF16), 'w_pb': w_proj_b[l].astype(BF16),
            'w_pc': w_proj_c[l].astype(BF16), 'w_out': w_out[l].astype(BF16),
            'ln_g': ln_g[l], 'ln_b': ln_b[l],
        })

    h = x_prompt.reshape(batch * seq, d)
    zero_state = jnp.zeros((N_SSM_GROUPS, batch, 4 * SSM_STATE), F32)
    k_new = jnp.zeros((batch, depth, seq, D_KV), F32)
    v_new = jnp.zeros((batch, depth, seq, D_KV), F32)
    ss = []
    for l in range(depth):
        h, k_new, v_new, fin = _layer(h, batch, seq, mod[l, 0:1], layers[l], k_new, v_new, l,
                                      None, None, zero_state, None, alpha)
        ss.append(_lanes_to_states(fin))
    y_prompt = h.reshape(batch, seq, d)
    new_cache_k = k_new.reshape(batch, depth, seq, N_KV_HEADS, HEAD_DIM)
    new_cache_v = v_new.reshape(batch, depth, seq, N_KV_HEADS, HEAD_DIM)
    new_state_ssm = jnp.stack(ss, axis=1)

    rope = _rope_tables(dec_seq) + (_window_bias(), _rope_swap_matrix())
    z = x_sample.reshape(dec_batch * dec_seq, d)
    k_lat = jnp.zeros((dec_batch, 1, dec_seq, D_KV), F32)
    v_lat = jnp.zeros((dec_batch, 1, dec_seq, D_KV), F32)
    for l in range(depth):
        ctx_k = cache_k[:, l].reshape(dec_batch, past_len, D_KV).astype(F32)
        ctx_v = cache_v[:, l].reshape(dec_batch, past_len, D_KV).astype(F32)
        z, k_lat, v_lat, _ = _layer(z, dec_batch, dec_seq, mod[l, 1:1 + dec_batch], layers[l], k_lat, v_lat, 0,
                                    ctx_k, ctx_v, _states_to_lanes(state_ssm[:, l]), rope, alpha)
    y_sample = z.reshape(dec_batch, dec_seq, d)
    return (y_prompt, y_sample, new_cache_k, new_cache_v, new_state_ssm)
```
